```python
import math
import jax, jax.numpy as jnp
from jax import lax
import numpy as np


D_MODEL = 2048
BATCH = 1
SEQ = 8192
DEPTH = 2

N_EVEN = (DEPTH + 1) // 2
N_ODD = DEPTH // 2
RMS_EPS = 1e-6

CONV_WIDTH = D_MODEL // 2
CONV_GROUPS = 8
CONV_K = 3
S5_WIDTH = D_MODEL // 2
S5_GROUP = 16
S5_GROUPS = S5_WIDTH // S5_GROUP
S5_STATE = 64
DT_MIN = 0.001
DT_MAX = 0.1
AB_IN_WIDTH = 3 * CONV_WIDTH + S5_WIDTH
FOX_HEADS = 16
FOX_HEAD_DIM = D_MODEL // FOX_HEADS
Q_BLOCK = 128
FORGET_BIAS_LO = 1.0
FORGET_BIAS_HI = 5.0
N_GROUPS = 4
EXPERTS_PER_GROUP = 8
TOP_K = 2
EXPERT_FF = 512

kernel_name = 'hybrid_conv_s5_fox_hmoe'


def rmsnorm(x, g):
    xf = x.astype(jnp.float32)
    y = xf * lax.rsqrt(jnp.mean(xf * xf, axis=-1, keepdims=True) + RMS_EPS)
    return (y * g.astype(jnp.float32)).astype(x.dtype)


def short_conv_mixer(gate_b, gate_c, u, conv_w):
    v = gate_c * u
    rhs = conv_w.astype(v.dtype)[:, None, :]
    conv = lax.conv_general_dilated(v, rhs, window_strides=(1,), padding=[(CONV_K - 1, 0)],
                                    dimension_numbers=('NWC', 'WIO', 'NWC'),
                                    feature_group_count=CONV_WIDTH)
    return gate_b * conv


def s5_mixer(u, lam_re, lam_im, log_dt, b_re, b_im, c_re, c_im, d, w_glu, b_glu):
    bsz, L, _ = u.shape
    f32 = jnp.float32
    uf = u.astype(f32).reshape(bsz, L, S5_GROUPS, S5_GROUP)
    lam = lax.complex(lam_re.astype(f32), lam_im.astype(f32))
    dt = jnp.exp(log_dt.astype(f32))[:, None]
    lam_bar = jnp.exp(lam * dt)
    bmat = lax.complex(b_re.astype(f32), b_im.astype(f32))
    b_bar = ((lam_bar - 1.0) / lam)[..., None] * bmat
    bu = jnp.einsum('gnh,blgh->blgn', b_bar, uf.astype(jnp.complex64))
    a = jnp.broadcast_to(lam_bar, bu.shape)

    def combine(e1, e2):
        a1, b1 = e1
        a2, b2 = e2
        return a2 * a1, a2 * b1 + b2

    _, states = lax.associative_scan(combine, (a, bu), axis=1)
    cmat = lax.complex(c_re.astype(f32), c_im.astype(f32))
    y = jnp.real(jnp.einsum('ghn,blgn->blgh', cmat, states))
    y = y + d.astype(f32).reshape(S5_GROUPS, S5_GROUP) * uf
    y = jax.nn.gelu(y.reshape(bsz, L, S5_WIDTH))
    y = y * jax.nn.sigmoid(y @ w_glu.astype(f32) + b_glu.astype(f32))
    return y.astype(u.dtype)


def forgetting_attention(q, k, v, cum):
    bsz, L, H, dh = q.shape
    nb = L // Q_BLOCK
    scale = dh ** -0.5
    qb = q.reshape(bsz, nb, Q_BLOCK, H, dh).transpose(1, 0, 2, 3, 4)
    cq = cum.reshape(bsz, nb, Q_BLOCK, H).transpose(1, 0, 3, 2)
    ck = cum.transpose(0, 2, 1)
    key_pos = jnp.arange(L)

    def block(args):
        i, qi, ci = args
        s = jnp.einsum('bqhd,bkhd->bhqk', qi, k, preferred_element_type=jnp.float32) * scale
        s = s + (ci[..., :, None] - ck[:, :, None, :])
        q_pos = i * Q_BLOCK + jnp.arange(Q_BLOCK)
        mask = key_pos[None, :] <= q_pos[:, None]
        s = jnp.where(mask, s, -jnp.inf)
        p = jax.nn.softmax(s, axis=-1)
        return jnp.einsum('bhqk,bkhd->bqhd', p.astype(v.dtype), v)

    out = lax.map(block, (jnp.arange(nb), qb, cq))
    return out.transpose(1, 0, 2, 3, 4).reshape(bsz, L, H * dh)


def hierarchical_moe(x, w_group, b_group, w_expert, b_expert, w_gate, w_up, w_down):
    bsz, L, D = x.shape
    t = x.reshape(-1, D)
    xf = t.astype(jnp.float32)
    gprob = jax.nn.softmax(xf @ w_group.astype(jnp.float32) + b_group.astype(jnp.float32), axis=-1)
    gw, gidx = lax.top_k(gprob, 1)
    elog = jnp.einsum('td,dge->tge', xf, w_expert.astype(jnp.float32)) + b_expert.astype(jnp.float32)
    elog = jnp.take_along_axis(elog, gidx[:, :, None], axis=1)[:, 0]
    ev, eidx = lax.top_k(elog, TOP_K)
    ew = jax.nn.softmax(ev, axis=-1) * gw
    comb = jnp.einsum('tk,tke->te', ew, jax.nn.one_hot(eidx, EXPERTS_PER_GROUP, dtype=jnp.float32))
    out = jnp.zeros((t.shape[0], D), jnp.float32)
    for g in range(N_GROUPS):
        cg = comb * (gidx == g).astype(jnp.float32)
        h = jax.nn.silu(jnp.einsum('td,edf->tef', t, w_gate[g])) * jnp.einsum('td,edf->tef', t, w_up[g])
        out = out + jnp.einsum('tef,efd->td', h * cg[..., None].astype(h.dtype), w_down[g],
                               preferred_element_type=jnp.float32)
    return out.astype(x.dtype).reshape(bsz, L, D)


def setup_inputs(seed: int = 0) -> dict:
    key = jax.random.key(seed)
    ks = list(jax.random.split(key, 32))
    f32 = jnp.float32

    def nrm(k, shape, scale):
        return jax.random.normal(k, shape, f32) * scale

    D = D_MODEL
    n = jnp.arange(S5_STATE, dtype=f32)
    inp = {}
    inp['x'] = nrm(ks[0], (BATCH, SEQ, D), 1.0)
    inp['ab_norm'] = 1.0 + nrm(ks[1], (N_EVEN, D), 0.01)
    inp['ab_w_in'] = nrm(ks[2], (N_EVEN, D, AB_IN_WIDTH), D ** -0.5)
    inp['ab_conv_w'] = nrm(ks[3], (N_EVEN, CONV_K, CONV_WIDTH), CONV_K ** -0.5)
    inp['s5_lambda_re'] = -0.5 + nrm(ks[4], (N_EVEN, S5_GROUPS, S5_STATE), 0.01)
    inp['s5_lambda_im'] = math.pi * n + nrm(ks[5], (N_EVEN, S5_GROUPS, S5_STATE), 0.01)
    inp['s5_log_dt'] = jax.random.uniform(ks[6], (N_EVEN, S5_GROUPS), f32, math.log(DT_MIN), math.log(DT_MAX))
    inp['s5_b_re'] = nrm(ks[7], (N_EVEN, S5_GROUPS, S5_STATE, S5_GROUP), (2 * S5_GROUP) ** -0.5)
    inp['s5_b_im'] = nrm(ks[8], (N_EVEN, S5_GROUPS, S5_STATE, S5_GROUP), (2 * S5_GROUP) ** -0.5)
    inp['s5_c_re'] = nrm(ks[9], (N_EVEN, S5_GROUPS, S5_GROUP, S5_STATE), (2 * S5_STATE) ** -0.5)
    inp['s5_c_im'] = nrm(ks[10], (N_EVEN, S5_GROUPS, S5_GROUP, S5_STATE), (2 * S5_STATE) ** -0.5)
    inp['s5_d'] = nrm(ks[11], (N_EVEN, S5_WIDTH), 1.0)
    inp['s5_w_glu'] = nrm(ks[12], (N_EVEN, S5_WIDTH, S5_WIDTH), S5_WIDTH ** -0.5)
    inp['s5_b_glu'] = nrm(ks[13], (N_EVEN, S5_WIDTH), 0.01)
    inp['ab_w_out'] = nrm(ks[14], (N_EVEN, CONV_WIDTH + S5_WIDTH, D), (CONV_WIDTH + S5_WIDTH) ** -0.5)
    inp['c_norm'] = 1.0 + nrm(ks[15], (N_ODD, D), 0.01)
    inp['c_w_in'] = nrm(ks[16], (N_ODD, D, 3 * D + FOX_HEADS), D ** -0.5)
    inp['c_b_forget'] = jax.random.uniform(ks[17], (N_ODD, FOX_HEADS), f32, FORGET_BIAS_LO, FORGET_BIAS_HI)
    inp['c_w_out'] = nrm(ks[18], (N_ODD, D, D), D ** -0.5)
    inp['ffn_norm'] = 1.0 + nrm(ks[19], (DEPTH, D), 0.01)
    inp['router_w_group'] = nrm(ks[20], (DEPTH, D, N_GROUPS), D ** -0.5)
    inp['router_b_group'] = nrm(ks[21], (DEPTH, N_GROUPS), 0.01)
    inp['router_w_expert'] = nrm(ks[22], (DEPTH, D, N_GROUPS, EXPERTS_PER_GROUP), D ** -0.5)
    inp['router_b_expert'] = nrm(ks[23], (DEPTH, N_GROUPS, EXPERTS_PER_GROUP), 0.01)
    inp['moe_w_gate'] = nrm(ks[24], (DEPTH, N_GROUPS, EXPERTS_PER_GROUP, D, EXPERT_FF), D ** -0.5)
    inp['moe_w_up'] = nrm(ks[25], (DEPTH, N_GROUPS, EXPERTS_PER_GROUP, D, EXPERT_FF), D ** -0.5)
    inp['moe_w_down'] = nrm(ks[26], (DEPTH, N_GROUPS, EXPERTS_PER_GROUP, EXPERT_FF, D), EXPERT_FF ** -0.5)
    inp['final_norm'] = 1.0 + nrm(ks[27], (D,), 0.01)
    return inp


def reference(x, ab_norm, ab_w_in, ab_conv_w, s5_lambda_re, s5_lambda_im, s5_log_dt, s5_b_re, s5_b_im,
              s5_c_re, s5_c_im, s5_d, s5_w_glu, s5_b_glu, ab_w_out, c_norm, c_w_in, c_b_forget, c_w_out,
              ffn_norm, router_w_group, router_b_group, router_w_expert, router_b_expert,
              moe_w_gate, moe_w_up, moe_w_down, final_norm):
    bsz, L, D = x.shape
    cw = CONV_WIDTH
    for i in range(DEPTH):
        j = i // 2
        if i % 2 == 0:
            h = rmsnorm(x, ab_norm[j])
            proj = h @ ab_w_in[j]
            gate_b = proj[..., :cw]
            gate_c = proj[..., cw:2 * cw]
            u_conv = proj[..., 2 * cw:3 * cw]
            u_ssm = proj[..., 3 * cw:]
            y_conv = short_conv_mixer(gate_b, gate_c, u_conv, ab_conv_w[j])
            y_ssm = s5_mixer(u_ssm, s5_lambda_re[j], s5_lambda_im[j], s5_log_dt[j], s5_b_re[j], s5_b_im[j],
                             s5_c_re[j], s5_c_im[j], s5_d[j], s5_w_glu[j], s5_b_glu[j])
            x = x + jnp.concatenate([y_conv, y_ssm], axis=-1) @ ab_w_out[j]
        else:
            h = rmsnorm(x, c_norm[j])
            proj = h @ c_w_in[j]
            q = proj[..., :D].reshape(bsz, L, FOX_HEADS, FOX_HEAD_DIM)
            k = proj[..., D:2 * D].reshape(bsz, L, FOX_HEADS, FOX_HEAD_DIM)
            v = proj[..., 2 * D:3 * D].reshape(bsz, L, FOX_HEADS, FOX_HEAD_DIM)
            log_f = jax.nn.log_sigmoid(proj[..., 3 * D:].astype(jnp.float32) + c_b_forget[j].astype(jnp.float32))
            cum = jnp.cumsum(log_f, axis=1)
            x = x + forgetting_attention(q, k, v, cum) @ c_w_out[j]
        h = rmsnorm(x, ffn_norm[i])
        x = x + hierarchical_moe(h, router_w_group[i], router_b_group[i], router_w_expert[i],
                                 router_b_expert[i], moe_w_gate[i], moe_w_up[i], moe_w_down[i])
    return rmsnorm(x, final_norm)
```

```python
import functools
import math

import jax
import jax.numpy as jnp
from jax import lax
from jax.experimental import pallas as pl
from jax.experimental.pallas import tpu as pltpu

F32 = jnp.float32
BF16 = jnp.bfloat16
I32 = jnp.int32
U32 = jnp.uint32

RMS_EPS = 1e-6
LANES = 128
MIB = 1024 * 1024

CONV_K = 3
S5_GROUP = 16
S5_STATE = 64
S5_CHUNK = 16
S5_PAIR = 2
S5_SEGMENTS = 8
FOX_HEADS = 16
FOX_HEAD_DIM = 128
N_GROUPS = 4
EXPERTS_PER_GROUP = 8
N_EXPERTS = N_GROUPS * EXPERTS_PER_GROUP
EXPERT_LANE0 = N_GROUPS
MOE_BLOCK = 256


def _cparams(sem, vmem_mib):
    return pltpu.CompilerParams(dimension_semantics=sem, vmem_limit_bytes=vmem_mib * MIB)


def _rms(x, g):
    ms = jnp.mean(x * x, axis=-1, keepdims=True)
    return x * lax.rsqrt(ms + RMS_EPS) * g


def _norm_matmul_kernel(x_ref, g_ref, w_ref, o_ref, h_ref):
    @pl.when(pl.program_id(1) == 0)
    def _():
        h_ref[...] = _rms(x_ref[...], g_ref[...]).astype(BF16)

    o_ref[...] = jnp.dot(h_ref[...], w_ref[...], preferred_element_type=F32).astype(o_ref.dtype)


def norm_matmul(x, g, w, out_dtype, tm=1024, tn=1024):
    T, D = x.shape
    N = w.shape[1]
    tm, tn = min(tm, T), min(tn, N)
    return pl.pallas_call(
        _norm_matmul_kernel,
        grid=(T // tm, N // tn),
        in_specs=[pl.BlockSpec((tm, D), lambda i, j: (i, 0)),
                  pl.BlockSpec((1, D), lambda i, j: (0, 0)),
                  pl.BlockSpec((D, tn), lambda i, j: (0, j))],
        out_specs=pl.BlockSpec((tm, tn), lambda i, j: (i, j)),
        out_shape=jax.ShapeDtypeStruct((T, N), out_dtype),
        scratch_shapes=[pltpu.VMEM((tm, D), BF16)],
        compiler_params=_cparams(("parallel", "arbitrary"), 56),
        name="norm_matmul",
    )(x, g.reshape(1, D), w)


def _matmul_kernel(a_ref, w_ref, o_ref):
    o_ref[...] = jnp.dot(a_ref[...], w_ref[...], preferred_element_type=F32).astype(o_ref.dtype)


def matmul(a, w, out_dtype, tm=1024, tn=1024):
    T, K = a.shape
    N = w.shape[1]
    tm, tn = min(tm, T), min(tn, N)
    return pl.pallas_call(
        _matmul_kernel,
        grid=(T // tm, N // tn),
        in_specs=[pl.BlockSpec((tm, K), lambda i, j: (i, 0)),
                  pl.BlockSpec((K, tn), lambda i, j: (0, j))],
        out_specs=pl.BlockSpec((tm, tn), lambda i, j: (i, j)),
        out_shape=jax.ShapeDtypeStruct((T, N), out_dtype),
        compiler_params=_cparams(("parallel", "arbitrary"), 48),
        name="matmul",
    )(a, w)


def _conv_kernel(gb_ref, gc_ref, u_ref, gcp_ref, up_ref, w_ref, o_ref):
    w0, w1, w2 = w_ref[0:1, :], w_ref[1:2, :], w_ref[2:3, :]
    v = gc_ref[...] * u_ref[...]
    y = w2 * v + w1 * pltpu.roll(v, 1, 0) + w0 * pltpu.roll(v, 2, 0)
    o_ref[...] = (gb_ref[...] * y).astype(o_ref.dtype)
    vp = gcp_ref[...] * up_ref[...]
    vp = jnp.where(pl.program_id(0) > 0, vp, jnp.zeros_like(vp))
    v8 = v[0:8, :]
    row = lax.broadcasted_iota(I32, v8.shape, 0)
    v1 = jnp.where(row < 1, pltpu.roll(vp, 1, 0), pltpu.roll(v8, 1, 0))
    v2 = jnp.where(row < 2, pltpu.roll(vp, 2, 0), pltpu.roll(v8, 2, 0))
    o_ref[0:8, :] = (gb_ref[0:8, :] * (w2 * v8 + w1 * v1 + w0 * v2)).astype(o_ref.dtype)


def conv_mixer(proj, conv_w, tm=512):
    T = proj.shape[0]
    CW = conv_w.shape[1]
    tm = min(tm, T)
    r8 = tm // 8
    cur = lambda c: pl.BlockSpec((tm, CW), lambda i: (i, c))
    prev = lambda c: pl.BlockSpec((8, CW), lambda i: (jnp.maximum(i * r8 - 1, 0), c))
    return pl.pallas_call(
        _conv_kernel,
        grid=(T // tm,),
        in_specs=[cur(0), cur(1), cur(2), prev(1), prev(2),
                  pl.BlockSpec((CONV_K, CW), lambda i: (0, 0))],
        out_specs=pl.BlockSpec((tm, CW), lambda i: (i, 0)),
        out_shape=jax.ShapeDtypeStruct((T, CW), BF16),
        compiler_params=_cparams(("parallel",), 40),
        name="conv_mixer",
    )(proj, proj, proj, proj, proj, conv_w)


def _cmul(a, b):
    return a[0] * b[0] - a[1] * b[1], a[0] * b[1] + a[1] * b[0]


def s5_matrices(lam_re, lam_im, log_dt, b_re, b_im, c_re, c_im):
    C = S5_CHUNK
    dt = jnp.exp(log_dt)[:, None]
    a, b = lam_re * dt, lam_im * dt
    mag = jnp.exp(a)
    lbar = (mag * jnp.cos(b), mag * jnp.sin(b))
    den = lam_re * lam_re + lam_im * lam_im
    inv_lam = (lam_re / den, -lam_im / den)
    coef = _cmul((lbar[0] - 1.0, lbar[1]), inv_lam)
    bbar = _cmul((coef[0][..., None], coef[1][..., None]), (b_re, b_im))
    j = jnp.arange(C + 1, dtype=F32)[None, :, None]
    pmag = jnp.exp(a[:, None, :] * j)
    pw = (pmag * jnp.cos(b[:, None, :] * j), pmag * jnp.sin(b[:, None, :] * j))
    hi = lax.Precision.HIGHEST
    cp = _cmul((c_re[:, None], c_im[:, None]), (pw[0][:, :C, None, :], pw[1][:, :C, None, :]))
    kj = (jnp.einsum('gjhn,gnk->gjhk', cp[0], bbar[0], precision=hi)
          - jnp.einsum('gjhn,gnk->gjhk', cp[1], bbar[1], precision=hi))
    s_idx = jnp.arange(C)[:, None]
    t_idx = jnp.arange(C)[None, :]
    lag = t_idx - s_idx
    tm = kj[:, jnp.clip(lag, 0, C - 1)]
    tm = jnp.where((lag >= 0)[None, :, :, None, None], tm, 0.0)
    G = lam_re.shape[0]
    H = S5_GROUP
    tm = tm.transpose(0, 1, 4, 2, 3).reshape(G, C * H, C * H)
    prev = (pw[0][:, C - 1::-1][:, :C], pw[1][:, C - 1::-1][:, :C])
    wm = _cmul((prev[0][:, :, None, :], prev[1][:, :, None, :]),
               (bbar[0].transpose(0, 2, 1)[:, None], bbar[1].transpose(0, 2, 1)[:, None]))
    wm = (wm[0].reshape(G, C * H, -1), wm[1].reshape(G, C * H, -1))
    gq = _cmul((c_re[:, None], c_im[:, None]), (pw[0][:, 1:, None, :], pw[1][:, 1:, None, :]))
    v_re = gq[0].transpose(0, 3, 1, 2).reshape(G, -1, C * H)
    v_im = -gq[1].transpose(0, 3, 1, 2).reshape(G, -1, C * H)
    lam_c = (pw[0][:, C], pw[1][:, C])
    return tm, wm, (v_re, v_im), lam_c


def _pair_blockdiag(m):
    G, r, c = m.shape
    m = m.reshape(G // 2, 2, r, c)
    z = jnp.zeros((G // 2, r, c), m.dtype)
    top = jnp.concatenate([m[:, 0], z], axis=2)
    bot = jnp.concatenate([z, m[:, 1]], axis=2)
    return jnp.concatenate([top, bot], axis=1)


def _cpow_table(lr, li, n):
    tr, ti = jnp.ones_like(lr)[:, None], jnp.zeros_like(li)[:, None]
    cr, ci = lr, li
    size = 1
    while size < n:
        nr, ni = _cmul((tr, ti), (cr[:, None], ci[:, None]))
        tr, ti = jnp.concatenate([tr, nr], axis=1), jnp.concatenate([ti, ni], axis=1)
        cr, ci = _cmul((cr, ci), (cr, ci))
        size *= 2
    return (tr, ti), (cr, ci)


def _s5_kernel(u_ref, tm_ref, w_ref, v_ref, lam_ref, pw_ref, d_ref, o_ref, s_ref, x_ref):
    PB, NC = u_ref.shape[0], u_ref.shape[1]
    NS = NC // S5_SEGMENTS
    for p in range(PB):
        s_ref[p] = jnp.dot(u_ref[p].astype(BF16), w_ref[p], preferred_element_type=F32)
    lr = [lam_ref[p, 0:1, :] for p in range(PB)]
    li = [lam_ref[p, 1:2, :] for p in range(PB)]

    def step(i, carry):
        r0 = pl.multiple_of(i * S5_SEGMENTS, S5_SEGMENTS)
        out = []
        for p in range(PB):
            xr, xi = carry[p]
            x_ref[p, pl.ds(r0, S5_SEGMENTS), 0:LANES] = xr
            x_ref[p, pl.ds(r0, S5_SEGMENTS), LANES:2 * LANES] = xi
            sr = s_ref[p, pl.ds(r0, S5_SEGMENTS), 0:LANES]
            si = s_ref[p, pl.ds(r0, S5_SEGMENTS), LANES:2 * LANES]
            out.append((lr[p] * xr - li[p] * xi + sr, lr[p] * xi + li[p] * xr + si))
        return tuple(out)

    zero = jnp.zeros((S5_SEGMENTS, LANES), F32)
    ends = lax.fori_loop(0, NS, step, tuple((zero, zero) for _ in range(PB)))
    seg = lax.broadcasted_iota(I32, (S5_SEGMENTS, LANES), 0)
    for p in range(PB):
        er, ei = ends[p]
        sr, si = lam_ref[p, 2:3, :], lam_ref[p, 3:4, :]
        zr = zi = jnp.zeros((1, LANES), F32)
        z_re = z_im = zero
        for s in range(1, S5_SEGMENTS):
            zr, zi = (sr * zr - si * zi + er[s - 1:s, :], sr * zi + si * zr + ei[s - 1:s, :])
            z_re = jnp.where(seg == s, zr, z_re)
            z_im = jnp.where(seg == s, zi, z_im)
        x3 = x_ref[p].reshape(NS, S5_SEGMENTS, 2 * LANES)
        pr = pw_ref[p, :, 0:LANES][:, None, :]
        pi = pw_ref[p, :, LANES:2 * LANES][:, None, :]
        xr = (x3[:, :, 0:LANES] + pr * z_re[None] - pi * z_im[None]).reshape(NC, LANES)
        xi = (x3[:, :, LANES:2 * LANES] + pr * z_im[None] + pi * z_re[None]).reshape(NC, LANES)
        u = u_ref[p]
        y = (jnp.dot(u.astype(BF16), tm_ref[p], preferred_element_type=F32)
             + jnp.dot(xr.astype(BF16), v_ref[p, 0:LANES, :], preferred_element_type=F32)
             + jnp.dot(xi.astype(BF16), v_ref[p, LANES:2 * LANES, :], preferred_element_type=F32)
             + d_ref[p] * u)
        o_ref[p] = jax.nn.gelu(y)


def s5_mixer(u, lam_re, lam_im, log_dt, b_re, b_im, c_re, c_im, d, pairs_per_step=4):
    T, W = u.shape
    H, C = S5_GROUP, S5_CHUNK
    G = W // H
    P = G // S5_PAIR
    NC = T // C
    tm, wm, vm, lam_c = s5_matrices(lam_re, lam_im, log_dt, b_re, b_im, c_re, c_im)
    tm_p = _pair_blockdiag(tm).astype(BF16)
    w_p = jnp.concatenate([_pair_blockdiag(wm[0]), _pair_blockdiag(wm[1])], axis=2).astype(BF16)
    v_p = jnp.concatenate([_pair_blockdiag(vm[0]), _pair_blockdiag(vm[1])], axis=1).astype(BF16)
    NSEG = S5_SEGMENTS
    NS = NC // NSEG
    lc = (lam_c[0].reshape(P, -1), lam_c[1].reshape(P, -1))
    pw, lseg = _cpow_table(lc[0], lc[1], NS)
    lam_p = jnp.stack([lc[0], lc[1], lseg[0], lseg[1]], axis=1)
    pw_p = jnp.concatenate([pw[0], pw[1]], axis=2)
    d_p = jnp.broadcast_to(d.reshape(P, 2, 1, H), (P, 2, C, H)).reshape(P, 1, 2 * C * H)
    u_p = u.reshape(NSEG, NS, C, P, 2, H).transpose(3, 1, 0, 4, 2, 5).reshape(P, NC, 2 * C * H)
    PB = min(pairs_per_step, P)
    RW = 2 * C * H
    SW = 2 * LANES
    blk = lambda r, c: pl.BlockSpec((PB, r, c), lambda i: (i, 0, 0))
    y_p = pl.pallas_call(
        _s5_kernel,
        grid=(P // PB,),
        in_specs=[blk(NC, RW), blk(RW, RW), blk(RW, SW), blk(SW, RW), blk(4, LANES), blk(NS, SW),
                  blk(1, RW)],
        out_specs=blk(NC, RW),
        out_shape=jax.ShapeDtypeStruct((P, NC, RW), F32),
        scratch_shapes=[pltpu.VMEM((PB, NC, SW), F32), pltpu.VMEM((PB, NC, SW), F32)],
        compiler_params=_cparams(("parallel",), 56),
        name="s5_scan",
    )(u_p, tm_p, w_p, v_p, lam_p, pw_p, d_p)
    return y_p.reshape(P, NS, NSEG, 2, C, H).transpose(2, 1, 4, 0, 3, 5).reshape(T, W)


def _glu_out_kernel(x_ref, yc_ref, yg_ref, wglu_ref, bglu_ref, wout_ref, g_ref, xo_ref, ho_ref):
    yg = yg_ref[...]
    z = jnp.dot(yg.astype(BF16), wglu_ref[...], preferred_element_type=F32) + bglu_ref[...]
    ys = (yg * jax.nn.sigmoid(z)).astype(BF16)
    cw = yc_ref.shape[1]
    xn = (x_ref[...]
          + jnp.dot(yc_ref[...], wout_ref[0:cw, :], preferred_element_type=F32)
          + jnp.dot(ys, wout_ref[cw:, :], preferred_element_type=F32))
    xo_ref[...] = xn
    ho_ref[...] = _rms(xn, g_ref[...])


def glu_out(x, y_conv, y_gelu, w_glu, b_glu, w_out, g_next, tm=256):
    T, D = x.shape
    CW, SW = y_conv.shape[1], y_gelu.shape[1]
    tm = min(tm, T)
    row = lambda w: pl.BlockSpec((tm, w), lambda i: (i, 0))
    full = lambda a: pl.BlockSpec(a.shape, lambda i: (0,) * a.ndim)
    bg, g2 = b_glu.reshape(1, SW), g_next.reshape(1, D)
    return pl.pallas_call(
        _glu_out_kernel,
        grid=(T // tm,),
        in_specs=[row(D), row(CW), row(SW), full(w_glu), full(bg), full(w_out), full(g2)],
        out_specs=[row(D), row(D)],
        out_shape=[jax.ShapeDtypeStruct((T, D), F32), jax.ShapeDtypeStruct((T, D), F32)],
        compiler_params=_cparams(("parallel",), 56),
        name="glu_out",
    )(x, y_conv, y_gelu, w_glu, bg, w_out, g2)


def _proj_residual_kernel(x_ref, a_ref, w_ref, g_ref, xo_ref, ho_ref):
    xn = x_ref[...] + jnp.dot(a_ref[...], w_ref[...], preferred_element_type=F32)
    xo_ref[...] = xn
    ho_ref[...] = _rms(xn, g_ref[...])


def proj_residual(x, a, w, g_next, tm=256):
    T, D = x.shape
    K = a.shape[1]
    tm = min(tm, T)
    row = lambda w_: pl.BlockSpec((tm, w_), lambda i: (i, 0))
    full = lambda arr: pl.BlockSpec(arr.shape, lambda i: (0,) * arr.ndim)
    g2 = g_next.reshape(1, D)
    return pl.pallas_call(
        _proj_residual_kernel,
        grid=(T // tm,),
        in_specs=[row(D), row(K), full(w), full(g2)],
        out_specs=[row(D), row(D)],
        out_shape=[jax.ShapeDtypeStruct((T, D), F32), jax.ShapeDtypeStruct((T, D), F32)],
        compiler_params=_cparams(("parallel",), 56),
        name="proj_residual",
    )(x, a, w, g2)


def _router_kernel(h_ref, w_ref, b_ref, mi_ref, mf_ref, cnt_ref, run_ref):
    i = pl.program_id(0)

    @pl.when(i == 0)
    def _():
        run_ref[...] = jnp.zeros_like(run_ref)

    h = h_ref[...]
    tm, D = h.shape
    logits = jnp.dot(h, w_ref[...], precision=lax.Precision.HIGHEST,
                     preferred_element_type=F32) + b_ref[...]
    lane = lax.broadcasted_iota(I32, logits.shape, 1)
    neg = jnp.float32(-jnp.inf)
    gl = jnp.where(lane < N_GROUPS, logits, neg)
    gmax = jnp.max(gl, axis=1, keepdims=True)
    gsum = jnp.sum(jnp.where(lane < N_GROUPS, jnp.exp(gl - gmax), 0.0), axis=1, keepdims=True)
    gw = 1.0 / gsum
    gidx = jnp.min(jnp.where(gl == gmax, lane, LANES), axis=1, keepdims=True)
    lo = EXPERT_LANE0 + EXPERTS_PER_GROUP * gidx
    el = jnp.where((lane >= lo) & (lane < lo + EXPERTS_PER_GROUP), logits, neg)
    v1 = jnp.max(el, axis=1, keepdims=True)
    i1 = jnp.min(jnp.where(el == v1, lane, LANES), axis=1, keepdims=True)
    el2 = jnp.where(lane == i1, neg, el)
    v2 = jnp.max(el2, axis=1, keepdims=True)
    i2 = jnp.min(jnp.where(el2 == v2, lane, LANES), axis=1, keepdims=True)
    t = jnp.exp(v2 - v1)
    w1 = gw / (1.0 + t)
    w2 = gw * t / (1.0 + t)
    hit1 = lane == i1
    hit2 = lane == i2
    cnt = (hit1 | hit2).astype(BF16)
    r = lax.broadcasted_iota(I32, (tm, tm), 0)
    c = lax.broadcasted_iota(I32, (tm, tm), 1)
    before = (c < r).astype(BF16)
    cum = jnp.dot(before, cnt, preferred_element_type=F32) + run_ref[0:1, :]
    rank1 = jnp.sum(jnp.where(hit1, cum, 0.0), axis=1, keepdims=True).astype(I32)
    rank2 = jnp.sum(jnp.where(hit2, cum, 0.0), axis=1, keepdims=True).astype(I32)
    run = run_ref[0:1, :] + jnp.sum(cnt.astype(F32), axis=0, keepdims=True)
    run_ref[...] = jnp.broadcast_to(run, run_ref.shape)
    cnt_ref[...] = jnp.broadcast_to(run, cnt_ref.shape)
    e1 = i1 - EXPERT_LANE0
    e2 = i2 - EXPERT_LANE0
    mi_ref[...] = jnp.where(lane == 0, e1, jnp.where(lane == 1, e2,
                            jnp.where(lane == 2, rank1, jnp.where(lane == 3, rank2, 0))))
    mf_ref[...] = jnp.where(lane == 0, w1, jnp.where(lane == 1, w2, 0.0))


def router(h, w_r, b_r, tm=256):
    T, D = h.shape
    tm = min(tm, T)
    row = lambda w: pl.BlockSpec((tm, w), lambda i: (i, 0))
    full = lambda a: pl.BlockSpec(a.shape, lambda i: (0,) * a.ndim)
    return pl.pallas_call(
        _router_kernel,
        grid=(T // tm,),
        in_specs=[row(D), full(w_r), full(b_r)],
        out_specs=[row(LANES), row(LANES), pl.BlockSpec((8, LANES), lambda i: (0, 0))],
        out_shape=[jax.ShapeDtypeStruct((T, LANES), I32), jax.ShapeDtypeStruct((T, LANES), F32),
                   jax.ShapeDtypeStruct((8, LANES), F32)],
        scratch_shapes=[pltpu.VMEM((8, LANES), F32)],
        compiler_params=_cparams(("arbitrary",), 40),
        name="moe_router",
    )(h, w_r, b_r)


def moe_plan(meta_i, counts, n_blocks):
    B = MOE_BLOCK
    e1, e2, r1, r2 = meta_i[:, 0], meta_i[:, 1], meta_i[:, 2], meta_i[:, 3]
    cnt = counts[0, EXPERT_LANE0:EXPERT_LANE0 + N_EXPERTS].astype(I32)
    nblk = (cnt + B - 1) // B
    blk_end = jnp.cumsum(nblk)
    blk_off = blk_end - nblk
    pos1 = blk_off[e1] * B + r1
    pos2 = blk_off[e2] * B + r2
    b = jnp.arange(n_blocks, dtype=I32)
    total = blk_end[-1]
    owner = jnp.minimum(jnp.sum((blk_end[None, :] <= b[:, None]).astype(I32), axis=1), N_EXPERTS - 1)
    valid = b < total
    last_owner = owner[jnp.maximum(total - 1, 0)]
    blk_expert = jnp.where(valid, owner, last_owner)
    blk_rows = jnp.where(valid, jnp.clip(cnt[owner] - (b - blk_off[owner]) * B, 0, B), 0)
    blk_first = (valid & (b == blk_off[owner])).astype(I32)
    return pos1, pos2, blk_expert, blk_rows, blk_first


def _row_copy(src, dst, sem):
    return pltpu.make_async_copy(src, dst, sem)


def _dispatch_kernel(pos1_ref, pos2_ref, hp_ref, xs_in_ref, xs_ref, sem):
    del xs_in_ref
    tm = hp_ref.shape[0]
    base = pl.program_id(0) * tm

    def copies(r):
        src = hp_ref.at[pl.ds(r, 1)]
        return (_row_copy(src, xs_ref.at[pl.ds(pos1_ref[base + r], 1)], sem),
                _row_copy(src, xs_ref.at[pl.ds(pos2_ref[base + r], 1)], sem))

    def start(r, carry):
        for cp in copies(r):
            cp.start()
        return carry

    def wait(r, carry):
        for cp in copies(r):
            cp.wait()
        return carry

    lax.fori_loop(0, tm, start, 0)
    lax.fori_loop(0, tm, wait, 0)


def dispatch(hp, pos1, pos2, xs0, tm=256):
    T, W = hp.shape
    n_rows = xs0.shape[0]
    tm = min(tm, T)
    return pl.pallas_call(
        _dispatch_kernel,
        grid_spec=pltpu.PrefetchScalarGridSpec(
            num_scalar_prefetch=2,
            grid=(T // tm,),
            in_specs=[pl.BlockSpec((tm, W), lambda i, p1, p2: (i, 0)),
                      pl.BlockSpec(memory_space=pl.ANY)],
            out_specs=pl.BlockSpec(memory_space=pl.ANY),
            scratch_shapes=[pltpu.SemaphoreType.DMA(())]),
        out_shape=jax.ShapeDtypeStruct((n_rows, W), hp.dtype),
        input_output_aliases={3: 0},
        compiler_params=_cparams(("arbitrary",), 32),
        name="moe_dispatch",
    )(pos1, pos2, hp, xs0)


def _experts_kernel(be_ref, br_ref, bf_ref, xs_ref, wg_ref, wu_ref, wd_ref, ys_ref, wg_s, wu_s, wd_s):
    del be_ref
    b = pl.program_id(0)

    @pl.when(br_ref[b] > 0)
    def _():
        @pl.when(bf_ref[b] == 1)
        def _():
            wg_s[...] = wg_ref[0].astype(BF16)
            wu_s[...] = wu_ref[0].astype(BF16)
            wd_s[...] = wd_ref[0].astype(BF16)

        a = xs_ref[...].astype(BF16)
        gate = jnp.dot(a, wg_s[...], preferred_element_type=F32)
        up = jnp.dot(a, wu_s[...], preferred_element_type=F32)
        mid = (jax.nn.silu(gate) * up).astype(BF16)
        ys_ref[...] = jnp.dot(mid, wd_s[...], preferred_element_type=F32)

    @pl.when(br_ref[b] == 0)
    def _():
        ys_ref[...] = jnp.zeros_like(ys_ref)


def experts(xs, blk_expert, blk_rows, blk_first, w_gate, w_up, w_down):
    B = MOE_BLOCK
    n_rows = xs.shape[0]
    E, D, FF = w_gate.shape
    nb = n_rows // B
    return pl.pallas_call(
        _experts_kernel,
        grid_spec=pltpu.PrefetchScalarGridSpec(
            num_scalar_prefetch=3,
            grid=(nb,),
            in_specs=[pl.BlockSpec((B, D), lambda b, be, br, bf: (b, 0)),
                      pl.BlockSpec((1, D, FF), lambda b, be, br, bf: (be[b], 0, 0)),
                      pl.BlockSpec((1, D, FF), lambda b, be, br, bf: (be[b], 0, 0)),
                      pl.BlockSpec((1, FF, D), lambda b, be, br, bf: (be[b], 0, 0))],
            out_specs=pl.BlockSpec((B, D), lambda b, be, br, bf: (b, 0)),
            scratch_shapes=[pltpu.VMEM((D, FF), BF16), pltpu.VMEM((D, FF), BF16),
                            pltpu.VMEM((FF, D), BF16)]),
        out_shape=jax.ShapeDtypeStruct((n_rows, D), F32),
        compiler_params=_cparams(("arbitrary",), 56),
        name="moe_experts",
    )(blk_expert, blk_rows, blk_first, xs, w_gate, w_up, w_down)


def _combine_kernel(pos1_ref, pos2_ref, x_ref, mf_ref, g_ref, ys_ref, xo_ref, ho_ref, buf, sem):
    tm = x_ref.shape[0]
    base = pl.program_id(0) * tm

    def copies(r):
        dst = pl.ds(r, 1)
        return (_row_copy(ys_ref.at[pl.ds(pos1_ref[base + r], 1)], buf.at[0, dst], sem),
                _row_copy(ys_ref.at[pl.ds(pos2_ref[base + r], 1)], buf.at[1, dst], sem))

    def start(r, carry):
        for cp in copies(r):
            cp.start()
        return carry

    def wait(r, carry):
        for cp in copies(r):
            cp.wait()
        return carry

    lax.fori_loop(0, tm, start, 0)
    lax.fori_loop(0, tm, wait, 0)
    mf = mf_ref[...]
    xn = x_ref[...] + mf[:, 0:1] * buf[0] + mf[:, 1:2] * buf[1]
    xo_ref[...] = xn
    ho_ref[...] = _rms(xn, g_ref[...]).astype(ho_ref.dtype)


def combine(x, ys, meta_f, pos1, pos2, g_next, h_dtype, tm=256):
    T, D = x.shape
    tm = min(tm, T)
    row = lambda w: pl.BlockSpec((tm, w), lambda i, p1, p2: (i, 0))
    return pl.pallas_call(
        _combine_kernel,
        grid_spec=pltpu.PrefetchScalarGridSpec(
            num_scalar_prefetch=2,
            grid=(T // tm,),
            in_specs=[row(D), row(LANES), pl.BlockSpec((1, D), lambda i, p1, p2: (0, 0)),
                      pl.BlockSpec(memory_space=pl.ANY)],
            out_specs=[row(D), row(D)],
            scratch_shapes=[pltpu.VMEM((2, tm, D), F32), pltpu.SemaphoreType.DMA(())]),
        out_shape=[jax.ShapeDtypeStruct((T, D), F32), jax.ShapeDtypeStruct((T, D), h_dtype)],
        compiler_params=_cparams(("arbitrary",), 40),
        name="moe_combine",
    )(pos1, pos2, x, meta_f, g_next.reshape(1, D), ys)


def moe_row_blocks(T):
    return (2 * T) // MOE_BLOCK + N_EXPERTS


def moe_layer(x, h, xs_buf, w_group, b_group, w_expert, b_expert, w_gate, w_up, w_down, g_next, h_dtype):
    T, D = x.shape
    n_blocks = xs_buf.shape[0] // MOE_BLOCK
    pad = LANES - N_GROUPS - N_EXPERTS
    w_r = jnp.concatenate([w_group, w_expert.reshape(D, N_EXPERTS), jnp.zeros((D, pad), F32)], axis=1)
    b_r = jnp.concatenate([b_group, b_expert.reshape(N_EXPERTS), jnp.zeros((pad,), F32)]).reshape(1, LANES)
    meta_i, meta_f, counts = router(h, w_r, b_r)
    pos1, pos2, blk_expert, blk_rows, blk_first = moe_plan(meta_i, counts, n_blocks)
    xs = dispatch(h, pos1, pos2, xs_buf)
    FF = w_gate.shape[-1]
    ys = experts(xs, blk_expert, blk_rows, blk_first, w_gate.reshape(N_EXPERTS, D, FF),
                 w_up.reshape(N_EXPERTS, D, FF), w_down.reshape(N_EXPERTS, FF, D))
    xo, ho = combine(x, ys, meta_f, pos1, pos2, g_next, h_dtype)
    return xo, ho, xs


def _forget_kernel(h_ref, w_ref, b_ref, o_ref, run_ref):
    @pl.when(pl.program_id(0) == 0)
    def _():
        run_ref[...] = jnp.zeros_like(run_ref)

    hi = lax.Precision.HIGHEST
    z = jnp.dot(h_ref[...].astype(F32), w_ref[...], precision=hi, preferred_element_type=F32) + b_ref[...]
    log_f = jax.nn.log_sigmoid(z)
    tm = z.shape[0]
    r = lax.broadcasted_iota(I32, (tm, tm), 0)
    c = lax.broadcasted_iota(I32, (tm, tm), 1)
    upto = (c <= r).astype(F32)
    cum = jnp.dot(upto, log_f, precision=hi, preferred_element_type=F32) + run_ref[0:1, :]
    o_ref[...] = cum
    run_ref[...] = jnp.broadcast_to(cum[tm - 1:tm, :], run_ref.shape)


def forget_cumsum(h, w_f, b_f, tm=256):
    T, D = h.shape
    tm = min(tm, T)
    return pl.pallas_call(
        _forget_kernel,
        grid=(T // tm,),
        in_specs=[pl.BlockSpec((tm, D), lambda i: (i, 0)),
                  pl.BlockSpec((D, LANES), lambda i: (0, 0)),
                  pl.BlockSpec((1, LANES), lambda i: (0, 0))],
        out_specs=pl.BlockSpec((tm, LANES), lambda i: (i, 0)),
        out_shape=jax.ShapeDtypeStruct((T, LANES), F32),
        scratch_shapes=[pltpu.VMEM((8, LANES), F32)],
        compiler_params=_cparams(("arbitrary",), 40),
        name="forget_cumsum",
    )(h, w_f, b_f)


def _flash_kernel(q_ref, k_ref, v_ref, ck_ref, cq_ref, o_ref, m_ref, l_ref, acc_ref, *, scale):
    qi = pl.program_id(1)
    bq = q_ref.shape[0]
    q = q_ref[...]
    c0 = cq_ref[0, :, 0:1]
    m_ref[...] = jnp.full_like(m_ref, -jnp.inf)
    l_ref[...] = jnp.zeros_like(l_ref)
    acc_ref[...] = jnp.zeros_like(acc_ref)

    def chunk(start, diagonal):
        k = k_ref[pl.ds(start, bq), :]
        v = v_ref[pl.ds(start, bq), :]
        s = lax.dot_general(q, k, (((1,), (1,)), ((), ())), preferred_element_type=F32)
        s = s * scale + (c0 - ck_ref[0, :, pl.ds(start, bq)])
        if diagonal:
            row = lax.broadcasted_iota(I32, s.shape, 0)
            col = lax.broadcasted_iota(I32, s.shape, 1)
            s = jnp.where(col <= row, s, -jnp.inf)
        m_prev = m_ref[...]
        m_new = jnp.maximum(m_prev, jnp.max(s, axis=1, keepdims=True))
        p = jnp.exp(s - m_new)
        alpha = jnp.exp(m_prev - m_new)
        l_ref[...] = alpha * l_ref[...] + jnp.sum(p, axis=1, keepdims=True)
        acc_ref[...] = alpha * acc_ref[...] + jnp.dot(p.astype(BF16), v, preferred_element_type=F32)
        m_ref[...] = m_new

    def body(j, carry):
        chunk(pl.multiple_of(j * bq, bq), False)
        return carry

    lax.fori_loop(0, qi, body, 0)
    chunk(pl.multiple_of(qi * bq, bq), True)
    o_ref[...] = (acc_ref[...] / l_ref[...]).astype(o_ref.dtype)


def flash_attention(qkv, cum_t, bq=512):
    T = qkv.shape[0]
    H, dh = FOX_HEADS, FOX_HEAD_DIM
    bq = min(bq, T)
    kern = functools.partial(_flash_kernel, scale=dh ** -0.5)
    return pl.pallas_call(
        kern,
        grid=(H, T // bq),
        in_specs=[pl.BlockSpec((bq, dh), lambda h, i: (i, h)),
                  pl.BlockSpec((T, dh), lambda h, i: (0, H + h)),
                  pl.BlockSpec((T, dh), lambda h, i: (0, 2 * H + h)),
                  pl.BlockSpec((1, 1, T), lambda h, i: (h, 0, 0)),
                  pl.BlockSpec((1, 1, bq), lambda h, i: (h, 0, i))],
        out_specs=pl.BlockSpec((bq, dh), lambda h, i: (i, h)),
        out_shape=jax.ShapeDtypeStruct((T, H * dh), BF16),
        scratch_shapes=[pltpu.VMEM((bq, 1), F32), pltpu.VMEM((bq, 1), F32), pltpu.VMEM((bq, dh), F32)],
        compiler_params=_cparams(("parallel", "arbitrary"), 48),
        name="fox_attention",
    )(qkv, qkv, qkv, cum_t, cum_t)


def kernel(x, ab_norm, ab_w_in, ab_conv_w, s5_lambda_re, s5_lambda_im, s5_log_dt, s5_b_re, s5_b_im,
           s5_c_re, s5_c_im, s5_d, s5_w_glu, s5_b_glu, ab_w_out, c_norm, c_w_in, c_b_forget, c_w_out,
           ffn_norm, router_w_group, router_b_group, router_w_expert, router_b_expert,
           moe_w_gate, moe_w_up, moe_w_down, final_norm):
    bsz, L, D = x.shape
    depth = ffn_norm.shape[0]
    cw = ab_conv_w.shape[-1]
    xt = x.reshape(bsz * L, D)
    h = None
    xs_buf = jnp.zeros((moe_row_blocks(bsz * L) * MOE_BLOCK, D), F32)
    for i in range(depth):
        j = i // 2
        if i % 2 == 0:
            g_in = ab_norm[j]
            if h is not None:
                proj = matmul(h, ab_w_in[j].astype(BF16), F32)
            else:
                proj = norm_matmul(xt, g_in, ab_w_in[j].astype(BF16), F32)
            y_conv = conv_mixer(proj, ab_conv_w[j])
            y_gelu = s5_mixer(proj[:, 3 * cw:], s5_lambda_re[j], s5_lambda_im[j], s5_log_dt[j],
                              s5_b_re[j], s5_b_im[j], s5_c_re[j], s5_c_im[j], s5_d[j])
            xt, hf = glu_out(xt, y_conv, y_gelu, s5_w_glu[j].astype(BF16), s5_b_glu[j],
                             ab_w_out[j].astype(BF16), ffn_norm[i])
        else:
            assert h is not None, "an attention layer always follows a MoE combine that emits its norm"
            hd = FOX_HEADS * FOX_HEAD_DIM
            qkv = matmul(h, c_w_in[j][:, :3 * hd].astype(BF16), BF16)
            w_f = jnp.pad(c_w_in[j][:, 3 * hd:], ((0, 0), (0, LANES - FOX_HEADS)))
            b_f = jnp.pad(c_b_forget[j], (0, LANES - FOX_HEADS)).reshape(1, LANES)
            cum = forget_cumsum(h, w_f, b_f)
            cum_t = cum[:, :FOX_HEADS].T.reshape(FOX_HEADS, 1, bsz * L)
            att = flash_attention(qkv, cum_t)
            xt, hf = proj_residual(xt, att, c_w_out[j].astype(BF16), ffn_norm[i])
        last = i == depth - 1
        if last:
            g_next = final_norm
        elif (i + 1) % 2 == 0:
            g_next = ab_norm[(i + 1) // 2]
        else:
            g_next = c_norm[(i + 1) // 2]
        xt, h, xs_buf = moe_layer(xt, hf, xs_buf, router_w_group[i], router_b_group[i], router_w_expert[i],
                                  router_b_expert[i], moe_w_gate[i], moe_w_up[i], moe_w_down[i],
                                  g_next, F32 if last else BF16)
    return h.reshape(bsz, L, D)
```

```python
import functools
import math

import jax
import jax.numpy as jnp
from jax import lax
from jax.experimental import pallas as pl
from jax.experimental.pallas import tpu as pltpu

F32 = jnp.float32
BF16 = jnp.bfloat16
I32 = jnp.int32
U32 = jnp.uint32

RMS_EPS = 1e-6
LANES = 128
LOG2E = math.log2(math.e)
MIB = 1024 * 1024

CONV_K = 3
S5_GROUP = 16
S5_STATE = 64
S5_CHUNK = 16
S5_PAIR = 2
S5_SEGMENTS = 8
FOX_HEADS = 16
FOX_HEAD_DIM = 128
N_GROUPS = 4
EXPERTS_PER_GROUP = 8
N_EXPERTS = N_GROUPS * EXPERTS_PER_GROUP
EXPERT_LANE0 = N_GROUPS
MOE_BLOCK = 256


def _cparams(sem, vmem_mib):
    return pltpu.CompilerParams(dimension_semantics=sem, vmem_limit_bytes=vmem_mib * MIB)


def _rms(x, g):
    ms = jnp.mean(x * x, axis=-1, keepdims=True)
    return x * lax.rsqrt(ms + RMS_EPS) * g


def _norm_matmul_kernel(x_ref, g_ref, w_ref, o_ref, h_ref):
    @pl.when(pl.program_id(1) == 0)
    def _():
        h_ref[...] = _rms(x_ref[...], g_ref[...]).astype(BF16)

    o_ref[...] = jnp.dot(h_ref[...], w_ref[...], preferred_element_type=F32).astype(o_ref.dtype)


def norm_matmul(x, g, w, out_dtype, tm=1024, tn=1024):
    T, D = x.shape
    N = w.shape[1]
    tm, tn = min(tm, T), min(tn, N)
    return pl.pallas_call(
        _norm_matmul_kernel,
        grid=(T // tm, N // tn),
        in_specs=[pl.BlockSpec((tm, D), lambda i, j: (i, 0)),
                  pl.BlockSpec((1, D), lambda i, j: (0, 0)),
                  pl.BlockSpec((D, tn), lambda i, j: (0, j))],
        out_specs=pl.BlockSpec((tm, tn), lambda i, j: (i, j)),
        out_shape=jax.ShapeDtypeStruct((T, N), out_dtype),
        scratch_shapes=[pltpu.VMEM((tm, D), BF16)],
        compiler_params=_cparams(("parallel", "arbitrary"), 56),
        name="norm_matmul",
    )(x, g.reshape(1, D), w)


def _matmul_kernel(a_ref, w_ref, o_ref):
    o_ref[...] = jnp.dot(a_ref[...], w_ref[...], preferred_element_type=F32).astype(o_ref.dtype)


def matmul(a, w, out_dtype, tm=1024, tn=1024):
    T, K = a.shape
    N = w.shape[1]
    tm, tn = min(tm, T), min(tn, N)
    return pl.pallas_call(
        _matmul_kernel,
        grid=(T // tm, N // tn),
        in_specs=[pl.BlockSpec((tm, K), lambda i, j: (i, 0)),
                  pl.BlockSpec((K, tn), lambda i, j: (0, j))],
        out_specs=pl.BlockSpec((tm, tn), lambda i, j: (i, j)),
        out_shape=jax.ShapeDtypeStruct((T, N), out_dtype),
        compiler_params=_cparams(("parallel", "arbitrary"), 48),
        name="matmul",
    )(a, w)


def _conv_kernel(gb_ref, gc_ref, u_ref, gcp_ref, up_ref, w_ref, o_ref):
    w0, w1, w2 = w_ref[0:1, :], w_ref[1:2, :], w_ref[2:3, :]
    v = gc_ref[...] * u_ref[...]
    y = w2 * v + w1 * pltpu.roll(v, 1, 0) + w0 * pltpu.roll(v, 2, 0)
    o_ref[...] = (gb_ref[...] * y).astype(o_ref.dtype)
    vp = gcp_ref[...] * up_ref[...]
    vp = jnp.where(pl.program_id(0) > 0, vp, jnp.zeros_like(vp))
    v8 = v[0:8, :]
    row = lax.broadcasted_iota(I32, v8.shape, 0)
    v1 = jnp.where(row < 1, pltpu.roll(vp, 1, 0), pltpu.roll(v8, 1, 0))
    v2 = jnp.where(row < 2, pltpu.roll(vp, 2, 0), pltpu.roll(v8, 2, 0))
    o_ref[0:8, :] = (gb_ref[0:8, :] * (w2 * v8 + w1 * v1 + w0 * v2)).astype(o_ref.dtype)


def conv_mixer(proj, conv_w, tm=512):
    T = proj.shape[0]
    CW = conv_w.shape[1]
    tm = min(tm, T)
    r8 = tm // 8
    cur = lambda c: pl.BlockSpec((tm, CW), lambda i: (i, c))
    prev = lambda c: pl.BlockSpec((8, CW), lambda i: (jnp.maximum(i * r8 - 1, 0), c))
    return pl.pallas_call(
        _conv_kernel,
        grid=(T // tm,),
        in_specs=[cur(0), cur(1), cur(2), prev(1), prev(2),
                  pl.BlockSpec((CONV_K, CW), lambda i: (0, 0))],
        out_specs=pl.BlockSpec((tm, CW), lambda i: (i, 0)),
        out_shape=jax.ShapeDtypeStruct((T, CW), BF16),
        compiler_params=_cparams(("parallel",), 40),
        name="conv_mixer",
    )(proj, proj, proj, proj, proj, conv_w)


def _cmul(a, b):
    return a[0] * b[0] - a[1] * b[1], a[0] * b[1] + a[1] * b[0]


def s5_matrices(lam_re, lam_im, log_dt, b_re, b_im, c_re, c_im):
    C = S5_CHUNK
    dt = jnp.exp(log_dt)[:, None]
    a, b = lam_re * dt, lam_im * dt
    mag = jnp.exp(a)
    lbar = (mag * jnp.cos(b), mag * jnp.sin(b))
    den = lam_re * lam_re + lam_im * lam_im
    inv_lam = (lam_re / den, -lam_im / den)
    coef = _cmul((lbar[0] - 1.0, lbar[1]), inv_lam)
    bbar = _cmul((coef[0][..., None], coef[1][..., None]), (b_re, b_im))
    j = jnp.arange(C + 1, dtype=F32)[None, :, None]
    pmag = jnp.exp(a[:, None, :] * j)
    pw = (pmag * jnp.cos(b[:, None, :] * j), pmag * jnp.sin(b[:, None, :] * j))
    hi = lax.Precision.HIGHEST
    cp = _cmul((c_re[:, None], c_im[:, None]), (pw[0][:, :C, None, :], pw[1][:, :C, None, :]))
    kj = (jnp.einsum('gjhn,gnk->gjhk', cp[0], bbar[0], precision=hi)
          - jnp.einsum('gjhn,gnk->gjhk', cp[1], bbar[1], precision=hi))
    s_idx = jnp.arange(C)[:, None]
    t_idx = jnp.arange(C)[None, :]
    lag = t_idx - s_idx
    tm = kj[:, jnp.clip(lag, 0, C - 1)]
    tm = jnp.where((lag >= 0)[None, :, :, None, None], tm, 0.0)
    G = lam_re.shape[0]
    H = S5_GROUP
    tm = tm.transpose(0, 1, 4, 2, 3).reshape(G, C * H, C * H)
    prev = (pw[0][:, C - 1::-1][:, :C], pw[1][:, C - 1::-1][:, :C])
    wm = _cmul((prev[0][:, :, None, :], prev[1][:, :, None, :]),
               (bbar[0].transpose(0, 2, 1)[:, None], bbar[1].transpose(0, 2, 1)[:, None]))
    wm = (wm[0].reshape(G, C * H, -1), wm[1].reshape(G, C * H, -1))
    gq = _cmul((c_re[:, None], c_im[:, None]), (pw[0][:, 1:, None, :], pw[1][:, 1:, None, :]))
    v_re = gq[0].transpose(0, 3, 1, 2).reshape(G, -1, C * H)
    v_im = -gq[1].transpose(0, 3, 1, 2).reshape(G, -1, C * H)
    lam_c = (pw[0][:, C], pw[1][:, C])
    return tm, wm, (v_re, v_im), lam_c


def _pair_blockdiag(m):
    G, r, c = m.shape
    m = m.reshape(G // 2, 2, r, c)
    z = jnp.zeros((G // 2, r, c), m.dtype)
    top = jnp.concatenate([m[:, 0], z], axis=2)
    bot = jnp.concatenate([z, m[:, 1]], axis=2)
    return jnp.concatenate([top, bot], axis=1)


def _cpow_table(lr, li, n):
    tr, ti = jnp.ones_like(lr)[:, None], jnp.zeros_like(li)[:, None]
    cr, ci = lr, li
    size = 1
    while size < n:
        nr, ni = _cmul((tr, ti), (cr[:, None], ci[:, None]))
        tr, ti = jnp.concatenate([tr, nr], axis=1), jnp.concatenate([ti, ni], axis=1)
        cr, ci = _cmul((cr, ci), (cr, ci))
        size *= 2
    return (tr, ti), (cr, ci)


def _s5_kernel(u_ref, tm_ref, w_ref, v_ref, lam_ref, pw_ref, d_ref, o_ref, s_ref, x_ref):
    PB, NC = u_ref.shape[0], u_ref.shape[1]
    NS = NC // S5_SEGMENTS
    for p in range(PB):
        s_ref[p] = jnp.dot(u_ref[p].astype(BF16), w_ref[p], preferred_element_type=F32)
    lr = [lam_ref[p, 0:1, :] for p in range(PB)]
    li = [lam_ref[p, 1:2, :] for p in range(PB)]

    def step(i, carry):
        r0 = pl.multiple_of(i * S5_SEGMENTS, S5_SEGMENTS)
        out = []
        for p in range(PB):
            xr, xi = carry[p]
            x_ref[p, pl.ds(r0, S5_SEGMENTS), 0:LANES] = xr
            x_ref[p, pl.ds(r0, S5_SEGMENTS), LANES:2 * LANES] = xi
            sr = s_ref[p, pl.ds(r0, S5_SEGMENTS), 0:LANES]
            si = s_ref[p, pl.ds(r0, S5_SEGMENTS), LANES:2 * LANES]
            out.append((lr[p] * xr - li[p] * xi + sr, lr[p] * xi + li[p] * xr + si))
        return tuple(out)

    zero = jnp.zeros((S5_SEGMENTS, LANES), F32)
    ends = lax.fori_loop(0, NS, step, tuple((zero, zero) for _ in range(PB)))
    seg = lax.broadcasted_iota(I32, (S5_SEGMENTS, LANES), 0)
    for p in range(PB):
        er, ei = ends[p]
        sr, si = lam_ref[p, 2:3, :], lam_ref[p, 3:4, :]
        zr = zi = jnp.zeros((1, LANES), F32)
        z_re = z_im = zero
        for s in range(1, S5_SEGMENTS):
            zr, zi = (sr * zr - si * zi + er[s - 1:s, :], sr * zi + si * zr + ei[s - 1:s, :])
            z_re = jnp.where(seg == s, zr, z_re)
            z_im = jnp.where(seg == s, zi, z_im)
        x3 = x_ref[p].reshape(NS, S5_SEGMENTS, 2 * LANES)
        pr = pw_ref[p, :, 0:LANES][:, None, :]
        pi = pw_ref[p, :, LANES:2 * LANES][:, None, :]
        xr = (x3[:, :, 0:LANES] + pr * z_re[None] - pi * z_im[None]).reshape(NC, LANES)
        xi = (x3[:, :, LANES:2 * LANES] + pr * z_im[None] + pi * z_re[None]).reshape(NC, LANES)
        u = u_ref[p]
        y = (jnp.dot(u.astype(BF16), tm_ref[p], preferred_element_type=F32)
             + jnp.dot(xr.astype(BF16), v_ref[p, 0:LANES, :], preferred_element_type=F32)
             + jnp.dot(xi.astype(BF16), v_ref[p, LANES:2 * LANES, :], preferred_element_type=F32)
             + d_ref[p] * u)
        o_ref[p] = jax.nn.gelu(y)


def s5_mixer(u, lam_re, lam_im, log_dt, b_re, b_im, c_re, c_im, d, pairs_per_step=4):
    T, W = u.shape
    H, C = S5_GROUP, S5_CHUNK
    G = W // H
    P = G // S5_PAIR
    NC = T // C
    tm, wm, vm, lam_c = s5_matrices(lam_re, lam_im, log_dt, b_re, b_im, c_re, c_im)
    tm_p = _pair_blockdiag(tm).astype(BF16)
    w_p = jnp.concatenate([_pair_blockdiag(wm[0]), _pair_blockdiag(wm[1])], axis=2).astype(BF16)
    v_p = jnp.concatenate([_pair_blockdiag(vm[0]), _pair_blockdiag(vm[1])], axis=1).astype(BF16)
    NSEG = S5_SEGMENTS
    NS = NC // NSEG
    lc = (lam_c[0].reshape(P, -1), lam_c[1].reshape(P, -1))
    pw, lseg = _cpow_table(lc[0], lc[1], NS)
    lam_p = jnp.stack([lc[0], lc[1], lseg[0], lseg[1]], axis=1)
    pw_p = jnp.concatenate([pw[0], pw[1]], axis=2)
    d_p = jnp.broadcast_to(d.reshape(P, 2, 1, H), (P, 2, C, H)).reshape(P, 1, 2 * C * H)
    u_p = u.reshape(NSEG, NS, C, P, 2, H).transpose(3, 1, 0, 4, 2, 5).reshape(P, NC, 2 * C * H)
    PB = min(pairs_per_step, P)
    RW = 2 * C * H
    SW = 2 * LANES
    blk = lambda r, c: pl.BlockSpec((PB, r, c), lambda i: (i, 0, 0))
    y_p = pl.pallas_call(
        _s5_kernel,
        grid=(P // PB,),
        in_specs=[blk(NC, RW), blk(RW, RW), blk(RW, SW), blk(SW, RW), blk(4, LANES), blk(NS, SW),
                  blk(1, RW)],
        out_specs=blk(NC, RW),
        out_shape=jax.ShapeDtypeStruct((P, NC, RW), F32),
        scratch_shapes=[pltpu.VMEM((PB, NC, SW), F32), pltpu.VMEM((PB, NC, SW), F32)],
        compiler_params=_cparams(("parallel",), 56),
        name="s5_scan",
    )(u_p, tm_p, w_p, v_p, lam_p, pw_p, d_p)
    return y_p.reshape(P, NS, NSEG, 2, C, H).transpose(2, 1, 4, 0, 3, 5).reshape(T, W)


def _glu_out_kernel(x_ref, yc_ref, yg_ref, wglu_ref, bglu_ref, wout_ref, g_ref, xo_ref, ho_ref):
    yg = yg_ref[...]
    z = jnp.dot(yg.astype(BF16), wglu_ref[...], preferred_element_type=F32) + bglu_ref[...]
    ys = (yg * jax.nn.sigmoid(z)).astype(BF16)
    cw = yc_ref.shape[1]
    xn = (x_ref[...]
          + jnp.dot(yc_ref[...], wout_ref[0:cw, :], preferred_element_type=F32)
          + jnp.dot(ys, wout_ref[cw:, :], preferred_element_type=F32))
    xo_ref[...] = xn
    ho_ref[...] = _rms(xn, g_ref[...])


def glu_out(x, y_conv, y_gelu, w_glu, b_glu, w_out, g_next, tm=256):
    T, D = x.shape
    CW, SW = y_conv.shape[1], y_gelu.shape[1]
    tm = min(tm, T)
    row = lambda w: pl.BlockSpec((tm, w), lambda i: (i, 0))
    full = lambda a: pl.BlockSpec(a.shape, lambda i: (0,) * a.ndim)
    bg, g2 = b_glu.reshape(1, SW), g_next.reshape(1, D)
    return pl.pallas_call(
        _glu_out_kernel,
        grid=(T // tm,),
        in_specs=[row(D), row(CW), row(SW), full(w_glu), full(bg), full(w_out), full(g2)],
        out_specs=[row(D), row(D)],
        out_shape=[jax.ShapeDtypeStruct((T, D), F32), jax.ShapeDtypeStruct((T, D), F32)],
        compiler_params=_cparams(("parallel",), 56),
        name="glu_out",
    )(x, y_conv, y_gelu, w_glu, bg, w_out, g2)


def _proj_residual_kernel(x_ref, a_ref, w_ref, g_ref, xo_ref, ho_ref):
    xn = x_ref[...] + jnp.dot(a_ref[...], w_ref[...], preferred_element_type=F32)
    xo_ref[...] = xn
    ho_ref[...] = _rms(xn, g_ref[...])


def proj_residual(x, a, w, g_next, tm=256):
    T, D = x.shape
    K = a.shape[1]
    tm = min(tm, T)
    row = lambda w_: pl.BlockSpec((tm, w_), lambda i: (i, 0))
    full = lambda arr: pl.BlockSpec(arr.shape, lambda i: (0,) * arr.ndim)
    g2 = g_next.reshape(1, D)
    return pl.pallas_call(
        _proj_residual_kernel,
        grid=(T // tm,),
        in_specs=[row(D), row(K), full(w), full(g2)],
        out_specs=[row(D), row(D)],
        out_shape=[jax.ShapeDtypeStruct((T, D), F32), jax.ShapeDtypeStruct((T, D), F32)],
        compiler_params=_cparams(("parallel",), 56),
        name="proj_residual",
    )(x, a, w, g2)


def _router_kernel(h_ref, w_ref, b_ref, mi_ref, mf_ref, cnt_ref, run_ref):
    i = pl.program_id(0)

    @pl.when(i == 0)
    def _():
        run_ref[...] = jnp.zeros_like(run_ref)

    h = h_ref[...]
    tm, D = h.shape
    logits = jnp.dot(h, w_ref[...], precision=lax.Precision.HIGHEST,
                     preferred_element_type=F32) + b_ref[...]
    lane = lax.broadcasted_iota(I32, logits.shape, 1)
    neg = jnp.float32(-jnp.inf)
    gl = jnp.where(lane < N_GROUPS, logits, neg)
    gmax = jnp.max(gl, axis=1, keepdims=True)
    gsum = jnp.sum(jnp.where(lane < N_GROUPS, jnp.exp(gl - gmax), 0.0), axis=1, keepdims=True)
    gw = 1.0 / gsum
    gidx = jnp.min(jnp.where(gl == gmax, lane, LANES), axis=1, keepdims=True)
    lo = EXPERT_LANE0 + EXPERTS_PER_GROUP * gidx
    el = jnp.where((lane >= lo) & (lane < lo + EXPERTS_PER_GROUP), logits, neg)
    v1 = jnp.max(el, axis=1, keepdims=True)
    i1 = jnp.min(jnp.where(el == v1, lane, LANES), axis=1, keepdims=True)
    el2 = jnp.where(lane == i1, neg, el)
    v2 = jnp.max(el2, axis=1, keepdims=True)
    i2 = jnp.min(jnp.where(el2 == v2, lane, LANES), axis=1, keepdims=True)
    t = jnp.exp(v2 - v1)
    w1 = gw / (1.0 + t)
    w2 = gw * t / (1.0 + t)
    hit1 = lane == i1
    hit2 = lane == i2
    cnt = (hit1 | hit2).astype(BF16)
    r = lax.broadcasted_iota(I32, (tm, tm), 0)
    c = lax.broadcasted_iota(I32, (tm, tm), 1)
    before = (c < r).astype(BF16)
    cum = jnp.dot(before, cnt, preferred_element_type=F32) + run_ref[0:1, :]
    rank1 = jnp.sum(jnp.where(hit1, cum, 0.0), axis=1, keepdims=True).astype(I32)
    rank2 = jnp.sum(jnp.where(hit2, cum, 0.0), axis=1, keepdims=True).astype(I32)
    run = run_ref[0:1, :] + jnp.sum(cnt.astype(F32), axis=0, keepdims=True)
    run_ref[...] = jnp.broadcast_to(run, run_ref.shape)
    cnt_ref[...] = jnp.broadcast_to(run, cnt_ref.shape)
    e1 = i1 - EXPERT_LANE0
    e2 = i2 - EXPERT_LANE0
    mi_ref[...] = jnp.where(lane == 0, e1, jnp.where(lane == 1, e2,
                            jnp.where(lane == 2, rank1, jnp.where(lane == 3, rank2, 0))))
    mf_ref[...] = jnp.where(lane == 0, w1, jnp.where(lane == 1, w2, 0.0))


def router(h, w_r, b_r, tm=256):
    T, D = h.shape
    tm = min(tm, T)
    row = lambda w: pl.BlockSpec((tm, w), lambda i: (i, 0))
    full = lambda a: pl.BlockSpec(a.shape, lambda i: (0,) * a.ndim)
    return pl.pallas_call(
        _router_kernel,
        grid=(T // tm,),
        in_specs=[row(D), full(w_r), full(b_r)],
        out_specs=[row(LANES), row(LANES), pl.BlockSpec((8, LANES), lambda i: (0, 0))],
        out_shape=[jax.ShapeDtypeStruct((T, LANES), I32), jax.ShapeDtypeStruct((T, LANES), F32),
                   jax.ShapeDtypeStruct((8, LANES), F32)],
        scratch_shapes=[pltpu.VMEM((8, LANES), F32)],
        compiler_params=_cparams(("arbitrary",), 40),
        name="moe_router",
    )(h, w_r, b_r)


def moe_plan(meta_i, counts, n_blocks):
    B = MOE_BLOCK
    e1, e2, r1, r2 = meta_i[:, 0], meta_i[:, 1], meta_i[:, 2], meta_i[:, 3]
    cnt = counts[0, EXPERT_LANE0:EXPERT_LANE0 + N_EXPERTS].astype(I32)
    nblk = (cnt + B - 1) // B
    blk_end = jnp.cumsum(nblk)
    blk_off = blk_end - nblk
    pos1 = blk_off[e1] * B + r1
    pos2 = blk_off[e2] * B + r2
    b = jnp.arange(n_blocks, dtype=I32)
    total = blk_end[-1]
    owner = jnp.minimum(jnp.sum((blk_end[None, :] <= b[:, None]).astype(I32), axis=1), N_EXPERTS - 1)
    valid = b < total
    last_owner = owner[jnp.maximum(total - 1, 0)]
    blk_expert = jnp.where(valid, owner, last_owner)
    blk_rows = jnp.where(valid, jnp.clip(cnt[owner] - (b - blk_off[owner]) * B, 0, B), 0)
    blk_first = (valid & (b == blk_off[owner])).astype(I32)
    return pos1, pos2, blk_expert, blk_rows, blk_first


def _row_copy(src, dst, sem):
    return pltpu.make_async_copy(src, dst, sem)


def _dispatch_kernel(pos1_ref, pos2_ref, hp_ref, xs_in_ref, xs_ref, sem):
    del xs_in_ref
    tm = hp_ref.shape[0]
    base = pl.program_id(0) * tm

    def copies(r):
        src = hp_ref.at[pl.ds(r, 1)]
        return (_row_copy(src, xs_ref.at[pl.ds(pos1_ref[base + r], 1)], sem),
                _row_copy(src, xs_ref.at[pl.ds(pos2_ref[base + r], 1)], sem))

    def start(r, carry):
        for cp in copies(r):
            cp.start()
        return carry

    def wait(r, carry):
        for cp in copies(r):
            cp.wait()
        return carry

    lax.fori_loop(0, tm, start, 0)
    lax.fori_loop(0, tm, wait, 0)


def dispatch(hp, pos1, pos2, xs0, tm=256):
    T, W = hp.shape
    n_rows = xs0.shape[0]
    tm = min(tm, T)
    return pl.pallas_call(
        _dispatch_kernel,
        grid_spec=pltpu.PrefetchScalarGridSpec(
            num_scalar_prefetch=2,
            grid=(T // tm,),
            in_specs=[pl.BlockSpec((tm, W), lambda i, p1, p2: (i, 0)),
                      pl.BlockSpec(memory_space=pl.ANY)],
            out_specs=pl.BlockSpec(memory_space=pl.ANY),
            scratch_shapes=[pltpu.SemaphoreType.DMA(())]),
        out_shape=jax.ShapeDtypeStruct((n_rows, W), hp.dtype),
        input_output_aliases={3: 0},
        compiler_params=_cparams(("arbitrary",), 32),
        name="moe_dispatch",
    )(pos1, pos2, hp, xs0)


def _experts_kernel(be_ref, br_ref, bf_ref, xs_ref, wg_ref, wu_ref, wd_ref, ys_ref, wg_s, wu_s, wd_s):
    del be_ref
    b = pl.program_id(0)

    @pl.when(br_ref[b] > 0)
    def _():
        @pl.when(bf_ref[b] == 1)
        def _():
            wg_s[...] = wg_ref[0].astype(BF16)
            wu_s[...] = wu_ref[0].astype(BF16)
            wd_s[...] = wd_ref[0].astype(BF16)

        a = xs_ref[...].astype(BF16)
        gate = jnp.dot(a, wg_s[...], preferred_element_type=F32)
        up = jnp.dot(a, wu_s[...], preferred_element_type=F32)
        mid = (jax.nn.silu(gate) * up).astype(BF16)
        ys_ref[...] = jnp.dot(mid, wd_s[...], preferred_element_type=F32)

    @pl.when(br_ref[b] == 0)
    def _():
        ys_ref[...] = jnp.zeros_like(ys_ref)


def experts(xs, blk_expert, blk_rows, blk_first, w_gate, w_up, w_down):
    B = MOE_BLOCK
    n_rows = xs.shape[0]
    E, D, FF = w_gate.shape
    nb = n_rows // B
    return pl.pallas_call(
        _experts_kernel,
        grid_spec=pltpu.PrefetchScalarGridSpec(
            num_scalar_prefetch=3,
            grid=(nb,),
            in_specs=[pl.BlockSpec((B, D), lambda b, be, br, bf: (b, 0)),
                      pl.BlockSpec((1, D, FF), lambda b, be, br, bf: (be[b], 0, 0)),
                      pl.BlockSpec((1, D, FF), lambda b, be, br, bf: (be[b], 0, 0)),
                      pl.BlockSpec((1, FF, D), lambda b, be, br, bf: (be[b], 0, 0))],
            out_specs=pl.BlockSpec((B, D), lambda b, be, br, bf: (b, 0)),
            scratch_shapes=[pltpu.VMEM((D, FF), BF16), pltpu.VMEM((D, FF), BF16),
                            pltpu.VMEM((FF, D), BF16)]),
        out_shape=jax.ShapeDtypeStruct((n_rows, D), F32),
        compiler_params=_cparams(("arbitrary",), 56),
        name="moe_experts",
    )(blk_expert, blk_rows, blk_first, xs, w_gate, w_up, w_down)


def _combine_kernel(pos1_ref, pos2_ref, x_ref, mf_ref, g_ref, ys_ref, xo_ref, ho_ref, buf, sem):
    tm = x_ref.shape[0]
    base = pl.program_id(0) * tm

    def copies(r):
        dst = pl.ds(r, 1)
        return (_row_copy(ys_ref.at[pl.ds(pos1_ref[base + r], 1)], buf.at[0, dst], sem),
                _row_copy(ys_ref.at[pl.ds(pos2_ref[base + r], 1)], buf.at[1, dst], sem))

    def start(r, carry):
        for cp in copies(r):
            cp.start()
        return carry

    def wait(r, carry):
        for cp in copies(r):
            cp.wait()
        return carry

    lax.fori_loop(0, tm, start, 0)
    lax.fori_loop(0, tm, wait, 0)
    mf = mf_ref[...]
    xn = x_ref[...] + mf[:, 0:1] * buf[0] + mf[:, 1:2] * buf[1]
    xo_ref[...] = xn
    ho_ref[...] = _rms(xn, g_ref[...]).astype(ho_ref.dtype)


def combine(x, ys, meta_f, pos1, pos2, g_next, h_dtype, tm=256):
    T, D = x.shape
    tm = min(tm, T)
    row = lambda w: pl.BlockSpec((tm, w), lambda i, p1, p2: (i, 0))
    return pl.pallas_call(
        _combine_kernel,
        grid_spec=pltpu.PrefetchScalarGridSpec(
            num_scalar_prefetch=2,
            grid=(T // tm,),
            in_specs=[row(D), row(LANES), pl.BlockSpec((1, D), lambda i, p1, p2: (0, 0)),
                      pl.BlockSpec(memory_space=pl.ANY)],
            out_specs=[row(D), row(D)],
            scratch_shapes=[pltpu.VMEM((2, tm, D), F32), pltpu.SemaphoreType.DMA(())]),
        out_shape=[jax.ShapeDtypeStruct((T, D), F32), jax.ShapeDtypeStruct((T, D), h_dtype)],
        compiler_params=_cparams(("arbitrary",), 40),
        name="moe_combine",
    )(pos1, pos2, x, meta_f, g_next.reshape(1, D), ys)


def moe_row_blocks(T):
    return (2 * T) // MOE_BLOCK + N_EXPERTS


def moe_layer(x, h, xs_buf, w_group, b_group, w_expert, b_expert, layer, w_gate, w_up, w_down, g_next,
              h_dtype):
    T, D = x.shape
    n_blocks = xs_buf.shape[0] // MOE_BLOCK
    pad = LANES - N_GROUPS - N_EXPERTS
    w_r = jnp.concatenate([w_group, w_expert.reshape(D, N_EXPERTS), jnp.zeros((D, pad), F32)], axis=1)
    b_r = jnp.concatenate([b_group, b_expert.reshape(N_EXPERTS), jnp.zeros((pad,), F32)]).reshape(1, LANES)
    meta_i, meta_f, counts = router(h, w_r, b_r)
    pos1, pos2, blk_expert, blk_rows, blk_first = moe_plan(meta_i, counts, n_blocks)
    xs = dispatch(h, pos1, pos2, xs_buf)
    FF = w_gate.shape[-1]
    ys = experts(xs, blk_expert + layer * N_EXPERTS, blk_rows, blk_first, w_gate.reshape(-1, D, FF),
                 w_up.reshape(-1, D, FF), w_down.reshape(-1, FF, D))
    xo, ho = combine(x, ys, meta_f, pos1, pos2, g_next, h_dtype)
    return xo, ho, xs


def _forget_kernel(h_ref, w_ref, b_ref, o_ref, run_ref):
    @pl.when(pl.program_id(0) == 0)
    def _():
        run_ref[...] = jnp.zeros_like(run_ref)

    hi = lax.Precision.HIGHEST
    z = jnp.dot(h_ref[...].astype(F32), w_ref[...], precision=hi, preferred_element_type=F32) + b_ref[...]
    log_f = jax.nn.log_sigmoid(z)
    tm = z.shape[0]
    r = lax.broadcasted_iota(I32, (tm, tm), 0)
    c = lax.broadcasted_iota(I32, (tm, tm), 1)
    upto = (c <= r).astype(F32)
    cum = jnp.dot(upto, log_f, precision=hi, preferred_element_type=F32) + run_ref[0:1, :]
    o_ref[...] = cum
    run_ref[...] = jnp.broadcast_to(cum[tm - 1:tm, :], run_ref.shape)


def forget_cumsum(h, w_f, b_f, tm=256):
    T, D = h.shape
    tm = min(tm, T)
    return pl.pallas_call(
        _forget_kernel,
        grid=(T // tm,),
        in_specs=[pl.BlockSpec((tm, D), lambda i: (i, 0)),
                  pl.BlockSpec((D, LANES), lambda i: (0, 0)),
                  pl.BlockSpec((1, LANES), lambda i: (0, 0))],
        out_specs=pl.BlockSpec((tm, LANES), lambda i: (i, 0)),
        out_shape=jax.ShapeDtypeStruct((T, LANES), F32),
        scratch_shapes=[pltpu.VMEM((8, LANES), F32)],
        compiler_params=_cparams(("arbitrary",), 40),
        name="forget_cumsum",
    )(h, w_f, b_f)


def _flash_kernel(q_ref, k_ref, v_ref, ck_ref, cq_ref, o_ref, m_ref, acc_ref, *, scale, sub):
    qi = pl.program_id(1)
    bq = q_ref.shape[0]
    dh = FOX_HEAD_DIM
    heads = q_ref.shape[1] // dh
    m_ref[...] = jnp.full_like(m_ref, -jnp.inf)
    acc_ref[...] = jnp.zeros_like(acc_ref)
    ones = jnp.ones((bq, dh), BF16)
    n_sub = bq // sub

    def chunk(start, diagonal):
        for hh in range(heads):
            cols = slice(hh * dh, (hh + 1) * dh)
            c0 = cq_ref[hh, :, 0:1]
            k = k_ref[pl.ds(start, bq), cols]
            v1 = jnp.concatenate([v_ref[pl.ds(start, bq), cols], ones], axis=1)
            s_all = lax.dot_general(q_ref[:, cols], k, (((1,), (1,)), ((), ())),
                                    preferred_element_type=F32)
            bias = (c0 - ck_ref[hh, :, pl.ds(start, bq)]) * LOG2E
            ps, alphas = [], []
            for r in range(n_sub):
                rows = slice(r * sub, (r + 1) * sub)
                s = s_all[rows, :] * (scale * LOG2E) + bias
                if diagonal:
                    row = lax.broadcasted_iota(I32, s.shape, 0) + r * sub
                    col = lax.broadcasted_iota(I32, s.shape, 1)
                    s = jnp.where(col <= row, s, -jnp.inf)
                m_prev = m_ref[hh, rows, :]
                m_new = jnp.maximum(m_prev, jnp.max(s, axis=1, keepdims=True))
                m_ref[hh, rows, :] = m_new
                ps.append(jnp.exp2(s - jnp.concatenate([m_new] * (bq // LANES), axis=1)).astype(BF16))
                alphas.append(jnp.exp2(m_prev - m_new))
            pv = jnp.dot(jnp.concatenate(ps, axis=0), v1, preferred_element_type=F32)
            for r in range(n_sub):
                rows = slice(r * sub, (r + 1) * sub)
                acc_ref[hh, rows, :] = (jnp.concatenate([alphas[r]] * 2, axis=1) * acc_ref[hh, rows, :]
                                        + pv[rows, :])

    def body(j, carry):
        chunk(pl.multiple_of(j * bq, bq), False)
        return carry

    lax.fori_loop(0, qi, body, 0)
    chunk(pl.multiple_of(qi * bq, bq), True)
    for hh in range(heads):
        o_ref[:, hh * dh:(hh + 1) * dh] = (acc_ref[hh, :, 0:dh] / acc_ref[hh, :, dh:2 * dh]).astype(o_ref.dtype)


def flash_attention(qkv, cum_t, bq=512, sub=64, heads=2):
    T = qkv.shape[0]
    H, dh = FOX_HEADS, FOX_HEAD_DIM
    bq = min(bq, T)
    kern = functools.partial(_flash_kernel, scale=dh ** -0.5, sub=min(sub, bq))
    hw = heads * dh
    nhb = H // heads
    return pl.pallas_call(
        kern,
        grid=(nhb, T // bq),
        in_specs=[pl.BlockSpec((bq, hw), lambda h, i: (i, h)),
                  pl.BlockSpec((T, hw), lambda h, i: (0, nhb + h)),
                  pl.BlockSpec((T, hw), lambda h, i: (0, 2 * nhb + h)),
                  pl.BlockSpec((heads, 1, T), lambda h, i: (h, 0, 0)),
                  pl.BlockSpec((heads, 1, bq), lambda h, i: (h, 0, i))],
        out_specs=pl.BlockSpec((bq, hw), lambda h, i: (i, h)),
        out_shape=jax.ShapeDtypeStruct((T, H * dh), BF16),
        scratch_shapes=[pltpu.VMEM((heads, bq, LANES), F32), pltpu.VMEM((heads, bq, 2 * dh), F32)],
        compiler_params=_cparams(("parallel", "arbitrary"), 48),
        name="fox_attention",
    )(qkv, qkv, qkv, cum_t, cum_t)


def kernel(x, ab_norm, ab_w_in, ab_conv_w, s5_lambda_re, s5_lambda_im, s5_log_dt, s5_b_re, s5_b_im,
           s5_c_re, s5_c_im, s5_d, s5_w_glu, s5_b_glu, ab_w_out, c_norm, c_w_in, c_b_forget, c_w_out,
           ffn_norm, router_w_group, router_b_group, router_w_expert, router_b_expert,
           moe_w_gate, moe_w_up, moe_w_down, final_norm):
    bsz, L, D = x.shape
    depth = ffn_norm.shape[0]
    cw = ab_conv_w.shape[-1]
    xt = x.reshape(bsz * L, D)
    h = None
    xs_buf = jnp.zeros((moe_row_blocks(bsz * L) * MOE_BLOCK, D), F32)
    for i in range(depth):
        j = i // 2
        if i % 2 == 0:
            g_in = ab_norm[j]
            if h is not None:
                proj = matmul(h, ab_w_in[j].astype(BF16), F32)
            else:
                proj = norm_matmul(xt, g_in, ab_w_in[j].astype(BF16), F32)
            y_conv = conv_mixer(proj, ab_conv_w[j])
            y_gelu = s5_mixer(proj[:, 3 * cw:], s5_lambda_re[j], s5_lambda_im[j], s5_log_dt[j],
                              s5_b_re[j], s5_b_im[j], s5_c_re[j], s5_c_im[j], s5_d[j])
            xt, hf = glu_out(xt, y_conv, y_gelu, s5_w_glu[j].astype(BF16), s5_b_glu[j],
                             ab_w_out[j].astype(BF16), ffn_norm[i])
        else:
            assert h is not None, "an attention layer always follows a MoE combine that emits its norm"
            hd = FOX_HEADS * FOX_HEAD_DIM
            qkv = matmul(h, c_w_in[j][:, :3 * hd].astype(BF16), BF16)
            w_f = jnp.pad(c_w_in[j][:, 3 * hd:], ((0, 0), (0, LANES - FOX_HEADS)))
            b_f = jnp.pad(c_b_forget[j], (0, LANES - FOX_HEADS)).reshape(1, LANES)
            cum = forget_cumsum(h, w_f, b_f)
            cum_t = cum[:, :FOX_HEADS].T.reshape(FOX_HEADS, 1, bsz * L)
            att = flash_attention(qkv, cum_t)
            xt, hf = proj_residual(xt, att, c_w_out[j].astype(BF16), ffn_norm[i])
        last = i == depth - 1
        if last:
            g_next = final_norm
        elif (i + 1) % 2 == 0:
            g_next = ab_norm[(i + 1) // 2]
        else:
            g_next = c_norm[(i + 1) // 2]
        xt, h, xs_buf = moe_layer(xt, hf, xs_buf, router_w_group[i], router_b_group[i], router_w_expert[i],
                                  router_b_expert[i], i, moe_w_gate, moe_w_up, moe_w_down,
                                  g_next, F32 if last else BF16)
    return h.reshape(bsz, L, D)
```

```python
import functools
import math

import jax
import jax.numpy as jnp
from jax import lax
from jax.experimental import pallas as pl
from jax.experimental.pallas import tpu as pltpu

F32 = jnp.float32
BF16 = jnp.bfloat16
I32 = jnp.int32
U32 = jnp.uint32

RMS_EPS = 1e-6
LANES = 128
LOG2E = math.log2(math.e)
MIB = 1024 * 1024

CONV_K = 3
S5_GROUP = 16
S5_STATE = 64
S5_CHUNK = 16
S5_OCT = LANES // S5_GROUP
FOX_HEADS = 16
FOX_HEAD_DIM = 128
N_GROUPS = 4
EXPERTS_PER_GROUP = 8
N_EXPERTS = N_GROUPS * EXPERTS_PER_GROUP
EXPERT_LANE0 = N_GROUPS
MOE_BLOCK = 256


def _cparams(sem, vmem_mib):
    return pltpu.CompilerParams(dimension_semantics=sem, vmem_limit_bytes=vmem_mib * MIB)


def _rms(x, g):
    ms = jnp.mean(x * x, axis=-1, keepdims=True)
    return x * lax.rsqrt(ms + RMS_EPS) * g


def _norm_matmul_kernel(x_ref, g_ref, w_ref, o_ref, h_ref):
    @pl.when(pl.program_id(1) == 0)
    def _():
        h_ref[...] = _rms(x_ref[...], g_ref[...]).astype(BF16)

    o_ref[...] = jnp.dot(h_ref[...], w_ref[...], preferred_element_type=F32).astype(o_ref.dtype)


def norm_matmul(x, g, w, out_dtype, tm=1024, tn=1024):
    T, D = x.shape
    N = w.shape[1]
    tm, tn = min(tm, T), min(tn, N)
    return pl.pallas_call(
        _norm_matmul_kernel,
        grid=(T // tm, N // tn),
        in_specs=[pl.BlockSpec((tm, D), lambda i, j: (i, 0)),
                  pl.BlockSpec((1, D), lambda i, j: (0, 0)),
                  pl.BlockSpec((D, tn), lambda i, j: (0, j))],
        out_specs=pl.BlockSpec((tm, tn), lambda i, j: (i, j)),
        out_shape=jax.ShapeDtypeStruct((T, N), out_dtype),
        scratch_shapes=[pltpu.VMEM((tm, D), BF16)],
        compiler_params=_cparams(("parallel", "arbitrary"), 56),
        name="norm_matmul",
    )(x, g.reshape(1, D), w)


def _matmul_kernel(a_ref, w_ref, o_ref):
    o_ref[...] = jnp.dot(a_ref[...], w_ref[...], preferred_element_type=F32).astype(o_ref.dtype)


def matmul(a, w, out_dtype, tm=1024, tn=1024):
    T, K = a.shape
    N = w.shape[1]
    tm, tn = min(tm, T), min(tn, N)
    return pl.pallas_call(
        _matmul_kernel,
        grid=(T // tm, N // tn),
        in_specs=[pl.BlockSpec((tm, K), lambda i, j: (i, 0)),
                  pl.BlockSpec((K, tn), lambda i, j: (0, j))],
        out_specs=pl.BlockSpec((tm, tn), lambda i, j: (i, j)),
        out_shape=jax.ShapeDtypeStruct((T, N), out_dtype),
        compiler_params=_cparams(("parallel", "arbitrary"), 48),
        name="matmul",
    )(a, w)


def _conv_kernel(gb_ref, gc_ref, u_ref, gcp_ref, up_ref, w_ref, o_ref):
    w0, w1, w2 = w_ref[0:1, :], w_ref[1:2, :], w_ref[2:3, :]
    v = gc_ref[...] * u_ref[...]
    y = w2 * v + w1 * pltpu.roll(v, 1, 0) + w0 * pltpu.roll(v, 2, 0)
    o_ref[...] = (gb_ref[...] * y).astype(o_ref.dtype)
    vp = gcp_ref[...] * up_ref[...]
    vp = jnp.where(pl.program_id(0) > 0, vp, jnp.zeros_like(vp))
    v8 = v[0:8, :]
    row = lax.broadcasted_iota(I32, v8.shape, 0)
    v1 = jnp.where(row < 1, pltpu.roll(vp, 1, 0), pltpu.roll(v8, 1, 0))
    v2 = jnp.where(row < 2, pltpu.roll(vp, 2, 0), pltpu.roll(v8, 2, 0))
    o_ref[0:8, :] = (gb_ref[0:8, :] * (w2 * v8 + w1 * v1 + w0 * v2)).astype(o_ref.dtype)


def conv_mixer(proj, conv_w, tm=512):
    T = proj.shape[0]
    CW = conv_w.shape[1]
    tm = min(tm, T)
    r8 = tm // 8
    cur = lambda c: pl.BlockSpec((tm, CW), lambda i: (i, c))
    prev = lambda c: pl.BlockSpec((8, CW), lambda i: (jnp.maximum(i * r8 - 1, 0), c))
    return pl.pallas_call(
        _conv_kernel,
        grid=(T // tm,),
        in_specs=[cur(0), cur(1), cur(2), prev(1), prev(2),
                  pl.BlockSpec((CONV_K, CW), lambda i: (0, 0))],
        out_specs=pl.BlockSpec((tm, CW), lambda i: (i, 0)),
        out_shape=jax.ShapeDtypeStruct((T, CW), BF16),
        compiler_params=_cparams(("parallel",), 40),
        name="conv_mixer",
    )(proj, proj, proj, proj, proj, conv_w)


def _cmul(a, b):
    return a[0] * b[0] - a[1] * b[1], a[0] * b[1] + a[1] * b[0]


def s5_matrices(lam_re, lam_im, log_dt, b_re, b_im, c_re, c_im):
    C = S5_CHUNK
    dt = jnp.exp(log_dt)[:, None]
    a, b = lam_re * dt, lam_im * dt
    mag = jnp.exp(a)
    lbar = (mag * jnp.cos(b), mag * jnp.sin(b))
    den = lam_re * lam_re + lam_im * lam_im
    inv_lam = (lam_re / den, -lam_im / den)
    coef = _cmul((lbar[0] - 1.0, lbar[1]), inv_lam)
    bbar = _cmul((coef[0][..., None], coef[1][..., None]), (b_re, b_im))
    j = jnp.arange(C + 1, dtype=F32)[None, :, None]
    pmag = jnp.exp(a[:, None, :] * j)
    pw = (pmag * jnp.cos(b[:, None, :] * j), pmag * jnp.sin(b[:, None, :] * j))
    hi = lax.Precision.HIGHEST
    cp = _cmul((c_re[:, None], c_im[:, None]), (pw[0][:, :C, None, :], pw[1][:, :C, None, :]))
    kj = (jnp.einsum('gjhn,gnk->gjhk', cp[0], bbar[0], precision=hi)
          - jnp.einsum('gjhn,gnk->gjhk', cp[1], bbar[1], precision=hi))
    G = lam_re.shape[0]
    H = S5_GROUP
    prev = (pw[0][:, C - 1::-1][:, :C], pw[1][:, C - 1::-1][:, :C])
    wm = _cmul((prev[0][:, :, None, :], prev[1][:, :, None, :]),
               (bbar[0].transpose(0, 2, 1)[:, None], bbar[1].transpose(0, 2, 1)[:, None]))
    wm = (wm[0].reshape(G, C * H, -1), wm[1].reshape(G, C * H, -1))
    gq = _cmul((c_re[:, None], c_im[:, None]), (pw[0][:, 1:, None, :], pw[1][:, 1:, None, :]))
    v_re = gq[0].transpose(0, 3, 1, 2).reshape(G, -1, C * H)
    v_im = -gq[1].transpose(0, 3, 1, 2).reshape(G, -1, C * H)
    lam_c = (pw[0][:, C], pw[1][:, C])
    return kj, wm, (v_re, v_im), lam_c


def _group_blockdiag(m, g_axis):
    out = jnp.zeros(m.shape[:-1] + (S5_OCT,) + m.shape[-1:], m.dtype)
    for g in range(S5_OCT):
        dst = [slice(None)] * out.ndim
        dst[g_axis], dst[-2] = g, g
        out = out.at[tuple(dst)].set(jnp.take(m, g, axis=g_axis))
    return out


def s5_lane_tile_operators(lam_re, lam_im, log_dt, b_re, b_im, c_re, c_im):
    C, H, N, O = S5_CHUNK, S5_GROUP, S5_STATE, S5_OCT
    kj, wm, vm, lam_c = s5_matrices(lam_re, lam_im, log_dt, b_re, b_im, c_re, c_im)
    Q = kj.shape[0] // O
    kq = kj.reshape(Q, O, C, H, H).transpose(0, 2, 1, 4, 3)
    bd = _group_blockdiag(kq, 2).reshape(Q, C, O * H, O * H)
    w = [_group_blockdiag(m.reshape(Q, O, C, H, N).transpose(0, 2, 1, 3, 4), 2).reshape(Q, C * O * H, O * N)
         for m in wm]
    wbig = jnp.concatenate(w, axis=2)
    v = [_group_blockdiag(m.reshape(Q, O, N, C, H), 1).reshape(Q, O * N, C * O * H)
         for m in vm]
    vbig = jnp.concatenate(v, axis=1)
    lam_q = jnp.stack([lam_c[0].reshape(Q, O * N), lam_c[1].reshape(Q, O * N)], axis=1)
    return bd.astype(BF16), wbig.astype(BF16), vbig.astype(BF16), lam_q


def _s5_kernel(u_ref, bd_ref, w_ref, v_ref, lam_ref, d_ref, o_ref, t_ref, uf_ref, s_ref, x_ref, c_ref):
    C = S5_CHUNK
    tb = u_ref.shape[0]
    nc = tb // C
    sw = lam_ref.shape[2]

    @pl.when((pl.program_id(0) == 0) & (pl.program_id(1) == 0))
    def _():
        t_ref[...] = jnp.zeros_like(t_ref)

    @pl.when(pl.program_id(1) == 0)
    def _():
        for s in range(C):
            for t in range(s, C):
                t_ref[s * LANES:(s + 1) * LANES, t * LANES:(t + 1) * LANES] = bd_ref[0, t - s]
        c_ref[...] = jnp.zeros_like(c_ref)

    for s in range(C):
        uf_ref[:, s * LANES:(s + 1) * LANES] = u_ref[pl.ds(s, nc, stride=C), :].astype(BF16)
    s_ref[...] = jnp.dot(uf_ref[...], w_ref[0], preferred_element_type=F32)
    lr, li = lam_ref[0, 0:1, :], lam_ref[0, 1:2, :]
    row = lax.broadcasted_iota(I32, (8, sw), 0)

    def tile_step(i, carry):
        xr, xi = carry
        r0 = pl.multiple_of(i * 8, 8)
        sr = s_ref[pl.ds(r0, 8), 0:sw]
        si = s_ref[pl.ds(r0, 8), sw:2 * sw]
        tr = jnp.zeros((8, sw), F32)
        ti = jnp.zeros((8, sw), F32)
        for r in range(8):
            tr = jnp.where(row == r, xr, tr)
            ti = jnp.where(row == r, xi, ti)
            xr, xi = lr * xr - li * xi + sr[r:r + 1, :], lr * xi + li * xr + si[r:r + 1, :]
        x_ref[pl.ds(r0, 8), 0:sw] = tr
        x_ref[pl.ds(r0, 8), sw:2 * sw] = ti
        return xr, xi

    xr, xi = lax.fori_loop(0, nc // 8, tile_step, (c_ref[0:1, 0:sw], c_ref[0:1, sw:2 * sw]))
    c_ref[0:1, 0:sw] = xr
    c_ref[0:1, sw:2 * sw] = xi
    y = (jnp.dot(uf_ref[...], t_ref[...], preferred_element_type=F32)
         + jnp.dot(x_ref[...].astype(BF16), v_ref[0], preferred_element_type=F32))
    for s in range(C):
        ys = y[:, s * LANES:(s + 1) * LANES] + d_ref[...] * u_ref[pl.ds(s, nc, stride=C), :]
        o_ref[pl.ds(s, nc, stride=C), :] = jax.nn.gelu(ys)


def s5_mixer(proj, col0, lam_re, lam_im, log_dt, b_re, b_im, c_re, c_im, d, tb=4096):
    T = proj.shape[0]
    C, N, O = S5_CHUNK, S5_STATE, S5_OCT
    W = d.shape[0]
    Q = W // LANES
    tb = min(tb, T)
    bd, wbig, vbig, lam_q = s5_lane_tile_operators(lam_re, lam_im, log_dt, b_re, b_im, c_re, c_im)
    cb0 = col0 // LANES
    nc = tb // C
    sw = O * N
    return pl.pallas_call(
        _s5_kernel,
        grid=(Q, T // tb),
        in_specs=[pl.BlockSpec((tb, LANES), lambda q, t: (t, cb0 + q)),
                  pl.BlockSpec((1, C, LANES, LANES), lambda q, t: (q, 0, 0, 0)),
                  pl.BlockSpec((1, C * LANES, 2 * sw), lambda q, t: (q, 0, 0)),
                  pl.BlockSpec((1, 2 * sw, C * LANES), lambda q, t: (q, 0, 0)),
                  pl.BlockSpec((1, 2, sw), lambda q, t: (q, 0, 0)),
                  pl.BlockSpec((1, LANES), lambda q, t: (0, q))],
        out_specs=pl.BlockSpec((tb, LANES), lambda q, t: (t, q)),
        out_shape=jax.ShapeDtypeStruct((T, W), F32),
        scratch_shapes=[pltpu.VMEM((C * LANES, C * LANES), BF16),
                        pltpu.VMEM((nc, C * LANES), BF16),
                        pltpu.VMEM((nc, 2 * sw), F32),
                        pltpu.VMEM((nc, 2 * sw), F32),
                        pltpu.VMEM((8, 2 * sw), F32)],
        compiler_params=_cparams(("arbitrary", "arbitrary"), 56),
        name="s5_scan",
    )(proj, bd, wbig, vbig, lam_q, d.reshape(1, W))


def _glu_out_kernel(x_ref, yc_ref, yg_ref, wglu_ref, bglu_ref, wout_ref, g_ref, xo_ref, ho_ref):
    yg = yg_ref[...]
    z = jnp.dot(yg.astype(BF16), wglu_ref[...], preferred_element_type=F32) + bglu_ref[...]
    ys = (yg * jax.nn.sigmoid(z)).astype(BF16)
    cw = yc_ref.shape[1]
    xn = (x_ref[...]
          + jnp.dot(yc_ref[...], wout_ref[0:cw, :], preferred_element_type=F32)
          + jnp.dot(ys, wout_ref[cw:, :], preferred_element_type=F32))
    xo_ref[...] = xn
    ho_ref[...] = _rms(xn, g_ref[...])


def glu_out(x, y_conv, y_gelu, w_glu, b_glu, w_out, g_next, tm=256):
    T, D = x.shape
    CW, SW = y_conv.shape[1], y_gelu.shape[1]
    tm = min(tm, T)
    row = lambda w: pl.BlockSpec((tm, w), lambda i: (i, 0))
    full = lambda a: pl.BlockSpec(a.shape, lambda i: (0,) * a.ndim)
    bg, g2 = b_glu.reshape(1, SW), g_next.reshape(1, D)
    return pl.pallas_call(
        _glu_out_kernel,
        grid=(T // tm,),
        in_specs=[row(D), row(CW), row(SW), full(w_glu), full(bg), full(w_out), full(g2)],
        out_specs=[row(D), row(D)],
        out_shape=[jax.ShapeDtypeStruct((T, D), F32), jax.ShapeDtypeStruct((T, D), F32)],
        compiler_params=_cparams(("parallel",), 56),
        name="glu_out",
    )(x, y_conv, y_gelu, w_glu, bg, w_out, g2)


def _proj_residual_kernel(x_ref, a_ref, w_ref, g_ref, xo_ref, ho_ref):
    xn = x_ref[...] + jnp.dot(a_ref[...], w_ref[...], preferred_element_type=F32)
    xo_ref[...] = xn
    ho_ref[...] = _rms(xn, g_ref[...])


def proj_residual(x, a, w, g_next, tm=256):
    T, D = x.shape
    K = a.shape[1]
    tm = min(tm, T)
    row = lambda w_: pl.BlockSpec((tm, w_), lambda i: (i, 0))
    full = lambda arr: pl.BlockSpec(arr.shape, lambda i: (0,) * arr.ndim)
    g2 = g_next.reshape(1, D)
    return pl.pallas_call(
        _proj_residual_kernel,
        grid=(T // tm,),
        in_specs=[row(D), row(K), full(w), full(g2)],
        out_specs=[row(D), row(D)],
        out_shape=[jax.ShapeDtypeStruct((T, D), F32), jax.ShapeDtypeStruct((T, D), F32)],
        compiler_params=_cparams(("parallel",), 56),
        name="proj_residual",
    )(x, a, w, g2)


def _router_kernel(h_ref, w_ref, b_ref, mi_ref, mf_ref, cnt_ref, run_ref):
    i = pl.program_id(0)

    @pl.when(i == 0)
    def _():
        run_ref[...] = jnp.zeros_like(run_ref)

    h = h_ref[...]
    tm, D = h.shape
    logits = jnp.dot(h, w_ref[...], precision=lax.Precision.HIGHEST,
                     preferred_element_type=F32) + b_ref[...]
    lane = lax.broadcasted_iota(I32, logits.shape, 1)
    neg = jnp.float32(-jnp.inf)
    gl = jnp.where(lane < N_GROUPS, logits, neg)
    gmax = jnp.max(gl, axis=1, keepdims=True)
    gsum = jnp.sum(jnp.where(lane < N_GROUPS, jnp.exp(gl - gmax), 0.0), axis=1, keepdims=True)
    gw = 1.0 / gsum
    gidx = jnp.min(jnp.where(gl == gmax, lane, LANES), axis=1, keepdims=True)
    lo = EXPERT_LANE0 + EXPERTS_PER_GROUP * gidx
    el = jnp.where((lane >= lo) & (lane < lo + EXPERTS_PER_GROUP), logits, neg)
    v1 = jnp.max(el, axis=1, keepdims=True)
    i1 = jnp.min(jnp.where(el == v1, lane, LANES), axis=1, keepdims=True)
    el2 = jnp.where(lane == i1, neg, el)
    v2 = jnp.max(el2, axis=1, keepdims=True)
    i2 = jnp.min(jnp.where(el2 == v2, lane, LANES), axis=1, keepdims=True)
    t = jnp.exp(v2 - v1)
    w1 = gw / (1.0 + t)
    w2 = gw * t / (1.0 + t)
    hit1 = lane == i1
    hit2 = lane == i2
    cnt = (hit1 | hit2).astype(BF16)
    r = lax.broadcasted_iota(I32, (tm, tm), 0)
    c = lax.broadcasted_iota(I32, (tm, tm), 1)
    before = (c < r).astype(BF16)
    cum = jnp.dot(before, cnt, preferred_element_type=F32) + run_ref[0:1, :]
    rank1 = jnp.sum(jnp.where(hit1, cum, 0.0), axis=1, keepdims=True).astype(I32)
    rank2 = jnp.sum(jnp.where(hit2, cum, 0.0), axis=1, keepdims=True).astype(I32)
    run = run_ref[0:1, :] + jnp.sum(cnt.astype(F32), axis=0, keepdims=True)
    run_ref[...] = jnp.broadcast_to(run, run_ref.shape)
    cnt_ref[...] = jnp.broadcast_to(run, cnt_ref.shape)
    e1 = i1 - EXPERT_LANE0
    e2 = i2 - EXPERT_LANE0
    mi_ref[...] = jnp.where(lane == 0, e1, jnp.where(lane == 1, e2,
                            jnp.where(lane == 2, rank1, jnp.where(lane == 3, rank2, 0))))
    mf_ref[...] = jnp.where(lane == 0, w1, jnp.where(lane == 1, w2, 0.0))


def router(h, w_r, b_r, tm=256):
    T, D = h.shape
    tm = min(tm, T)
    row = lambda w: pl.BlockSpec((tm, w), lambda i: (i, 0))
    full = lambda a: pl.BlockSpec(a.shape, lambda i: (0,) * a.ndim)
    return pl.pallas_call(
        _router_kernel,
        grid=(T // tm,),
        in_specs=[row(D), full(w_r), full(b_r)],
        out_specs=[row(LANES), row(LANES), pl.BlockSpec((8, LANES), lambda i: (0, 0))],
        out_shape=[jax.ShapeDtypeStruct((T, LANES), I32), jax.ShapeDtypeStruct((T, LANES), F32),
                   jax.ShapeDtypeStruct((8, LANES), F32)],
        scratch_shapes=[pltpu.VMEM((8, LANES), F32)],
        compiler_params=_cparams(("arbitrary",), 40),
        name="moe_router",
    )(h, w_r, b_r)


def moe_plan(meta_i, counts, n_blocks):
    B = MOE_BLOCK
    e1, e2, r1, r2 = meta_i[:, 0], meta_i[:, 1], meta_i[:, 2], meta_i[:, 3]
    cnt = counts[0, EXPERT_LANE0:EXPERT_LANE0 + N_EXPERTS].astype(I32)
    nblk = (cnt + B - 1) // B
    blk_end = jnp.cumsum(nblk)
    blk_off = blk_end - nblk
    pos1 = blk_off[e1] * B + r1
    pos2 = blk_off[e2] * B + r2
    b = jnp.arange(n_blocks, dtype=I32)
    total = blk_end[-1]
    owner = jnp.minimum(jnp.sum((blk_end[None, :] <= b[:, None]).astype(I32), axis=1), N_EXPERTS - 1)
    valid = b < total
    last_owner = owner[jnp.maximum(total - 1, 0)]
    blk_expert = jnp.where(valid, owner, last_owner)
    blk_rows = jnp.where(valid, jnp.clip(cnt[owner] - (b - blk_off[owner]) * B, 0, B), 0)
    blk_first = (valid & (b == blk_off[owner])).astype(I32)
    return pos1, pos2, blk_expert, blk_rows, blk_first


def _row_copy(src, dst, sem):
    return pltpu.make_async_copy(src, dst, sem)


def _dispatch_kernel(pos1_ref, pos2_ref, hp_ref, xs_in_ref, xs_ref, sem):
    del xs_in_ref
    tm = hp_ref.shape[0]
    base = pl.program_id(0) * tm

    def copies(r):
        src = hp_ref.at[pl.ds(r, 1)]
        return (_row_copy(src, xs_ref.at[pl.ds(pos1_ref[base + r], 1)], sem),
                _row_copy(src, xs_ref.at[pl.ds(pos2_ref[base + r], 1)], sem))

    def start(r, carry):
        for cp in copies(r):
            cp.start()
        return carry

    def wait(r, carry):
        for cp in copies(r):
            cp.wait()
        return carry

    lax.fori_loop(0, tm, start, 0)
    lax.fori_loop(0, tm, wait, 0)


def dispatch(hp, pos1, pos2, xs0, tm=256):
    T, W = hp.shape
    n_rows = xs0.shape[0]
    tm = min(tm, T)
    return pl.pallas_call(
        _dispatch_kernel,
        grid_spec=pltpu.PrefetchScalarGridSpec(
            num_scalar_prefetch=2,
            grid=(T // tm,),
            in_specs=[pl.BlockSpec((tm, W), lambda i, p1, p2: (i, 0)),
                      pl.BlockSpec(memory_space=pl.ANY)],
            out_specs=pl.BlockSpec(memory_space=pl.ANY),
            scratch_shapes=[pltpu.SemaphoreType.DMA(())]),
        out_shape=jax.ShapeDtypeStruct((n_rows, W), hp.dtype),
        input_output_aliases={3: 0},
        compiler_params=_cparams(("arbitrary",), 32),
        name="moe_dispatch",
    )(pos1, pos2, hp, xs0)


def _experts_kernel(be_ref, br_ref, bf_ref, xs_ref, wg_ref, wu_ref, wd_ref, ys_ref, wg_s, wu_s, wd_s):
    del be_ref
    b = pl.program_id(0)

    @pl.when(br_ref[b] > 0)
    def _():
        @pl.when(bf_ref[b] == 1)
        def _():
            wg_s[...] = wg_ref[0].astype(BF16)
            wu_s[...] = wu_ref[0].astype(BF16)
            wd_s[...] = wd_ref[0].astype(BF16)

        a = xs_ref[...].astype(BF16)
        gate = jnp.dot(a, wg_s[...], preferred_element_type=F32)
        up = jnp.dot(a, wu_s[...], preferred_element_type=F32)
        mid = (jax.nn.silu(gate) * up).astype(BF16)
        ys_ref[...] = jnp.dot(mid, wd_s[...], preferred_element_type=F32)

    @pl.when(br_ref[b] == 0)
    def _():
        ys_ref[...] = jnp.zeros_like(ys_ref)


def experts(xs, blk_expert, blk_rows, blk_first, w_gate, w_up, w_down):
    B = MOE_BLOCK
    n_rows = xs.shape[0]
    E, D, FF = w_gate.shape
    nb = n_rows // B
    return pl.pallas_call(
        _experts_kernel,
        grid_spec=pltpu.PrefetchScalarGridSpec(
            num_scalar_prefetch=3,
            grid=(nb,),
            in_specs=[pl.BlockSpec((B, D), lambda b, be, br, bf: (b, 0)),
                      pl.BlockSpec((1, D, FF), lambda b, be, br, bf: (be[b], 0, 0)),
                      pl.BlockSpec((1, D, FF), lambda b, be, br, bf: (be[b], 0, 0)),
                      pl.BlockSpec((1, FF, D), lambda b, be, br, bf: (be[b], 0, 0))],
            out_specs=pl.BlockSpec((B, D), lambda b, be, br, bf: (b, 0)),
            scratch_shapes=[pltpu.VMEM((D, FF), BF16), pltpu.VMEM((D, FF), BF16),
                            pltpu.VMEM((FF, D), BF16)]),
        out_shape=jax.ShapeDtypeStruct((n_rows, D), F32),
        compiler_params=_cparams(("arbitrary",), 56),
        name="moe_experts",
    )(blk_expert, blk_rows, blk_first, xs, w_gate, w_up, w_down)


def _combine_kernel(pos1_ref, pos2_ref, x_ref, mf_ref, g_ref, ys_ref, xo_ref, ho_ref, buf, sem):
    tm = x_ref.shape[0]
    base = pl.program_id(0) * tm

    def copies(r):
        dst = pl.ds(r, 1)
        return (_row_copy(ys_ref.at[pl.ds(pos1_ref[base + r], 1)], buf.at[0, dst], sem),
                _row_copy(ys_ref.at[pl.ds(pos2_ref[base + r], 1)], buf.at[1, dst], sem))

    def start(r, carry):
        for cp in copies(r):
            cp.start()
        return carry

    def wait(r, carry):
        for cp in copies(r):
            cp.wait()
        return carry

    lax.fori_loop(0, tm, start, 0)
    lax.fori_loop(0, tm, wait, 0)
    mf = mf_ref[...]
    xn = x_ref[...] + mf[:, 0:1] * buf[0] + mf[:, 1:2] * buf[1]
    xo_ref[...] = xn
    ho_ref[...] = _rms(xn, g_ref[...]).astype(ho_ref.dtype)


def combine(x, ys, meta_f, pos1, pos2, g_next, h_dtype, tm=256):
    T, D = x.shape
    tm = min(tm, T)
    row = lambda w: pl.BlockSpec((tm, w), lambda i, p1, p2: (i, 0))
    return pl.pallas_call(
        _combine_kernel,
        grid_spec=pltpu.PrefetchScalarGridSpec(
            num_scalar_prefetch=2,
            grid=(T // tm,),
            in_specs=[row(D), row(LANES), pl.BlockSpec((1, D), lambda i, p1, p2: (0, 0)),
                      pl.BlockSpec(memory_space=pl.ANY)],
            out_specs=[row(D), row(D)],
            scratch_shapes=[pltpu.VMEM((2, tm, D), F32), pltpu.SemaphoreType.DMA(())]),
        out_shape=[jax.ShapeDtypeStruct((T, D), F32), jax.ShapeDtypeStruct((T, D), h_dtype)],
        compiler_params=_cparams(("arbitrary",), 40),
        name="moe_combine",
    )(pos1, pos2, x, meta_f, g_next.reshape(1, D), ys)


def moe_row_blocks(T):
    return (2 * T) // MOE_BLOCK + N_EXPERTS


def moe_layer(x, h, xs_buf, w_group, b_group, w_expert, b_expert, layer, w_gate, w_up, w_down, g_next,
              h_dtype):
    T, D = x.shape
    n_blocks = xs_buf.shape[0] // MOE_BLOCK
    pad = LANES - N_GROUPS - N_EXPERTS
    w_r = jnp.concatenate([w_group, w_expert.reshape(D, N_EXPERTS), jnp.zeros((D, pad), F32)], axis=1)
    b_r = jnp.concatenate([b_group, b_expert.reshape(N_EXPERTS), jnp.zeros((pad,), F32)]).reshape(1, LANES)
    meta_i, meta_f, counts = router(h, w_r, b_r)
    pos1, pos2, blk_expert, blk_rows, blk_first = moe_plan(meta_i, counts, n_blocks)
    xs = dispatch(h, pos1, pos2, xs_buf)
    FF = w_gate.shape[-1]
    ys = experts(xs, blk_expert + layer * N_EXPERTS, blk_rows, blk_first, w_gate.reshape(-1, D, FF),
                 w_up.reshape(-1, D, FF), w_down.reshape(-1, FF, D))
    xo, ho = combine(x, ys, meta_f, pos1, pos2, g_next, h_dtype)
    return xo, ho, xs


def _forget_kernel(h_ref, w_ref, b_ref, o_ref, run_ref):
    @pl.when(pl.program_id(0) == 0)
    def _():
        run_ref[...] = jnp.zeros_like(run_ref)

    hi = lax.Precision.HIGHEST
    z = jnp.dot(h_ref[...].astype(F32), w_ref[...], precision=hi, preferred_element_type=F32) + b_ref[...]
    log_f = jax.nn.log_sigmoid(z)
    tm = z.shape[0]
    r = lax.broadcasted_iota(I32, (tm, tm), 0)
    c = lax.broadcasted_iota(I32, (tm, tm), 1)
    upto = (c <= r).astype(F32)
    cum = jnp.dot(upto, log_f, precision=hi, preferred_element_type=F32) + run_ref[0:1, :]
    o_ref[...] = cum
    run_ref[...] = jnp.broadcast_to(cum[tm - 1:tm, :], run_ref.shape)


def forget_cumsum(h, w_f, b_f, tm=256):
    T, D = h.shape
    tm = min(tm, T)
    return pl.pallas_call(
        _forget_kernel,
        grid=(T // tm,),
        in_specs=[pl.BlockSpec((tm, D), lambda i: (i, 0)),
                  pl.BlockSpec((D, LANES), lambda i: (0, 0)),
                  pl.BlockSpec((1, LANES), lambda i: (0, 0))],
        out_specs=pl.BlockSpec((tm, LANES), lambda i: (i, 0)),
        out_shape=jax.ShapeDtypeStruct((T, LANES), F32),
        scratch_shapes=[pltpu.VMEM((8, LANES), F32)],
        compiler_params=_cparams(("arbitrary",), 40),
        name="forget_cumsum",
    )(h, w_f, b_f)


def _flash_kernel(q_ref, k_ref, v_ref, ck_ref, cq_ref, o_ref, m_ref, acc_ref, *, scale, sub):
    qi = pl.program_id(1)
    bq = q_ref.shape[0]
    dh = FOX_HEAD_DIM
    heads = q_ref.shape[1] // dh
    m_ref[...] = jnp.full_like(m_ref, -jnp.inf)
    acc_ref[...] = jnp.zeros_like(acc_ref)
    ones = jnp.ones((bq, dh), BF16)
    n_sub = bq // sub

    def chunk(start, diagonal):
        for hh in range(heads):
            cols = slice(hh * dh, (hh + 1) * dh)
            c0 = cq_ref[hh, :, 0:1]
            k = k_ref[pl.ds(start, bq), cols]
            v1 = jnp.concatenate([v_ref[pl.ds(start, bq), cols], ones], axis=1)
            s_all = lax.dot_general(q_ref[:, cols], k, (((1,), (1,)), ((), ())),
                                    preferred_element_type=F32)
            bias = (c0 - ck_ref[hh, :, pl.ds(start, bq)]) * LOG2E
            ps, alphas = [], []
            for r in range(n_sub):
                rows = slice(r * sub, (r + 1) * sub)
                s = s_all[rows, :] * (scale * LOG2E) + bias
                if diagonal:
                    row = lax.broadcasted_iota(I32, s.shape, 0) + r * sub
                    col = lax.broadcasted_iota(I32, s.shape, 1)
                    s = jnp.where(col <= row, s, -jnp.inf)
                m_prev = m_ref[hh, rows, :]
                m_new = jnp.maximum(m_prev, jnp.max(s, axis=1, keepdims=True))
                m_ref[hh, rows, :] = m_new
                ps.append(jnp.exp2(s - jnp.concatenate([m_new] * (bq // LANES), axis=1)).astype(BF16))
                alphas.append(jnp.exp2(m_prev - m_new))
            pv = jnp.dot(jnp.concatenate(ps, axis=0), v1, preferred_element_type=F32)
            for r in range(n_sub):
                rows = slice(r * sub, (r + 1) * sub)
                acc_ref[hh, rows, :] = (jnp.concatenate([alphas[r]] * 2, axis=1) * acc_ref[hh, rows, :]
                                        + pv[rows, :])

    def body(j, carry):
        chunk(pl.multiple_of(j * bq, bq), False)
        return carry

    lax.fori_loop(0, qi, body, 0)
    chunk(pl.multiple_of(qi * bq, bq), True)
    for hh in range(heads):
        o_ref[:, hh * dh:(hh + 1) * dh] = (acc_ref[hh, :, 0:dh] / acc_ref[hh, :, dh:2 * dh]).astype(o_ref.dtype)


def flash_attention(qkv, cum_t, bq=512, sub=64, heads=2):
    T = qkv.shape[0]
    H, dh = FOX_HEADS, FOX_HEAD_DIM
    bq = min(bq, T)
    kern = functools.partial(_flash_kernel, scale=dh ** -0.5, sub=min(sub, bq))
    hw = heads * dh
    nhb = H // heads
    return pl.pallas_call(
        kern,
        grid=(nhb, T // bq),
        in_specs=[pl.BlockSpec((bq, hw), lambda h, i: (i, h)),
                  pl.BlockSpec((T, hw), lambda h, i: (0, nhb + h)),
                  pl.BlockSpec((T, hw), lambda h, i: (0, 2 * nhb + h)),
                  pl.BlockSpec((heads, 1, T), lambda h, i: (h, 0, 0)),
                  pl.BlockSpec((heads, 1, bq), lambda h, i: (h, 0, i))],
        out_specs=pl.BlockSpec((bq, hw), lambda h, i: (i, h)),
        out_shape=jax.ShapeDtypeStruct((T, H * dh), BF16),
        scratch_shapes=[pltpu.VMEM((heads, bq, LANES), F32), pltpu.VMEM((heads, bq, 2 * dh), F32)],
        compiler_params=_cparams(("parallel", "arbitrary"), 48),
        name="fox_attention",
    )(qkv, qkv, qkv, cum_t, cum_t)


def kernel(x, ab_norm, ab_w_in, ab_conv_w, s5_lambda_re, s5_lambda_im, s5_log_dt, s5_b_re, s5_b_im,
           s5_c_re, s5_c_im, s5_d, s5_w_glu, s5_b_glu, ab_w_out, c_norm, c_w_in, c_b_forget, c_w_out,
           ffn_norm, router_w_group, router_b_group, router_w_expert, router_b_expert,
           moe_w_gate, moe_w_up, moe_w_down, final_norm):
    bsz, L, D = x.shape
    depth = ffn_norm.shape[0]
    cw = ab_conv_w.shape[-1]
    xt = x.reshape(bsz * L, D)
    h = None
    xs_buf = jnp.zeros((moe_row_blocks(bsz * L) * MOE_BLOCK, D), F32)
    for i in range(depth):
        j = i // 2
        if i % 2 == 0:
            g_in = ab_norm[j]
            if h is not None:
                proj = matmul(h, ab_w_in[j].astype(BF16), F32)
            else:
                proj = norm_matmul(xt, g_in, ab_w_in[j].astype(BF16), F32)
            y_conv = conv_mixer(proj, ab_conv_w[j])
            y_gelu = s5_mixer(proj, 3 * cw, s5_lambda_re[j], s5_lambda_im[j], s5_log_dt[j],
                              s5_b_re[j], s5_b_im[j], s5_c_re[j], s5_c_im[j], s5_d[j])
            xt, hf = glu_out(xt, y_conv, y_gelu, s5_w_glu[j].astype(BF16), s5_b_glu[j],
                             ab_w_out[j].astype(BF16), ffn_norm[i])
        else:
            assert h is not None, "an attention layer always follows a MoE combine that emits its norm"
            hd = FOX_HEADS * FOX_HEAD_DIM
            qkv = matmul(h, c_w_in[j][:, :3 * hd].astype(BF16), BF16)
            w_f = jnp.pad(c_w_in[j][:, 3 * hd:], ((0, 0), (0, LANES - FOX_HEADS)))
            b_f = jnp.pad(c_b_forget[j], (0, LANES - FOX_HEADS)).reshape(1, LANES)
            cum = forget_cumsum(h, w_f, b_f)
            cum_t = cum[:, :FOX_HEADS].T.reshape(FOX_HEADS, 1, bsz * L)
            att = flash_attention(qkv, cum_t)
            xt, hf = proj_residual(xt, att, c_w_out[j].astype(BF16), ffn_norm[i])
        last = i == depth - 1
        if last:
            g_next = final_norm
        elif (i + 1) % 2 == 0:
            g_next = ab_norm[(i + 1) // 2]
        else:
            g_next = c_norm[(i + 1) // 2]
        xt, h, xs_buf = moe_layer(xt, hf, xs_buf, router_w_group[i], router_b_group[i], router_w_expert[i],
                                  router_b_expert[i], i, moe_w_gate, moe_w_up, moe_w_down,
                                  g_next, F32 if last else BF16)
    return h.reshape(bsz, L, D)
```

```python
import functools
import math

import jax
import jax.numpy as jnp
from jax import lax
from jax.experimental import pallas as pl
from jax.experimental.pallas import tpu as pltpu

F32 = jnp.float32
BF16 = jnp.bfloat16
I32 = jnp.int32
U32 = jnp.uint32

RMS_EPS = 1e-6
LANES = 128
LOG2E = math.log2(math.e)
MIB = 1024 * 1024

CONV_K = 3
S5_GROUP = 16
S5_STATE = 64
S5_CHUNK = 16
S5_OCT = LANES // S5_GROUP
FOX_HEADS = 16
FOX_HEAD_DIM = 128
N_GROUPS = 4
EXPERTS_PER_GROUP = 8
N_EXPERTS = N_GROUPS * EXPERTS_PER_GROUP
EXPERT_LANE0 = N_GROUPS
MOE_BLOCK = 256
MOE_CODE_BITS = 16
MOE_CODE = 1 << MOE_CODE_BITS


def _cparams(sem, vmem_mib):
    return pltpu.CompilerParams(dimension_semantics=sem, vmem_limit_bytes=vmem_mib * MIB)


def _rms(x, g):
    ms = jnp.mean(x * x, axis=-1, keepdims=True)
    return x * lax.rsqrt(ms + RMS_EPS) * g


def _norm_matmul_kernel(x_ref, g_ref, w_ref, o_ref, h_ref):
    @pl.when(pl.program_id(1) == 0)
    def _():
        h_ref[...] = _rms(x_ref[...], g_ref[...]).astype(BF16)

    o_ref[...] = jnp.dot(h_ref[...], w_ref[...], preferred_element_type=F32).astype(o_ref.dtype)


def norm_matmul(x, g, w, out_dtype, tm=1024, tn=1024):
    T, D = x.shape
    N = w.shape[1]
    tm, tn = min(tm, T), min(tn, N)
    return pl.pallas_call(
        _norm_matmul_kernel,
        grid=(T // tm, N // tn),
        in_specs=[pl.BlockSpec((tm, D), lambda i, j: (i, 0)),
                  pl.BlockSpec((1, D), lambda i, j: (0, 0)),
                  pl.BlockSpec((D, tn), lambda i, j: (0, j))],
        out_specs=pl.BlockSpec((tm, tn), lambda i, j: (i, j)),
        out_shape=jax.ShapeDtypeStruct((T, N), out_dtype),
        scratch_shapes=[pltpu.VMEM((tm, D), BF16)],
        compiler_params=_cparams(("parallel", "arbitrary"), 56),
        name="norm_matmul",
    )(x, g.reshape(1, D), w)


def _matmul_kernel(a_ref, w_ref, o_ref):
    o_ref[...] = jnp.dot(a_ref[...], w_ref[...], preferred_element_type=F32).astype(o_ref.dtype)


def matmul(a, w, out_dtype, tm=1024, tn=1024):
    T, K = a.shape
    N = w.shape[1]
    tm, tn = min(tm, T), min(tn, N)
    return pl.pallas_call(
        _matmul_kernel,
        grid=(T // tm, N // tn),
        in_specs=[pl.BlockSpec((tm, K), lambda i, j: (i, 0)),
                  pl.BlockSpec((K, tn), lambda i, j: (0, j))],
        out_specs=pl.BlockSpec((tm, tn), lambda i, j: (i, j)),
        out_shape=jax.ShapeDtypeStruct((T, N), out_dtype),
        compiler_params=_cparams(("parallel", "arbitrary"), 48),
        name="matmul",
    )(a, w)


def _conv_kernel(gb_ref, gc_ref, u_ref, gcp_ref, up_ref, w_ref, o_ref):
    w0, w1, w2 = w_ref[0:1, :], w_ref[1:2, :], w_ref[2:3, :]
    v = gc_ref[...] * u_ref[...]
    y = w2 * v + w1 * pltpu.roll(v, 1, 0) + w0 * pltpu.roll(v, 2, 0)
    o_ref[...] = (gb_ref[...] * y).astype(o_ref.dtype)
    vp = gcp_ref[...] * up_ref[...]
    vp = jnp.where(pl.program_id(0) > 0, vp, jnp.zeros_like(vp))
    v8 = v[0:8, :]
    row = lax.broadcasted_iota(I32, v8.shape, 0)
    v1 = jnp.where(row < 1, pltpu.roll(vp, 1, 0), pltpu.roll(v8, 1, 0))
    v2 = jnp.where(row < 2, pltpu.roll(vp, 2, 0), pltpu.roll(v8, 2, 0))
    o_ref[0:8, :] = (gb_ref[0:8, :] * (w2 * v8 + w1 * v1 + w0 * v2)).astype(o_ref.dtype)


def conv_mixer(proj, conv_w, tm=512):
    T = proj.shape[0]
    CW = conv_w.shape[1]
    tm = min(tm, T)
    r8 = tm // 8
    cur = lambda c: pl.BlockSpec((tm, CW), lambda i: (i, c))
    prev = lambda c: pl.BlockSpec((8, CW), lambda i: (jnp.maximum(i * r8 - 1, 0), c))
    return pl.pallas_call(
        _conv_kernel,
        grid=(T // tm,),
        in_specs=[cur(0), cur(1), cur(2), prev(1), prev(2),
                  pl.BlockSpec((CONV_K, CW), lambda i: (0, 0))],
        out_specs=pl.BlockSpec((tm, CW), lambda i: (i, 0)),
        out_shape=jax.ShapeDtypeStruct((T, CW), BF16),
        compiler_params=_cparams(("parallel",), 40),
        name="conv_mixer",
    )(proj, proj, proj, proj, proj, conv_w)


def _cmul(a, b):
    return a[0] * b[0] - a[1] * b[1], a[0] * b[1] + a[1] * b[0]


def s5_matrices(lam_re, lam_im, log_dt, b_re, b_im, c_re, c_im):
    C = S5_CHUNK
    dt = jnp.exp(log_dt)[:, None]
    a, b = lam_re * dt, lam_im * dt
    mag = jnp.exp(a)
    lbar = (mag * jnp.cos(b), mag * jnp.sin(b))
    den = lam_re * lam_re + lam_im * lam_im
    inv_lam = (lam_re / den, -lam_im / den)
    coef = _cmul((lbar[0] - 1.0, lbar[1]), inv_lam)
    bbar = _cmul((coef[0][..., None], coef[1][..., None]), (b_re, b_im))
    j = jnp.arange(C + 1, dtype=F32)[None, :, None]
    pmag = jnp.exp(a[:, None, :] * j)
    pw = (pmag * jnp.cos(b[:, None, :] * j), pmag * jnp.sin(b[:, None, :] * j))
    hi = lax.Precision.HIGHEST
    cp = _cmul((c_re[:, None], c_im[:, None]), (pw[0][:, :C, None, :], pw[1][:, :C, None, :]))
    kj = (jnp.einsum('gjhn,gnk->gjhk', cp[0], bbar[0], precision=hi)
          - jnp.einsum('gjhn,gnk->gjhk', cp[1], bbar[1], precision=hi))
    return pw, bbar, kj


def _lane_tile_blockdiag(m):
    Q, O, r, c = m.shape
    out = jnp.zeros((Q, O, r, O, c), m.dtype)
    for g in range(O):
        out = out.at[:, g, :, g, :].set(m[:, g])
    return out.reshape(Q, O * r, O * c)


def s5_lane_tile_operands(lam_re, lam_im, log_dt, b_re, b_im, c_re, c_im):
    C, H, N, O = S5_CHUNK, S5_GROUP, S5_STATE, S5_OCT
    pw, bbar, kj = s5_matrices(lam_re, lam_im, log_dt, b_re, b_im, c_re, c_im)
    Q = lam_re.shape[0] // O
    pw_q = jnp.stack([p.reshape(Q, O, C + 1, N).transpose(0, 2, 1, 3).reshape(Q, C + 1, O * N) for p in pw],
                     axis=1)
    bb = jnp.stack([_lane_tile_blockdiag(m.transpose(0, 2, 1).reshape(Q, O, H, N)) for m in bbar], axis=1)
    cc = jnp.stack([_lane_tile_blockdiag(m.reshape(Q, O, H, N)) for m in (c_re, c_im)], axis=1)
    kq = kj.reshape(Q, O, C, H, H).transpose(0, 2, 1, 4, 3).reshape(Q * C, O, H, H)
    bd = _lane_tile_blockdiag(kq).reshape(Q, C, O * H, O * H).astype(BF16)
    return pw_q, bb, cc, bd


def _s5_kernel(u_ref, pw_ref, bb_ref, cc_ref, kj_ref, d_ref, o_ref, t_ref, w_ref, v_ref, uf_ref, s_ref,
               x_ref, c_ref):
    C = S5_CHUNK
    tb = u_ref.shape[0]
    nc = tb // C
    sw = pw_ref.shape[3]

    @pl.when((pl.program_id(0) == 0) & (pl.program_id(1) == 0))
    def _():
        t_ref[...] = jnp.zeros_like(t_ref)

    @pl.when(pl.program_id(1) == 0)
    def _():
        bb = (bb_ref[0, 0], bb_ref[0, 1])
        cc = (cc_ref[0, 0], cc_ref[0, 1])
        for j in range(C):
            a = _cmul(bb, (pw_ref[0, 0, j:j + 1, :], pw_ref[0, 1, j:j + 1, :]))
            s = C - 1 - j
            w_ref[s * LANES:(s + 1) * LANES, 0:sw] = a[0].astype(BF16)
            w_ref[s * LANES:(s + 1) * LANES, sw:2 * sw] = a[1].astype(BF16)
            for s0 in range(C - j):
                t_ref[s0 * LANES:(s0 + 1) * LANES, (s0 + j) * LANES:(s0 + j + 1) * LANES] = kj_ref[0, j]
            g = _cmul(cc, (pw_ref[0, 0, j + 1:j + 2, :], pw_ref[0, 1, j + 1:j + 2, :]))
            v_ref[j * LANES:(j + 1) * LANES, 0:sw] = g[0].astype(BF16)
            v_ref[j * LANES:(j + 1) * LANES, sw:2 * sw] = (-g[1]).astype(BF16)
        c_ref[...] = jnp.zeros_like(c_ref)

    for s in range(C):
        uf_ref[:, s * LANES:(s + 1) * LANES] = u_ref[pl.ds(s, nc, stride=C), :].astype(BF16)
    s_ref[...] = jnp.dot(uf_ref[...], w_ref[...], preferred_element_type=F32)
    lr, li = pw_ref[0, 0, C:C + 1, :], pw_ref[0, 1, C:C + 1, :]
    row = lax.broadcasted_iota(I32, (8, sw), 0)

    def tile_step(i, carry):
        xr, xi = carry
        r0 = pl.multiple_of(i * 8, 8)
        sr = s_ref[pl.ds(r0, 8), 0:sw]
        si = s_ref[pl.ds(r0, 8), sw:2 * sw]
        tr = jnp.zeros((8, sw), F32)
        ti = jnp.zeros((8, sw), F32)
        for r in range(8):
            tr = jnp.where(row == r, xr, tr)
            ti = jnp.where(row == r, xi, ti)
            xr, xi = lr * xr - li * xi + sr[r:r + 1, :], lr * xi + li * xr + si[r:r + 1, :]
        x_ref[pl.ds(r0, 8), 0:sw] = tr
        x_ref[pl.ds(r0, 8), sw:2 * sw] = ti
        return xr, xi

    xr, xi = lax.fori_loop(0, nc // 8, tile_step, (c_ref[0:1, 0:sw], c_ref[0:1, sw:2 * sw]))
    c_ref[0:1, 0:sw] = xr
    c_ref[0:1, sw:2 * sw] = xi
    y_state = lax.dot_general(x_ref[...].astype(BF16), v_ref[...], (((1,), (1,)), ((), ())),
                              preferred_element_type=F32)
    quarter = C // 4
    for qt in range(4):
        rows = (qt + 1) * quarter * LANES
        cols = slice(qt * quarter * LANES, (qt + 1) * quarter * LANES)
        y = jnp.dot(uf_ref[:, 0:rows], t_ref[0:rows, cols], preferred_element_type=F32) + y_state[:, cols]
        for s in range(quarter):
            st = qt * quarter + s
            ys = y[:, s * LANES:(s + 1) * LANES] + d_ref[...] * u_ref[pl.ds(st, nc, stride=C), :]
            o_ref[pl.ds(st, nc, stride=C), :] = jax.nn.gelu(ys)


def s5_mixer(proj, col0, lam_re, lam_im, log_dt, b_re, b_im, c_re, c_im, d, tb=8192):
    T = proj.shape[0]
    C, N, O = S5_CHUNK, S5_STATE, S5_OCT
    W = d.shape[0]
    Q = W // LANES
    tb = min(tb, T)
    pw_q, bb, cc, bd = s5_lane_tile_operands(lam_re, lam_im, log_dt, b_re, b_im, c_re, c_im)
    cb0 = col0 // LANES
    nc = tb // C
    sw = O * N
    quad = lambda a: pl.BlockSpec((1,) + a.shape[1:], lambda q, t: (q, 0, 0, 0))
    return pl.pallas_call(
        _s5_kernel,
        grid=(Q, T // tb),
        in_specs=[pl.BlockSpec((tb, LANES), lambda q, t: (t, cb0 + q)),
                  quad(pw_q), quad(bb), quad(cc), quad(bd),
                  pl.BlockSpec((1, LANES), lambda q, t: (0, q))],
        out_specs=pl.BlockSpec((tb, LANES), lambda q, t: (t, q)),
        out_shape=jax.ShapeDtypeStruct((T, W), F32),
        scratch_shapes=[pltpu.VMEM((C * LANES, C * LANES), BF16),
                        pltpu.VMEM((C * LANES, 2 * sw), BF16),
                        pltpu.VMEM((C * LANES, 2 * sw), BF16),
                        pltpu.VMEM((nc, C * LANES), BF16),
                        pltpu.VMEM((nc, 2 * sw), F32),
                        pltpu.VMEM((nc, 2 * sw), F32),
                        pltpu.VMEM((8, 2 * sw), F32)],
        compiler_params=_cparams(("arbitrary", "arbitrary"), 56),
        name="s5_scan",
    )(proj, pw_q, bb, cc, bd, d.reshape(1, W))


def _glu_out_kernel(x_ref, yc_ref, yg_ref, wglu_ref, bglu_ref, wout_ref, g_ref, xo_ref, ho_ref):
    yg = yg_ref[...]
    z = jnp.dot(yg.astype(BF16), wglu_ref[...], preferred_element_type=F32) + bglu_ref[...]
    ys = (yg * jax.nn.sigmoid(z)).astype(BF16)
    cw = yc_ref.shape[1]
    xn = (x_ref[...]
          + jnp.dot(yc_ref[...], wout_ref[0:cw, :], preferred_element_type=F32)
          + jnp.dot(ys, wout_ref[cw:, :], preferred_element_type=F32))
    xo_ref[...] = xn
    ho_ref[...] = _rms(xn, g_ref[...])


def glu_out(x, y_conv, y_gelu, w_glu, b_glu, w_out, g_next, tm=256):
    T, D = x.shape
    CW, SW = y_conv.shape[1], y_gelu.shape[1]
    tm = min(tm, T)
    row = lambda w: pl.BlockSpec((tm, w), lambda i: (i, 0))
    full = lambda a: pl.BlockSpec(a.shape, lambda i: (0,) * a.ndim)
    bg, g2 = b_glu.reshape(1, SW), g_next.reshape(1, D)
    return pl.pallas_call(
        _glu_out_kernel,
        grid=(T // tm,),
        in_specs=[row(D), row(CW), row(SW), full(w_glu), full(bg), full(w_out), full(g2)],
        out_specs=[row(D), row(D)],
        out_shape=[jax.ShapeDtypeStruct((T, D), F32), jax.ShapeDtypeStruct((T, D), F32)],
        compiler_params=_cparams(("parallel",), 56),
        name="glu_out",
    )(x, y_conv, y_gelu, w_glu, bg, w_out, g2)


def _proj_residual_kernel(x_ref, a_ref, w_ref, g_ref, xo_ref, ho_ref):
    xn = x_ref[...] + jnp.dot(a_ref[...], w_ref[...], preferred_element_type=F32)
    xo_ref[...] = xn
    ho_ref[...] = _rms(xn, g_ref[...])


def proj_residual(x, a, w, g_next, tm=256):
    T, D = x.shape
    K = a.shape[1]
    tm = min(tm, T)
    row = lambda w_: pl.BlockSpec((tm, w_), lambda i: (i, 0))
    full = lambda arr: pl.BlockSpec(arr.shape, lambda i: (0,) * arr.ndim)
    g2 = g_next.reshape(1, D)
    return pl.pallas_call(
        _proj_residual_kernel,
        grid=(T // tm,),
        in_specs=[row(D), row(K), full(w), full(g2)],
        out_specs=[row(D), row(D)],
        out_shape=[jax.ShapeDtypeStruct((T, D), F32), jax.ShapeDtypeStruct((T, D), F32)],
        compiler_params=_cparams(("parallel",), 56),
        name="proj_residual",
    )(x, a, w, g2)


def _router_kernel(h_ref, w_ref, b_ref, mi_ref, mf_ref, cnt_ref, run_ref):
    i = pl.program_id(0)

    @pl.when(i == 0)
    def _():
        run_ref[...] = jnp.zeros_like(run_ref)

    h = h_ref[...]
    tm, D = h.shape
    logits = jnp.dot(h, w_ref[...], precision=lax.Precision.HIGHEST,
                     preferred_element_type=F32) + b_ref[...]
    lane = lax.broadcasted_iota(I32, logits.shape, 1)
    neg = jnp.float32(-jnp.inf)
    gl = jnp.where(lane < N_GROUPS, logits, neg)
    gmax = jnp.max(gl, axis=1, keepdims=True)
    gsum = jnp.sum(jnp.where(lane < N_GROUPS, jnp.exp(gl - gmax), 0.0), axis=1, keepdims=True)
    gw = 1.0 / gsum
    gidx = jnp.min(jnp.where(gl == gmax, lane, LANES), axis=1, keepdims=True)
    lo = EXPERT_LANE0 + EXPERTS_PER_GROUP * gidx
    el = jnp.where((lane >= lo) & (lane < lo + EXPERTS_PER_GROUP), logits, neg)
    v1 = jnp.max(el, axis=1, keepdims=True)
    i1 = jnp.min(jnp.where(el == v1, lane, LANES), axis=1, keepdims=True)
    el2 = jnp.where(lane == i1, neg, el)
    v2 = jnp.max(el2, axis=1, keepdims=True)
    i2 = jnp.min(jnp.where(el2 == v2, lane, LANES), axis=1, keepdims=True)
    t = jnp.exp(v2 - v1)
    w1 = gw / (1.0 + t)
    w2 = gw * t / (1.0 + t)
    hit1 = lane == i1
    hit2 = lane == i2
    cnt = (hit1 | hit2).astype(BF16)
    r = lax.broadcasted_iota(I32, (tm, tm), 0)
    c = lax.broadcasted_iota(I32, (tm, tm), 1)
    before = (c < r).astype(BF16)
    cum = jnp.dot(before, cnt, preferred_element_type=F32) + run_ref[0:1, :]
    rank1 = jnp.sum(jnp.where(hit1, cum, 0.0), axis=1, keepdims=True).astype(I32)
    rank2 = jnp.sum(jnp.where(hit2, cum, 0.0), axis=1, keepdims=True).astype(I32)
    run = run_ref[0:1, :] + jnp.sum(cnt.astype(F32), axis=0, keepdims=True)
    run_ref[...] = jnp.broadcast_to(run, run_ref.shape)
    cnt_ref[...] = jnp.broadcast_to(run, cnt_ref.shape)
    code1 = (i1 - EXPERT_LANE0) * MOE_CODE + rank1
    code2 = (i2 - EXPERT_LANE0) * MOE_CODE + rank2
    mi_ref[...] = jnp.where(lane == 0, code1, jnp.where(lane == 1, code2, 0))
    mf_ref[...] = jnp.where(lane == 0, w1, jnp.where(lane == 1, w2, 0.0))


def router(h, w_r, b_r, tm=256):
    T, D = h.shape
    tm = min(tm, T)
    row = lambda w: pl.BlockSpec((tm, w), lambda i: (i, 0))
    full = lambda a: pl.BlockSpec(a.shape, lambda i: (0,) * a.ndim)
    return pl.pallas_call(
        _router_kernel,
        grid=(T // tm,),
        in_specs=[row(D), full(w_r), full(b_r)],
        out_specs=[row(LANES), row(LANES), pl.BlockSpec((8, LANES), lambda i: (0, 0))],
        out_shape=[jax.ShapeDtypeStruct((T, LANES), I32), jax.ShapeDtypeStruct((T, LANES), F32),
                   jax.ShapeDtypeStruct((8, LANES), F32)],
        scratch_shapes=[pltpu.VMEM((8, LANES), F32)],
        compiler_params=_cparams(("arbitrary",), 40),
        name="moe_router",
    )(h, w_r, b_r)


def moe_plan(meta_i, counts, n_blocks):
    B = MOE_BLOCK
    code1, code2 = meta_i[:, 0], meta_i[:, 1]
    cnt = counts[0, EXPERT_LANE0:EXPERT_LANE0 + N_EXPERTS].astype(I32)
    nblk = (cnt + B - 1) // B
    blk_end = jnp.cumsum(nblk)
    blk_off = blk_end - nblk
    b = jnp.arange(n_blocks, dtype=I32)
    total = blk_end[-1]
    owner = jnp.minimum(jnp.sum((blk_end[None, :] <= b[:, None]).astype(I32), axis=1), N_EXPERTS - 1)
    valid = b < total
    last = jnp.maximum(total - 1, 0)
    blk_expert = jnp.where(valid, owner, owner[last])
    blk_src = jnp.where(valid, b, last)
    blk_first = (valid & (b == blk_off[owner])).astype(I32)
    return code1, code2, blk_off * B, blk_expert, blk_src, valid.astype(I32), blk_first


def _row_copy(src, dst, sem):
    return pltpu.make_async_copy(src, dst, sem)


def _sorted_row(code, off_ref):
    return off_ref[code >> MOE_CODE_BITS] + (code & (MOE_CODE - 1))


def _dispatch_kernel(code1_ref, code2_ref, off_ref, hp_ref, xs_in_ref, xs_ref, sem):
    del xs_in_ref
    tm = hp_ref.shape[0]
    base = pl.program_id(0) * tm

    def copies(r):
        src = hp_ref.at[pl.ds(r, 1)]
        return (_row_copy(src, xs_ref.at[pl.ds(_sorted_row(code1_ref[base + r], off_ref), 1)], sem),
                _row_copy(src, xs_ref.at[pl.ds(_sorted_row(code2_ref[base + r], off_ref), 1)], sem))

    def start(r, carry):
        for cp in copies(r):
            cp.start()
        return carry

    lax.fori_loop(0, tm, start, 0)
    for _ in range(2):
        _row_copy(hp_ref, xs_ref.at[pl.ds(0, tm)], sem).wait()


def dispatch(hp, code1, code2, row_off, xs0, tm=256):
    T, W = hp.shape
    n_rows = xs0.shape[0]
    tm = min(tm, T)
    return pl.pallas_call(
        _dispatch_kernel,
        grid_spec=pltpu.PrefetchScalarGridSpec(
            num_scalar_prefetch=3,
            grid=(T // tm,),
            in_specs=[pl.BlockSpec((tm, W), lambda i, c1, c2, off: (i, 0)),
                      pl.BlockSpec(memory_space=pl.ANY)],
            out_specs=pl.BlockSpec(memory_space=pl.ANY),
            scratch_shapes=[pltpu.SemaphoreType.DMA(())]),
        out_shape=jax.ShapeDtypeStruct((n_rows, W), hp.dtype),
        input_output_aliases={4: 0},
        compiler_params=_cparams(("arbitrary",), 32),
        name="moe_dispatch",
    )(code1, code2, row_off, hp, xs0)


def _experts_kernel(be_ref, bs_ref, bv_ref, bf_ref, xs_ref, wg_ref, wu_ref, wd_ref, ys_ref, wg_s, wu_s, wd_s):
    del be_ref, bs_ref
    b = pl.program_id(0)

    @pl.when(bv_ref[b] == 0)
    def _():
        ys_ref[...] = jnp.zeros_like(ys_ref)

    @pl.when(bv_ref[b] > 0)
    def _():
        @pl.when(bf_ref[b] == 1)
        def _():
            wg_s[...] = wg_ref[0].astype(BF16)
            wu_s[...] = wu_ref[0].astype(BF16)
            wd_s[...] = wd_ref[0].astype(BF16)

        a = xs_ref[...].astype(BF16)
        gate = jnp.dot(a, wg_s[...], preferred_element_type=F32)
        up = jnp.dot(a, wu_s[...], preferred_element_type=F32)
        mid = (jax.nn.silu(gate) * up).astype(BF16)
        ys_ref[...] = jnp.dot(mid, wd_s[...], preferred_element_type=F32)


def experts(xs, blk_expert, blk_src, blk_valid, blk_first, w_gate, w_up, w_down):
    B = MOE_BLOCK
    n_rows = xs.shape[0]
    E, D, FF = w_gate.shape
    nb = n_rows // B
    rows = pl.BlockSpec((B, D), lambda b, be, bs, bv, bf: (bs[b], 0))
    return pl.pallas_call(
        _experts_kernel,
        grid_spec=pltpu.PrefetchScalarGridSpec(
            num_scalar_prefetch=4,
            grid=(nb,),
            in_specs=[rows,
                      pl.BlockSpec((1, D, FF), lambda b, be, bs, bv, bf: (be[b], 0, 0)),
                      pl.BlockSpec((1, D, FF), lambda b, be, bs, bv, bf: (be[b], 0, 0)),
                      pl.BlockSpec((1, FF, D), lambda b, be, bs, bv, bf: (be[b], 0, 0))],
            out_specs=pl.BlockSpec((B, D), lambda b, be, bs, bv, bf: (b, 0)),
            scratch_shapes=[pltpu.VMEM((D, FF), BF16), pltpu.VMEM((D, FF), BF16),
                            pltpu.VMEM((FF, D), BF16)]),
        out_shape=jax.ShapeDtypeStruct((n_rows, D), F32),
        compiler_params=_cparams(("arbitrary",), 56),
        name="moe_experts",
    )(blk_expert, blk_src, blk_valid, blk_first, xs, w_gate, w_up, w_down)


def _combine_kernel(code1_ref, code2_ref, off_ref, x_ref, mf_ref, g_ref, ys_ref, xo_ref, ho_ref, buf, sem):
    tm = x_ref.shape[0]
    base = pl.program_id(0) * tm

    def copies(r):
        dst = pl.ds(r, 1)
        return (_row_copy(ys_ref.at[pl.ds(_sorted_row(code1_ref[base + r], off_ref), 1)], buf.at[0, dst], sem),
                _row_copy(ys_ref.at[pl.ds(_sorted_row(code2_ref[base + r], off_ref), 1)], buf.at[1, dst], sem))

    def start(r, carry):
        for cp in copies(r):
            cp.start()
        return carry

    lax.fori_loop(0, tm, start, 0)
    for k in range(2):
        _row_copy(ys_ref.at[pl.ds(0, tm)], buf.at[k], sem).wait()
    mf = mf_ref[...]
    xn = x_ref[...] + mf[:, 0:1] * buf[0] + mf[:, 1:2] * buf[1]
    xo_ref[...] = xn
    ho_ref[...] = _rms(xn, g_ref[...]).astype(ho_ref.dtype)


def combine(x, ys, meta_f, code1, code2, row_off, g_next, h_dtype, tm=256):
    T, D = x.shape
    tm = min(tm, T)
    row = lambda w: pl.BlockSpec((tm, w), lambda i, c1, c2, off: (i, 0))
    return pl.pallas_call(
        _combine_kernel,
        grid_spec=pltpu.PrefetchScalarGridSpec(
            num_scalar_prefetch=3,
            grid=(T // tm,),
            in_specs=[row(D), row(LANES), pl.BlockSpec((1, D), lambda i, c1, c2, off: (0, 0)),
                      pl.BlockSpec(memory_space=pl.ANY)],
            out_specs=[row(D), row(D)],
            scratch_shapes=[pltpu.VMEM((2, tm, D), F32), pltpu.SemaphoreType.DMA(())]),
        out_shape=[jax.ShapeDtypeStruct((T, D), F32), jax.ShapeDtypeStruct((T, D), h_dtype)],
        compiler_params=_cparams(("arbitrary",), 40),
        name="moe_combine",
    )(code1, code2, row_off, x, meta_f, g_next.reshape(1, D), ys)


def moe_row_blocks(T):
    return (2 * T) // MOE_BLOCK + N_EXPERTS


def moe_layer(x, h, xs_buf, w_group, b_group, w_expert, b_expert, layer, w_gate, w_up, w_down, g_next,
              h_dtype):
    T, D = x.shape
    assert T <= MOE_CODE, "ranks inside one expert must fit the packed (expert, rank) code"
    n_blocks = xs_buf.shape[0] // MOE_BLOCK
    pad = LANES - N_GROUPS - N_EXPERTS
    w_r = jnp.concatenate([w_group, w_expert.reshape(D, N_EXPERTS), jnp.zeros((D, pad), F32)], axis=1)
    b_r = jnp.concatenate([b_group, b_expert.reshape(N_EXPERTS), jnp.zeros((pad,), F32)]).reshape(1, LANES)
    meta_i, meta_f, counts = router(h, w_r, b_r)
    code1, code2, row_off, blk_expert, blk_src, blk_valid, blk_first = moe_plan(meta_i, counts, n_blocks)
    xs = dispatch(h, code1, code2, row_off, xs_buf)
    FF = w_gate.shape[-1]
    ys = experts(xs, blk_expert + layer * N_EXPERTS, blk_src, blk_valid, blk_first,
                 w_gate.reshape(-1, D, FF), w_up.reshape(-1, D, FF), w_down.reshape(-1, FF, D))
    xo, ho = combine(x, ys, meta_f, code1, code2, row_off, g_next, h_dtype)
    return xo, ho, xs


def _forget_kernel(h_ref, w_ref, b_ref, o_ref, run_ref):
    @pl.when(pl.program_id(0) == 0)
    def _():
        run_ref[...] = jnp.zeros_like(run_ref)

    hi = lax.Precision.HIGHEST
    z = jnp.dot(h_ref[...].astype(F32), w_ref[...], precision=hi, preferred_element_type=F32) + b_ref[...]
    log_f = jax.nn.log_sigmoid(z)
    tm = z.shape[0]
    r = lax.broadcasted_iota(I32, (tm, tm), 0)
    c = lax.broadcasted_iota(I32, (tm, tm), 1)
    upto = (c <= r).astype(F32)
    cum = jnp.dot(upto, log_f, precision=hi, preferred_element_type=F32) + run_ref[0:1, :]
    o_ref[...] = cum
    run_ref[...] = jnp.broadcast_to(cum[tm - 1:tm, :], run_ref.shape)


def forget_cumsum(h, w_f, b_f, tm=256):
    T, D = h.shape
    tm = min(tm, T)
    return pl.pallas_call(
        _forget_kernel,
        grid=(T // tm,),
        in_specs=[pl.BlockSpec((tm, D), lambda i: (i, 0)),
                  pl.BlockSpec((D, LANES), lambda i: (0, 0)),
                  pl.BlockSpec((1, LANES), lambda i: (0, 0))],
        out_specs=pl.BlockSpec((tm, LANES), lambda i: (i, 0)),
        out_shape=jax.ShapeDtypeStruct((T, LANES), F32),
        scratch_shapes=[pltpu.VMEM((8, LANES), F32)],
        compiler_params=_cparams(("arbitrary",), 40),
        name="forget_cumsum",
    )(h, w_f, b_f)


def _flash_kernel(q_ref, k_ref, v_ref, ck_ref, cq_ref, o_ref, m_ref, acc_ref, *, scale, sub):
    qi = pl.program_id(1)
    bq = q_ref.shape[0]
    dh = FOX_HEAD_DIM
    heads = q_ref.shape[1] // dh
    m_ref[...] = jnp.full_like(m_ref, -jnp.inf)
    acc_ref[...] = jnp.zeros_like(acc_ref)
    ones = jnp.ones((bq, dh), BF16)
    n_sub = bq // sub

    def chunk(start, diagonal):
        for hh in range(heads):
            cols = slice(hh * dh, (hh + 1) * dh)
            c0 = cq_ref[hh, :, 0:1]
            k = k_ref[pl.ds(start, bq), cols]
            v1 = jnp.concatenate([v_ref[pl.ds(start, bq), cols], ones], axis=1)
            s_all = lax.dot_general(q_ref[:, cols], k, (((1,), (1,)), ((), ())),
                                    preferred_element_type=F32)
            bias = (c0 - ck_ref[hh, :, pl.ds(start, bq)]) * LOG2E
            ps, alphas = [], []
            for r in range(n_sub):
                rows = slice(r * sub, (r + 1) * sub)
                s = s_all[rows, :] * (scale * LOG2E) + bias
                if diagonal:
                    row = lax.broadcasted_iota(I32, s.shape, 0) + r * sub
                    col = lax.broadcasted_iota(I32, s.shape, 1)
                    s = jnp.where(col <= row, s, -jnp.inf)
                m_prev = m_ref[hh, rows, :]
                m_new = jnp.maximum(m_prev, jnp.max(s, axis=1, keepdims=True))
                m_ref[hh, rows, :] = m_new
                ps.append(jnp.exp2(s - jnp.concatenate([m_new] * (bq // LANES), axis=1)).astype(BF16))
                alphas.append(jnp.exp2(m_prev - m_new))
            pv = jnp.dot(jnp.concatenate(ps, axis=0), v1, preferred_element_type=F32)
            for r in range(n_sub):
                rows = slice(r * sub, (r + 1) * sub)
                acc_ref[hh, rows, :] = (jnp.concatenate([alphas[r]] * 2, axis=1) * acc_ref[hh, rows, :]
                                        + pv[rows, :])

    def body(j, carry):
        chunk(pl.multiple_of(j * bq, bq), False)
        return carry

    lax.fori_loop(0, qi, body, 0)
    chunk(pl.multiple_of(qi * bq, bq), True)
    for hh in range(heads):
        o_ref[:, hh * dh:(hh + 1) * dh] = (acc_ref[hh, :, 0:dh] / acc_ref[hh, :, dh:2 * dh]).astype(o_ref.dtype)


def flash_attention(qkv, cum_t, bq=512, sub=64, heads=2):
    T = qkv.shape[0]
    H, dh = FOX_HEADS, FOX_HEAD_DIM
    bq = min(bq, T)
    kern = functools.partial(_flash_kernel, scale=dh ** -0.5, sub=min(sub, bq))
    hw = heads * dh
    nhb = H // heads
    return pl.pallas_call(
        kern,
        grid=(nhb, T // bq),
        in_specs=[pl.BlockSpec((bq, hw), lambda h, i: (i, h)),
                  pl.BlockSpec((T, hw), lambda h, i: (0, nhb + h)),
                  pl.BlockSpec((T, hw), lambda h, i: (0, 2 * nhb + h)),
                  pl.BlockSpec((heads, 1, T), lambda h, i: (h, 0, 0)),
                  pl.BlockSpec((heads, 1, bq), lambda h, i: (h, 0, i))],
        out_specs=pl.BlockSpec((bq, hw), lambda h, i: (i, h)),
        out_shape=jax.ShapeDtypeStruct((T, H * dh), BF16),
        scratch_shapes=[pltpu.VMEM((heads, bq, LANES), F32), pltpu.VMEM((heads, bq, 2 * dh), F32)],
        compiler_params=_cparams(("parallel", "arbitrary"), 48),
        name="fox_attention",
    )(qkv, qkv, qkv, cum_t, cum_t)


def kernel(x, ab_norm, ab_w_in, ab_conv_w, s5_lambda_re, s5_lambda_im, s5_log_dt, s5_b_re, s5_b_im,
           s5_c_re, s5_c_im, s5_d, s5_w_glu, s5_b_glu, ab_w_out, c_norm, c_w_in, c_b_forget, c_w_out,
           ffn_norm, router_w_group, router_b_group, router_w_expert, router_b_expert,
           moe_w_gate, moe_w_up, moe_w_down, final_norm):
    bsz, L, D = x.shape
    depth = ffn_norm.shape[0]
    cw = ab_conv_w.shape[-1]
    xt = x.reshape(bsz * L, D)
    h = None
    xs_buf = jnp.zeros((moe_row_blocks(bsz * L) * MOE_BLOCK, D), F32)
    for i in range(depth):
        j = i // 2
        if i % 2 == 0:
            g_in = ab_norm[j]
            if h is not None:
                proj = matmul(h, ab_w_in[j].astype(BF16), F32)
            else:
                proj = norm_matmul(xt, g_in, ab_w_in[j].astype(BF16), F32)
            y_conv = conv_mixer(proj, ab_conv_w[j])
            y_gelu = s5_mixer(proj, 3 * cw, s5_lambda_re[j], s5_lambda_im[j], s5_log_dt[j],
                              s5_b_re[j], s5_b_im[j], s5_c_re[j], s5_c_im[j], s5_d[j])
            xt, hf = glu_out(xt, y_conv, y_gelu, s5_w_glu[j].astype(BF16), s5_b_glu[j],
                             ab_w_out[j].astype(BF16), ffn_norm[i])
        else:
            assert h is not None, "an attention layer always follows a MoE combine that emits its norm"
            hd = FOX_HEADS * FOX_HEAD_DIM
            qkv = matmul(h, c_w_in[j][:, :3 * hd].astype(BF16), BF16)
            w_f = jnp.pad(c_w_in[j][:, 3 * hd:], ((0, 0), (0, LANES - FOX_HEADS)))
            b_f = jnp.pad(c_b_forget[j], (0, LANES - FOX_HEADS)).reshape(1, LANES)
            cum = forget_cumsum(h, w_f, b_f)
            cum_t = cum[:, :FOX_HEADS].T.reshape(FOX_HEADS, 1, bsz * L)
            att = flash_attention(qkv, cum_t)
            xt, hf = proj_residual(xt, att, c_w_out[j].astype(BF16), ffn_norm[i])
        last = i == depth - 1
        if last:
            g_next = final_norm
        elif (i + 1) % 2 == 0:
            g_next = ab_norm[(i + 1) // 2]
        else:
            g_next = c_norm[(i + 1) // 2]
        xt, h, xs_buf = moe_layer(xt, hf, xs_buf, router_w_group[i], router_b_group[i], router_w_expert[i],
                                  router_b_expert[i], i, moe_w_gate, moe_w_up, moe_w_down,
                                  g_next, F32 if last else BF16)
    return h.reshape(bsz, L, D)
```

```python
import functools
import math

import jax
import jax.numpy as jnp
from jax import lax
from jax.experimental import pallas as pl
from jax.experimental.pallas import tpu as pltpu

F32 = jnp.float32
BF16 = jnp.bfloat16
I32 = jnp.int32
U32 = jnp.uint32

RMS_EPS = 1e-6
LANES = 128
LOG2E = math.log2(math.e)
MIB = 1024 * 1024

CONV_K = 3
S5_GROUP = 16
S5_STATE = 64
S5_CHUNK = 16
S5_OCT = LANES // S5_GROUP
FOX_HEADS = 16
FOX_HEAD_DIM = 128
N_GROUPS = 4
EXPERTS_PER_GROUP = 8
N_EXPERTS = N_GROUPS * EXPERTS_PER_GROUP
EXPERT_LANE0 = N_GROUPS
MOE_BLOCK = 256
MOE_CODE_BITS = 16
MOE_CODE = 1 << MOE_CODE_BITS


def _cparams(sem, vmem_mib):
    return pltpu.CompilerParams(dimension_semantics=sem, vmem_limit_bytes=vmem_mib * MIB)


def _rms(x, g):
    ms = jnp.mean(x * x, axis=-1, keepdims=True)
    return x * lax.rsqrt(ms + RMS_EPS) * g


def _norm_matmul_kernel(x_ref, g_ref, w_ref, o_ref, h_ref):
    @pl.when(pl.program_id(1) == 0)
    def _():
        h_ref[...] = _rms(x_ref[...], g_ref[...]).astype(BF16)

    o_ref[...] = jnp.dot(h_ref[...], w_ref[...], preferred_element_type=F32).astype(o_ref.dtype)


def norm_matmul(x, g, w, out_dtype, tm=1024, tn=1024):
    T, D = x.shape
    N = w.shape[1]
    tm, tn = min(tm, T), min(tn, N)
    return pl.pallas_call(
        _norm_matmul_kernel,
        grid=(T // tm, N // tn),
        in_specs=[pl.BlockSpec((tm, D), lambda i, j: (i, 0)),
                  pl.BlockSpec((1, D), lambda i, j: (0, 0)),
                  pl.BlockSpec((D, tn), lambda i, j: (0, j))],
        out_specs=pl.BlockSpec((tm, tn), lambda i, j: (i, j)),
        out_shape=jax.ShapeDtypeStruct((T, N), out_dtype),
        scratch_shapes=[pltpu.VMEM((tm, D), BF16)],
        compiler_params=_cparams(("parallel", "arbitrary"), 56),
        name="norm_matmul",
    )(x, g.reshape(1, D), w)


def _matmul_kernel(a_ref, w_ref, o_ref):
    o_ref[...] = jnp.dot(a_ref[...], w_ref[...], preferred_element_type=F32).astype(o_ref.dtype)


def matmul(a, w, out_dtype, tm=1024, tn=1024):
    T, K = a.shape
    N = w.shape[1]
    tm, tn = min(tm, T), min(tn, N)
    return pl.pallas_call(
        _matmul_kernel,
        grid=(T // tm, N // tn),
        in_specs=[pl.BlockSpec((tm, K), lambda i, j: (i, 0)),
                  pl.BlockSpec((K, tn), lambda i, j: (0, j))],
        out_specs=pl.BlockSpec((tm, tn), lambda i, j: (i, j)),
        out_shape=jax.ShapeDtypeStruct((T, N), out_dtype),
        compiler_params=_cparams(("parallel", "arbitrary"), 48),
        name="matmul",
    )(a, w)


def _conv_kernel(gb_ref, gc_ref, u_ref, gcp_ref, up_ref, w_ref, o_ref):
    w0, w1, w2 = w_ref[0:1, :], w_ref[1:2, :], w_ref[2:3, :]
    v = gc_ref[...] * u_ref[...]
    y = w2 * v + w1 * pltpu.roll(v, 1, 0) + w0 * pltpu.roll(v, 2, 0)
    o_ref[...] = (gb_ref[...] * y).astype(o_ref.dtype)
    vp = gcp_ref[...] * up_ref[...]
    vp = jnp.where(pl.program_id(0) > 0, vp, jnp.zeros_like(vp))
    v8 = v[0:8, :]
    row = lax.broadcasted_iota(I32, v8.shape, 0)
    v1 = jnp.where(row < 1, pltpu.roll(vp, 1, 0), pltpu.roll(v8, 1, 0))
    v2 = jnp.where(row < 2, pltpu.roll(vp, 2, 0), pltpu.roll(v8, 2, 0))
    o_ref[0:8, :] = (gb_ref[0:8, :] * (w2 * v8 + w1 * v1 + w0 * v2)).astype(o_ref.dtype)


def conv_mixer(proj, conv_w, tm=512):
    T = proj.shape[0]
    CW = conv_w.shape[1]
    tm = min(tm, T)
    r8 = tm // 8
    cur = lambda c: pl.BlockSpec((tm, CW), lambda i: (i, c))
    prev = lambda c: pl.BlockSpec((8, CW), lambda i: (jnp.maximum(i * r8 - 1, 0), c))
    return pl.pallas_call(
        _conv_kernel,
        grid=(T // tm,),
        in_specs=[cur(0), cur(1), cur(2), prev(1), prev(2),
                  pl.BlockSpec((CONV_K, CW), lambda i: (0, 0))],
        out_specs=pl.BlockSpec((tm, CW), lambda i: (i, 0)),
        out_shape=jax.ShapeDtypeStruct((T, CW), BF16),
        compiler_params=_cparams(("parallel",), 40),
        name="conv_mixer",
    )(proj, proj, proj, proj, proj, conv_w)


def _cmul(a, b):
    return a[0] * b[0] - a[1] * b[1], a[0] * b[1] + a[1] * b[0]


def s5_matrices(lam_re, lam_im, log_dt, b_re, b_im, c_re, c_im):
    C = S5_CHUNK
    dt = jnp.exp(log_dt)[:, None]
    a, b = lam_re * dt, lam_im * dt
    mag = jnp.exp(a)
    lbar = (mag * jnp.cos(b), mag * jnp.sin(b))
    den = lam_re * lam_re + lam_im * lam_im
    inv_lam = (lam_re / den, -lam_im / den)
    coef = _cmul((lbar[0] - 1.0, lbar[1]), inv_lam)
    bbar = _cmul((coef[0][..., None], coef[1][..., None]), (b_re, b_im))
    j = jnp.arange(C + 1, dtype=F32)[None, :, None]
    pmag = jnp.exp(a[:, None, :] * j)
    pw = (pmag * jnp.cos(b[:, None, :] * j), pmag * jnp.sin(b[:, None, :] * j))
    hi = lax.Precision.HIGHEST
    cp = _cmul((c_re[:, None], c_im[:, None]), (pw[0][:, :C, None, :], pw[1][:, :C, None, :]))
    kj = (jnp.einsum('gjhn,gnk->gjhk', cp[0], bbar[0], precision=hi)
          - jnp.einsum('gjhn,gnk->gjhk', cp[1], bbar[1], precision=hi))
    return pw, bbar, kj


def _lane_tile_blockdiag(m):
    Q, O, r, c = m.shape
    out = jnp.zeros((Q, O, r, O, c), m.dtype)
    for g in range(O):
        out = out.at[:, g, :, g, :].set(m[:, g])
    return out.reshape(Q, O * r, O * c)


def s5_lane_tile_operands(lam_re, lam_im, log_dt, b_re, b_im, c_re, c_im):
    C, H, N, O = S5_CHUNK, S5_GROUP, S5_STATE, S5_OCT
    pw, bbar, kj = s5_matrices(lam_re, lam_im, log_dt, b_re, b_im, c_re, c_im)
    Q = lam_re.shape[0] // O
    pw_q = jnp.stack([p.reshape(Q, O, C + 1, N).transpose(0, 2, 1, 3).reshape(Q, C + 1, O * N) for p in pw],
                     axis=1)
    bb = jnp.stack([_lane_tile_blockdiag(m.transpose(0, 2, 1).reshape(Q, O, H, N)) for m in bbar], axis=1)
    cc = jnp.stack([_lane_tile_blockdiag(m.reshape(Q, O, H, N)) for m in (c_re, c_im)], axis=1)
    kq = kj.reshape(Q, O, C, H, H).transpose(0, 2, 1, 4, 3).reshape(Q * C, O, H, H)
    bd = _lane_tile_blockdiag(kq).reshape(Q, C, O * H, O * H).astype(BF16)
    return pw_q, bb, cc, bd


def _s5_kernel(u_ref, pw_ref, bb_ref, cc_ref, kj_ref, d_ref, o_ref, t_ref, w_ref, v_ref, uf_ref, s_ref,
               x_ref, c_ref):
    C = S5_CHUNK
    tb = u_ref.shape[0]
    nc = tb // C
    sw = pw_ref.shape[3]

    @pl.when((pl.program_id(0) == 0) & (pl.program_id(1) == 0))
    def _():
        t_ref[...] = jnp.zeros_like(t_ref)

    @pl.when(pl.program_id(1) == 0)
    def _():
        bb = (bb_ref[0, 0], bb_ref[0, 1])
        cc = (cc_ref[0, 0], cc_ref[0, 1])
        for j in range(C):
            a = _cmul(bb, (pw_ref[0, 0, j:j + 1, :], pw_ref[0, 1, j:j + 1, :]))
            s = C - 1 - j
            w_ref[s * LANES:(s + 1) * LANES, 0:sw] = a[0].astype(BF16)
            w_ref[s * LANES:(s + 1) * LANES, sw:2 * sw] = a[1].astype(BF16)
            for s0 in range(C - j):
                t_ref[s0 * LANES:(s0 + 1) * LANES, (s0 + j) * LANES:(s0 + j + 1) * LANES] = kj_ref[0, j]
            g = _cmul(cc, (pw_ref[0, 0, j + 1:j + 2, :], pw_ref[0, 1, j + 1:j + 2, :]))
            v_ref[j * LANES:(j + 1) * LANES, 0:sw] = g[0].astype(BF16)
            v_ref[j * LANES:(j + 1) * LANES, sw:2 * sw] = (-g[1]).astype(BF16)
        c_ref[...] = jnp.zeros_like(c_ref)

    for s in range(C):
        uf_ref[:, s * LANES:(s + 1) * LANES] = u_ref[pl.ds(s, nc, stride=C), :].astype(BF16)
    s_ref[...] = jnp.dot(uf_ref[...], w_ref[...], preferred_element_type=F32)
    lr, li = pw_ref[0, 0, C:C + 1, :], pw_ref[0, 1, C:C + 1, :]
    row = lax.broadcasted_iota(I32, (8, sw), 0)

    def tile_step(i, carry):
        xr, xi = carry
        r0 = pl.multiple_of(i * 8, 8)
        sr = s_ref[pl.ds(r0, 8), 0:sw]
        si = s_ref[pl.ds(r0, 8), sw:2 * sw]
        tr = jnp.zeros((8, sw), F32)
        ti = jnp.zeros((8, sw), F32)
        for r in range(8):
            tr = jnp.where(row == r, xr, tr)
            ti = jnp.where(row == r, xi, ti)
            xr, xi = lr * xr - li * xi + sr[r:r + 1, :], lr * xi + li * xr + si[r:r + 1, :]
        x_ref[pl.ds(r0, 8), 0:sw] = tr
        x_ref[pl.ds(r0, 8), sw:2 * sw] = ti
        return xr, xi

    xr, xi = lax.fori_loop(0, nc // 8, tile_step, (c_ref[0:1, 0:sw], c_ref[0:1, sw:2 * sw]))
    c_ref[0:1, 0:sw] = xr
    c_ref[0:1, sw:2 * sw] = xi
    y_state = lax.dot_general(x_ref[...].astype(BF16), v_ref[...], (((1,), (1,)), ((), ())),
                              preferred_element_type=F32)
    quarter = C // 4
    for qt in range(4):
        rows = (qt + 1) * quarter * LANES
        cols = slice(qt * quarter * LANES, (qt + 1) * quarter * LANES)
        y = jnp.dot(uf_ref[:, 0:rows], t_ref[0:rows, cols], preferred_element_type=F32) + y_state[:, cols]
        for s in range(quarter):
            st = qt * quarter + s
            ys = y[:, s * LANES:(s + 1) * LANES] + d_ref[...] * u_ref[pl.ds(st, nc, stride=C), :]
            o_ref[pl.ds(st, nc, stride=C), :] = jax.nn.gelu(ys)


def s5_mixer(proj, col0, lam_re, lam_im, log_dt, b_re, b_im, c_re, c_im, d, tb=8192):
    T = proj.shape[0]
    C, N, O = S5_CHUNK, S5_STATE, S5_OCT
    W = d.shape[0]
    Q = W // LANES
    tb = min(tb, T)
    pw_q, bb, cc, bd = s5_lane_tile_operands(lam_re, lam_im, log_dt, b_re, b_im, c_re, c_im)
    cb0 = col0 // LANES
    nc = tb // C
    sw = O * N
    quad = lambda a: pl.BlockSpec((1,) + a.shape[1:], lambda q, t: (q, 0, 0, 0))
    return pl.pallas_call(
        _s5_kernel,
        grid=(Q, T // tb),
        in_specs=[pl.BlockSpec((tb, LANES), lambda q, t: (t, cb0 + q)),
                  quad(pw_q), quad(bb), quad(cc), quad(bd),
                  pl.BlockSpec((1, LANES), lambda q, t: (0, q))],
        out_specs=pl.BlockSpec((tb, LANES), lambda q, t: (t, q)),
        out_shape=jax.ShapeDtypeStruct((T, W), F32),
        scratch_shapes=[pltpu.VMEM((C * LANES, C * LANES), BF16),
                        pltpu.VMEM((C * LANES, 2 * sw), BF16),
                        pltpu.VMEM((C * LANES, 2 * sw), BF16),
                        pltpu.VMEM((nc, C * LANES), BF16),
                        pltpu.VMEM((nc, 2 * sw), F32),
                        pltpu.VMEM((nc, 2 * sw), F32),
                        pltpu.VMEM((8, 2 * sw), F32)],
        compiler_params=_cparams(("arbitrary", "arbitrary"), 56),
        name="s5_scan",
    )(proj, pw_q, bb, cc, bd, d.reshape(1, W))


def _store_row_tiles(ref, val, lead=()):
    rows, width = val.shape
    n = width // LANES
    for j in range(n):
        ref[lead + (pl.ds(j, rows, stride=n), slice(None))] = val[:, j * LANES:(j + 1) * LANES]


def _load_row_tiles(ref, rows, n, lead=()):
    return jnp.concatenate([ref[lead + (pl.ds(j, rows, stride=n), slice(None))] for j in range(n)], axis=1)


def _row_tile_spec(tm, width, index_map):
    return pl.BlockSpec((tm * (width // LANES), LANES), index_map)


def _glu_out_kernel(x_ref, yc_ref, yg_ref, wglu_ref, bglu_ref, wout_ref, g_ref, xo_ref, ho_ref):
    yg = yg_ref[...]
    z = jnp.dot(yg.astype(BF16), wglu_ref[...], preferred_element_type=F32) + bglu_ref[...]
    ys = (yg * jax.nn.sigmoid(z)).astype(BF16)
    cw = yc_ref.shape[1]
    xn = (x_ref[...]
          + jnp.dot(yc_ref[...], wout_ref[0:cw, :], preferred_element_type=F32)
          + jnp.dot(ys, wout_ref[cw:, :], preferred_element_type=F32))
    xo_ref[...] = xn
    _store_row_tiles(ho_ref, _rms(xn, g_ref[...]))


def glu_out(x, y_conv, y_gelu, w_glu, b_glu, w_out, g_next, tm=256):
    T, D = x.shape
    CW, SW = y_conv.shape[1], y_gelu.shape[1]
    tm = min(tm, T)
    row = lambda w: pl.BlockSpec((tm, w), lambda i: (i, 0))
    full = lambda a: pl.BlockSpec(a.shape, lambda i: (0,) * a.ndim)
    bg, g2 = b_glu.reshape(1, SW), g_next.reshape(1, D)
    return pl.pallas_call(
        _glu_out_kernel,
        grid=(T // tm,),
        in_specs=[row(D), row(CW), row(SW), full(w_glu), full(bg), full(w_out), full(g2)],
        out_specs=[row(D), _row_tile_spec(tm, D, lambda i: (i, 0))],
        out_shape=[jax.ShapeDtypeStruct((T, D), F32), jax.ShapeDtypeStruct((T * D // LANES, LANES), F32)],
        compiler_params=_cparams(("parallel",), 56),
        name="glu_out",
    )(x, y_conv, y_gelu, w_glu, bg, w_out, g2)


def _proj_residual_kernel(x_ref, a_ref, w_ref, g_ref, xo_ref, ho_ref):
    xn = x_ref[...] + jnp.dot(a_ref[...], w_ref[...], preferred_element_type=F32)
    xo_ref[...] = xn
    _store_row_tiles(ho_ref, _rms(xn, g_ref[...]))


def proj_residual(x, a, w, g_next, tm=256):
    T, D = x.shape
    K = a.shape[1]
    tm = min(tm, T)
    row = lambda w_: pl.BlockSpec((tm, w_), lambda i: (i, 0))
    full = lambda arr: pl.BlockSpec(arr.shape, lambda i: (0,) * arr.ndim)
    g2 = g_next.reshape(1, D)
    return pl.pallas_call(
        _proj_residual_kernel,
        grid=(T // tm,),
        in_specs=[row(D), row(K), full(w), full(g2)],
        out_specs=[row(D), _row_tile_spec(tm, D, lambda i: (i, 0))],
        out_shape=[jax.ShapeDtypeStruct((T, D), F32), jax.ShapeDtypeStruct((T * D // LANES, LANES), F32)],
        compiler_params=_cparams(("parallel",), 56),
        name="proj_residual",
    )(x, a, w, g2)


def _router_kernel(h_ref, w_ref, b_ref, mi_ref, mf_ref, cnt_ref, run_ref):
    i = pl.program_id(0)

    @pl.when(i == 0)
    def _():
        run_ref[...] = jnp.zeros_like(run_ref)

    D = w_ref.shape[0]
    tm = h_ref.shape[0] * LANES // D
    h = _load_row_tiles(h_ref, tm, D // LANES)
    logits = jnp.dot(h, w_ref[...], precision=lax.Precision.HIGHEST,
                     preferred_element_type=F32) + b_ref[...]
    lane = lax.broadcasted_iota(I32, logits.shape, 1)
    neg = jnp.float32(-jnp.inf)
    gl = jnp.where(lane < N_GROUPS, logits, neg)
    gmax = jnp.max(gl, axis=1, keepdims=True)
    gsum = jnp.sum(jnp.where(lane < N_GROUPS, jnp.exp(gl - gmax), 0.0), axis=1, keepdims=True)
    gw = 1.0 / gsum
    gidx = jnp.min(jnp.where(gl == gmax, lane, LANES), axis=1, keepdims=True)
    lo = EXPERT_LANE0 + EXPERTS_PER_GROUP * gidx
    el = jnp.where((lane >= lo) & (lane < lo + EXPERTS_PER_GROUP), logits, neg)
    v1 = jnp.max(el, axis=1, keepdims=True)
    i1 = jnp.min(jnp.where(el == v1, lane, LANES), axis=1, keepdims=True)
    el2 = jnp.where(lane == i1, neg, el)
    v2 = jnp.max(el2, axis=1, keepdims=True)
    i2 = jnp.min(jnp.where(el2 == v2, lane, LANES), axis=1, keepdims=True)
    t = jnp.exp(v2 - v1)
    w1 = gw / (1.0 + t)
    w2 = gw * t / (1.0 + t)
    hit1 = lane == i1
    hit2 = lane == i2
    cnt = (hit1 | hit2).astype(BF16)
    r = lax.broadcasted_iota(I32, (tm, tm), 0)
    c = lax.broadcasted_iota(I32, (tm, tm), 1)
    before = (c < r).astype(BF16)
    cum = jnp.dot(before, cnt, preferred_element_type=F32) + run_ref[0:1, :]
    rank1 = jnp.sum(jnp.where(hit1, cum, 0.0), axis=1, keepdims=True).astype(I32)
    rank2 = jnp.sum(jnp.where(hit2, cum, 0.0), axis=1, keepdims=True).astype(I32)
    run = run_ref[0:1, :] + jnp.sum(cnt.astype(F32), axis=0, keepdims=True)
    run_ref[...] = jnp.broadcast_to(run, run_ref.shape)
    cnt_ref[...] = jnp.broadcast_to(run, cnt_ref.shape)
    code1 = (i1 - EXPERT_LANE0) * MOE_CODE + rank1
    code2 = (i2 - EXPERT_LANE0) * MOE_CODE + rank2
    mi_ref[...] = jnp.where(lane == 0, code1, jnp.where(lane == 1, code2, 0))
    mf_ref[...] = jnp.where(lane == 0, w1, jnp.where(lane == 1, w2, 0.0))


def router(h, w_r, b_r, tm=256):
    D = w_r.shape[0]
    T = h.shape[0] * LANES // D
    tm = min(tm, T)
    row = lambda w: pl.BlockSpec((tm, w), lambda i: (i, 0))
    full = lambda a: pl.BlockSpec(a.shape, lambda i: (0,) * a.ndim)
    return pl.pallas_call(
        _router_kernel,
        grid=(T // tm,),
        in_specs=[_row_tile_spec(tm, D, lambda i: (i, 0)), full(w_r), full(b_r)],
        out_specs=[row(LANES), row(LANES), pl.BlockSpec((8, LANES), lambda i: (0, 0))],
        out_shape=[jax.ShapeDtypeStruct((T, LANES), I32), jax.ShapeDtypeStruct((T, LANES), F32),
                   jax.ShapeDtypeStruct((8, LANES), F32)],
        scratch_shapes=[pltpu.VMEM((8, LANES), F32)],
        compiler_params=_cparams(("arbitrary",), 40),
        name="moe_router",
    )(h, w_r, b_r)


def moe_plan(meta_i, counts, n_blocks):
    B = MOE_BLOCK
    code1, code2 = meta_i[:, 0], meta_i[:, 1]
    cnt = counts[0, EXPERT_LANE0:EXPERT_LANE0 + N_EXPERTS].astype(I32)
    nblk = (cnt + B - 1) // B
    blk_end = jnp.cumsum(nblk)
    blk_off = blk_end - nblk
    b = jnp.arange(n_blocks, dtype=I32)
    total = blk_end[-1]
    owner = jnp.minimum(jnp.sum((blk_end[None, :] <= b[:, None]).astype(I32), axis=1), N_EXPERTS - 1)
    valid = b < total
    last = jnp.maximum(total - 1, 0)
    blk_expert = jnp.where(valid, owner, owner[last])
    blk_src = jnp.where(valid, b, last)
    blk_first = (valid & (b == blk_off[owner])).astype(I32)
    return code1, code2, blk_off * B, blk_expert, blk_src, valid.astype(I32), blk_first


def _row_copy(src, dst, sem):
    return pltpu.make_async_copy(src, dst, sem)


def _sorted_row(code, off_ref):
    return off_ref[code >> MOE_CODE_BITS] + (code & (MOE_CODE - 1))


def _tile_rows(row, n):
    return pl.ds(pl.multiple_of(row * n, n), n)


def _dispatch_kernel(code1_ref, code2_ref, off_ref, hp_ref, xs_in_ref, xs_ref, sem, *, n):
    del xs_in_ref
    tm = hp_ref.shape[0] // n
    base = pl.program_id(0) * tm

    def copies(r):
        src = hp_ref.at[_tile_rows(r, n)]
        return (_row_copy(src, xs_ref.at[_tile_rows(_sorted_row(code1_ref[base + r], off_ref), n)], sem),
                _row_copy(src, xs_ref.at[_tile_rows(_sorted_row(code2_ref[base + r], off_ref), n)], sem))

    def start(r, carry):
        for cp in copies(r):
            cp.start()
        return carry

    lax.fori_loop(0, tm, start, 0)
    for _ in range(2):
        _row_copy(hp_ref, xs_ref.at[pl.ds(0, tm * n)], sem).wait()


def dispatch(hp, width, code1, code2, row_off, xs0, tm=256):
    n = width // LANES
    T = hp.shape[0] // n
    tm = min(tm, T)
    return pl.pallas_call(
        functools.partial(_dispatch_kernel, n=n),
        grid_spec=pltpu.PrefetchScalarGridSpec(
            num_scalar_prefetch=3,
            grid=(T // tm,),
            in_specs=[_row_tile_spec(tm, width, lambda i, c1, c2, off: (i, 0)),
                      pl.BlockSpec(memory_space=pl.ANY)],
            out_specs=pl.BlockSpec(memory_space=pl.ANY),
            scratch_shapes=[pltpu.SemaphoreType.DMA(())]),
        out_shape=jax.ShapeDtypeStruct(xs0.shape, hp.dtype),
        input_output_aliases={4: 0},
        compiler_params=_cparams(("arbitrary",), 32),
        name="moe_dispatch",
    )(code1, code2, row_off, hp, xs0)


def _experts_kernel(be_ref, bs_ref, bv_ref, bf_ref, xs_ref, wg_ref, wu_ref, wd_ref, ys_ref, wg_s, wu_s, wd_s):
    del be_ref, bs_ref
    b = pl.program_id(0)

    @pl.when(bv_ref[b] == 0)
    def _():
        ys_ref[...] = jnp.zeros_like(ys_ref)

    @pl.when(bv_ref[b] > 0)
    def _():
        @pl.when(bf_ref[b] == 1)
        def _():
            wg_s[...] = wg_ref[0].astype(BF16)
            wu_s[...] = wu_ref[0].astype(BF16)
            wd_s[...] = wd_ref[0].astype(BF16)

        D = wg_s.shape[0]
        a = _load_row_tiles(xs_ref, MOE_BLOCK, D // LANES).astype(BF16)
        gate = jnp.dot(a, wg_s[...], preferred_element_type=F32)
        up = jnp.dot(a, wu_s[...], preferred_element_type=F32)
        mid = (jax.nn.silu(gate) * up).astype(BF16)
        _store_row_tiles(ys_ref, jnp.dot(mid, wd_s[...], preferred_element_type=F32))


def experts(xs, blk_expert, blk_src, blk_valid, blk_first, w_gate, w_up, w_down):
    B = MOE_BLOCK
    E, D, FF = w_gate.shape
    nb = xs.shape[0] * LANES // D // B
    return pl.pallas_call(
        _experts_kernel,
        grid_spec=pltpu.PrefetchScalarGridSpec(
            num_scalar_prefetch=4,
            grid=(nb,),
            in_specs=[_row_tile_spec(B, D, lambda b, be, bs, bv, bf: (bs[b], 0)),
                      pl.BlockSpec((1, D, FF), lambda b, be, bs, bv, bf: (be[b], 0, 0)),
                      pl.BlockSpec((1, D, FF), lambda b, be, bs, bv, bf: (be[b], 0, 0)),
                      pl.BlockSpec((1, FF, D), lambda b, be, bs, bv, bf: (be[b], 0, 0))],
            out_specs=_row_tile_spec(B, D, lambda b, be, bs, bv, bf: (b, 0)),
            scratch_shapes=[pltpu.VMEM((D, FF), BF16), pltpu.VMEM((D, FF), BF16),
                            pltpu.VMEM((FF, D), BF16)]),
        out_shape=jax.ShapeDtypeStruct(xs.shape, F32),
        compiler_params=_cparams(("arbitrary",), 56),
        name="moe_experts",
    )(blk_expert, blk_src, blk_valid, blk_first, xs, w_gate, w_up, w_down)


def _combine_kernel(code1_ref, code2_ref, off_ref, x_ref, mf_ref, g_ref, ys_ref, xo_ref, ho_ref, buf, sem):
    tm, D = x_ref.shape
    n = D // LANES
    base = pl.program_id(0) * tm

    def copies(r):
        dst = _tile_rows(r, n)
        return (_row_copy(ys_ref.at[_tile_rows(_sorted_row(code1_ref[base + r], off_ref), n)],
                          buf.at[0, dst], sem),
                _row_copy(ys_ref.at[_tile_rows(_sorted_row(code2_ref[base + r], off_ref), n)],
                          buf.at[1, dst], sem))

    def start(r, carry):
        for cp in copies(r):
            cp.start()
        return carry

    lax.fori_loop(0, tm, start, 0)
    for k in range(2):
        _row_copy(ys_ref.at[pl.ds(0, tm * n)], buf.at[k], sem).wait()
    mf = mf_ref[...]
    xn = (x_ref[...] + mf[:, 0:1] * _load_row_tiles(buf, tm, n, lead=(0,))
          + mf[:, 1:2] * _load_row_tiles(buf, tm, n, lead=(1,)))
    xo_ref[...] = xn
    ho_ref[...] = _rms(xn, g_ref[...]).astype(ho_ref.dtype)


def combine(x, ys, meta_f, code1, code2, row_off, g_next, h_dtype, tm=256):
    T, D = x.shape
    tm = min(tm, T)
    row = lambda w: pl.BlockSpec((tm, w), lambda i, c1, c2, off: (i, 0))
    return pl.pallas_call(
        _combine_kernel,
        grid_spec=pltpu.PrefetchScalarGridSpec(
            num_scalar_prefetch=3,
            grid=(T // tm,),
            in_specs=[row(D), row(LANES), pl.BlockSpec((1, D), lambda i, c1, c2, off: (0, 0)),
                      pl.BlockSpec(memory_space=pl.ANY)],
            out_specs=[row(D), row(D)],
            scratch_shapes=[pltpu.VMEM((2, tm * D // LANES, LANES), F32), pltpu.SemaphoreType.DMA(())]),
        out_shape=[jax.ShapeDtypeStruct((T, D), F32), jax.ShapeDtypeStruct((T, D), h_dtype)],
        compiler_params=_cparams(("arbitrary",), 40),
        name="moe_combine",
    )(code1, code2, row_off, x, meta_f, g_next.reshape(1, D), ys)


def moe_row_blocks(T):
    return (2 * T) // MOE_BLOCK + N_EXPERTS


def moe_layer(x, h, xs_buf, w_group, b_group, w_expert, b_expert, layer, w_gate, w_up, w_down, g_next,
              h_dtype):
    T, D = x.shape
    assert T <= MOE_CODE, "ranks inside one expert must fit the packed (expert, rank) code"
    n_blocks = xs_buf.shape[0] * LANES // D // MOE_BLOCK
    pad = LANES - N_GROUPS - N_EXPERTS
    w_r = jnp.concatenate([w_group, w_expert.reshape(D, N_EXPERTS), jnp.zeros((D, pad), F32)], axis=1)
    b_r = jnp.concatenate([b_group, b_expert.reshape(N_EXPERTS), jnp.zeros((pad,), F32)]).reshape(1, LANES)
    meta_i, meta_f, counts = router(h, w_r, b_r)
    code1, code2, row_off, blk_expert, blk_src, blk_valid, blk_first = moe_plan(meta_i, counts, n_blocks)
    xs = dispatch(h, D, code1, code2, row_off, xs_buf)
    FF = w_gate.shape[-1]
    ys = experts(xs, blk_expert + layer * N_EXPERTS, blk_src, blk_valid, blk_first,
                 w_gate.reshape(-1, D, FF), w_up.reshape(-1, D, FF), w_down.reshape(-1, FF, D))
    xo, ho = combine(x, ys, meta_f, code1, code2, row_off, g_next, h_dtype)
    return xo, ho, xs


def _forget_kernel(h_ref, w_ref, b_ref, o_ref, run_ref):
    @pl.when(pl.program_id(0) == 0)
    def _():
        run_ref[...] = jnp.zeros_like(run_ref)

    hi = lax.Precision.HIGHEST
    z = jnp.dot(h_ref[...].astype(F32), w_ref[...], precision=hi, preferred_element_type=F32) + b_ref[...]
    log_f = jax.nn.log_sigmoid(z)
    tm = z.shape[0]
    r = lax.broadcasted_iota(I32, (tm, tm), 0)
    c = lax.broadcasted_iota(I32, (tm, tm), 1)
    upto = (c <= r).astype(F32)
    cum = jnp.dot(upto, log_f, precision=hi, preferred_element_type=F32) + run_ref[0:1, :]
    o_ref[...] = cum
    run_ref[...] = jnp.broadcast_to(cum[tm - 1:tm, :], run_ref.shape)


def forget_cumsum(h, w_f, b_f, tm=256):
    T, D = h.shape
    tm = min(tm, T)
    return pl.pallas_call(
        _forget_kernel,
        grid=(T // tm,),
        in_specs=[pl.BlockSpec((tm, D), lambda i: (i, 0)),
                  pl.BlockSpec((D, LANES), lambda i: (0, 0)),
                  pl.BlockSpec((1, LANES), lambda i: (0, 0))],
        out_specs=pl.BlockSpec((tm, LANES), lambda i: (i, 0)),
        out_shape=jax.ShapeDtypeStruct((T, LANES), F32),
        scratch_shapes=[pltpu.VMEM((8, LANES), F32)],
        compiler_params=_cparams(("arbitrary",), 40),
        name="forget_cumsum",
    )(h, w_f, b_f)


def _flash_kernel(q_ref, k_ref, v_ref, ck_ref, cq_ref, o_ref, m_ref, acc_ref, *, scale, sub):
    qi = pl.program_id(1)
    bq = q_ref.shape[0]
    dh = FOX_HEAD_DIM
    heads = q_ref.shape[1] // dh
    m_ref[...] = jnp.full_like(m_ref, -jnp.inf)
    acc_ref[...] = jnp.zeros_like(acc_ref)
    ones = jnp.ones((bq, dh), BF16)
    n_sub = bq // sub

    def chunk(start, diagonal):
        for hh in range(heads):
            cols = slice(hh * dh, (hh + 1) * dh)
            c0 = cq_ref[hh, :, 0:1]
            k = k_ref[pl.ds(start, bq), cols]
            v1 = jnp.concatenate([v_ref[pl.ds(start, bq), cols], ones], axis=1)
            s_all = lax.dot_general(q_ref[:, cols], k, (((1,), (1,)), ((), ())),
                                    preferred_element_type=F32)
            bias = (c0 - ck_ref[hh, :, pl.ds(start, bq)]) * LOG2E
            ps, alphas = [], []
            for r in range(n_sub):
                rows = slice(r * sub, (r + 1) * sub)
                s = s_all[rows, :] * (scale * LOG2E) + bias
                if diagonal:
                    row = lax.broadcasted_iota(I32, s.shape, 0) + r * sub
                    col = lax.broadcasted_iota(I32, s.shape, 1)
                    s = jnp.where(col <= row, s, -jnp.inf)
                m_prev = m_ref[hh, rows, :]
                m_new = jnp.maximum(m_prev, jnp.max(s, axis=1, keepdims=True))
                m_ref[hh, rows, :] = m_new
                ps.append(jnp.exp2(s - jnp.concatenate([m_new] * (bq // LANES), axis=1)).astype(BF16))
                alphas.append(jnp.exp2(m_prev - m_new))
            pv = jnp.dot(jnp.concatenate(ps, axis=0), v1, preferred_element_type=F32)
            for r in range(n_sub):
                rows = slice(r * sub, (r + 1) * sub)
                acc_ref[hh, rows, :] = (jnp.concatenate([alphas[r]] * 2, axis=1) * acc_ref[hh, rows, :]
                                        + pv[rows, :])

    def body(j, carry):
        chunk(pl.multiple_of(j * bq, bq), False)
        return carry

    lax.fori_loop(0, qi, body, 0)
    chunk(pl.multiple_of(qi * bq, bq), True)
    for hh in range(heads):
        o_ref[:, hh * dh:(hh + 1) * dh] = (acc_ref[hh, :, 0:dh] / acc_ref[hh, :, dh:2 * dh]).astype(o_ref.dtype)


def flash_attention(qkv, cum_t, bq=512, sub=64, heads=2):
    T = qkv.shape[0]
    H, dh = FOX_HEADS, FOX_HEAD_DIM
    bq = min(bq, T)
    kern = functools.partial(_flash_kernel, scale=dh ** -0.5, sub=min(sub, bq))
    hw = heads * dh
    nhb = H // heads
    return pl.pallas_call(
        kern,
        grid=(nhb, T // bq),
        in_specs=[pl.BlockSpec((bq, hw), lambda h, i: (i, h)),
                  pl.BlockSpec((T, hw), lambda h, i: (0, nhb + h)),
                  pl.BlockSpec((T, hw), lambda h, i: (0, 2 * nhb + h)),
                  pl.BlockSpec((heads, 1, T), lambda h, i: (h, 0, 0)),
                  pl.BlockSpec((heads, 1, bq), lambda h, i: (h, 0, i))],
        out_specs=pl.BlockSpec((bq, hw), lambda h, i: (i, h)),
        out_shape=jax.ShapeDtypeStruct((T, H * dh), BF16),
        scratch_shapes=[pltpu.VMEM((heads, bq, LANES), F32), pltpu.VMEM((heads, bq, 2 * dh), F32)],
        compiler_params=_cparams(("parallel", "arbitrary"), 48),
        name="fox_attention",
    )(qkv, qkv, qkv, cum_t, cum_t)


def kernel(x, ab_norm, ab_w_in, ab_conv_w, s5_lambda_re, s5_lambda_im, s5_log_dt, s5_b_re, s5_b_im,
           s5_c_re, s5_c_im, s5_d, s5_w_glu, s5_b_glu, ab_w_out, c_norm, c_w_in, c_b_forget, c_w_out,
           ffn_norm, router_w_group, router_b_group, router_w_expert, router_b_expert,
           moe_w_gate, moe_w_up, moe_w_down, final_norm):
    bsz, L, D = x.shape
    depth = ffn_norm.shape[0]
    cw = ab_conv_w.shape[-1]
    xt = x.reshape(bsz * L, D)
    h = None
    xs_buf = jnp.zeros((moe_row_blocks(bsz * L) * MOE_BLOCK * D // LANES, LANES), F32)
    for i in range(depth):
        j = i // 2
        if i % 2 == 0:
            g_in = ab_norm[j]
            if h is not None:
                proj = matmul(h, ab_w_in[j].astype(BF16), F32)
            else:
                proj = norm_matmul(xt, g_in, ab_w_in[j].astype(BF16), F32)
            y_conv = conv_mixer(proj, ab_conv_w[j])
            y_gelu = s5_mixer(proj, 3 * cw, s5_lambda_re[j], s5_lambda_im[j], s5_log_dt[j],
                              s5_b_re[j], s5_b_im[j], s5_c_re[j], s5_c_im[j], s5_d[j])
            xt, hf = glu_out(xt, y_conv, y_gelu, s5_w_glu[j].astype(BF16), s5_b_glu[j],
                             ab_w_out[j].astype(BF16), ffn_norm[i])
        else:
            assert h is not None, "an attention layer always follows a MoE combine that emits its norm"
            hd = FOX_HEADS * FOX_HEAD_DIM
            qkv = matmul(h, c_w_in[j][:, :3 * hd].astype(BF16), BF16)
            w_f = jnp.pad(c_w_in[j][:, 3 * hd:], ((0, 0), (0, LANES - FOX_HEADS)))
            b_f = jnp.pad(c_b_forget[j], (0, LANES - FOX_HEADS)).reshape(1, LANES)
            cum = forget_cumsum(h, w_f, b_f)
            cum_t = cum[:, :FOX_HEADS].T.reshape(FOX_HEADS, 1, bsz * L)
            att = flash_attention(qkv, cum_t)
            xt, hf = proj_residual(xt, att, c_w_out[j].astype(BF16), ffn_norm[i])
        last = i == depth - 1
        if last:
            g_next = final_norm
        elif (i + 1) % 2 == 0:
            g_next = ab_norm[(i + 1) // 2]
        else:
            g_next = c_norm[(i + 1) // 2]
        xt, h, xs_buf = moe_layer(xt, hf, xs_buf, router_w_group[i], router_b_group[i], router_w_expert[i],
                                  router_b_expert[i], i, moe_w_gate, moe_w_up, moe_w_down,
                                  g_next, F32 if last else BF16)
    return h.reshape(bsz, L, D)
```

```python
import functools
import math

import jax
import jax.numpy as jnp
from jax import lax
from jax.experimental import pallas as pl
from jax.experimental.pallas import tpu as pltpu

F32 = jnp.float32
BF16 = jnp.bfloat16
I32 = jnp.int32
U32 = jnp.uint32

RMS_EPS = 1e-6
LANES = 128
LOG2E = math.log2(math.e)
MIB = 1024 * 1024

CONV_K = 3
S5_GROUP = 16
S5_STATE = 64
S5_CHUNK = 16
S5_OCT = LANES // S5_GROUP
FOX_HEADS = 16
FOX_HEAD_DIM = 128
N_GROUPS = 4
EXPERTS_PER_GROUP = 8
N_EXPERTS = N_GROUPS * EXPERTS_PER_GROUP
EXPERT_LANE0 = N_GROUPS
MOE_BLOCK = 256
MOE_CODE_BITS = 16
MOE_CODE = 1 << MOE_CODE_BITS


def _cparams(sem, vmem_mib):
    return pltpu.CompilerParams(dimension_semantics=sem, vmem_limit_bytes=vmem_mib * MIB)


def _rms(x, g):
    ms = jnp.mean(x * x, axis=-1, keepdims=True)
    return x * lax.rsqrt(ms + RMS_EPS) * g


def _norm_matmul_kernel(x_ref, g_ref, w_ref, o_ref, h_ref):
    @pl.when(pl.program_id(1) == 0)
    def _():
        h_ref[...] = _rms(x_ref[...], g_ref[...]).astype(BF16)

    o_ref[...] = jnp.dot(h_ref[...], w_ref[...], preferred_element_type=F32).astype(o_ref.dtype)


def norm_matmul(x, g, w, out_dtype, tm=1024, tn=1024):
    T, D = x.shape
    N = w.shape[1]
    tm, tn = min(tm, T), min(tn, N)
    return pl.pallas_call(
        _norm_matmul_kernel,
        grid=(T // tm, N // tn),
        in_specs=[pl.BlockSpec((tm, D), lambda i, j: (i, 0)),
                  pl.BlockSpec((1, D), lambda i, j: (0, 0)),
                  pl.BlockSpec((D, tn), lambda i, j: (0, j))],
        out_specs=pl.BlockSpec((tm, tn), lambda i, j: (i, j)),
        out_shape=jax.ShapeDtypeStruct((T, N), out_dtype),
        scratch_shapes=[pltpu.VMEM((tm, D), BF16)],
        compiler_params=_cparams(("parallel", "arbitrary"), 56),
        name="norm_matmul",
    )(x, g.reshape(1, D), w)


def _matmul_kernel(a_ref, w_ref, o_ref):
    o_ref[...] = jnp.dot(a_ref[...], w_ref[...], preferred_element_type=F32).astype(o_ref.dtype)


def matmul(a, w, out_dtype, n_cols=None, tm=1024, tn=1024):
    T, K = a.shape
    N = w.shape[1] if n_cols is None else n_cols
    tm, tn = min(tm, T), min(tn, N)
    return pl.pallas_call(
        _matmul_kernel,
        grid=(T // tm, N // tn),
        in_specs=[pl.BlockSpec((tm, K), lambda i, j: (i, 0)),
                  pl.BlockSpec((K, tn), lambda i, j: (0, j))],
        out_specs=pl.BlockSpec((tm, tn), lambda i, j: (i, j)),
        out_shape=jax.ShapeDtypeStruct((T, N), out_dtype),
        compiler_params=_cparams(("parallel", "arbitrary"), 48),
        name="matmul",
    )(a, w)


def _conv_kernel(gb_ref, gc_ref, u_ref, gcp_ref, up_ref, w_ref, o_ref):
    w0, w1, w2 = w_ref[0:1, :], w_ref[1:2, :], w_ref[2:3, :]
    v = gc_ref[...] * u_ref[...]
    y = w2 * v + w1 * pltpu.roll(v, 1, 0) + w0 * pltpu.roll(v, 2, 0)
    o_ref[...] = (gb_ref[...] * y).astype(o_ref.dtype)
    vp = gcp_ref[...] * up_ref[...]
    vp = jnp.where(pl.program_id(0) > 0, vp, jnp.zeros_like(vp))
    v8 = v[0:8, :]
    row = lax.broadcasted_iota(I32, v8.shape, 0)
    v1 = jnp.where(row < 1, pltpu.roll(vp, 1, 0), pltpu.roll(v8, 1, 0))
    v2 = jnp.where(row < 2, pltpu.roll(vp, 2, 0), pltpu.roll(v8, 2, 0))
    o_ref[0:8, :] = (gb_ref[0:8, :] * (w2 * v8 + w1 * v1 + w0 * v2)).astype(o_ref.dtype)


def conv_mixer(proj, conv_w, tm=512):
    T = proj.shape[0]
    CW = conv_w.shape[1]
    tm = min(tm, T)
    r8 = tm // 8
    cur = lambda c: pl.BlockSpec((tm, CW), lambda i: (i, c))
    prev = lambda c: pl.BlockSpec((8, CW), lambda i: (jnp.maximum(i * r8 - 1, 0), c))
    return pl.pallas_call(
        _conv_kernel,
        grid=(T // tm,),
        in_specs=[cur(0), cur(1), cur(2), prev(1), prev(2),
                  pl.BlockSpec((CONV_K, CW), lambda i: (0, 0))],
        out_specs=pl.BlockSpec((tm, CW), lambda i: (i, 0)),
        out_shape=jax.ShapeDtypeStruct((T, CW), BF16),
        compiler_params=_cparams(("parallel",), 40),
        name="conv_mixer",
    )(proj, proj, proj, proj, proj, conv_w)


def _cmul(a, b):
    return a[0] * b[0] - a[1] * b[1], a[0] * b[1] + a[1] * b[0]


def s5_matrices(lam_re, lam_im, log_dt, b_re, b_im, c_re, c_im):
    C = S5_CHUNK
    dt = jnp.exp(log_dt)[:, None]
    a, b = lam_re * dt, lam_im * dt
    mag = jnp.exp(a)
    lbar = (mag * jnp.cos(b), mag * jnp.sin(b))
    den = lam_re * lam_re + lam_im * lam_im
    inv_lam = (lam_re / den, -lam_im / den)
    coef = _cmul((lbar[0] - 1.0, lbar[1]), inv_lam)
    bbar = _cmul((coef[0][..., None], coef[1][..., None]), (b_re, b_im))
    j = jnp.arange(C + 1, dtype=F32)[None, :, None]
    pmag = jnp.exp(a[:, None, :] * j)
    pw = (pmag * jnp.cos(b[:, None, :] * j), pmag * jnp.sin(b[:, None, :] * j))
    hi = lax.Precision.HIGHEST
    cp = _cmul((c_re[:, None], c_im[:, None]), (pw[0][:, :C, None, :], pw[1][:, :C, None, :]))
    kj = (jnp.einsum('gjhn,gnk->gjhk', cp[0], bbar[0], precision=hi)
          - jnp.einsum('gjhn,gnk->gjhk', cp[1], bbar[1], precision=hi))
    return pw, bbar, kj


def _lane_tile_blockdiag(m):
    Q, O, r, c = m.shape
    out = jnp.zeros((Q, O, r, O, c), m.dtype)
    for g in range(O):
        out = out.at[:, g, :, g, :].set(m[:, g])
    return out.reshape(Q, O * r, O * c)


def s5_lane_tile_operands(lam_re, lam_im, log_dt, b_re, b_im, c_re, c_im):
    C, H, N, O = S5_CHUNK, S5_GROUP, S5_STATE, S5_OCT
    pw, bbar, kj = s5_matrices(lam_re, lam_im, log_dt, b_re, b_im, c_re, c_im)
    Q = lam_re.shape[0] // O
    pw_q = jnp.stack([p.reshape(Q, O, C + 1, N).transpose(0, 2, 1, 3).reshape(Q, C + 1, O * N) for p in pw],
                     axis=1)
    bb = jnp.stack([_lane_tile_blockdiag(m.transpose(0, 2, 1).reshape(Q, O, H, N)) for m in bbar], axis=1)
    cc = jnp.stack([_lane_tile_blockdiag(m.reshape(Q, O, H, N)) for m in (c_re, c_im)], axis=1)
    kq = kj.reshape(Q, O, C, H, H).transpose(0, 2, 1, 4, 3).reshape(Q, C, O * H, H).astype(BF16)
    return pw_q, bb, cc, kq


def _s5_kernel(u_ref, pw_ref, bb_ref, cc_ref, kj_ref, d_ref, o_ref, t_ref, w_ref, v_ref, uf_ref, s_ref,
               x_ref, c_ref):
    C = S5_CHUNK
    tb = u_ref.shape[0]
    nc = tb // C
    sw = pw_ref.shape[3]

    @pl.when((pl.program_id(0) == 0) & (pl.program_id(1) == 0))
    def _():
        t_ref[...] = jnp.zeros_like(t_ref)

    @pl.when(pl.program_id(1) == 0)
    def _():
        bb = (bb_ref[0, 0], bb_ref[0, 1])
        cc = (cc_ref[0, 0], cc_ref[0, 1])
        H = kj_ref.shape[3]
        hb = H.bit_length() - 1
        spread = ((lax.broadcasted_iota(I32, (H, LANES), 1) & (H - 1))
                  == lax.broadcasted_iota(I32, (H, LANES), 0)).astype(BF16)
        same_group = ((lax.broadcasted_iota(I32, (LANES, LANES), 0) >> hb)
                      == (lax.broadcasted_iota(I32, (LANES, LANES), 1) >> hb))
        for j in range(C):
            a = _cmul(bb, (pw_ref[0, 0, j:j + 1, :], pw_ref[0, 1, j:j + 1, :]))
            s = C - 1 - j
            w_ref[s * LANES:(s + 1) * LANES, 0:sw] = a[0].astype(BF16)
            w_ref[s * LANES:(s + 1) * LANES, sw:2 * sw] = a[1].astype(BF16)
            lag = jnp.dot(kj_ref[0, j], spread, preferred_element_type=F32)
            lag = jnp.where(same_group, lag, 0.0).astype(BF16)
            for s0 in range(C - j):
                t_ref[s0 * LANES:(s0 + 1) * LANES, (s0 + j) * LANES:(s0 + j + 1) * LANES] = lag
            g = _cmul(cc, (pw_ref[0, 0, j + 1:j + 2, :], pw_ref[0, 1, j + 1:j + 2, :]))
            v_ref[j * LANES:(j + 1) * LANES, 0:sw] = g[0].astype(BF16)
            v_ref[j * LANES:(j + 1) * LANES, sw:2 * sw] = (-g[1]).astype(BF16)
        c_ref[...] = jnp.zeros_like(c_ref)

    for s in range(C):
        uf_ref[:, s * LANES:(s + 1) * LANES] = u_ref[pl.ds(s, nc, stride=C), :].astype(BF16)
    s_ref[...] = jnp.dot(uf_ref[...], w_ref[...], preferred_element_type=F32)
    lr, li = pw_ref[0, 0, C:C + 1, :], pw_ref[0, 1, C:C + 1, :]
    row = lax.broadcasted_iota(I32, (8, sw), 0)

    def tile_step(i, carry):
        xr, xi = carry
        r0 = pl.multiple_of(i * 8, 8)
        sr = s_ref[pl.ds(r0, 8), 0:sw]
        si = s_ref[pl.ds(r0, 8), sw:2 * sw]
        tr = jnp.zeros((8, sw), F32)
        ti = jnp.zeros((8, sw), F32)
        for r in range(8):
            tr = jnp.where(row == r, xr, tr)
            ti = jnp.where(row == r, xi, ti)
            xr, xi = lr * xr - li * xi + sr[r:r + 1, :], lr * xi + li * xr + si[r:r + 1, :]
        x_ref[pl.ds(r0, 8), 0:sw] = tr
        x_ref[pl.ds(r0, 8), sw:2 * sw] = ti
        return xr, xi

    xr, xi = lax.fori_loop(0, nc // 8, tile_step, (c_ref[0:1, 0:sw], c_ref[0:1, sw:2 * sw]))
    c_ref[0:1, 0:sw] = xr
    c_ref[0:1, sw:2 * sw] = xi
    y_state = lax.dot_general(x_ref[...].astype(BF16), v_ref[...], (((1,), (1,)), ((), ())),
                              preferred_element_type=F32)
    quarter = C // 4
    for qt in range(4):
        rows = (qt + 1) * quarter * LANES
        cols = slice(qt * quarter * LANES, (qt + 1) * quarter * LANES)
        y = jnp.dot(uf_ref[:, 0:rows], t_ref[0:rows, cols], preferred_element_type=F32) + y_state[:, cols]
        for s in range(quarter):
            st = qt * quarter + s
            ys = y[:, s * LANES:(s + 1) * LANES] + d_ref[...] * u_ref[pl.ds(st, nc, stride=C), :]
            o_ref[pl.ds(st, nc, stride=C), :] = jax.nn.gelu(ys)


def s5_mixer(proj, col0, lam_re, lam_im, log_dt, b_re, b_im, c_re, c_im, d, tb=8192):
    T = proj.shape[0]
    C, N, O = S5_CHUNK, S5_STATE, S5_OCT
    W = d.shape[0]
    Q = W // LANES
    tb = min(tb, T)
    pw_q, bb, cc, kq = s5_lane_tile_operands(lam_re, lam_im, log_dt, b_re, b_im, c_re, c_im)
    cb0 = col0 // LANES
    nc = tb // C
    sw = O * N
    quad = lambda a: pl.BlockSpec((1,) + a.shape[1:], lambda q, t: (q, 0, 0, 0))
    return pl.pallas_call(
        _s5_kernel,
        grid=(Q, T // tb),
        in_specs=[pl.BlockSpec((tb, LANES), lambda q, t: (t, cb0 + q)),
                  quad(pw_q), quad(bb), quad(cc), quad(kq),
                  pl.BlockSpec((1, LANES), lambda q, t: (0, q))],
        out_specs=pl.BlockSpec((tb, LANES), lambda q, t: (t, q)),
        out_shape=jax.ShapeDtypeStruct((T, W), F32),
        scratch_shapes=[pltpu.VMEM((C * LANES, C * LANES), BF16),
                        pltpu.VMEM((C * LANES, 2 * sw), BF16),
                        pltpu.VMEM((C * LANES, 2 * sw), BF16),
                        pltpu.VMEM((nc, C * LANES), BF16),
                        pltpu.VMEM((nc, 2 * sw), F32),
                        pltpu.VMEM((nc, 2 * sw), F32),
                        pltpu.VMEM((8, 2 * sw), F32)],
        compiler_params=_cparams(("arbitrary", "arbitrary"), 56),
        name="s5_scan",
    )(proj, pw_q, bb, cc, kq, d.reshape(1, W))


def _store_row_tiles(ref, val, lead=()):
    rows, width = val.shape
    n = width // LANES
    for j in range(n):
        ref[lead + (pl.ds(j, rows, stride=n), slice(None))] = val[:, j * LANES:(j + 1) * LANES]


def _load_row_tiles(ref, rows, n, lead=()):
    return jnp.concatenate([ref[lead + (pl.ds(j, rows, stride=n), slice(None))] for j in range(n)], axis=1)


def _row_tile_spec(tm, width, index_map):
    return pl.BlockSpec((tm * (width // LANES), LANES), index_map)


def _glu_out_kernel(x_ref, yc_ref, yg_ref, wglu_ref, bglu_ref, wout_ref, g_ref, xo_ref, ho_ref):
    yg = yg_ref[...]
    z = jnp.dot(yg.astype(BF16), wglu_ref[...], preferred_element_type=F32) + bglu_ref[...]
    ys = (yg * jax.nn.sigmoid(z)).astype(BF16)
    cw = yc_ref.shape[1]
    xn = (x_ref[...]
          + jnp.dot(yc_ref[...], wout_ref[0:cw, :], preferred_element_type=F32)
          + jnp.dot(ys, wout_ref[cw:, :], preferred_element_type=F32))
    xo_ref[...] = xn
    _store_row_tiles(ho_ref, _rms(xn, g_ref[...]))


def glu_out(x, y_conv, y_gelu, w_glu, b_glu, w_out, g_next, tm=256):
    T, D = x.shape
    CW, SW = y_conv.shape[1], y_gelu.shape[1]
    tm = min(tm, T)
    row = lambda w: pl.BlockSpec((tm, w), lambda i: (i, 0))
    full = lambda a: pl.BlockSpec(a.shape, lambda i: (0,) * a.ndim)
    bg, g2 = b_glu.reshape(1, SW), g_next.reshape(1, D)
    return pl.pallas_call(
        _glu_out_kernel,
        grid=(T // tm,),
        in_specs=[row(D), row(CW), row(SW), full(w_glu), full(bg), full(w_out), full(g2)],
        out_specs=[row(D), _row_tile_spec(tm, D, lambda i: (i, 0))],
        out_shape=[jax.ShapeDtypeStruct((T, D), F32), jax.ShapeDtypeStruct((T * D // LANES, LANES), F32)],
        compiler_params=_cparams(("parallel",), 56),
        name="glu_out",
    )(x, y_conv, y_gelu, w_glu, bg, w_out, g2)


def _proj_residual_kernel(x_ref, a_ref, w_ref, g_ref, xo_ref, ho_ref):
    xn = x_ref[...] + jnp.dot(a_ref[...], w_ref[...], preferred_element_type=F32)
    xo_ref[...] = xn
    _store_row_tiles(ho_ref, _rms(xn, g_ref[...]))


def proj_residual(x, a, w, g_next, tm=256):
    T, D = x.shape
    K = a.shape[1]
    tm = min(tm, T)
    row = lambda w_: pl.BlockSpec((tm, w_), lambda i: (i, 0))
    full = lambda arr: pl.BlockSpec(arr.shape, lambda i: (0,) * arr.ndim)
    g2 = g_next.reshape(1, D)
    return pl.pallas_call(
        _proj_residual_kernel,
        grid=(T // tm,),
        in_specs=[row(D), row(K), full(w), full(g2)],
        out_specs=[row(D), _row_tile_spec(tm, D, lambda i: (i, 0))],
        out_shape=[jax.ShapeDtypeStruct((T, D), F32), jax.ShapeDtypeStruct((T * D // LANES, LANES), F32)],
        compiler_params=_cparams(("parallel",), 56),
        name="proj_residual",
    )(x, a, w, g2)


def _router_kernel(h_ref, w_ref, b_ref, mi_ref, mf_ref, cnt_ref, run_ref):
    i = pl.program_id(0)

    @pl.when(i == 0)
    def _():
        run_ref[...] = jnp.zeros_like(run_ref)

    D = w_ref.shape[0]
    tm = h_ref.shape[0] * LANES // D
    h = _load_row_tiles(h_ref, tm, D // LANES)
    logits = jnp.dot(h, w_ref[...], precision=lax.Precision.HIGHEST,
                     preferred_element_type=F32) + b_ref[...]
    lane = lax.broadcasted_iota(I32, logits.shape, 1)
    neg = jnp.float32(-jnp.inf)
    gl = jnp.where(lane < N_GROUPS, logits, neg)
    gmax = jnp.max(gl, axis=1, keepdims=True)
    gsum = jnp.sum(jnp.where(lane < N_GROUPS, jnp.exp(gl - gmax), 0.0), axis=1, keepdims=True)
    gw = 1.0 / gsum
    gidx = jnp.min(jnp.where(gl == gmax, lane, LANES), axis=1, keepdims=True)
    lo = EXPERT_LANE0 + EXPERTS_PER_GROUP * gidx
    el = jnp.where((lane >= lo) & (lane < lo + EXPERTS_PER_GROUP), logits, neg)
    v1 = jnp.max(el, axis=1, keepdims=True)
    i1 = jnp.min(jnp.where(el == v1, lane, LANES), axis=1, keepdims=True)
    el2 = jnp.where(lane == i1, neg, el)
    v2 = jnp.max(el2, axis=1, keepdims=True)
    i2 = jnp.min(jnp.where(el2 == v2, lane, LANES), axis=1, keepdims=True)
    t = jnp.exp(v2 - v1)
    w1 = gw / (1.0 + t)
    w2 = gw * t / (1.0 + t)
    hit1 = lane == i1
    hit2 = lane == i2
    cnt = (hit1 | hit2).astype(BF16)
    r = lax.broadcasted_iota(I32, (tm, tm), 0)
    c = lax.broadcasted_iota(I32, (tm, tm), 1)
    before = (c < r).astype(BF16)
    cum = jnp.dot(before, cnt, preferred_element_type=F32) + run_ref[0:1, :]
    rank1 = jnp.sum(jnp.where(hit1, cum, 0.0), axis=1, keepdims=True).astype(I32)
    rank2 = jnp.sum(jnp.where(hit2, cum, 0.0), axis=1, keepdims=True).astype(I32)
    run = run_ref[0:1, :] + jnp.sum(cnt.astype(F32), axis=0, keepdims=True)
    run_ref[...] = jnp.broadcast_to(run, run_ref.shape)
    cnt_ref[...] = jnp.broadcast_to(run, cnt_ref.shape)
    code1 = (i1 - EXPERT_LANE0) * MOE_CODE + rank1
    code2 = (i2 - EXPERT_LANE0) * MOE_CODE + rank2
    mi_ref[...] = jnp.where(lane == 0, code1, jnp.where(lane == 1, code2, 0))
    mf_ref[...] = jnp.where(lane == 0, w1, jnp.where(lane == 1, w2, 0.0))


def router(h, w_r, b_r, tm=256):
    D = w_r.shape[0]
    T = h.shape[0] * LANES // D
    tm = min(tm, T)
    row = lambda w: pl.BlockSpec((tm, w), lambda i: (i, 0))
    full = lambda a: pl.BlockSpec(a.shape, lambda i: (0,) * a.ndim)
    return pl.pallas_call(
        _router_kernel,
        grid=(T // tm,),
        in_specs=[_row_tile_spec(tm, D, lambda i: (i, 0)), full(w_r), full(b_r)],
        out_specs=[row(LANES), row(LANES), pl.BlockSpec((8, LANES), lambda i: (0, 0))],
        out_shape=[jax.ShapeDtypeStruct((T, LANES), I32), jax.ShapeDtypeStruct((T, LANES), F32),
                   jax.ShapeDtypeStruct((8, LANES), F32)],
        scratch_shapes=[pltpu.VMEM((8, LANES), F32)],
        compiler_params=_cparams(("arbitrary",), 40),
        name="moe_router",
    )(h, w_r, b_r)


def moe_plan(meta_i, counts, n_blocks):
    B = MOE_BLOCK
    code1, code2 = meta_i[:, 0], meta_i[:, 1]
    cnt = counts[0, EXPERT_LANE0:EXPERT_LANE0 + N_EXPERTS].astype(I32)
    nblk = (cnt + B - 1) // B
    blk_end = jnp.cumsum(nblk)
    blk_off = blk_end - nblk
    b = jnp.arange(n_blocks, dtype=I32)
    total = blk_end[-1]
    owner = jnp.minimum(jnp.sum((blk_end[None, :] <= b[:, None]).astype(I32), axis=1), N_EXPERTS - 1)
    valid = b < total
    last = jnp.maximum(total - 1, 0)
    blk_expert = jnp.where(valid, owner, owner[last])
    blk_src = jnp.where(valid, b, last)
    blk_first = (valid & (b == blk_off[owner])).astype(I32)
    return code1, code2, blk_off * B, blk_expert, blk_src, valid.astype(I32), blk_first


def _row_copy(src, dst, sem):
    return pltpu.make_async_copy(src, dst, sem)


def _sorted_row(code, off_ref):
    return off_ref[code >> MOE_CODE_BITS] + (code & (MOE_CODE - 1))


def _tile_rows(row, n):
    return pl.ds(pl.multiple_of(row * n, n), n)


def _dispatch_kernel(code1_ref, code2_ref, off_ref, hp_ref, xs_in_ref, xs_ref, sem, *, n):
    del xs_in_ref
    tm = hp_ref.shape[0] // n
    base = pl.program_id(0) * tm

    def copies(r):
        src = hp_ref.at[_tile_rows(r, n)]
        return (_row_copy(src, xs_ref.at[_tile_rows(_sorted_row(code1_ref[base + r], off_ref), n)], sem),
                _row_copy(src, xs_ref.at[_tile_rows(_sorted_row(code2_ref[base + r], off_ref), n)], sem))

    def start(r, carry):
        for cp in copies(r):
            cp.start()
        return carry

    lax.fori_loop(0, tm, start, 0)
    for _ in range(2):
        _row_copy(hp_ref, xs_ref.at[pl.ds(0, tm * n)], sem).wait()


def dispatch(hp, width, code1, code2, row_off, xs0, tm=256):
    n = width // LANES
    T = hp.shape[0] // n
    tm = min(tm, T)
    return pl.pallas_call(
        functools.partial(_dispatch_kernel, n=n),
        grid_spec=pltpu.PrefetchScalarGridSpec(
            num_scalar_prefetch=3,
            grid=(T // tm,),
            in_specs=[_row_tile_spec(tm, width, lambda i, c1, c2, off: (i, 0)),
                      pl.BlockSpec(memory_space=pl.ANY)],
            out_specs=pl.BlockSpec(memory_space=pl.ANY),
            scratch_shapes=[pltpu.SemaphoreType.DMA(())]),
        out_shape=jax.ShapeDtypeStruct(xs0.shape, hp.dtype),
        input_output_aliases={4: 0},
        compiler_params=_cparams(("arbitrary",), 32),
        name="moe_dispatch",
    )(code1, code2, row_off, hp, xs0)


def _experts_kernel(be_ref, bs_ref, bv_ref, bf_ref, xs_ref, wg_ref, wu_ref, wd_ref, ys_ref, wg_s, wu_s, wd_s):
    del be_ref, bs_ref
    b = pl.program_id(0)

    @pl.when(bv_ref[b] == 0)
    def _():
        ys_ref[...] = jnp.zeros_like(ys_ref)

    @pl.when(bv_ref[b] > 0)
    def _():
        @pl.when(bf_ref[b] == 1)
        def _():
            wg_s[...] = wg_ref[0].astype(BF16)
            wu_s[...] = wu_ref[0].astype(BF16)
            wd_s[...] = wd_ref[0].astype(BF16)

        D = wg_s.shape[0]
        a = _load_row_tiles(xs_ref, MOE_BLOCK, D // LANES).astype(BF16)
        gate = jnp.dot(a, wg_s[...], preferred_element_type=F32)
        up = jnp.dot(a, wu_s[...], preferred_element_type=F32)
        mid = (jax.nn.silu(gate) * up).astype(BF16)
        _store_row_tiles(ys_ref, jnp.dot(mid, wd_s[...], preferred_element_type=F32))


def experts(xs, blk_expert, blk_src, blk_valid, blk_first, w_gate, w_up, w_down):
    B = MOE_BLOCK
    E, D, FF = w_gate.shape
    nb = xs.shape[0] * LANES // D // B
    return pl.pallas_call(
        _experts_kernel,
        grid_spec=pltpu.PrefetchScalarGridSpec(
            num_scalar_prefetch=4,
            grid=(nb,),
            in_specs=[_row_tile_spec(B, D, lambda b, be, bs, bv, bf: (bs[b], 0)),
                      pl.BlockSpec((1, D, FF), lambda b, be, bs, bv, bf: (be[b], 0, 0)),
                      pl.BlockSpec((1, D, FF), lambda b, be, bs, bv, bf: (be[b], 0, 0)),
                      pl.BlockSpec((1, FF, D), lambda b, be, bs, bv, bf: (be[b], 0, 0))],
            out_specs=_row_tile_spec(B, D, lambda b, be, bs, bv, bf: (b, 0)),
            scratch_shapes=[pltpu.VMEM((D, FF), BF16), pltpu.VMEM((D, FF), BF16),
                            pltpu.VMEM((FF, D), BF16)]),
        out_shape=jax.ShapeDtypeStruct(xs.shape, F32),
        compiler_params=_cparams(("arbitrary",), 56),
        name="moe_experts",
    )(blk_expert, blk_src, blk_valid, blk_first, xs, w_gate, w_up, w_down)


def _combine_kernel(code1_ref, code2_ref, off_ref, x_ref, mf_ref, g_ref, ys_ref, xo_ref, ho_ref, buf, sem):
    tm, D = x_ref.shape
    n = D // LANES
    base = pl.program_id(0) * tm

    def copies(r):
        dst = _tile_rows(r, n)
        return (_row_copy(ys_ref.at[_tile_rows(_sorted_row(code1_ref[base + r], off_ref), n)],
                          buf.at[0, dst], sem),
                _row_copy(ys_ref.at[_tile_rows(_sorted_row(code2_ref[base + r], off_ref), n)],
                          buf.at[1, dst], sem))

    def start(r, carry):
        for cp in copies(r):
            cp.start()
        return carry

    lax.fori_loop(0, tm, start, 0)
    for k in range(2):
        _row_copy(ys_ref.at[pl.ds(0, tm * n)], buf.at[k], sem).wait()
    mf = mf_ref[...]
    xn = (x_ref[...] + mf[:, 0:1] * _load_row_tiles(buf, tm, n, lead=(0,))
          + mf[:, 1:2] * _load_row_tiles(buf, tm, n, lead=(1,)))
    xo_ref[...] = xn
    ho_ref[...] = _rms(xn, g_ref[...]).astype(ho_ref.dtype)


def combine(x, ys, meta_f, code1, code2, row_off, g_next, h_dtype, tm=256):
    T, D = x.shape
    tm = min(tm, T)
    row = lambda w: pl.BlockSpec((tm, w), lambda i, c1, c2, off: (i, 0))
    return pl.pallas_call(
        _combine_kernel,
        grid_spec=pltpu.PrefetchScalarGridSpec(
            num_scalar_prefetch=3,
            grid=(T // tm,),
            in_specs=[row(D), row(LANES), pl.BlockSpec((1, D), lambda i, c1, c2, off: (0, 0)),
                      pl.BlockSpec(memory_space=pl.ANY)],
            out_specs=[row(D), row(D)],
            scratch_shapes=[pltpu.VMEM((2, tm * D // LANES, LANES), F32), pltpu.SemaphoreType.DMA(())]),
        out_shape=[jax.ShapeDtypeStruct((T, D), F32), jax.ShapeDtypeStruct((T, D), h_dtype)],
        compiler_params=_cparams(("arbitrary",), 40),
        name="moe_combine",
    )(code1, code2, row_off, x, meta_f, g_next.reshape(1, D), ys)


def moe_row_blocks(T):
    return (2 * T) // MOE_BLOCK + N_EXPERTS


def moe_layer(x, h, xs_buf, w_group, b_group, w_expert, b_expert, layer, w_gate, w_up, w_down, g_next,
              h_dtype):
    T, D = x.shape
    assert T <= MOE_CODE, "ranks inside one expert must fit the packed (expert, rank) code"
    n_blocks = xs_buf.shape[0] * LANES // D // MOE_BLOCK
    pad = LANES - N_GROUPS - N_EXPERTS
    w_r = jnp.concatenate([w_group, w_expert.reshape(D, N_EXPERTS), jnp.zeros((D, pad), F32)], axis=1)
    b_r = jnp.concatenate([b_group, b_expert.reshape(N_EXPERTS), jnp.zeros((pad,), F32)]).reshape(1, LANES)
    meta_i, meta_f, counts = router(h, w_r, b_r)
    code1, code2, row_off, blk_expert, blk_src, blk_valid, blk_first = moe_plan(meta_i, counts, n_blocks)
    xs = dispatch(h, D, code1, code2, row_off, xs_buf)
    FF = w_gate.shape[-1]
    ys = experts(xs, blk_expert + layer * N_EXPERTS, blk_src, blk_valid, blk_first,
                 w_gate.reshape(-1, D, FF), w_up.reshape(-1, D, FF), w_down.reshape(-1, FF, D))
    xo, ho = combine(x, ys, meta_f, code1, code2, row_off, g_next, h_dtype)
    return xo, ho, xs


def _forget_kernel(h_ref, w_ref, b_ref, o_ref, run_ref):
    @pl.when(pl.program_id(0) == 0)
    def _():
        run_ref[...] = jnp.zeros_like(run_ref)

    hi = lax.Precision.HIGHEST
    z = jnp.dot(h_ref[...].astype(F32), w_ref[...], precision=hi, preferred_element_type=F32) + b_ref[...]
    log_f = jax.nn.log_sigmoid(z)
    tm = z.shape[0]
    r = lax.broadcasted_iota(I32, (tm, tm), 0)
    c = lax.broadcasted_iota(I32, (tm, tm), 1)
    upto = (c <= r).astype(F32)
    cum = jnp.dot(upto, log_f, precision=hi, preferred_element_type=F32) + run_ref[0:1, :]
    o_ref[...] = cum
    run_ref[...] = jnp.broadcast_to(cum[tm - 1:tm, :], run_ref.shape)


def forget_cumsum(h, w_f, b_f, tm=256):
    T, D = h.shape
    tm = min(tm, T)
    return pl.pallas_call(
        _forget_kernel,
        grid=(T // tm,),
        in_specs=[pl.BlockSpec((tm, D), lambda i: (i, 0)),
                  pl.BlockSpec((D, LANES), lambda i: (0, 0)),
                  pl.BlockSpec((1, LANES), lambda i: (0, 0))],
        out_specs=pl.BlockSpec((tm, LANES), lambda i: (i, 0)),
        out_shape=jax.ShapeDtypeStruct((T, LANES), F32),
        scratch_shapes=[pltpu.VMEM((8, LANES), F32)],
        compiler_params=_cparams(("arbitrary",), 40),
        name="forget_cumsum",
    )(h, w_f, b_f)


def _flash_kernel(q_ref, k_ref, v_ref, ck_ref, cq_ref, o_ref, m_ref, acc_ref, *, scale, sub):
    qi = pl.program_id(1)
    bq = q_ref.shape[0]
    dh = FOX_HEAD_DIM
    heads = q_ref.shape[1] // dh
    m_ref[...] = jnp.full_like(m_ref, -jnp.inf)
    acc_ref[...] = jnp.zeros_like(acc_ref)
    ones = jnp.ones((bq, dh), BF16)
    n_sub = bq // sub

    def chunk(start, diagonal):
        def scores(hh):
            cols = slice(hh * dh, (hh + 1) * dh)
            k = k_ref[pl.ds(start, bq), cols]
            return lax.dot_general(q_ref[:, cols], k, (((1,), (1,)), ((), ())), preferred_element_type=F32)

        def softmax(hh, s_all):
            c0 = cq_ref[hh, :, 0:1]
            bias = (c0 - ck_ref[hh, :, pl.ds(start, bq)]) * LOG2E
            ps, alphas = [], []
            for r in range(n_sub):
                rows = slice(r * sub, (r + 1) * sub)
                s = s_all[rows, :] * (scale * LOG2E) + bias
                if diagonal:
                    row = lax.broadcasted_iota(I32, s.shape, 0) + r * sub
                    col = lax.broadcasted_iota(I32, s.shape, 1)
                    s = jnp.where(col <= row, s, -jnp.inf)
                m_prev = m_ref[hh, rows, :]
                m_new = jnp.maximum(m_prev, jnp.max(s, axis=1, keepdims=True))
                m_ref[hh, rows, :] = m_new
                ps.append(jnp.exp2(s - jnp.concatenate([m_new] * (bq // LANES), axis=1)).astype(BF16))
                alphas.append(jnp.exp2(m_prev - m_new))
            return jnp.concatenate(ps, axis=0), alphas

        def update(hh, p, alphas):
            cols = slice(hh * dh, (hh + 1) * dh)
            v1 = jnp.concatenate([v_ref[pl.ds(start, bq), cols], ones], axis=1)
            pv = jnp.dot(p, v1, preferred_element_type=F32)
            for r in range(n_sub):
                rows = slice(r * sub, (r + 1) * sub)
                acc_ref[hh, rows, :] = (jnp.concatenate([alphas[r]] * 2, axis=1) * acc_ref[hh, rows, :]
                                        + pv[rows, :])

        s_next = scores(0)
        for hh in range(heads):
            s_cur = s_next
            if hh + 1 < heads:
                s_next = scores(hh + 1)
            p, alphas = softmax(hh, s_cur)
            update(hh, p, alphas)

    def body(j, carry):
        chunk(pl.multiple_of(j * bq, bq), False)
        return carry

    lax.fori_loop(0, qi, body, 0)
    chunk(pl.multiple_of(qi * bq, bq), True)
    for hh in range(heads):
        o_ref[:, hh * dh:(hh + 1) * dh] = (acc_ref[hh, :, 0:dh] / acc_ref[hh, :, dh:2 * dh]).astype(o_ref.dtype)


def flash_attention(qkv, cum_t, bq=512, sub=32, heads=4):
    T = qkv.shape[0]
    H, dh = FOX_HEADS, FOX_HEAD_DIM
    bq = min(bq, T)
    kern = functools.partial(_flash_kernel, scale=dh ** -0.5, sub=min(sub, bq))
    hw = heads * dh
    nhb = H // heads
    return pl.pallas_call(
        kern,
        grid=(nhb, T // bq),
        in_specs=[pl.BlockSpec((bq, hw), lambda h, i: (i, h)),
                  pl.BlockSpec((T, hw), lambda h, i: (0, nhb + h)),
                  pl.BlockSpec((T, hw), lambda h, i: (0, 2 * nhb + h)),
                  pl.BlockSpec((heads, 1, T), lambda h, i: (h, 0, 0)),
                  pl.BlockSpec((heads, 1, bq), lambda h, i: (h, 0, i))],
        out_specs=pl.BlockSpec((bq, hw), lambda h, i: (i, h)),
        out_shape=jax.ShapeDtypeStruct((T, H * dh), BF16),
        scratch_shapes=[pltpu.VMEM((heads, bq, LANES), F32), pltpu.VMEM((heads, bq, 2 * dh), F32)],
        compiler_params=_cparams(("parallel", "arbitrary"), 48),
        name="fox_attention",
    )(qkv, qkv, qkv, cum_t, cum_t)


def kernel(x, ab_norm, ab_w_in, ab_conv_w, s5_lambda_re, s5_lambda_im, s5_log_dt, s5_b_re, s5_b_im,
           s5_c_re, s5_c_im, s5_d, s5_w_glu, s5_b_glu, ab_w_out, c_norm, c_w_in, c_b_forget, c_w_out,
           ffn_norm, router_w_group, router_b_group, router_w_expert, router_b_expert,
           moe_w_gate, moe_w_up, moe_w_down, final_norm):
    bsz, L, D = x.shape
    depth = ffn_norm.shape[0]
    cw = ab_conv_w.shape[-1]
    xt = x.reshape(bsz * L, D)
    h = None
    xs_buf = jnp.zeros((moe_row_blocks(bsz * L) * MOE_BLOCK * D // LANES, LANES), F32)
    for i in range(depth):
        j = i // 2
        if i % 2 == 0:
            g_in = ab_norm[j]
            if h is not None:
                proj = matmul(h, ab_w_in[j].astype(BF16), F32)
            else:
                proj = norm_matmul(xt, g_in, ab_w_in[j].astype(BF16), F32)
            y_conv = conv_mixer(proj, ab_conv_w[j])
            y_gelu = s5_mixer(proj, 3 * cw, s5_lambda_re[j], s5_lambda_im[j], s5_log_dt[j],
                              s5_b_re[j], s5_b_im[j], s5_c_re[j], s5_c_im[j], s5_d[j])
            xt, hf = glu_out(xt, y_conv, y_gelu, s5_w_glu[j].astype(BF16), s5_b_glu[j],
                             ab_w_out[j].astype(BF16), ffn_norm[i])
        else:
            assert h is not None, "an attention layer always follows a MoE combine that emits its norm"
            hd = FOX_HEADS * FOX_HEAD_DIM
            qkv = matmul(h, c_w_in[j].astype(BF16), BF16, n_cols=3 * hd)
            w_f = jnp.pad(c_w_in[j][:, 3 * hd:], ((0, 0), (0, LANES - FOX_HEADS)))
            b_f = jnp.pad(c_b_forget[j], (0, LANES - FOX_HEADS)).reshape(1, LANES)
            cum = forget_cumsum(h, w_f, b_f)
            cum_t = cum[:, :FOX_HEADS].T.reshape(FOX_HEADS, 1, bsz * L)
            att = flash_attention(qkv, cum_t)
            xt, hf = proj_residual(xt, att, c_w_out[j].astype(BF16), ffn_norm[i])
        last = i == depth - 1
        if last:
            g_next = final_norm
        elif (i + 1) % 2 == 0:
            g_next = ab_norm[(i + 1) // 2]
        else:
            g_next = c_norm[(i + 1) // 2]
        xt, h, xs_buf = moe_layer(xt, hf, xs_buf, router_w_group[i], router_b_group[i], router_w_expert[i],
                                  router_b_expert[i], i, moe_w_gate, moe_w_up, moe_w_down,
                                  g_next, F32 if last else BF16)
    return h.reshape(bsz, L, D)
```

```python
import functools
import math

import jax
import jax.numpy as jnp
from jax import lax
from jax.experimental import pallas as pl
from jax.experimental.pallas import tpu as pltpu

F32 = jnp.float32
BF16 = jnp.bfloat16
I32 = jnp.int32
U32 = jnp.uint32

RMS_EPS = 1e-6
LANES = 128
LOG2E = math.log2(math.e)
MIB = 1024 * 1024

CONV_K = 3
S5_GROUP = 16
S5_STATE = 64
S5_CHUNK = 16
S5_OCT = LANES // S5_GROUP
FOX_HEADS = 16
FOX_HEAD_DIM = 128
N_GROUPS = 4
EXPERTS_PER_GROUP = 8
N_EXPERTS = N_GROUPS * EXPERTS_PER_GROUP
EXPERT_LANE0 = N_GROUPS
MOE_BLOCK = 256
MOE_CODE_BITS = 16
MOE_CODE = 1 << MOE_CODE_BITS


def _cparams(sem, vmem_mib):
    return pltpu.CompilerParams(dimension_semantics=sem, vmem_limit_bytes=vmem_mib * MIB)


def _rms(x, g):
    ms = jnp.mean(x * x, axis=-1, keepdims=True)
    return x * lax.rsqrt(ms + RMS_EPS) * g


def _norm_matmul_kernel(x_ref, g_ref, w_ref, o_ref, h_ref):
    @pl.when(pl.program_id(1) == 0)
    def _():
        h_ref[...] = _rms(x_ref[...], g_ref[...]).astype(BF16)

    o_ref[...] = jnp.dot(h_ref[...], w_ref[...], preferred_element_type=F32).astype(o_ref.dtype)


def norm_matmul(x, g, w, out_dtype, tm=1024, tn=1024):
    T, D = x.shape
    N = w.shape[1]
    tm, tn = min(tm, T), min(tn, N)
    return pl.pallas_call(
        _norm_matmul_kernel,
        grid=(T // tm, N // tn),
        in_specs=[pl.BlockSpec((tm, D), lambda i, j: (i, 0)),
                  pl.BlockSpec((1, D), lambda i, j: (0, 0)),
                  pl.BlockSpec((D, tn), lambda i, j: (0, j))],
        out_specs=pl.BlockSpec((tm, tn), lambda i, j: (i, j)),
        out_shape=jax.ShapeDtypeStruct((T, N), out_dtype),
        scratch_shapes=[pltpu.VMEM((tm, D), BF16)],
        compiler_params=_cparams(("parallel", "arbitrary"), 56),
        name="norm_matmul",
    )(x, g.reshape(1, D), w)


def _matmul_kernel(a_ref, w_ref, o_ref):
    o_ref[...] = jnp.dot(a_ref[...], w_ref[...], preferred_element_type=F32).astype(o_ref.dtype)


def matmul(a, w, out_dtype, n_cols=None, tm=1024, tn=1024):
    T, K = a.shape
    N = w.shape[1] if n_cols is None else n_cols
    tm, tn = min(tm, T), min(tn, N)
    return pl.pallas_call(
        _matmul_kernel,
        grid=(T // tm, N // tn),
        in_specs=[pl.BlockSpec((tm, K), lambda i, j: (i, 0)),
                  pl.BlockSpec((K, tn), lambda i, j: (0, j))],
        out_specs=pl.BlockSpec((tm, tn), lambda i, j: (i, j)),
        out_shape=jax.ShapeDtypeStruct((T, N), out_dtype),
        compiler_params=_cparams(("parallel", "arbitrary"), 48),
        name="matmul",
    )(a, w)


def _conv_kernel(gb_ref, gc_ref, u_ref, gcp_ref, up_ref, w_ref, o_ref):
    w0, w1, w2 = w_ref[0:1, :], w_ref[1:2, :], w_ref[2:3, :]
    v = gc_ref[...] * u_ref[...]
    y = w2 * v + w1 * pltpu.roll(v, 1, 0) + w0 * pltpu.roll(v, 2, 0)
    o_ref[...] = (gb_ref[...] * y).astype(o_ref.dtype)
    vp = gcp_ref[...] * up_ref[...]
    vp = jnp.where(pl.program_id(0) > 0, vp, jnp.zeros_like(vp))
    v8 = v[0:8, :]
    row = lax.broadcasted_iota(I32, v8.shape, 0)
    v1 = jnp.where(row < 1, pltpu.roll(vp, 1, 0), pltpu.roll(v8, 1, 0))
    v2 = jnp.where(row < 2, pltpu.roll(vp, 2, 0), pltpu.roll(v8, 2, 0))
    o_ref[0:8, :] = (gb_ref[0:8, :] * (w2 * v8 + w1 * v1 + w0 * v2)).astype(o_ref.dtype)


def conv_mixer(proj, conv_w, tm=512):
    T = proj.shape[0]
    CW = conv_w.shape[1]
    tm = min(tm, T)
    r8 = tm // 8
    cur = lambda c: pl.BlockSpec((tm, CW), lambda i: (i, c))
    prev = lambda c: pl.BlockSpec((8, CW), lambda i: (jnp.maximum(i * r8 - 1, 0), c))
    return pl.pallas_call(
        _conv_kernel,
        grid=(T // tm,),
        in_specs=[cur(0), cur(1), cur(2), prev(1), prev(2),
                  pl.BlockSpec((CONV_K, CW), lambda i: (0, 0))],
        out_specs=pl.BlockSpec((tm, CW), lambda i: (i, 0)),
        out_shape=jax.ShapeDtypeStruct((T, CW), BF16),
        compiler_params=_cparams(("parallel",), 40),
        name="conv_mixer",
    )(proj, proj, proj, proj, proj, conv_w)


def _cmul(a, b):
    return a[0] * b[0] - a[1] * b[1], a[0] * b[1] + a[1] * b[0]


def s5_matrices(lam_re, lam_im, log_dt, b_re, b_im, c_re, c_im):
    C = S5_CHUNK
    dt = jnp.exp(log_dt)[:, None]
    a, b = lam_re * dt, lam_im * dt
    mag = jnp.exp(a)
    lbar = (mag * jnp.cos(b), mag * jnp.sin(b))
    den = lam_re * lam_re + lam_im * lam_im
    inv_lam = (lam_re / den, -lam_im / den)
    coef = _cmul((lbar[0] - 1.0, lbar[1]), inv_lam)
    bbar = _cmul((coef[0][..., None], coef[1][..., None]), (b_re, b_im))
    j = jnp.arange(C + 1, dtype=F32)[None, :, None]
    pmag = jnp.exp(a[:, None, :] * j)
    pw = (pmag * jnp.cos(b[:, None, :] * j), pmag * jnp.sin(b[:, None, :] * j))
    hi = lax.Precision.HIGHEST
    cp = _cmul((c_re[:, None], c_im[:, None]), (pw[0][:, :C, None, :], pw[1][:, :C, None, :]))
    kj = (jnp.einsum('gjhn,gnk->gjhk', cp[0], bbar[0], precision=hi)
          - jnp.einsum('gjhn,gnk->gjhk', cp[1], bbar[1], precision=hi))
    return pw, bbar, kj


def _lane_tile_blockdiag(m):
    Q, O, r, c = m.shape
    out = jnp.zeros((Q, O, r, O, c), m.dtype)
    for g in range(O):
        out = out.at[:, g, :, g, :].set(m[:, g])
    return out.reshape(Q, O * r, O * c)


def s5_lane_tile_operands(lam_re, lam_im, log_dt, b_re, b_im, c_re, c_im):
    C, H, N, O = S5_CHUNK, S5_GROUP, S5_STATE, S5_OCT
    pw, bbar, kj = s5_matrices(lam_re, lam_im, log_dt, b_re, b_im, c_re, c_im)
    Q = lam_re.shape[0] // O
    pw_q = jnp.stack([p.reshape(Q, O, C + 1, N).transpose(0, 2, 1, 3).reshape(Q, C + 1, O * N) for p in pw],
                     axis=1)
    bb = jnp.stack([_lane_tile_blockdiag(m.transpose(0, 2, 1).reshape(Q, O, H, N)) for m in bbar], axis=1)
    cc = jnp.stack([_lane_tile_blockdiag(m.reshape(Q, O, H, N)) for m in (c_re, c_im)], axis=1)
    kq = kj.reshape(Q, O, C, H, H).transpose(0, 2, 1, 4, 3).reshape(Q, C, O * H, H).astype(BF16)
    return pw_q, bb, cc, kq


def _s5_kernel(u_ref, pw_ref, bb_ref, cc_ref, kj_ref, d_ref, o_ref, t_ref, w_ref, v_ref, uf_ref, s_ref,
               x_ref, c_ref):
    C = S5_CHUNK
    tb = u_ref.shape[0]
    nc = tb // C
    sw = pw_ref.shape[3]

    @pl.when((pl.program_id(0) == 0) & (pl.program_id(1) == 0))
    def _():
        t_ref[...] = jnp.zeros_like(t_ref)

    @pl.when(pl.program_id(1) == 0)
    def _():
        bb = (bb_ref[0, 0], bb_ref[0, 1])
        cc = (cc_ref[0, 0], cc_ref[0, 1])
        H = kj_ref.shape[3]
        hb = H.bit_length() - 1
        spread = ((lax.broadcasted_iota(I32, (H, LANES), 1) & (H - 1))
                  == lax.broadcasted_iota(I32, (H, LANES), 0)).astype(BF16)
        same_group = ((lax.broadcasted_iota(I32, (LANES, LANES), 0) >> hb)
                      == (lax.broadcasted_iota(I32, (LANES, LANES), 1) >> hb))
        for j in range(C):
            a = _cmul(bb, (pw_ref[0, 0, j:j + 1, :], pw_ref[0, 1, j:j + 1, :]))
            s = C - 1 - j
            w_ref[s * LANES:(s + 1) * LANES, 0:sw] = a[0].astype(BF16)
            w_ref[s * LANES:(s + 1) * LANES, sw:2 * sw] = a[1].astype(BF16)
            lag = jnp.dot(kj_ref[0, j], spread, preferred_element_type=F32)
            lag = jnp.where(same_group, lag, 0.0).astype(BF16)
            for s0 in range(C - j):
                t_ref[s0 * LANES:(s0 + 1) * LANES, (s0 + j) * LANES:(s0 + j + 1) * LANES] = lag
            g = _cmul(cc, (pw_ref[0, 0, j + 1:j + 2, :], pw_ref[0, 1, j + 1:j + 2, :]))
            v_ref[j * LANES:(j + 1) * LANES, 0:sw] = g[0].astype(BF16)
            v_ref[j * LANES:(j + 1) * LANES, sw:2 * sw] = (-g[1]).astype(BF16)
        c_ref[...] = jnp.zeros_like(c_ref)

    for s in range(C):
        uf_ref[:, s * LANES:(s + 1) * LANES] = u_ref[pl.ds(s, nc, stride=C), :].astype(BF16)
    s_ref[...] = jnp.dot(uf_ref[...], w_ref[...], preferred_element_type=F32)
    lr, li = pw_ref[0, 0, C:C + 1, :], pw_ref[0, 1, C:C + 1, :]
    row = lax.broadcasted_iota(I32, (8, sw), 0)

    def tile_step(i, carry):
        xr, xi = carry
        r0 = pl.multiple_of(i * 8, 8)
        sr = s_ref[pl.ds(r0, 8), 0:sw]
        si = s_ref[pl.ds(r0, 8), sw:2 * sw]
        tr = jnp.zeros((8, sw), F32)
        ti = jnp.zeros((8, sw), F32)
        for r in range(8):
            tr = jnp.where(row == r, xr, tr)
            ti = jnp.where(row == r, xi, ti)
            xr, xi = lr * xr - li * xi + sr[r:r + 1, :], lr * xi + li * xr + si[r:r + 1, :]
        x_ref[pl.ds(r0, 8), 0:sw] = tr
        x_ref[pl.ds(r0, 8), sw:2 * sw] = ti
        return xr, xi

    xr, xi = lax.fori_loop(0, nc // 8, tile_step, (c_ref[0:1, 0:sw], c_ref[0:1, sw:2 * sw]))
    c_ref[0:1, 0:sw] = xr
    c_ref[0:1, sw:2 * sw] = xi
    y_state = lax.dot_general(x_ref[...].astype(BF16), v_ref[...], (((1,), (1,)), ((), ())),
                              preferred_element_type=F32)
    quarter = C // 4
    for qt in range(4):
        rows = (qt + 1) * quarter * LANES
        cols = slice(qt * quarter * LANES, (qt + 1) * quarter * LANES)
        y = jnp.dot(uf_ref[:, 0:rows], t_ref[0:rows, cols], preferred_element_type=F32) + y_state[:, cols]
        for s in range(quarter):
            st = qt * quarter + s
            ys = y[:, s * LANES:(s + 1) * LANES] + d_ref[...] * u_ref[pl.ds(st, nc, stride=C), :]
            o_ref[pl.ds(st, nc, stride=C), :] = jax.nn.gelu(ys)


def s5_mixer(proj, col0, lam_re, lam_im, log_dt, b_re, b_im, c_re, c_im, d, tb=8192):
    T = proj.shape[0]
    C, N, O = S5_CHUNK, S5_STATE, S5_OCT
    W = d.shape[0]
    Q = W // LANES
    tb = min(tb, T)
    pw_q, bb, cc, kq = s5_lane_tile_operands(lam_re, lam_im, log_dt, b_re, b_im, c_re, c_im)
    cb0 = col0 // LANES
    nc = tb // C
    sw = O * N
    quad = lambda a: pl.BlockSpec((1,) + a.shape[1:], lambda q, t: (q, 0, 0, 0))
    return pl.pallas_call(
        _s5_kernel,
        grid=(Q, T // tb),
        in_specs=[pl.BlockSpec((tb, LANES), lambda q, t: (t, cb0 + q)),
                  quad(pw_q), quad(bb), quad(cc), quad(kq),
                  pl.BlockSpec((1, LANES), lambda q, t: (0, q))],
        out_specs=pl.BlockSpec((tb, LANES), lambda q, t: (t, q)),
        out_shape=jax.ShapeDtypeStruct((T, W), F32),
        scratch_shapes=[pltpu.VMEM((C * LANES, C * LANES), BF16),
                        pltpu.VMEM((C * LANES, 2 * sw), BF16),
                        pltpu.VMEM((C * LANES, 2 * sw), BF16),
                        pltpu.VMEM((nc, C * LANES), BF16),
                        pltpu.VMEM((nc, 2 * sw), F32),
                        pltpu.VMEM((nc, 2 * sw), F32),
                        pltpu.VMEM((8, 2 * sw), F32)],
        compiler_params=_cparams(("arbitrary", "arbitrary"), 56),
        name="s5_scan",
    )(proj, pw_q, bb, cc, kq, d.reshape(1, W))


def _store_row_tiles(ref, val, lead=()):
    rows, width = val.shape
    n = width // LANES
    for j in range(n):
        ref[lead + (pl.ds(j, rows, stride=n), slice(None))] = val[:, j * LANES:(j + 1) * LANES]


def _load_row_tiles(ref, rows, n, lead=()):
    return jnp.concatenate([ref[lead + (pl.ds(j, rows, stride=n), slice(None))] for j in range(n)], axis=1)


def _row_tile_spec(tm, width, index_map):
    return pl.BlockSpec((tm * (width // LANES), LANES), index_map)


def _glu_out_kernel(x_ref, yc_ref, yg_ref, wglu_ref, bglu_ref, wout_ref, g_ref, xo_ref, ho_ref):
    yg = yg_ref[...]
    z = jnp.dot(yg.astype(BF16), wglu_ref[...], preferred_element_type=F32) + bglu_ref[...]
    ys = (yg * jax.nn.sigmoid(z)).astype(BF16)
    cw = yc_ref.shape[1]
    xn = (x_ref[...]
          + jnp.dot(yc_ref[...], wout_ref[0:cw, :], preferred_element_type=F32)
          + jnp.dot(ys, wout_ref[cw:, :], preferred_element_type=F32))
    xo_ref[...] = xn
    _store_row_tiles(ho_ref, _rms(xn, g_ref[...]))


def glu_out(x, y_conv, y_gelu, w_glu, b_glu, w_out, g_next, tm=512):
    T, D = x.shape
    CW, SW = y_conv.shape[1], y_gelu.shape[1]
    tm = min(tm, T)
    row = lambda w: pl.BlockSpec((tm, w), lambda i: (i, 0))
    full = lambda a: pl.BlockSpec(a.shape, lambda i: (0,) * a.ndim)
    bg, g2 = b_glu.reshape(1, SW), g_next.reshape(1, D)
    return pl.pallas_call(
        _glu_out_kernel,
        grid=(T // tm,),
        in_specs=[row(D), row(CW), row(SW), full(w_glu), full(bg), full(w_out), full(g2)],
        out_specs=[row(D), _row_tile_spec(tm, D, lambda i: (i, 0))],
        out_shape=[jax.ShapeDtypeStruct((T, D), F32), jax.ShapeDtypeStruct((T * D // LANES, LANES), F32)],
        compiler_params=_cparams(("parallel",), 56),
        name="glu_out",
    )(x, y_conv, y_gelu, w_glu, bg, w_out, g2)


def _proj_residual_kernel(x_ref, a_ref, w_ref, g_ref, xo_ref, ho_ref):
    xn = x_ref[...] + jnp.dot(a_ref[...], w_ref[...], preferred_element_type=F32)
    xo_ref[...] = xn
    _store_row_tiles(ho_ref, _rms(xn, g_ref[...]))


def proj_residual(x, a, w, g_next, tm=512):
    T, D = x.shape
    K = a.shape[1]
    tm = min(tm, T)
    row = lambda w_: pl.BlockSpec((tm, w_), lambda i: (i, 0))
    full = lambda arr: pl.BlockSpec(arr.shape, lambda i: (0,) * arr.ndim)
    g2 = g_next.reshape(1, D)
    return pl.pallas_call(
        _proj_residual_kernel,
        grid=(T // tm,),
        in_specs=[row(D), row(K), full(w), full(g2)],
        out_specs=[row(D), _row_tile_spec(tm, D, lambda i: (i, 0))],
        out_shape=[jax.ShapeDtypeStruct((T, D), F32), jax.ShapeDtypeStruct((T * D // LANES, LANES), F32)],
        compiler_params=_cparams(("parallel",), 56),
        name="proj_residual",
    )(x, a, w, g2)


def _router_kernel(h_ref, w_ref, b_ref, mi_ref, mf_ref, cnt_ref, hb_ref, run_ref):
    i = pl.program_id(0)

    @pl.when(i == 0)
    def _():
        run_ref[...] = jnp.zeros_like(run_ref)

    D = w_ref.shape[0]
    tm = h_ref.shape[0] * LANES // D
    h = _load_row_tiles(h_ref, tm, D // LANES)
    logits = jnp.dot(h, w_ref[...], precision=lax.Precision.HIGHEST,
                     preferred_element_type=F32) + b_ref[...]
    lane = lax.broadcasted_iota(I32, logits.shape, 1)
    neg = jnp.float32(-jnp.inf)
    gl = jnp.where(lane < N_GROUPS, logits, neg)
    gmax = jnp.max(gl, axis=1, keepdims=True)
    gsum = jnp.sum(jnp.where(lane < N_GROUPS, jnp.exp(gl - gmax), 0.0), axis=1, keepdims=True)
    gw = 1.0 / gsum
    gidx = jnp.min(jnp.where(gl == gmax, lane, LANES), axis=1, keepdims=True)
    lo = EXPERT_LANE0 + EXPERTS_PER_GROUP * gidx
    el = jnp.where((lane >= lo) & (lane < lo + EXPERTS_PER_GROUP), logits, neg)
    v1 = jnp.max(el, axis=1, keepdims=True)
    i1 = jnp.min(jnp.where(el == v1, lane, LANES), axis=1, keepdims=True)
    el2 = jnp.where(lane == i1, neg, el)
    v2 = jnp.max(el2, axis=1, keepdims=True)
    i2 = jnp.min(jnp.where(el2 == v2, lane, LANES), axis=1, keepdims=True)
    t = jnp.exp(v2 - v1)
    w1 = gw / (1.0 + t)
    w2 = gw * t / (1.0 + t)
    hit1 = lane == i1
    hit2 = lane == i2
    cnt = (hit1 | hit2).astype(BF16)
    r = lax.broadcasted_iota(I32, (tm, tm), 0)
    c = lax.broadcasted_iota(I32, (tm, tm), 1)
    before = (c < r).astype(BF16)
    cum = jnp.dot(before, cnt, preferred_element_type=F32) + run_ref[0:1, :]
    rank1 = jnp.sum(jnp.where(hit1, cum, 0.0), axis=1, keepdims=True).astype(I32)
    rank2 = jnp.sum(jnp.where(hit2, cum, 0.0), axis=1, keepdims=True).astype(I32)
    run = run_ref[0:1, :] + jnp.sum(cnt.astype(F32), axis=0, keepdims=True)
    run_ref[...] = jnp.broadcast_to(run, run_ref.shape)
    cnt_ref[...] = jnp.broadcast_to(run, cnt_ref.shape)
    code1 = (i1 - EXPERT_LANE0) * MOE_CODE + rank1
    code2 = (i2 - EXPERT_LANE0) * MOE_CODE + rank2
    mi_ref[...] = jnp.where(lane == 0, code1, jnp.where(lane == 1, code2, 0))
    mf_ref[...] = jnp.where(lane == 0, w1, jnp.where(lane == 1, w2, 0.0))
    hb_ref[...] = h_ref[...].astype(BF16)


def router(h, w_r, b_r, tm=256):
    D = w_r.shape[0]
    T = h.shape[0] * LANES // D
    tm = min(tm, T)
    row = lambda w: pl.BlockSpec((tm, w), lambda i: (i, 0))
    full = lambda a: pl.BlockSpec(a.shape, lambda i: (0,) * a.ndim)
    return pl.pallas_call(
        _router_kernel,
        grid=(T // tm,),
        in_specs=[_row_tile_spec(tm, D, lambda i: (i, 0)), full(w_r), full(b_r)],
        out_specs=[row(LANES), row(LANES), pl.BlockSpec((8, LANES), lambda i: (0, 0)),
                   _row_tile_spec(tm, D, lambda i: (i, 0))],
        out_shape=[jax.ShapeDtypeStruct((T, LANES), I32), jax.ShapeDtypeStruct((T, LANES), F32),
                   jax.ShapeDtypeStruct((8, LANES), F32), jax.ShapeDtypeStruct(h.shape, BF16)],
        scratch_shapes=[pltpu.VMEM((8, LANES), F32)],
        compiler_params=_cparams(("arbitrary",), 40),
        name="moe_router",
    )(h, w_r, b_r)


def moe_plan(meta_i, counts, n_blocks):
    B = MOE_BLOCK
    code1, code2 = meta_i[:, 0], meta_i[:, 1]
    cnt = counts[0, EXPERT_LANE0:EXPERT_LANE0 + N_EXPERTS].astype(I32)
    nblk = (cnt + B - 1) // B
    blk_end = jnp.cumsum(nblk)
    blk_off = blk_end - nblk
    b = jnp.arange(n_blocks, dtype=I32)
    total = blk_end[-1]
    owner = jnp.minimum(jnp.sum((blk_end[None, :] <= b[:, None]).astype(I32), axis=1), N_EXPERTS - 1)
    valid = b < total
    last = jnp.maximum(total - 1, 0)
    blk_expert = jnp.where(valid, owner, owner[last])
    blk_src = jnp.where(valid, b, last)
    blk_first = (valid & (b == blk_off[owner])).astype(I32)
    return code1, code2, blk_off * B, blk_expert, blk_src, valid.astype(I32), blk_first


def _row_copy(src, dst, sem):
    return pltpu.make_async_copy(src, dst, sem)


def _sorted_row(code, off_ref):
    return off_ref[code >> MOE_CODE_BITS] + (code & (MOE_CODE - 1))


def _tile_rows(row, n):
    return pl.ds(pl.multiple_of(row * n, n), n)


def _dispatch_kernel(code1_ref, code2_ref, off_ref, hp_ref, xs_in_ref, xs_ref, sem, *, n):
    del xs_in_ref
    tm = hp_ref.shape[0] // n
    base = pl.program_id(0) * tm

    def copies(r):
        src = hp_ref.at[_tile_rows(r, n)]
        return (_row_copy(src, xs_ref.at[_tile_rows(_sorted_row(code1_ref[base + r], off_ref), n)], sem),
                _row_copy(src, xs_ref.at[_tile_rows(_sorted_row(code2_ref[base + r], off_ref), n)], sem))

    def start(r, carry):
        for cp in copies(r):
            cp.start()
        return carry

    lax.fori_loop(0, tm, start, 0)
    for _ in range(2):
        _row_copy(hp_ref, xs_ref.at[pl.ds(0, tm * n)], sem).wait()


def dispatch(hp, width, code1, code2, row_off, xs0, tm=256):
    n = width // LANES
    T = hp.shape[0] // n
    tm = min(tm, T)
    return pl.pallas_call(
        functools.partial(_dispatch_kernel, n=n),
        grid_spec=pltpu.PrefetchScalarGridSpec(
            num_scalar_prefetch=3,
            grid=(T // tm,),
            in_specs=[_row_tile_spec(tm, width, lambda i, c1, c2, off: (i, 0)),
                      pl.BlockSpec(memory_space=pl.ANY)],
            out_specs=pl.BlockSpec(memory_space=pl.ANY),
            scratch_shapes=[pltpu.SemaphoreType.DMA(())]),
        out_shape=jax.ShapeDtypeStruct(xs0.shape, hp.dtype),
        input_output_aliases={4: 0},
        compiler_params=_cparams(("arbitrary",), 32),
        name="moe_dispatch",
    )(code1, code2, row_off, hp, xs0)


def _experts_kernel(be_ref, bs_ref, bv_ref, bf_ref, xs_ref, wg_ref, wu_ref, wd_ref, ys_ref, wg_s, wu_s, wd_s,
                    rt_s):
    del be_ref, bs_ref
    b = pl.program_id(0)

    @pl.when(bv_ref[b] == 0)
    def _():
        ys_ref[...] = jnp.zeros_like(ys_ref)

    @pl.when(bv_ref[b] > 0)
    def _():
        @pl.when(bf_ref[b] == 1)
        def _():
            wg_s[...] = wg_ref[0].astype(BF16)
            wu_s[...] = wu_ref[0].astype(BF16)
            wd_s[...] = wd_ref[0].astype(BF16)

        D = wg_s.shape[0]
        rt_s[...] = xs_ref[...].astype(F32)
        a = _load_row_tiles(rt_s, MOE_BLOCK, D // LANES).astype(BF16)
        gate = jnp.dot(a, wg_s[...], preferred_element_type=F32)
        up = jnp.dot(a, wu_s[...], preferred_element_type=F32)
        mid = (jax.nn.silu(gate) * up).astype(BF16)
        _store_row_tiles(rt_s, jnp.dot(mid, wd_s[...], preferred_element_type=F32))
        ys_ref[...] = rt_s[...].astype(ys_ref.dtype)


def experts(xs, blk_expert, blk_src, blk_valid, blk_first, w_gate, w_up, w_down):
    B = MOE_BLOCK
    E, D, FF = w_gate.shape
    nb = xs.shape[0] * LANES // D // B
    return pl.pallas_call(
        _experts_kernel,
        grid_spec=pltpu.PrefetchScalarGridSpec(
            num_scalar_prefetch=4,
            grid=(nb,),
            in_specs=[_row_tile_spec(B, D, lambda b, be, bs, bv, bf: (bs[b], 0)),
                      pl.BlockSpec((1, D, FF), lambda b, be, bs, bv, bf: (be[b], 0, 0)),
                      pl.BlockSpec((1, D, FF), lambda b, be, bs, bv, bf: (be[b], 0, 0)),
                      pl.BlockSpec((1, FF, D), lambda b, be, bs, bv, bf: (be[b], 0, 0))],
            out_specs=_row_tile_spec(B, D, lambda b, be, bs, bv, bf: (b, 0)),
            scratch_shapes=[pltpu.VMEM((D, FF), BF16), pltpu.VMEM((D, FF), BF16),
                            pltpu.VMEM((FF, D), BF16), pltpu.VMEM((B * D // LANES, LANES), F32)]),
        out_shape=jax.ShapeDtypeStruct(xs.shape, xs.dtype),
        compiler_params=_cparams(("arbitrary",), 56),
        name="moe_experts",
    )(blk_expert, blk_src, blk_valid, blk_first, xs, w_gate, w_up, w_down)


def _combine_kernel(code1_ref, code2_ref, off_ref, x_ref, mf_ref, g_ref, ys_ref, xo_ref, ho_ref, buf, rt_s,
                    sem):
    tm, D = x_ref.shape
    n = D // LANES
    base = pl.program_id(0) * tm

    def copies(r):
        dst = _tile_rows(r, n)
        return (_row_copy(ys_ref.at[_tile_rows(_sorted_row(code1_ref[base + r], off_ref), n)],
                          buf.at[0, dst], sem),
                _row_copy(ys_ref.at[_tile_rows(_sorted_row(code2_ref[base + r], off_ref), n)],
                          buf.at[1, dst], sem))

    def start(r, carry):
        for cp in copies(r):
            cp.start()
        return carry

    lax.fori_loop(0, tm, start, 0)
    for k in range(2):
        _row_copy(ys_ref.at[pl.ds(0, tm * n)], buf.at[k], sem).wait()
    mf = mf_ref[...]
    rt_s[...] = buf[...].astype(F32)
    xn = (x_ref[...] + mf[:, 0:1] * _load_row_tiles(rt_s, tm, n, lead=(0,))
          + mf[:, 1:2] * _load_row_tiles(rt_s, tm, n, lead=(1,)))
    xo_ref[...] = xn
    ho_ref[...] = _rms(xn, g_ref[...]).astype(ho_ref.dtype)


def combine(x, ys, meta_f, code1, code2, row_off, g_next, h_dtype, tm=256):
    T, D = x.shape
    tm = min(tm, T)
    row = lambda w: pl.BlockSpec((tm, w), lambda i, c1, c2, off: (i, 0))
    return pl.pallas_call(
        _combine_kernel,
        grid_spec=pltpu.PrefetchScalarGridSpec(
            num_scalar_prefetch=3,
            grid=(T // tm,),
            in_specs=[row(D), row(LANES), pl.BlockSpec((1, D), lambda i, c1, c2, off: (0, 0)),
                      pl.BlockSpec(memory_space=pl.ANY)],
            out_specs=[row(D), row(D)],
            scratch_shapes=[pltpu.VMEM((2, tm * D // LANES, LANES), ys.dtype),
                            pltpu.VMEM((2, tm * D // LANES, LANES), F32), pltpu.SemaphoreType.DMA(())]),
        out_shape=[jax.ShapeDtypeStruct((T, D), F32), jax.ShapeDtypeStruct((T, D), h_dtype)],
        compiler_params=_cparams(("arbitrary",), 40),
        name="moe_combine",
    )(code1, code2, row_off, x, meta_f, g_next.reshape(1, D), ys)


def moe_row_blocks(T):
    return (2 * T) // MOE_BLOCK + N_EXPERTS


def moe_layer(x, h, xs_buf, w_group, b_group, w_expert, b_expert, layer, w_gate, w_up, w_down, g_next,
              h_dtype):
    T, D = x.shape
    assert T <= MOE_CODE, "ranks inside one expert must fit the packed (expert, rank) code"
    n_blocks = xs_buf.shape[0] * LANES // D // MOE_BLOCK
    pad = LANES - N_GROUPS - N_EXPERTS
    w_r = jnp.concatenate([w_group, w_expert.reshape(D, N_EXPERTS), jnp.zeros((D, pad), F32)], axis=1)
    b_r = jnp.concatenate([b_group, b_expert.reshape(N_EXPERTS), jnp.zeros((pad,), F32)]).reshape(1, LANES)
    meta_i, meta_f, counts, hb = router(h, w_r, b_r)
    code1, code2, row_off, blk_expert, blk_src, blk_valid, blk_first = moe_plan(meta_i, counts, n_blocks)
    xs = dispatch(hb, D, code1, code2, row_off, xs_buf)
    FF = w_gate.shape[-1]
    ys = experts(xs, blk_expert + layer * N_EXPERTS, blk_src, blk_valid, blk_first,
                 w_gate.reshape(-1, D, FF), w_up.reshape(-1, D, FF), w_down.reshape(-1, FF, D))
    xo, ho = combine(x, ys, meta_f, code1, code2, row_off, g_next, h_dtype)
    return xo, ho, xs


def _forget_kernel(h_ref, w_ref, b_ref, o_ref, run_ref):
    @pl.when(pl.program_id(0) == 0)
    def _():
        run_ref[...] = jnp.zeros_like(run_ref)

    hi = lax.Precision.HIGHEST
    z = jnp.dot(h_ref[...].astype(F32), w_ref[...], precision=hi, preferred_element_type=F32) + b_ref[...]
    log_f = jax.nn.log_sigmoid(z)
    tm = z.shape[0]
    r = lax.broadcasted_iota(I32, (tm, tm), 0)
    c = lax.broadcasted_iota(I32, (tm, tm), 1)
    upto = (c <= r).astype(F32)
    cum = jnp.dot(upto, log_f, precision=hi, preferred_element_type=F32) + run_ref[0:1, :]
    o_ref[...] = cum
    run_ref[...] = jnp.broadcast_to(cum[tm - 1:tm, :], run_ref.shape)


def forget_cumsum(h, w_f, b_f, tm=256):
    T, D = h.shape
    tm = min(tm, T)
    return pl.pallas_call(
        _forget_kernel,
        grid=(T // tm,),
        in_specs=[pl.BlockSpec((tm, D), lambda i: (i, 0)),
                  pl.BlockSpec((D, LANES), lambda i: (0, 0)),
                  pl.BlockSpec((1, LANES), lambda i: (0, 0))],
        out_specs=pl.BlockSpec((tm, LANES), lambda i: (i, 0)),
        out_shape=jax.ShapeDtypeStruct((T, LANES), F32),
        scratch_shapes=[pltpu.VMEM((8, LANES), F32)],
        compiler_params=_cparams(("arbitrary",), 40),
        name="forget_cumsum",
    )(h, w_f, b_f)


def _flash_kernel(q_ref, k_ref, v_ref, ck_ref, cq_ref, o_ref, m_ref, acc_ref, *, scale, sub):
    qi = pl.program_id(1)
    bq = q_ref.shape[0]
    dh = FOX_HEAD_DIM
    heads = q_ref.shape[1] // dh
    m_ref[...] = jnp.full_like(m_ref, -jnp.inf)
    acc_ref[...] = jnp.zeros_like(acc_ref)
    ones = jnp.ones((bq, dh), BF16)
    n_sub = bq // sub

    def chunk(start, diagonal):
        def scores(hh):
            cols = slice(hh * dh, (hh + 1) * dh)
            k = k_ref[pl.ds(start, bq), cols]
            return lax.dot_general(q_ref[:, cols], k, (((1,), (1,)), ((), ())), preferred_element_type=F32)

        def softmax(hh, s_all):
            c0 = cq_ref[hh, :, 0:1]
            bias = (c0 - ck_ref[hh, :, pl.ds(start, bq)]) * LOG2E
            ps, alphas = [], []
            for r in range(n_sub):
                rows = slice(r * sub, (r + 1) * sub)
                s = s_all[rows, :] * (scale * LOG2E) + bias
                if diagonal:
                    row = lax.broadcasted_iota(I32, s.shape, 0) + r * sub
                    col = lax.broadcasted_iota(I32, s.shape, 1)
                    s = jnp.where(col <= row, s, -jnp.inf)
                m_prev = m_ref[hh, rows, :]
                m_new = jnp.maximum(m_prev, jnp.max(s, axis=1, keepdims=True))
                m_ref[hh, rows, :] = m_new
                ps.append(jnp.exp2(s - jnp.concatenate([m_new] * (bq // LANES), axis=1)).astype(BF16))
                alphas.append(jnp.exp2(m_prev - m_new))
            return jnp.concatenate(ps, axis=0), alphas

        def update(hh, p, alphas):
            cols = slice(hh * dh, (hh + 1) * dh)
            v1 = jnp.concatenate([v_ref[pl.ds(start, bq), cols], ones], axis=1)
            pv = jnp.dot(p, v1, preferred_element_type=F32)
            for r in range(n_sub):
                rows = slice(r * sub, (r + 1) * sub)
                acc_ref[hh, rows, :] = (jnp.concatenate([alphas[r]] * 2, axis=1) * acc_ref[hh, rows, :]
                                        + pv[rows, :])

        s_next = scores(0)
        for hh in range(heads):
            s_cur = s_next
            if hh + 1 < heads:
                s_next = scores(hh + 1)
            p, alphas = softmax(hh, s_cur)
            update(hh, p, alphas)

    def body(j, carry):
        chunk(pl.multiple_of(j * bq, bq), False)
        return carry

    lax.fori_loop(0, qi, body, 0)
    chunk(pl.multiple_of(qi * bq, bq), True)
    for hh in range(heads):
        o_ref[:, hh * dh:(hh + 1) * dh] = (acc_ref[hh, :, 0:dh] / acc_ref[hh, :, dh:2 * dh]).astype(o_ref.dtype)


def flash_attention(qkv, cum_t, bq=512, sub=32, heads=4):
    T = qkv.shape[0]
    H, dh = FOX_HEADS, FOX_HEAD_DIM
    bq = min(bq, T)
    kern = functools.partial(_flash_kernel, scale=dh ** -0.5, sub=min(sub, bq))
    hw = heads * dh
    nhb = H // heads
    return pl.pallas_call(
        kern,
        grid=(nhb, T // bq),
        in_specs=[pl.BlockSpec((bq, hw), lambda h, i: (i, h)),
                  pl.BlockSpec((T, hw), lambda h, i: (0, nhb + h)),
                  pl.BlockSpec((T, hw), lambda h, i: (0, 2 * nhb + h)),
                  pl.BlockSpec((heads, 1, T), lambda h, i: (h, 0, 0)),
                  pl.BlockSpec((heads, 1, bq), lambda h, i: (h, 0, i))],
        out_specs=pl.BlockSpec((bq, hw), lambda h, i: (i, h)),
        out_shape=jax.ShapeDtypeStruct((T, H * dh), BF16),
        scratch_shapes=[pltpu.VMEM((heads, bq, LANES), F32), pltpu.VMEM((heads, bq, 2 * dh), F32)],
        compiler_params=_cparams(("parallel", "arbitrary"), 48),
        name="fox_attention",
    )(qkv, qkv, qkv, cum_t, cum_t)


def kernel(x, ab_norm, ab_w_in, ab_conv_w, s5_lambda_re, s5_lambda_im, s5_log_dt, s5_b_re, s5_b_im,
           s5_c_re, s5_c_im, s5_d, s5_w_glu, s5_b_glu, ab_w_out, c_norm, c_w_in, c_b_forget, c_w_out,
           ffn_norm, router_w_group, router_b_group, router_w_expert, router_b_expert,
           moe_w_gate, moe_w_up, moe_w_down, final_norm):
    bsz, L, D = x.shape
    depth = ffn_norm.shape[0]
    cw = ab_conv_w.shape[-1]
    xt = x.reshape(bsz * L, D)
    h = None
    xs_buf = jnp.zeros((moe_row_blocks(bsz * L) * MOE_BLOCK * D // LANES, LANES), BF16)
    for i in range(depth):
        j = i // 2
        if i % 2 == 0:
            g_in = ab_norm[j]
            if h is not None:
                proj = matmul(h, ab_w_in[j].astype(BF16), F32)
            else:
                proj = norm_matmul(xt, g_in, ab_w_in[j].astype(BF16), F32)
            y_conv = conv_mixer(proj, ab_conv_w[j])
            y_gelu = s5_mixer(proj, 3 * cw, s5_lambda_re[j], s5_lambda_im[j], s5_log_dt[j],
                              s5_b_re[j], s5_b_im[j], s5_c_re[j], s5_c_im[j], s5_d[j])
            xt, hf = glu_out(xt, y_conv, y_gelu, s5_w_glu[j].astype(BF16), s5_b_glu[j],
                             ab_w_out[j].astype(BF16), ffn_norm[i])
        else:
            assert h is not None, "an attention layer always follows a MoE combine that emits its norm"
            hd = FOX_HEADS * FOX_HEAD_DIM
            qkv = matmul(h, c_w_in[j].astype(BF16), BF16, n_cols=3 * hd)
            w_f = jnp.pad(c_w_in[j][:, 3 * hd:], ((0, 0), (0, LANES - FOX_HEADS)))
            b_f = jnp.pad(c_b_forget[j], (0, LANES - FOX_HEADS)).reshape(1, LANES)
            cum = forget_cumsum(h, w_f, b_f)
            cum_t = cum[:, :FOX_HEADS].T.reshape(FOX_HEADS, 1, bsz * L)
            att = flash_attention(qkv, cum_t)
            xt, hf = proj_residual(xt, att, c_w_out[j].astype(BF16), ffn_norm[i])
        last = i == depth - 1
        if last:
            g_next = final_norm
        elif (i + 1) % 2 == 0:
            g_next = ab_norm[(i + 1) // 2]
        else:
            g_next = c_norm[(i + 1) // 2]
        xt, h, xs_buf = moe_layer(xt, hf, xs_buf, router_w_group[i], router_b_group[i], router_w_expert[i],
                                  router_b_expert[i], i, moe_w_gate, moe_w_up, moe_w_down,
                                  g_next, F32 if last else BF16)
    return h.reshape(bsz, L, D)
```

```python
import functools
import math

import jax
import jax.numpy as jnp
from jax import lax
from jax.experimental import pallas as pl
from jax.experimental.pallas import tpu as pltpu

F32 = jnp.float32
BF16 = jnp.bfloat16
I32 = jnp.int32
U32 = jnp.uint32

RMS_EPS = 1e-6
LANES = 128
LOG2E = math.log2(math.e)
MIB = 1024 * 1024

CONV_K = 3
S5_GROUP = 16
S5_STATE = 64
S5_CHUNK = 16
S5_OCT = LANES // S5_GROUP
FOX_HEADS = 16
FOX_HEAD_DIM = 128
N_GROUPS = 4
EXPERTS_PER_GROUP = 8
N_EXPERTS = N_GROUPS * EXPERTS_PER_GROUP
EXPERT_LANE0 = N_GROUPS
MOE_BLOCK = 256
MOE_CODE_BITS = 16
MOE_CODE = 1 << MOE_CODE_BITS


def _cparams(sem, vmem_mib):
    return pltpu.CompilerParams(dimension_semantics=sem, vmem_limit_bytes=vmem_mib * MIB)


def _bf16_pieces(x, n):
    pieces = []
    for _ in range(n):
        p = x.astype(BF16)
        pieces.append(p)
        x = x - p.astype(F32)
    return pieces


def _rms(x, g):
    ms = jnp.mean(x * x, axis=-1, keepdims=True)
    return x * lax.rsqrt(ms + RMS_EPS) * g


def _norm_matmul_kernel(x_ref, g_ref, w_ref, o_ref, h_ref):
    @pl.when(pl.program_id(1) == 0)
    def _():
        h_ref[...] = _rms(x_ref[...], g_ref[...]).astype(BF16)

    o_ref[...] = jnp.dot(h_ref[...], w_ref[...], preferred_element_type=F32).astype(o_ref.dtype)


def norm_matmul(x, g, w, out_dtype, tm=1024, tn=1024):
    T, D = x.shape
    N = w.shape[1]
    tm, tn = min(tm, T), min(tn, N)
    return pl.pallas_call(
        _norm_matmul_kernel,
        grid=(T // tm, N // tn),
        in_specs=[pl.BlockSpec((tm, D), lambda i, j: (i, 0)),
                  pl.BlockSpec((1, D), lambda i, j: (0, 0)),
                  pl.BlockSpec((D, tn), lambda i, j: (0, j))],
        out_specs=pl.BlockSpec((tm, tn), lambda i, j: (i, j)),
        out_shape=jax.ShapeDtypeStruct((T, N), out_dtype),
        scratch_shapes=[pltpu.VMEM((tm, D), BF16)],
        compiler_params=_cparams(("parallel", "arbitrary"), 56),
        name="norm_matmul",
    )(x, g.reshape(1, D), w)


def _matmul_kernel(a_ref, w_ref, o_ref):
    o_ref[...] = jnp.dot(a_ref[...], w_ref[...], preferred_element_type=F32).astype(o_ref.dtype)


def matmul(a, w, out_dtype, n_cols=None, tm=1024, tn=1024):
    T, K = a.shape
    N = w.shape[1] if n_cols is None else n_cols
    tm, tn = min(tm, T), min(tn, N)
    return pl.pallas_call(
        _matmul_kernel,
        grid=(T // tm, N // tn),
        in_specs=[pl.BlockSpec((tm, K), lambda i, j: (i, 0)),
                  pl.BlockSpec((K, tn), lambda i, j: (0, j))],
        out_specs=pl.BlockSpec((tm, tn), lambda i, j: (i, j)),
        out_shape=jax.ShapeDtypeStruct((T, N), out_dtype),
        compiler_params=_cparams(("parallel", "arbitrary"), 48),
        name="matmul",
    )(a, w)


def _conv_kernel(gb_ref, gc_ref, u_ref, gcp_ref, up_ref, w_ref, o_ref):
    w0, w1, w2 = w_ref[0:1, :], w_ref[1:2, :], w_ref[2:3, :]
    v = gc_ref[...] * u_ref[...]
    y = w2 * v + w1 * pltpu.roll(v, 1, 0) + w0 * pltpu.roll(v, 2, 0)
    o_ref[...] = (gb_ref[...] * y).astype(o_ref.dtype)
    vp = gcp_ref[...] * up_ref[...]
    vp = jnp.where(pl.program_id(0) > 0, vp, jnp.zeros_like(vp))
    v8 = v[0:8, :]
    row = lax.broadcasted_iota(I32, v8.shape, 0)
    v1 = jnp.where(row < 1, pltpu.roll(vp, 1, 0), pltpu.roll(v8, 1, 0))
    v2 = jnp.where(row < 2, pltpu.roll(vp, 2, 0), pltpu.roll(v8, 2, 0))
    o_ref[0:8, :] = (gb_ref[0:8, :] * (w2 * v8 + w1 * v1 + w0 * v2)).astype(o_ref.dtype)


def conv_mixer(proj, conv_w, tm=512):
    T = proj.shape[0]
    CW = conv_w.shape[1]
    tm = min(tm, T)
    r8 = tm // 8
    cur = lambda c: pl.BlockSpec((tm, CW), lambda i: (i, c))
    prev = lambda c: pl.BlockSpec((8, CW), lambda i: (jnp.maximum(i * r8 - 1, 0), c))
    return pl.pallas_call(
        _conv_kernel,
        grid=(T // tm,),
        in_specs=[cur(0), cur(1), cur(2), prev(1), prev(2),
                  pl.BlockSpec((CONV_K, CW), lambda i: (0, 0))],
        out_specs=pl.BlockSpec((tm, CW), lambda i: (i, 0)),
        out_shape=jax.ShapeDtypeStruct((T, CW), BF16),
        compiler_params=_cparams(("parallel",), 40),
        name="conv_mixer",
    )(proj, proj, proj, proj, proj, conv_w)


def _cmul(a, b):
    return a[0] * b[0] - a[1] * b[1], a[0] * b[1] + a[1] * b[0]


def s5_matrices(lam_re, lam_im, log_dt, b_re, b_im, c_re, c_im):
    C = S5_CHUNK
    dt = jnp.exp(log_dt)[:, None]
    a, b = lam_re * dt, lam_im * dt
    mag = jnp.exp(a)
    lbar = (mag * jnp.cos(b), mag * jnp.sin(b))
    den = lam_re * lam_re + lam_im * lam_im
    inv_lam = (lam_re / den, -lam_im / den)
    coef = _cmul((lbar[0] - 1.0, lbar[1]), inv_lam)
    bbar = _cmul((coef[0][..., None], coef[1][..., None]), (b_re, b_im))
    j = jnp.arange(C + 1, dtype=F32)[None, :, None]
    pmag = jnp.exp(a[:, None, :] * j)
    pw = (pmag * jnp.cos(b[:, None, :] * j), pmag * jnp.sin(b[:, None, :] * j))
    hi = lax.Precision.HIGHEST
    cp = _cmul((c_re[:, None], c_im[:, None]), (pw[0][:, :C, None, :], pw[1][:, :C, None, :]))
    kj = (jnp.einsum('gjhn,gnk->gjhk', cp[0], bbar[0], precision=hi)
          - jnp.einsum('gjhn,gnk->gjhk', cp[1], bbar[1], precision=hi))
    return pw, bbar, kj


def _lane_tile_blockdiag(m):
    Q, O, r, c = m.shape
    out = jnp.zeros((Q, O, r, O, c), m.dtype)
    for g in range(O):
        out = out.at[:, g, :, g, :].set(m[:, g])
    return out.reshape(Q, O * r, O * c)


def s5_lane_tile_operands(lam_re, lam_im, log_dt, b_re, b_im, c_re, c_im):
    C, H, N, O = S5_CHUNK, S5_GROUP, S5_STATE, S5_OCT
    pw, bbar, kj = s5_matrices(lam_re, lam_im, log_dt, b_re, b_im, c_re, c_im)
    Q = lam_re.shape[0] // O
    pw_q = jnp.stack([p.reshape(Q, O, C + 1, N).transpose(0, 2, 1, 3).reshape(Q, C + 1, O * N) for p in pw],
                     axis=1)
    bb = jnp.stack([_lane_tile_blockdiag(m.transpose(0, 2, 1).reshape(Q, O, H, N)) for m in bbar], axis=1)
    cc = jnp.stack([_lane_tile_blockdiag(m.reshape(Q, O, H, N)) for m in (c_re, c_im)], axis=1)
    kq = kj.reshape(Q, O, C, H, H).transpose(0, 2, 1, 4, 3).reshape(Q, C, O * H, H).astype(BF16)
    return pw_q, bb, cc, kq


def _s5_kernel(u_ref, pw_ref, bb_ref, cc_ref, kj_ref, d_ref, o_ref, t_ref, w_ref, v_ref, uf_ref, s_ref,
               x_ref, c_ref):
    C = S5_CHUNK
    tb = u_ref.shape[0]
    nc = tb // C
    sw = pw_ref.shape[3]

    @pl.when((pl.program_id(0) == 0) & (pl.program_id(1) == 0))
    def _():
        t_ref[...] = jnp.zeros_like(t_ref)

    @pl.when(pl.program_id(1) == 0)
    def _():
        bb = (bb_ref[0, 0], bb_ref[0, 1])
        cc = (cc_ref[0, 0], cc_ref[0, 1])
        H = kj_ref.shape[3]
        hb = H.bit_length() - 1
        spread = ((lax.broadcasted_iota(I32, (H, LANES), 1) & (H - 1))
                  == lax.broadcasted_iota(I32, (H, LANES), 0)).astype(BF16)
        same_group = ((lax.broadcasted_iota(I32, (LANES, LANES), 0) >> hb)
                      == (lax.broadcasted_iota(I32, (LANES, LANES), 1) >> hb))
        for j in range(C):
            a = _cmul(bb, (pw_ref[0, 0, j:j + 1, :], pw_ref[0, 1, j:j + 1, :]))
            s = C - 1 - j
            w_ref[s * LANES:(s + 1) * LANES, 0:sw] = a[0].astype(BF16)
            w_ref[s * LANES:(s + 1) * LANES, sw:2 * sw] = a[1].astype(BF16)
            lag = jnp.dot(kj_ref[0, j], spread, preferred_element_type=F32)
            lag = jnp.where(same_group, lag, 0.0).astype(BF16)
            for s0 in range(C - j):
                t_ref[s0 * LANES:(s0 + 1) * LANES, (s0 + j) * LANES:(s0 + j + 1) * LANES] = lag
            g = _cmul(cc, (pw_ref[0, 0, j + 1:j + 2, :], pw_ref[0, 1, j + 1:j + 2, :]))
            v_ref[j * LANES:(j + 1) * LANES, 0:sw] = g[0].astype(BF16)
            v_ref[j * LANES:(j + 1) * LANES, sw:2 * sw] = (-g[1]).astype(BF16)
        c_ref[...] = jnp.zeros_like(c_ref)

    for s in range(C):
        uf_ref[:, s * LANES:(s + 1) * LANES] = u_ref[pl.ds(s, nc, stride=C), :].astype(BF16)
    s_ref[...] = jnp.dot(uf_ref[...], w_ref[...], preferred_element_type=F32)
    lr, li = pw_ref[0, 0, C:C + 1, :], pw_ref[0, 1, C:C + 1, :]
    row = lax.broadcasted_iota(I32, (8, sw), 0)

    def tile_step(i, carry):
        xr, xi = carry
        r0 = pl.multiple_of(i * 8, 8)
        sr = s_ref[pl.ds(r0, 8), 0:sw]
        si = s_ref[pl.ds(r0, 8), sw:2 * sw]
        tr = jnp.zeros((8, sw), F32)
        ti = jnp.zeros((8, sw), F32)
        for r in range(8):
            tr = jnp.where(row == r, xr, tr)
            ti = jnp.where(row == r, xi, ti)
            xr, xi = lr * xr - li * xi + sr[r:r + 1, :], lr * xi + li * xr + si[r:r + 1, :]
        x_ref[pl.ds(r0, 8), 0:sw] = tr
        x_ref[pl.ds(r0, 8), sw:2 * sw] = ti
        return xr, xi

    xr, xi = lax.fori_loop(0, nc // 8, tile_step, (c_ref[0:1, 0:sw], c_ref[0:1, sw:2 * sw]))
    c_ref[0:1, 0:sw] = xr
    c_ref[0:1, sw:2 * sw] = xi
    y_state = lax.dot_general(x_ref[...].astype(BF16), v_ref[...], (((1,), (1,)), ((), ())),
                              preferred_element_type=F32)
    quarter = C // 4
    for qt in range(4):
        rows = (qt + 1) * quarter * LANES
        cols = slice(qt * quarter * LANES, (qt + 1) * quarter * LANES)
        y = jnp.dot(uf_ref[:, 0:rows], t_ref[0:rows, cols], preferred_element_type=F32) + y_state[:, cols]
        for s in range(quarter):
            st = qt * quarter + s
            ys = y[:, s * LANES:(s + 1) * LANES] + d_ref[...] * u_ref[pl.ds(st, nc, stride=C), :]
            o_ref[pl.ds(st, nc, stride=C), :] = jax.nn.gelu(ys)


def s5_mixer(proj, col0, lam_re, lam_im, log_dt, b_re, b_im, c_re, c_im, d, tb=8192):
    T = proj.shape[0]
    C, N, O = S5_CHUNK, S5_STATE, S5_OCT
    W = d.shape[0]
    Q = W // LANES
    tb = min(tb, T)
    pw_q, bb, cc, kq = s5_lane_tile_operands(lam_re, lam_im, log_dt, b_re, b_im, c_re, c_im)
    cb0 = col0 // LANES
    nc = tb // C
    sw = O * N
    quad = lambda a: pl.BlockSpec((1,) + a.shape[1:], lambda q, t: (q, 0, 0, 0))
    return pl.pallas_call(
        _s5_kernel,
        grid=(Q, T // tb),
        in_specs=[pl.BlockSpec((tb, LANES), lambda q, t: (t, cb0 + q)),
                  quad(pw_q), quad(bb), quad(cc), quad(kq),
                  pl.BlockSpec((1, LANES), lambda q, t: (0, q))],
        out_specs=pl.BlockSpec((tb, LANES), lambda q, t: (t, q)),
        out_shape=jax.ShapeDtypeStruct((T, W), F32),
        scratch_shapes=[pltpu.VMEM((C * LANES, C * LANES), BF16),
                        pltpu.VMEM((C * LANES, 2 * sw), BF16),
                        pltpu.VMEM((C * LANES, 2 * sw), BF16),
                        pltpu.VMEM((nc, C * LANES), BF16),
                        pltpu.VMEM((nc, 2 * sw), F32),
                        pltpu.VMEM((nc, 2 * sw), F32),
                        pltpu.VMEM((8, 2 * sw), F32)],
        compiler_params=_cparams(("arbitrary", "arbitrary"), 56),
        name="s5_scan",
    )(proj, pw_q, bb, cc, kq, d.reshape(1, W))


def _store_row_tiles(ref, val, lead=()):
    rows, width = val.shape
    n = width // LANES
    for j in range(n):
        ref[lead + (pl.ds(j, rows, stride=n), slice(None))] = val[:, j * LANES:(j + 1) * LANES]


def _load_row_tiles(ref, rows, n, lead=()):
    return jnp.concatenate([ref[lead + (pl.ds(j, rows, stride=n), slice(None))] for j in range(n)], axis=1)


def _row_tile_spec(tm, width, index_map):
    return pl.BlockSpec((tm * (width // LANES), LANES), index_map)


def _glu_out_kernel(x_ref, yc_ref, yg_ref, wglu_ref, bglu_ref, wout_ref, g_ref, xo_ref, ho_ref):
    yg = yg_ref[...]
    z = jnp.dot(yg.astype(BF16), wglu_ref[...], preferred_element_type=F32) + bglu_ref[...]
    ys = (yg * jax.nn.sigmoid(z)).astype(BF16)
    cw = yc_ref.shape[1]
    xn = (x_ref[...]
          + jnp.dot(yc_ref[...], wout_ref[0:cw, :], preferred_element_type=F32)
          + jnp.dot(ys, wout_ref[cw:, :], preferred_element_type=F32))
    xo_ref[...] = xn
    _store_row_tiles(ho_ref, _rms(xn, g_ref[...]))


def glu_out(x, y_conv, y_gelu, w_glu, b_glu, w_out, g_next, tm=512):
    T, D = x.shape
    CW, SW = y_conv.shape[1], y_gelu.shape[1]
    tm = min(tm, T)
    row = lambda w: pl.BlockSpec((tm, w), lambda i: (i, 0))
    full = lambda a: pl.BlockSpec(a.shape, lambda i: (0,) * a.ndim)
    bg, g2 = b_glu.reshape(1, SW), g_next.reshape(1, D)
    return pl.pallas_call(
        _glu_out_kernel,
        grid=(T // tm,),
        in_specs=[row(D), row(CW), row(SW), full(w_glu), full(bg), full(w_out), full(g2)],
        out_specs=[row(D), _row_tile_spec(tm, D, lambda i: (i, 0))],
        out_shape=[jax.ShapeDtypeStruct((T, D), F32), jax.ShapeDtypeStruct((T * D // LANES, LANES), F32)],
        compiler_params=_cparams(("parallel",), 56),
        name="glu_out",
    )(x, y_conv, y_gelu, w_glu, bg, w_out, g2)


def _proj_residual_kernel(x_ref, a_ref, w_ref, g_ref, xo_ref, ho_ref):
    xn = x_ref[...] + jnp.dot(a_ref[...], w_ref[...], preferred_element_type=F32)
    xo_ref[...] = xn
    _store_row_tiles(ho_ref, _rms(xn, g_ref[...]))


def proj_residual(x, a, w, g_next, tm=512):
    T, D = x.shape
    K = a.shape[1]
    tm = min(tm, T)
    row = lambda w_: pl.BlockSpec((tm, w_), lambda i: (i, 0))
    full = lambda arr: pl.BlockSpec(arr.shape, lambda i: (0,) * arr.ndim)
    g2 = g_next.reshape(1, D)
    return pl.pallas_call(
        _proj_residual_kernel,
        grid=(T // tm,),
        in_specs=[row(D), row(K), full(w), full(g2)],
        out_specs=[row(D), _row_tile_spec(tm, D, lambda i: (i, 0))],
        out_shape=[jax.ShapeDtypeStruct((T, D), F32), jax.ShapeDtypeStruct((T * D // LANES, LANES), F32)],
        compiler_params=_cparams(("parallel",), 56),
        name="proj_residual",
    )(x, a, w, g2)


def _router_kernel(h_ref, w_ref, b_ref, mi_ref, mf_ref, cnt_ref, run_ref):
    i = pl.program_id(0)

    @pl.when(i == 0)
    def _():
        run_ref[...] = jnp.zeros_like(run_ref)

    D = w_ref.shape[0]
    tm = h_ref.shape[0] * LANES // D
    h = _load_row_tiles(h_ref, tm, D // LANES)
    h_hi, h_lo = _bf16_pieces(h, 2)
    w_hi, w_lo = _bf16_pieces(w_ref[...], 2)
    logits = (jnp.dot(h_hi, w_hi, preferred_element_type=F32)
              + (jnp.dot(h_hi, w_lo, preferred_element_type=F32)
                 + jnp.dot(h_lo, w_hi, preferred_element_type=F32))) + b_ref[...]
    lane = lax.broadcasted_iota(I32, logits.shape, 1)
    neg = jnp.float32(-jnp.inf)
    gl = jnp.where(lane < N_GROUPS, logits, neg)
    gmax = jnp.max(gl, axis=1, keepdims=True)
    gsum = jnp.sum(jnp.where(lane < N_GROUPS, jnp.exp(gl - gmax), 0.0), axis=1, keepdims=True)
    gw = 1.0 / gsum
    gidx = jnp.min(jnp.where(gl == gmax, lane, LANES), axis=1, keepdims=True)
    lo = EXPERT_LANE0 + EXPERTS_PER_GROUP * gidx
    el = jnp.where((lane >= lo) & (lane < lo + EXPERTS_PER_GROUP), logits, neg)
    v1 = jnp.max(el, axis=1, keepdims=True)
    i1 = jnp.min(jnp.where(el == v1, lane, LANES), axis=1, keepdims=True)
    el2 = jnp.where(lane == i1, neg, el)
    v2 = jnp.max(el2, axis=1, keepdims=True)
    i2 = jnp.min(jnp.where(el2 == v2, lane, LANES), axis=1, keepdims=True)
    t = jnp.exp(v2 - v1)
    w1 = gw / (1.0 + t)
    w2 = gw * t / (1.0 + t)
    hit1 = lane == i1
    hit2 = lane == i2
    cnt = (hit1 | hit2).astype(BF16)
    r = lax.broadcasted_iota(I32, (tm, tm), 0)
    c = lax.broadcasted_iota(I32, (tm, tm), 1)
    before = (c < r).astype(BF16)
    cum = jnp.dot(before, cnt, preferred_element_type=F32) + run_ref[0:1, :]
    rank1 = jnp.sum(jnp.where(hit1, cum, 0.0), axis=1, keepdims=True).astype(I32)
    rank2 = jnp.sum(jnp.where(hit2, cum, 0.0), axis=1, keepdims=True).astype(I32)
    run = run_ref[0:1, :] + jnp.sum(cnt.astype(F32), axis=0, keepdims=True)
    run_ref[...] = jnp.broadcast_to(run, run_ref.shape)
    cnt_ref[...] = jnp.broadcast_to(run, cnt_ref.shape)
    code1 = (i1 - EXPERT_LANE0) * MOE_CODE + rank1
    code2 = (i2 - EXPERT_LANE0) * MOE_CODE + rank2
    mi_ref[...] = jnp.where(lane == 0, code1, jnp.where(lane == 1, code2, 0))
    mf_ref[...] = jnp.where(lane == 0, w1, jnp.where(lane == 1, w2, 0.0))


def router(h, w_r, b_r, tm=256):
    D = w_r.shape[0]
    T = h.shape[0] * LANES // D
    tm = min(tm, T)
    row = lambda w: pl.BlockSpec((tm, w), lambda i: (i, 0))
    full = lambda a: pl.BlockSpec(a.shape, lambda i: (0,) * a.ndim)
    return pl.pallas_call(
        _router_kernel,
        grid=(T // tm,),
        in_specs=[_row_tile_spec(tm, D, lambda i: (i, 0)), full(w_r), full(b_r)],
        out_specs=[row(LANES), row(LANES), pl.BlockSpec((8, LANES), lambda i: (0, 0))],
        out_shape=[jax.ShapeDtypeStruct((T, LANES), I32), jax.ShapeDtypeStruct((T, LANES), F32),
                   jax.ShapeDtypeStruct((8, LANES), F32)],
        scratch_shapes=[pltpu.VMEM((8, LANES), F32)],
        compiler_params=_cparams(("arbitrary",), 40),
        name="moe_router",
    )(h, w_r, b_r)


def moe_plan(meta_i, counts, n_blocks):
    B = MOE_BLOCK
    code1, code2 = meta_i[:, 0], meta_i[:, 1]
    cnt = counts[0, EXPERT_LANE0:EXPERT_LANE0 + N_EXPERTS].astype(I32)
    nblk = (cnt + B - 1) // B
    blk_end = jnp.cumsum(nblk)
    blk_off = blk_end - nblk
    b = jnp.arange(n_blocks, dtype=I32)
    total = blk_end[-1]
    owner = jnp.minimum(jnp.sum((blk_end[None, :] <= b[:, None]).astype(I32), axis=1), N_EXPERTS - 1)
    valid = b < total
    last = jnp.maximum(total - 1, 0)
    blk_expert = jnp.where(valid, owner, owner[last])
    blk_src = jnp.where(valid, b, last)
    blk_first = (valid & (b == blk_off[owner])).astype(I32)
    return code1, code2, blk_off * B, blk_expert, blk_src, valid.astype(I32), blk_first


def _row_copy(src, dst, sem):
    return pltpu.make_async_copy(src, dst, sem)


def _sorted_row(code, off_ref):
    return off_ref[code >> MOE_CODE_BITS] + (code & (MOE_CODE - 1))


def _tile_rows(row, n):
    return pl.ds(pl.multiple_of(row * n, n), n)


def _dispatch_kernel(code1_ref, code2_ref, off_ref, hp_ref, xs_in_ref, xs_ref, sem, *, n):
    del xs_in_ref
    tm = hp_ref.shape[0] // n
    base = pl.program_id(0) * tm

    def copies(r):
        src = hp_ref.at[_tile_rows(r, n)]
        return (_row_copy(src, xs_ref.at[_tile_rows(_sorted_row(code1_ref[base + r], off_ref), n)], sem),
                _row_copy(src, xs_ref.at[_tile_rows(_sorted_row(code2_ref[base + r], off_ref), n)], sem))

    def start(r, carry):
        for cp in copies(r):
            cp.start()
        return carry

    lax.fori_loop(0, tm, start, 0, unroll=8)
    for _ in range(2):
        _row_copy(hp_ref, xs_ref.at[pl.ds(0, tm * n)], sem).wait()


def dispatch(hp, width, code1, code2, row_off, xs0, tm=256):
    n = width // LANES
    T = hp.shape[0] // n
    tm = min(tm, T)
    return pl.pallas_call(
        functools.partial(_dispatch_kernel, n=n),
        grid_spec=pltpu.PrefetchScalarGridSpec(
            num_scalar_prefetch=3,
            grid=(T // tm,),
            in_specs=[_row_tile_spec(tm, width, lambda i, c1, c2, off: (i, 0)),
                      pl.BlockSpec(memory_space=pl.ANY)],
            out_specs=pl.BlockSpec(memory_space=pl.ANY),
            scratch_shapes=[pltpu.SemaphoreType.DMA(())]),
        out_shape=jax.ShapeDtypeStruct(xs0.shape, hp.dtype),
        input_output_aliases={4: 0},
        compiler_params=_cparams(("arbitrary",), 32),
        name="moe_dispatch",
    )(code1, code2, row_off, hp, xs0)


def _experts_kernel(be_ref, bs_ref, bv_ref, bf_ref, xs_ref, wg_ref, wu_ref, wd_ref, ys_ref, wg_s, wu_s, wd_s):
    del be_ref, bs_ref
    b = pl.program_id(0)

    @pl.when(bv_ref[b] == 0)
    def _():
        ys_ref[...] = jnp.zeros_like(ys_ref)

    @pl.when(bv_ref[b] > 0)
    def _():
        @pl.when(bf_ref[b] == 1)
        def _():
            wg_s[...] = wg_ref[0].astype(BF16)
            wu_s[...] = wu_ref[0].astype(BF16)
            wd_s[...] = wd_ref[0].astype(BF16)

        D = wg_s.shape[0]
        a = _load_row_tiles(xs_ref, MOE_BLOCK, D // LANES).astype(BF16)
        gate = jnp.dot(a, wg_s[...], preferred_element_type=F32)
        up = jnp.dot(a, wu_s[...], preferred_element_type=F32)
        mid = (jax.nn.silu(gate) * up).astype(BF16)
        _store_row_tiles(ys_ref, jnp.dot(mid, wd_s[...], preferred_element_type=F32))


def experts(xs, blk_expert, blk_src, blk_valid, blk_first, w_gate, w_up, w_down):
    B = MOE_BLOCK
    E, D, FF = w_gate.shape
    nb = xs.shape[0] * LANES // D // B
    return pl.pallas_call(
        _experts_kernel,
        grid_spec=pltpu.PrefetchScalarGridSpec(
            num_scalar_prefetch=4,
            grid=(nb,),
            in_specs=[_row_tile_spec(B, D, lambda b, be, bs, bv, bf: (bs[b], 0)),
                      pl.BlockSpec((1, D, FF), lambda b, be, bs, bv, bf: (be[b], 0, 0)),
                      pl.BlockSpec((1, D, FF), lambda b, be, bs, bv, bf: (be[b], 0, 0)),
                      pl.BlockSpec((1, FF, D), lambda b, be, bs, bv, bf: (be[b], 0, 0))],
            out_specs=_row_tile_spec(B, D, lambda b, be, bs, bv, bf: (b, 0)),
            scratch_shapes=[pltpu.VMEM((D, FF), BF16), pltpu.VMEM((D, FF), BF16),
                            pltpu.VMEM((FF, D), BF16)]),
        out_shape=jax.ShapeDtypeStruct(xs.shape, F32),
        compiler_params=_cparams(("arbitrary",), 56),
        name="moe_experts",
    )(blk_expert, blk_src, blk_valid, blk_first, xs, w_gate, w_up, w_down)


def _combine_kernel(code1_ref, code2_ref, off_ref, x_ref, mf_ref, g_ref, ys_ref, xo_ref, ho_ref, buf, sem):
    tm, D = x_ref.shape
    n = D // LANES
    base = pl.program_id(0) * tm

    def copies(r):
        dst = _tile_rows(r, n)
        return (_row_copy(ys_ref.at[_tile_rows(_sorted_row(code1_ref[base + r], off_ref), n)],
                          buf.at[0, dst], sem),
                _row_copy(ys_ref.at[_tile_rows(_sorted_row(code2_ref[base + r], off_ref), n)],
                          buf.at[1, dst], sem))

    def start(r, carry):
        for cp in copies(r):
            cp.start()
        return carry

    lax.fori_loop(0, tm, start, 0, unroll=8)
    for k in range(2):
        _row_copy(ys_ref.at[pl.ds(0, tm * n)], buf.at[k], sem).wait()
    mf = mf_ref[...]
    xn = (x_ref[...] + mf[:, 0:1] * _load_row_tiles(buf, tm, n, lead=(0,))
          + mf[:, 1:2] * _load_row_tiles(buf, tm, n, lead=(1,)))
    xo_ref[...] = xn
    ho_ref[...] = _rms(xn, g_ref[...]).astype(ho_ref.dtype)


def combine(x, ys, meta_f, code1, code2, row_off, g_next, h_dtype, tm=256):
    T, D = x.shape
    tm = min(tm, T)
    row = lambda w: pl.BlockSpec((tm, w), lambda i, c1, c2, off: (i, 0))
    return pl.pallas_call(
        _combine_kernel,
        grid_spec=pltpu.PrefetchScalarGridSpec(
            num_scalar_prefetch=3,
            grid=(T // tm,),
            in_specs=[row(D), row(LANES), pl.BlockSpec((1, D), lambda i, c1, c2, off: (0, 0)),
                      pl.BlockSpec(memory_space=pl.ANY)],
            out_specs=[row(D), row(D)],
            scratch_shapes=[pltpu.VMEM((2, tm * D // LANES, LANES), F32), pltpu.SemaphoreType.DMA(())]),
        out_shape=[jax.ShapeDtypeStruct((T, D), F32), jax.ShapeDtypeStruct((T, D), h_dtype)],
        compiler_params=_cparams(("arbitrary",), 40),
        name="moe_combine",
    )(code1, code2, row_off, x, meta_f, g_next.reshape(1, D), ys)


def moe_row_blocks(T):
    return (2 * T) // MOE_BLOCK + N_EXPERTS


def moe_layer(x, h, xs_buf, w_group, b_group, w_expert, b_expert, layer, w_gate, w_up, w_down, g_next,
              h_dtype):
    T, D = x.shape
    assert T <= MOE_CODE, "ranks inside one expert must fit the packed (expert, rank) code"
    n_blocks = xs_buf.shape[0] * LANES // D // MOE_BLOCK
    pad = LANES - N_GROUPS - N_EXPERTS
    w_r = jnp.concatenate([w_group, w_expert.reshape(D, N_EXPERTS), jnp.zeros((D, pad), F32)], axis=1)
    b_r = jnp.concatenate([b_group, b_expert.reshape(N_EXPERTS), jnp.zeros((pad,), F32)]).reshape(1, LANES)
    meta_i, meta_f, counts = router(h, w_r, b_r)
    code1, code2, row_off, blk_expert, blk_src, blk_valid, blk_first = moe_plan(meta_i, counts, n_blocks)
    xs = dispatch(h, D, code1, code2, row_off, xs_buf)
    FF = w_gate.shape[-1]
    ys = experts(xs, blk_expert + layer * N_EXPERTS, blk_src, blk_valid, blk_first,
                 w_gate.reshape(-1, D, FF), w_up.reshape(-1, D, FF), w_down.reshape(-1, FF, D))
    xo, ho = combine(x, ys, meta_f, code1, code2, row_off, g_next, h_dtype)
    return xo, ho, xs


def _forget_kernel(h_ref, w_ref, b_ref, o_ref, run_ref):
    @pl.when(pl.program_id(0) == 0)
    def _():
        run_ref[...] = jnp.zeros_like(run_ref)

    h = h_ref[...]
    z = sum(jnp.dot(h, w, preferred_element_type=F32) for w in _bf16_pieces(w_ref[...], 2)) + b_ref[...]
    log_f = jax.nn.log_sigmoid(z)
    tm = z.shape[0]
    r = lax.broadcasted_iota(I32, (tm, tm), 0)
    c = lax.broadcasted_iota(I32, (tm, tm), 1)
    upto = (c <= r).astype(BF16)
    cum = sum(jnp.dot(upto, p, preferred_element_type=F32) for p in _bf16_pieces(log_f, 3)) + run_ref[0:1, :]
    o_ref[...] = cum
    run_ref[...] = jnp.broadcast_to(cum[tm - 1:tm, :], run_ref.shape)


def forget_cumsum(h, w_f, b_f, tm=256):
    T, D = h.shape
    tm = min(tm, T)
    return pl.pallas_call(
        _forget_kernel,
        grid=(T // tm,),
        in_specs=[pl.BlockSpec((tm, D), lambda i: (i, 0)),
                  pl.BlockSpec((D, LANES), lambda i: (0, 0)),
                  pl.BlockSpec((1, LANES), lambda i: (0, 0))],
        out_specs=pl.BlockSpec((tm, LANES), lambda i: (i, 0)),
        out_shape=jax.ShapeDtypeStruct((T, LANES), F32),
        scratch_shapes=[pltpu.VMEM((8, LANES), F32)],
        compiler_params=_cparams(("arbitrary",), 40),
        name="forget_cumsum",
    )(h, w_f, b_f)


def _flash_kernel(q_ref, k_ref, v_ref, ck_ref, cq_ref, o_ref, m_ref, acc_ref, *, scale, sub):
    qi = pl.program_id(1)
    bq = q_ref.shape[0]
    dh = FOX_HEAD_DIM
    heads = q_ref.shape[1] // dh
    m_ref[...] = jnp.full_like(m_ref, -jnp.inf)
    acc_ref[...] = jnp.zeros_like(acc_ref)
    ones = jnp.ones((bq, dh), BF16)
    n_sub = bq // sub

    def chunk(start, diagonal):
        def scores(hh):
            cols = slice(hh * dh, (hh + 1) * dh)
            k = k_ref[pl.ds(start, bq), cols]
            return lax.dot_general(q_ref[:, cols], k, (((1,), (1,)), ((), ())), preferred_element_type=F32)

        def softmax(hh, s_all):
            c0 = cq_ref[hh, :, 0:1]
            bias = (c0 - ck_ref[hh, :, pl.ds(start, bq)]) * LOG2E
            ps, alphas = [], []
            for r in range(n_sub):
                rows = slice(r * sub, (r + 1) * sub)
                s = s_all[rows, :] * (scale * LOG2E) + bias
                if diagonal:
                    row = lax.broadcasted_iota(I32, s.shape, 0) + r * sub
                    col = lax.broadcasted_iota(I32, s.shape, 1)
                    s = jnp.where(col <= row, s, -jnp.inf)
                m_prev = m_ref[hh, rows, :]
                m_new = jnp.maximum(m_prev, jnp.max(s, axis=1, keepdims=True))
                m_ref[hh, rows, :] = m_new
                ps.append(jnp.exp2(s - jnp.concatenate([m_new] * (bq // LANES), axis=1)).astype(BF16))
                alphas.append(jnp.exp2(m_prev - m_new))
            return jnp.concatenate(ps, axis=0), alphas

        def update(hh, p, alphas):
            cols = slice(hh * dh, (hh + 1) * dh)
            v1 = jnp.concatenate([v_ref[pl.ds(start, bq), cols], ones], axis=1)
            pv = jnp.dot(p, v1, preferred_element_type=F32)
            for r in range(n_sub):
                rows = slice(r * sub, (r + 1) * sub)
                acc_ref[hh, rows, :] = (jnp.concatenate([alphas[r]] * 2, axis=1) * acc_ref[hh, rows, :]
                                        + pv[rows, :])

        s_next = scores(0)
        for hh in range(heads):
            s_cur = s_next
            if hh + 1 < heads:
                s_next = scores(hh + 1)
            p, alphas = softmax(hh, s_cur)
            update(hh, p, alphas)

    def body(j, carry):
        chunk(pl.multiple_of(j * bq, bq), False)
        return carry

    lax.fori_loop(0, qi, body, 0)
    chunk(pl.multiple_of(qi * bq, bq), True)
    for hh in range(heads):
        o_ref[:, hh * dh:(hh + 1) * dh] = (acc_ref[hh, :, 0:dh] / acc_ref[hh, :, dh:2 * dh]).astype(o_ref.dtype)


def flash_attention(qkv, cum_t, bq=512, sub=32, heads=4):
    T = qkv.shape[0]
    H, dh = FOX_HEADS, FOX_HEAD_DIM
    bq = min(bq, T)
    kern = functools.partial(_flash_kernel, scale=dh ** -0.5, sub=min(sub, bq))
    hw = heads * dh
    nhb = H // heads
    return pl.pallas_call(
        kern,
        grid=(nhb, T // bq),
        in_specs=[pl.BlockSpec((bq, hw), lambda h, i: (i, h)),
                  pl.BlockSpec((T, hw), lambda h, i: (0, nhb + h)),
                  pl.BlockSpec((T, hw), lambda h, i: (0, 2 * nhb + h)),
                  pl.BlockSpec((heads, 1, T), lambda h, i: (h, 0, 0)),
                  pl.BlockSpec((heads, 1, bq), lambda h, i: (h, 0, i))],
        out_specs=pl.BlockSpec((bq, hw), lambda h, i: (i, h)),
        out_shape=jax.ShapeDtypeStruct((T, H * dh), BF16),
        scratch_shapes=[pltpu.VMEM((heads, bq, LANES), F32), pltpu.VMEM((heads, bq, 2 * dh), F32)],
        compiler_params=_cparams(("parallel", "arbitrary"), 48),
        name="fox_attention",
    )(qkv, qkv, qkv, cum_t, cum_t)


def kernel(x, ab_norm, ab_w_in, ab_conv_w, s5_lambda_re, s5_lambda_im, s5_log_dt, s5_b_re, s5_b_im,
           s5_c_re, s5_c_im, s5_d, s5_w_glu, s5_b_glu, ab_w_out, c_norm, c_w_in, c_b_forget, c_w_out,
           ffn_norm, router_w_group, router_b_group, router_w_expert, router_b_expert,
           moe_w_gate, moe_w_up, moe_w_down, final_norm):
    bsz, L, D = x.shape
    depth = ffn_norm.shape[0]
    cw = ab_conv_w.shape[-1]
    xt = x.reshape(bsz * L, D)
    h = None
    xs_buf = jnp.zeros((moe_row_blocks(bsz * L) * MOE_BLOCK * D // LANES, LANES), F32)
    for i in range(depth):
        j = i // 2
        if i % 2 == 0:
            g_in = ab_norm[j]
            if h is not None:
                proj = matmul(h, ab_w_in[j].astype(BF16), F32)
            else:
                proj = norm_matmul(xt, g_in, ab_w_in[j].astype(BF16), F32)
            y_conv = conv_mixer(proj, ab_conv_w[j])
            y_gelu = s5_mixer(proj, 3 * cw, s5_lambda_re[j], s5_lambda_im[j], s5_log_dt[j],
                              s5_b_re[j], s5_b_im[j], s5_c_re[j], s5_c_im[j], s5_d[j])
            xt, hf = glu_out(xt, y_conv, y_gelu, s5_w_glu[j].astype(BF16), s5_b_glu[j],
                             ab_w_out[j].astype(BF16), ffn_norm[i])
        else:
            assert h is not None, "an attention layer always follows a MoE combine that emits its norm"
            hd = FOX_HEADS * FOX_HEAD_DIM
            qkv = matmul(h, c_w_in[j].astype(BF16), BF16, n_cols=3 * hd)
            w_f = jnp.pad(c_w_in[j][:, 3 * hd:], ((0, 0), (0, LANES - FOX_HEADS)))
            b_f = jnp.pad(c_b_forget[j], (0, LANES - FOX_HEADS)).reshape(1, LANES)
            cum = forget_cumsum(h, w_f, b_f)
            cum_t = cum[:, :FOX_HEADS].T.reshape(FOX_HEADS, 1, bsz * L)
            att = flash_attention(qkv, cum_t)
            xt, hf = proj_residual(xt, att, c_w_out[j].astype(BF16), ffn_norm[i])
        last = i == depth - 1
        if last:
            g_next = final_norm
        elif (i + 1) % 2 == 0:
            g_next = ab_norm[(i + 1) // 2]
        else:
            g_next = c_norm[(i + 1) // 2]
        xt, h, xs_buf = moe_layer(xt, hf, xs_buf, router_w_group[i], router_b_group[i], router_w_expert[i],
                                  router_b_expert[i], i, moe_w_gate, moe_w_up, moe_w_down,
                                  g_next, F32 if last else BF16)
    return h.reshape(bsz, L, D)
```

```python
import functools
import math

import jax
import jax.numpy as jnp
from jax import lax
from jax.experimental import pallas as pl
from jax.experimental.pallas import tpu as pltpu

F32 = jnp.float32
BF16 = jnp.bfloat16
I32 = jnp.int32
U32 = jnp.uint32

RMS_EPS = 1e-6
LANES = 128
LOG2E = math.log2(math.e)
MIB = 1024 * 1024

CONV_K = 3
S5_GROUP = 16
S5_STATE = 64
S5_CHUNK = 16
S5_OCT = LANES // S5_GROUP
FOX_HEADS = 16
FOX_HEAD_DIM = 128
N_GROUPS = 4
EXPERTS_PER_GROUP = 8
N_EXPERTS = N_GROUPS * EXPERTS_PER_GROUP
EXPERT_LANE0 = N_GROUPS
MOE_BLOCK = 256
MOE_CODE_BITS = 16
MOE_CODE = 1 << MOE_CODE_BITS


def _cparams(sem, vmem_mib):
    return pltpu.CompilerParams(dimension_semantics=sem, vmem_limit_bytes=vmem_mib * MIB)


def _bf16_pieces(x, n):
    pieces = []
    for _ in range(n):
        p = x.astype(BF16)
        pieces.append(p)
        x = x - p.astype(F32)
    return pieces


def _rms(x, g):
    ms = jnp.mean(x * x, axis=-1, keepdims=True)
    return x * lax.rsqrt(ms + RMS_EPS) * g


def _norm_matmul_kernel(x_ref, g_ref, w_ref, o_ref, h_ref):
    @pl.when(pl.program_id(1) == 0)
    def _():
        h_ref[...] = _rms(x_ref[...], g_ref[...]).astype(BF16)

    o_ref[...] = jnp.dot(h_ref[...], w_ref[...], preferred_element_type=F32).astype(o_ref.dtype)


def norm_matmul(x, g, w, out_dtype, tm=1024, tn=1024):
    T, D = x.shape
    N = w.shape[1]
    tm, tn = min(tm, T), min(tn, N)
    return pl.pallas_call(
        _norm_matmul_kernel,
        grid=(T // tm, N // tn),
        in_specs=[pl.BlockSpec((tm, D), lambda i, j: (i, 0)),
                  pl.BlockSpec((1, D), lambda i, j: (0, 0)),
                  pl.BlockSpec((D, tn), lambda i, j: (0, j))],
        out_specs=pl.BlockSpec((tm, tn), lambda i, j: (i, j)),
        out_shape=jax.ShapeDtypeStruct((T, N), out_dtype),
        scratch_shapes=[pltpu.VMEM((tm, D), BF16)],
        compiler_params=_cparams(("parallel", "arbitrary"), 56),
        name="norm_matmul",
    )(x, g.reshape(1, D), w)


def _matmul_kernel(a_ref, w_ref, o_ref, wb_ref):
    @pl.when(pl.program_id(1) == 0)
    def _():
        wb_ref[...] = w_ref[...].astype(BF16)

    o_ref[...] = jnp.dot(a_ref[...], wb_ref[...], preferred_element_type=F32).astype(o_ref.dtype)


def matmul(a, w, out_dtype, n_cols=None, tm=1024, tn=1024):
    T, K = a.shape
    N = w.shape[1] if n_cols is None else n_cols
    tm, tn = min(tm, T), min(tn, N)
    return pl.pallas_call(
        _matmul_kernel,
        grid=(N // tn, T // tm),
        in_specs=[pl.BlockSpec((tm, K), lambda j, i: (i, 0)),
                  pl.BlockSpec((K, tn), lambda j, i: (0, j))],
        out_specs=pl.BlockSpec((tm, tn), lambda j, i: (i, j)),
        out_shape=jax.ShapeDtypeStruct((T, N), out_dtype),
        scratch_shapes=[pltpu.VMEM((K, tn), BF16)],
        compiler_params=_cparams(("parallel", "arbitrary"), 48),
        name="matmul",
    )(a, w)


def _conv_kernel(gb_ref, gc_ref, u_ref, gcp_ref, up_ref, w_ref, o_ref):
    w0, w1, w2 = w_ref[0:1, :], w_ref[1:2, :], w_ref[2:3, :]
    v = gc_ref[...] * u_ref[...]
    y = w2 * v + w1 * pltpu.roll(v, 1, 0) + w0 * pltpu.roll(v, 2, 0)
    o_ref[...] = (gb_ref[...] * y).astype(o_ref.dtype)
    vp = gcp_ref[...] * up_ref[...]
    vp = jnp.where(pl.program_id(0) > 0, vp, jnp.zeros_like(vp))
    v8 = v[0:8, :]
    row = lax.broadcasted_iota(I32, v8.shape, 0)
    v1 = jnp.where(row < 1, pltpu.roll(vp, 1, 0), pltpu.roll(v8, 1, 0))
    v2 = jnp.where(row < 2, pltpu.roll(vp, 2, 0), pltpu.roll(v8, 2, 0))
    o_ref[0:8, :] = (gb_ref[0:8, :] * (w2 * v8 + w1 * v1 + w0 * v2)).astype(o_ref.dtype)


def conv_mixer(proj, conv_w, tm=512):
    T = proj.shape[0]
    CW = conv_w.shape[1]
    tm = min(tm, T)
    r8 = tm // 8
    cur = lambda c: pl.BlockSpec((tm, CW), lambda i: (i, c))
    prev = lambda c: pl.BlockSpec((8, CW), lambda i: (jnp.maximum(i * r8 - 1, 0), c))
    return pl.pallas_call(
        _conv_kernel,
        grid=(T // tm,),
        in_specs=[cur(0), cur(1), cur(2), prev(1), prev(2),
                  pl.BlockSpec((CONV_K, CW), lambda i: (0, 0))],
        out_specs=pl.BlockSpec((tm, CW), lambda i: (i, 0)),
        out_shape=jax.ShapeDtypeStruct((T, CW), BF16),
        compiler_params=_cparams(("parallel",), 40),
        name="conv_mixer",
    )(proj, proj, proj, proj, proj, conv_w)


def _cmul(a, b):
    return a[0] * b[0] - a[1] * b[1], a[0] * b[1] + a[1] * b[0]


def s5_matrices(lam_re, lam_im, log_dt, b_re, b_im, c_re, c_im):
    C = S5_CHUNK
    dt = jnp.exp(log_dt)[:, None]
    a, b = lam_re * dt, lam_im * dt
    mag = jnp.exp(a)
    lbar = (mag * jnp.cos(b), mag * jnp.sin(b))
    den = lam_re * lam_re + lam_im * lam_im
    inv_lam = (lam_re / den, -lam_im / den)
    coef = _cmul((lbar[0] - 1.0, lbar[1]), inv_lam)
    bbar = _cmul((coef[0][..., None], coef[1][..., None]), (b_re, b_im))
    j = jnp.arange(C + 1, dtype=F32)[None, :, None]
    pmag = jnp.exp(a[:, None, :] * j)
    pw = (pmag * jnp.cos(b[:, None, :] * j), pmag * jnp.sin(b[:, None, :] * j))
    hi = lax.Precision.HIGHEST
    cp = _cmul((c_re[:, None], c_im[:, None]), (pw[0][:, :C, None, :], pw[1][:, :C, None, :]))
    kj = (jnp.einsum('gjhn,gnk->gjhk', cp[0], bbar[0], precision=hi)
          - jnp.einsum('gjhn,gnk->gjhk', cp[1], bbar[1], precision=hi))
    return pw, bbar, kj


def _lane_tile_blockdiag(m):
    Q, O, r, c = m.shape
    out = jnp.zeros((Q, O, r, O, c), m.dtype)
    for g in range(O):
        out = out.at[:, g, :, g, :].set(m[:, g])
    return out.reshape(Q, O * r, O * c)


def s5_lane_tile_operands(lam_re, lam_im, log_dt, b_re, b_im, c_re, c_im):
    C, H, N, O = S5_CHUNK, S5_GROUP, S5_STATE, S5_OCT
    pw, bbar, kj = s5_matrices(lam_re, lam_im, log_dt, b_re, b_im, c_re, c_im)
    Q = lam_re.shape[0] // O
    pw_q = jnp.stack([p.reshape(Q, O, C + 1, N).transpose(0, 2, 1, 3).reshape(Q, C + 1, O * N) for p in pw],
                     axis=1)
    bb = jnp.stack([_lane_tile_blockdiag(m.transpose(0, 2, 1).reshape(Q, O, H, N)) for m in bbar], axis=1)
    cc = jnp.stack([_lane_tile_blockdiag(m.reshape(Q, O, H, N)) for m in (c_re, c_im)], axis=1)
    kq = kj.reshape(Q, O, C, H, H).transpose(0, 2, 1, 4, 3).reshape(Q, C, O * H, H).astype(BF16)
    return pw_q, bb, cc, kq


def _s5_kernel(u_ref, pw_ref, bb_ref, cc_ref, kj_ref, d_ref, o_ref, t_ref, w_ref, v_ref, uf_ref, s_ref,
               x_ref, c_ref):
    C = S5_CHUNK
    tb = u_ref.shape[0]
    nc = tb // C
    sw = pw_ref.shape[3]

    @pl.when((pl.program_id(0) == 0) & (pl.program_id(1) == 0))
    def _():
        t_ref[...] = jnp.zeros_like(t_ref)

    @pl.when(pl.program_id(1) == 0)
    def _():
        bb = (bb_ref[0, 0], bb_ref[0, 1])
        cc = (cc_ref[0, 0], cc_ref[0, 1])
        H = kj_ref.shape[3]
        hb = H.bit_length() - 1
        spread = ((lax.broadcasted_iota(I32, (H, LANES), 1) & (H - 1))
                  == lax.broadcasted_iota(I32, (H, LANES), 0)).astype(BF16)
        same_group = ((lax.broadcasted_iota(I32, (LANES, LANES), 0) >> hb)
                      == (lax.broadcasted_iota(I32, (LANES, LANES), 1) >> hb))
        for j in range(C):
            a = _cmul(bb, (pw_ref[0, 0, j:j + 1, :], pw_ref[0, 1, j:j + 1, :]))
            s = C - 1 - j
            w_ref[s * LANES:(s + 1) * LANES, 0:sw] = a[0].astype(BF16)
            w_ref[s * LANES:(s + 1) * LANES, sw:2 * sw] = a[1].astype(BF16)
            lag = jnp.dot(kj_ref[0, j], spread, preferred_element_type=F32)
            lag = jnp.where(same_group, lag, 0.0).astype(BF16)
            for s0 in range(C - j):
                t_ref[s0 * LANES:(s0 + 1) * LANES, (s0 + j) * LANES:(s0 + j + 1) * LANES] = lag
            g = _cmul(cc, (pw_ref[0, 0, j + 1:j + 2, :], pw_ref[0, 1, j + 1:j + 2, :]))
            v_ref[j * LANES:(j + 1) * LANES, 0:sw] = g[0].astype(BF16)
            v_ref[j * LANES:(j + 1) * LANES, sw:2 * sw] = (-g[1]).astype(BF16)
        c_ref[...] = jnp.zeros_like(c_ref)

    for s in range(C):
        uf_ref[:, s * LANES:(s + 1) * LANES] = u_ref[pl.ds(s, nc, stride=C), :].astype(BF16)
    s_ref[...] = jnp.dot(uf_ref[...], w_ref[...], preferred_element_type=F32)
    lr, li = pw_ref[0, 0, C:C + 1, :], pw_ref[0, 1, C:C + 1, :]
    row = lax.broadcasted_iota(I32, (8, sw), 0)

    def tile_step(i, carry):
        xr, xi = carry
        r0 = pl.multiple_of(i * 8, 8)
        sr = s_ref[pl.ds(r0, 8), 0:sw]
        si = s_ref[pl.ds(r0, 8), sw:2 * sw]
        tr = jnp.zeros((8, sw), F32)
        ti = jnp.zeros((8, sw), F32)
        for r in range(8):
            tr = jnp.where(row == r, xr, tr)
            ti = jnp.where(row == r, xi, ti)
            xr, xi = lr * xr - li * xi + sr[r:r + 1, :], lr * xi + li * xr + si[r:r + 1, :]
        x_ref[pl.ds(r0, 8), 0:sw] = tr
        x_ref[pl.ds(r0, 8), sw:2 * sw] = ti
        return xr, xi

    xr, xi = lax.fori_loop(0, nc // 8, tile_step, (c_ref[0:1, 0:sw], c_ref[0:1, sw:2 * sw]))
    c_ref[0:1, 0:sw] = xr
    c_ref[0:1, sw:2 * sw] = xi
    y_state = lax.dot_general(x_ref[...].astype(BF16), v_ref[...], (((1,), (1,)), ((), ())),
                              preferred_element_type=F32)
    quarter = C // 4
    for qt in range(4):
        rows = (qt + 1) * quarter * LANES
        cols = slice(qt * quarter * LANES, (qt + 1) * quarter * LANES)
        y = jnp.dot(uf_ref[:, 0:rows], t_ref[0:rows, cols], preferred_element_type=F32) + y_state[:, cols]
        for s in range(quarter):
            st = qt * quarter + s
            ys = y[:, s * LANES:(s + 1) * LANES] + d_ref[...] * u_ref[pl.ds(st, nc, stride=C), :]
            o_ref[pl.ds(st, nc, stride=C), :] = jax.nn.gelu(ys)


def s5_mixer(proj, col0, lam_re, lam_im, log_dt, b_re, b_im, c_re, c_im, d, tb=8192):
    T = proj.shape[0]
    C, N, O = S5_CHUNK, S5_STATE, S5_OCT
    W = d.shape[0]
    Q = W // LANES
    tb = min(tb, T)
    pw_q, bb, cc, kq = s5_lane_tile_operands(lam_re, lam_im, log_dt, b_re, b_im, c_re, c_im)
    cb0 = col0 // LANES
    nc = tb // C
    sw = O * N
    quad = lambda a: pl.BlockSpec((1,) + a.shape[1:], lambda q, t: (q, 0, 0, 0))
    return pl.pallas_call(
        _s5_kernel,
        grid=(Q, T // tb),
        in_specs=[pl.BlockSpec((tb, LANES), lambda q, t: (t, cb0 + q)),
                  quad(pw_q), quad(bb), quad(cc), quad(kq),
                  pl.BlockSpec((1, LANES), lambda q, t: (0, q))],
        out_specs=pl.BlockSpec((tb, LANES), lambda q, t: (t, q)),
        out_shape=jax.ShapeDtypeStruct((T, W), F32),
        scratch_shapes=[pltpu.VMEM((C * LANES, C * LANES), BF16),
                        pltpu.VMEM((C * LANES, 2 * sw), BF16),
                        pltpu.VMEM((C * LANES, 2 * sw), BF16),
                        pltpu.VMEM((nc, C * LANES), BF16),
                        pltpu.VMEM((nc, 2 * sw), F32),
                        pltpu.VMEM((nc, 2 * sw), F32),
                        pltpu.VMEM((8, 2 * sw), F32)],
        compiler_params=_cparams(("arbitrary", "arbitrary"), 56),
        name="s5_scan",
    )(proj, pw_q, bb, cc, kq, d.reshape(1, W))


def _store_row_tiles(ref, val, lead=()):
    rows, width = val.shape
    n = width // LANES
    for j in range(n):
        ref[lead + (pl.ds(j, rows, stride=n), slice(None))] = val[:, j * LANES:(j + 1) * LANES]


def _load_row_tiles(ref, rows, n, lead=()):
    return jnp.concatenate([ref[lead + (pl.ds(j, rows, stride=n), slice(None))] for j in range(n)], axis=1)


def _row_tile_spec(tm, width, index_map):
    return pl.BlockSpec((tm * (width // LANES), LANES), index_map)


def _glu_out_kernel(x_ref, yc_ref, yg_ref, wglu_ref, bglu_ref, wout_ref, g_ref, xo_ref, ho_ref):
    yg = yg_ref[...]
    z = jnp.dot(yg.astype(BF16), wglu_ref[...], preferred_element_type=F32) + bglu_ref[...]
    ys = (yg * jax.nn.sigmoid(z)).astype(BF16)
    cw = yc_ref.shape[1]
    xn = (x_ref[...]
          + jnp.dot(yc_ref[...], wout_ref[0:cw, :], preferred_element_type=F32)
          + jnp.dot(ys, wout_ref[cw:, :], preferred_element_type=F32))
    xo_ref[...] = xn
    _store_row_tiles(ho_ref, _rms(xn, g_ref[...]))


def glu_out(x, y_conv, y_gelu, w_glu, b_glu, w_out, g_next, tm=512):
    T, D = x.shape
    CW, SW = y_conv.shape[1], y_gelu.shape[1]
    tm = min(tm, T)
    row = lambda w: pl.BlockSpec((tm, w), lambda i: (i, 0))
    full = lambda a: pl.BlockSpec(a.shape, lambda i: (0,) * a.ndim)
    bg, g2 = b_glu.reshape(1, SW), g_next.reshape(1, D)
    return pl.pallas_call(
        _glu_out_kernel,
        grid=(T // tm,),
        in_specs=[row(D), row(CW), row(SW), full(w_glu), full(bg), full(w_out), full(g2)],
        out_specs=[row(D), _row_tile_spec(tm, D, lambda i: (i, 0))],
        out_shape=[jax.ShapeDtypeStruct((T, D), F32), jax.ShapeDtypeStruct((T * D // LANES, LANES), F32)],
        compiler_params=_cparams(("parallel",), 56),
        name="glu_out",
    )(x, y_conv, y_gelu, w_glu, bg, w_out, g2)


def _proj_residual_kernel(x_ref, a_ref, w_ref, g_ref, xo_ref, ho_ref):
    xn = x_ref[...] + jnp.dot(a_ref[...], w_ref[...], preferred_element_type=F32)
    xo_ref[...] = xn
    _store_row_tiles(ho_ref, _rms(xn, g_ref[...]))


def proj_residual(x, a, w, g_next, tm=512):
    T, D = x.shape
    K = a.shape[1]
    tm = min(tm, T)
    row = lambda w_: pl.BlockSpec((tm, w_), lambda i: (i, 0))
    full = lambda arr: pl.BlockSpec(arr.shape, lambda i: (0,) * arr.ndim)
    g2 = g_next.reshape(1, D)
    return pl.pallas_call(
        _proj_residual_kernel,
        grid=(T // tm,),
        in_specs=[row(D), row(K), full(w), full(g2)],
        out_specs=[row(D), _row_tile_spec(tm, D, lambda i: (i, 0))],
        out_shape=[jax.ShapeDtypeStruct((T, D), F32), jax.ShapeDtypeStruct((T * D // LANES, LANES), F32)],
        compiler_params=_cparams(("parallel",), 56),
        name="proj_residual",
    )(x, a, w, g2)


def _router_kernel(h_ref, w_ref, b_ref, mi_ref, mf_ref, cnt_ref, run_ref):
    i = pl.program_id(0)

    @pl.when(i == 0)
    def _():
        run_ref[...] = jnp.zeros_like(run_ref)

    D = w_ref.shape[0]
    tm = h_ref.shape[0] * LANES // D
    h = _load_row_tiles(h_ref, tm, D // LANES)
    h_hi, h_lo = _bf16_pieces(h, 2)
    w_hi, w_lo = _bf16_pieces(w_ref[...], 2)
    logits = (jnp.dot(h_hi, w_hi, preferred_element_type=F32)
              + (jnp.dot(h_hi, w_lo, preferred_element_type=F32)
                 + jnp.dot(h_lo, w_hi, preferred_element_type=F32))) + b_ref[...]
    lane = lax.broadcasted_iota(I32, logits.shape, 1)
    neg = jnp.float32(-jnp.inf)
    gl = jnp.where(lane < N_GROUPS, logits, neg)
    gmax = jnp.max(gl, axis=1, keepdims=True)
    gsum = jnp.sum(jnp.where(lane < N_GROUPS, jnp.exp(gl - gmax), 0.0), axis=1, keepdims=True)
    gw = 1.0 / gsum
    gidx = jnp.min(jnp.where(gl == gmax, lane, LANES), axis=1, keepdims=True)
    lo = EXPERT_LANE0 + EXPERTS_PER_GROUP * gidx
    el = jnp.where((lane >= lo) & (lane < lo + EXPERTS_PER_GROUP), logits, neg)
    v1 = jnp.max(el, axis=1, keepdims=True)
    i1 = jnp.min(jnp.where(el == v1, lane, LANES), axis=1, keepdims=True)
    el2 = jnp.where(lane == i1, neg, el)
    v2 = jnp.max(el2, axis=1, keepdims=True)
    i2 = jnp.min(jnp.where(el2 == v2, lane, LANES), axis=1, keepdims=True)
    t = jnp.exp(v2 - v1)
    w1 = gw / (1.0 + t)
    w2 = gw * t / (1.0 + t)
    hit1 = lane == i1
    hit2 = lane == i2
    cnt = (hit1 | hit2).astype(BF16)
    r = lax.broadcasted_iota(I32, (tm, tm), 0)
    c = lax.broadcasted_iota(I32, (tm, tm), 1)
    before = (c < r).astype(BF16)
    cum = jnp.dot(before, cnt, preferred_element_type=F32) + run_ref[0:1, :]
    rank1 = jnp.sum(jnp.where(hit1, cum, 0.0), axis=1, keepdims=True).astype(I32)
    rank2 = jnp.sum(jnp.where(hit2, cum, 0.0), axis=1, keepdims=True).astype(I32)
    run = run_ref[0:1, :] + jnp.sum(cnt.astype(F32), axis=0, keepdims=True)
    run_ref[...] = jnp.broadcast_to(run, run_ref.shape)
    cnt_ref[...] = jnp.broadcast_to(run, cnt_ref.shape)
    code1 = (i1 - EXPERT_LANE0) * MOE_CODE + rank1
    code2 = (i2 - EXPERT_LANE0) * MOE_CODE + rank2
    mi_ref[...] = jnp.where(lane == 0, code1, jnp.where(lane == 1, code2, 0))
    mf_ref[...] = jnp.where(lane == 0, w1, jnp.where(lane == 1, w2, 0.0))


def router(h, w_r, b_r, tm=256):
    D = w_r.shape[0]
    T = h.shape[0] * LANES // D
    tm = min(tm, T)
    row = lambda w: pl.BlockSpec((tm, w), lambda i: (i, 0))
    full = lambda a: pl.BlockSpec(a.shape, lambda i: (0,) * a.ndim)
    return pl.pallas_call(
        _router_kernel,
        grid=(T // tm,),
        in_specs=[_row_tile_spec(tm, D, lambda i: (i, 0)), full(w_r), full(b_r)],
        out_specs=[row(LANES), row(LANES), pl.BlockSpec((8, LANES), lambda i: (0, 0))],
        out_shape=[jax.ShapeDtypeStruct((T, LANES), I32), jax.ShapeDtypeStruct((T, LANES), F32),
                   jax.ShapeDtypeStruct((8, LANES), F32)],
        scratch_shapes=[pltpu.VMEM((8, LANES), F32)],
        compiler_params=_cparams(("arbitrary",), 40),
        name="moe_router",
    )(h, w_r, b_r)


def moe_plan(meta_i, counts, n_blocks):
    B = MOE_BLOCK
    code1, code2 = meta_i[:, 0], meta_i[:, 1]
    cnt = counts[0, EXPERT_LANE0:EXPERT_LANE0 + N_EXPERTS].astype(I32)
    nblk = (cnt + B - 1) // B
    blk_end = jnp.cumsum(nblk)
    blk_off = blk_end - nblk
    b = jnp.arange(n_blocks, dtype=I32)
    total = blk_end[-1]
    owner = jnp.minimum(jnp.sum((blk_end[None, :] <= b[:, None]).astype(I32), axis=1), N_EXPERTS - 1)
    valid = b < total
    last = jnp.maximum(total - 1, 0)
    blk_expert = jnp.where(valid, owner, owner[last])
    blk_src = jnp.where(valid, b, last)
    blk_first = (valid & (b == blk_off[owner])).astype(I32)
    return code1, code2, blk_off * B, blk_expert, blk_src, valid.astype(I32), blk_first


def _row_copy(src, dst, sem):
    return pltpu.make_async_copy(src, dst, sem)


def _sorted_row(code, off_ref):
    return off_ref[code >> MOE_CODE_BITS] + (code & (MOE_CODE - 1))


def _tile_rows(row, n):
    return pl.ds(pl.multiple_of(row * n, n), n)


def _dispatch_kernel(code1_ref, code2_ref, off_ref, hp_ref, xs_in_ref, xs_ref, sem, *, n):
    del xs_in_ref
    tm = hp_ref.shape[0] // n
    base = pl.program_id(0) * tm

    def copies(r):
        src = hp_ref.at[_tile_rows(r, n)]
        return (_row_copy(src, xs_ref.at[_tile_rows(_sorted_row(code1_ref[base + r], off_ref), n)], sem),
                _row_copy(src, xs_ref.at[_tile_rows(_sorted_row(code2_ref[base + r], off_ref), n)], sem))

    def start(r, carry):
        for cp in copies(r):
            cp.start()
        return carry

    lax.fori_loop(0, tm, start, 0, unroll=8)
    for _ in range(2):
        _row_copy(hp_ref, xs_ref.at[pl.ds(0, tm * n)], sem).wait()


def dispatch(hp, width, code1, code2, row_off, xs0, tm=256):
    n = width // LANES
    T = hp.shape[0] // n
    tm = min(tm, T)
    return pl.pallas_call(
        functools.partial(_dispatch_kernel, n=n),
        grid_spec=pltpu.PrefetchScalarGridSpec(
            num_scalar_prefetch=3,
            grid=(T // tm,),
            in_specs=[_row_tile_spec(tm, width, lambda i, c1, c2, off: (i, 0)),
                      pl.BlockSpec(memory_space=pl.ANY)],
            out_specs=pl.BlockSpec(memory_space=pl.ANY),
            scratch_shapes=[pltpu.SemaphoreType.DMA(())]),
        out_shape=jax.ShapeDtypeStruct(xs0.shape, hp.dtype),
        input_output_aliases={4: 0},
        compiler_params=_cparams(("arbitrary",), 32),
        name="moe_dispatch",
    )(code1, code2, row_off, hp, xs0)


def _experts_kernel(be_ref, bs_ref, bv_ref, bf_ref, xs_ref, wg_ref, wu_ref, wd_ref, ys_ref, wg_s, wu_s, wd_s):
    del be_ref, bs_ref
    b = pl.program_id(0)

    @pl.when(bv_ref[b] == 0)
    def _():
        ys_ref[...] = jnp.zeros_like(ys_ref)

    @pl.when(bv_ref[b] > 0)
    def _():
        @pl.when(bf_ref[b] == 1)
        def _():
            wg_s[...] = wg_ref[0].astype(BF16)
            wu_s[...] = wu_ref[0].astype(BF16)
            wd_s[...] = wd_ref[0].astype(BF16)

        D = wg_s.shape[0]
        a = _load_row_tiles(xs_ref, MOE_BLOCK, D // LANES).astype(BF16)
        gate = jnp.dot(a, wg_s[...], preferred_element_type=F32)
        up = jnp.dot(a, wu_s[...], preferred_element_type=F32)
        mid = (jax.nn.silu(gate) * up).astype(BF16)
        _store_row_tiles(ys_ref, jnp.dot(mid, wd_s[...], preferred_element_type=F32))


def experts(xs, blk_expert, blk_src, blk_valid, blk_first, w_gate, w_up, w_down):
    B = MOE_BLOCK
    E, D, FF = w_gate.shape
    nb = xs.shape[0] * LANES // D // B
    return pl.pallas_call(
        _experts_kernel,
        grid_spec=pltpu.PrefetchScalarGridSpec(
            num_scalar_prefetch=4,
            grid=(nb,),
            in_specs=[_row_tile_spec(B, D, lambda b, be, bs, bv, bf: (bs[b], 0)),
                      pl.BlockSpec((1, D, FF), lambda b, be, bs, bv, bf: (be[b], 0, 0)),
                      pl.BlockSpec((1, D, FF), lambda b, be, bs, bv, bf: (be[b], 0, 0)),
                      pl.BlockSpec((1, FF, D), lambda b, be, bs, bv, bf: (be[b], 0, 0))],
            out_specs=_row_tile_spec(B, D, lambda b, be, bs, bv, bf: (b, 0)),
            scratch_shapes=[pltpu.VMEM((D, FF), BF16), pltpu.VMEM((D, FF), BF16),
                            pltpu.VMEM((FF, D), BF16)]),
        out_shape=jax.ShapeDtypeStruct(xs.shape, F32),
        compiler_params=_cparams(("arbitrary",), 56),
        name="moe_experts",
    )(blk_expert, blk_src, blk_valid, blk_first, xs, w_gate, w_up, w_down)


def _combine_kernel(code1_ref, code2_ref, off_ref, x_ref, mf_ref, g_ref, ys_ref, xo_ref, ho_ref, buf, sem):
    tm, D = x_ref.shape
    n = D // LANES
    i = pl.program_id(0)
    slot = i % 2

    def fetch(step, dst_slot):
        base = step * tm

        def start(r, carry):
            dst = _tile_rows(r, n)
            _row_copy(ys_ref.at[_tile_rows(_sorted_row(code1_ref[base + r], off_ref), n)],
                      buf.at[dst_slot, 0, dst], sem.at[dst_slot]).start()
            _row_copy(ys_ref.at[_tile_rows(_sorted_row(code2_ref[base + r], off_ref), n)],
                      buf.at[dst_slot, 1, dst], sem.at[dst_slot]).start()
            return carry

        lax.fori_loop(0, tm, start, 0, unroll=8)

    @pl.when(i == 0)
    def _():
        fetch(0, 0)

    @pl.when(i + 1 < pl.num_programs(0))
    def _():
        fetch(i + 1, 1 - slot)

    for k in range(2):
        _row_copy(ys_ref.at[pl.ds(0, tm * n)], buf.at[slot, k], sem.at[slot]).wait()
    mf = mf_ref[...]
    xn = (x_ref[...] + mf[:, 0:1] * _load_row_tiles(buf, tm, n, lead=(slot, 0))
          + mf[:, 1:2] * _load_row_tiles(buf, tm, n, lead=(slot, 1)))
    xo_ref[...] = xn
    ho_ref[...] = _rms(xn, g_ref[...]).astype(ho_ref.dtype)


def combine(x, ys, meta_f, code1, code2, row_off, g_next, h_dtype, tm=256):
    T, D = x.shape
    tm = min(tm, T)
    row = lambda w: pl.BlockSpec((tm, w), lambda i, c1, c2, off: (i, 0))
    return pl.pallas_call(
        _combine_kernel,
        grid_spec=pltpu.PrefetchScalarGridSpec(
            num_scalar_prefetch=3,
            grid=(T // tm,),
            in_specs=[row(D), row(LANES), pl.BlockSpec((1, D), lambda i, c1, c2, off: (0, 0)),
                      pl.BlockSpec(memory_space=pl.ANY)],
            out_specs=[row(D), row(D)],
            scratch_shapes=[pltpu.VMEM((2, 2, tm * D // LANES, LANES), F32),
                            pltpu.SemaphoreType.DMA((2,))]),
        out_shape=[jax.ShapeDtypeStruct((T, D), F32), jax.ShapeDtypeStruct((T, D), h_dtype)],
        compiler_params=_cparams(("arbitrary",), 40),
        name="moe_combine",
    )(code1, code2, row_off, x, meta_f, g_next.reshape(1, D), ys)


def moe_row_blocks(T):
    return (2 * T) // MOE_BLOCK + N_EXPERTS


def moe_layer(x, h, xs_buf, w_group, b_group, w_expert, b_expert, layer, w_gate, w_up, w_down, g_next,
              h_dtype):
    T, D = x.shape
    assert T <= MOE_CODE, "ranks inside one expert must fit the packed (expert, rank) code"
    n_blocks = xs_buf.shape[0] * LANES // D // MOE_BLOCK
    pad = LANES - N_GROUPS - N_EXPERTS
    w_r = jnp.concatenate([w_group, w_expert.reshape(D, N_EXPERTS), jnp.zeros((D, pad), F32)], axis=1)
    b_r = jnp.concatenate([b_group, b_expert.reshape(N_EXPERTS), jnp.zeros((pad,), F32)]).reshape(1, LANES)
    meta_i, meta_f, counts = router(h, w_r, b_r)
    code1, code2, row_off, blk_expert, blk_src, blk_valid, blk_first = moe_plan(meta_i, counts, n_blocks)
    xs = dispatch(h, D, code1, code2, row_off, xs_buf)
    FF = w_gate.shape[-1]
    ys = experts(xs, blk_expert + layer * N_EXPERTS, blk_src, blk_valid, blk_first,
                 w_gate.reshape(-1, D, FF), w_up.reshape(-1, D, FF), w_down.reshape(-1, FF, D))
    xo, ho = combine(x, ys, meta_f, code1, code2, row_off, g_next, h_dtype)
    return xo, ho, xs


def _forget_kernel(h_ref, w_ref, b_ref, o_ref, run_ref):
    @pl.when(pl.program_id(0) == 0)
    def _():
        run_ref[...] = jnp.zeros_like(run_ref)

    h = h_ref[...]
    z = sum(jnp.dot(h, w, preferred_element_type=F32) for w in _bf16_pieces(w_ref[...], 2)) + b_ref[...]
    log_f = jax.nn.log_sigmoid(z)
    tm = z.shape[0]
    r = lax.broadcasted_iota(I32, (tm, tm), 0)
    c = lax.broadcasted_iota(I32, (tm, tm), 1)
    upto = (c <= r).astype(BF16)
    cum = sum(jnp.dot(upto, p, preferred_element_type=F32) for p in _bf16_pieces(log_f, 3)) + run_ref[0:1, :]
    o_ref[...] = cum
    run_ref[...] = jnp.broadcast_to(cum[tm - 1:tm, :], run_ref.shape)


def forget_cumsum(h, w_f, b_f, tm=256):
    T, D = h.shape
    tm = min(tm, T)
    return pl.pallas_call(
        _forget_kernel,
        grid=(T // tm,),
        in_specs=[pl.BlockSpec((tm, D), lambda i: (i, 0)),
                  pl.BlockSpec((D, LANES), lambda i: (0, 0)),
                  pl.BlockSpec((1, LANES), lambda i: (0, 0))],
        out_specs=pl.BlockSpec((tm, LANES), lambda i: (i, 0)),
        out_shape=jax.ShapeDtypeStruct((T, LANES), F32),
        scratch_shapes=[pltpu.VMEM((8, LANES), F32)],
        compiler_params=_cparams(("arbitrary",), 40),
        name="forget_cumsum",
    )(h, w_f, b_f)


def _flash_kernel(q_ref, k_ref, v_ref, ck_ref, cq_ref, o_ref, m_ref, acc_ref, *, scale, sub):
    qi = pl.program_id(1)
    bq = q_ref.shape[0]
    dh = FOX_HEAD_DIM
    heads = q_ref.shape[1] // dh
    m_ref[...] = jnp.full_like(m_ref, -jnp.inf)
    acc_ref[...] = jnp.zeros_like(acc_ref)
    ones = jnp.ones((bq, dh), BF16)
    n_sub = bq // sub

    def chunk(start, diagonal):
        def scores(hh):
            cols = slice(hh * dh, (hh + 1) * dh)
            k = k_ref[pl.ds(start, bq), cols]
            return lax.dot_general(q_ref[:, cols], k, (((1,), (1,)), ((), ())), preferred_element_type=F32)

        def softmax(hh, s_all):
            c0 = cq_ref[hh, :, 0:1]
            bias = (c0 - ck_ref[hh, :, pl.ds(start, bq)]) * LOG2E
            ps = []
            for r in range(n_sub):
                rows = slice(r * sub, (r + 1) * sub)
                s = s_all[rows, :] * (scale * LOG2E) + bias
                if diagonal:
                    row = lax.broadcasted_iota(I32, s.shape, 0) + r * sub
                    col = lax.broadcasted_iota(I32, s.shape, 1)
                    s = jnp.where(col <= row, s, -jnp.inf)
                m_prev = m_ref[hh, rows, :]
                m_new = jnp.maximum(m_prev, jnp.max(s, axis=1, keepdims=True))
                m_ref[hh, rows, :] = m_new
                ps.append(jnp.exp2(s - jnp.concatenate([m_new] * (bq // LANES), axis=1)).astype(BF16))
                alpha = jnp.exp2(m_prev - m_new)
                acc_ref[hh, rows, :] = jnp.concatenate([alpha] * 2, axis=1) * acc_ref[hh, rows, :]
            return jnp.concatenate(ps, axis=0)

        def update(hh, p):
            cols = slice(hh * dh, (hh + 1) * dh)
            v1 = jnp.concatenate([v_ref[pl.ds(start, bq), cols], ones], axis=1)
            acc_ref[hh] = acc_ref[hh] + jnp.dot(p, v1, preferred_element_type=F32)

        s_next = scores(0)
        for hh in range(heads):
            s_cur = s_next
            if hh + 1 < heads:
                s_next = scores(hh + 1)
            update(hh, softmax(hh, s_cur))

    def body(j, carry):
        chunk(pl.multiple_of(j * bq, bq), False)
        return carry

    lax.fori_loop(0, qi, body, 0)
    chunk(pl.multiple_of(qi * bq, bq), True)
    for hh in range(heads):
        o_ref[:, hh * dh:(hh + 1) * dh] = (acc_ref[hh, :, 0:dh] / acc_ref[hh, :, dh:2 * dh]).astype(o_ref.dtype)


def flash_attention(qkv, cum_t, bq=512, sub=32, heads=4):
    T = qkv.shape[0]
    H, dh = FOX_HEADS, FOX_HEAD_DIM
    bq = min(bq, T)
    kern = functools.partial(_flash_kernel, scale=dh ** -0.5, sub=min(sub, bq))
    hw = heads * dh
    nhb = H // heads
    return pl.pallas_call(
        kern,
        grid=(nhb, T // bq),
        in_specs=[pl.BlockSpec((bq, hw), lambda h, i: (i, h)),
                  pl.BlockSpec((T, hw), lambda h, i: (0, nhb + h)),
                  pl.BlockSpec((T, hw), lambda h, i: (0, 2 * nhb + h)),
                  pl.BlockSpec((heads, 1, T), lambda h, i: (h, 0, 0)),
                  pl.BlockSpec((heads, 1, bq), lambda h, i: (h, 0, i))],
        out_specs=pl.BlockSpec((bq, hw), lambda h, i: (i, h)),
        out_shape=jax.ShapeDtypeStruct((T, H * dh), BF16),
        scratch_shapes=[pltpu.VMEM((heads, bq, LANES), F32), pltpu.VMEM((heads, bq, 2 * dh), F32)],
        compiler_params=_cparams(("parallel", "arbitrary"), 48),
        name="fox_attention",
    )(qkv, qkv, qkv, cum_t, cum_t)


def kernel(x, ab_norm, ab_w_in, ab_conv_w, s5_lambda_re, s5_lambda_im, s5_log_dt, s5_b_re, s5_b_im,
           s5_c_re, s5_c_im, s5_d, s5_w_glu, s5_b_glu, ab_w_out, c_norm, c_w_in, c_b_forget, c_w_out,
           ffn_norm, router_w_group, router_b_group, router_w_expert, router_b_expert,
           moe_w_gate, moe_w_up, moe_w_down, final_norm):
    bsz, L, D = x.shape
    depth = ffn_norm.shape[0]
    cw = ab_conv_w.shape[-1]
    xt = x.reshape(bsz * L, D)
    h = None
    xs_buf = jnp.zeros((moe_row_blocks(bsz * L) * MOE_BLOCK * D // LANES, LANES), F32)
    for i in range(depth):
        j = i // 2
        if i % 2 == 0:
            g_in = ab_norm[j]
            if h is not None:
                proj = matmul(h, ab_w_in[j], F32)
            else:
                proj = norm_matmul(xt, g_in, ab_w_in[j].astype(BF16), F32)
            y_conv = conv_mixer(proj, ab_conv_w[j])
            y_gelu = s5_mixer(proj, 3 * cw, s5_lambda_re[j], s5_lambda_im[j], s5_log_dt[j],
                              s5_b_re[j], s5_b_im[j], s5_c_re[j], s5_c_im[j], s5_d[j])
            xt, hf = glu_out(xt, y_conv, y_gelu, s5_w_glu[j].astype(BF16), s5_b_glu[j],
                             ab_w_out[j].astype(BF16), ffn_norm[i])
        else:
            assert h is not None, "an attention layer always follows a MoE combine that emits its norm"
            hd = FOX_HEADS * FOX_HEAD_DIM
            qkv = matmul(h, c_w_in[j], BF16, n_cols=3 * hd)
            w_f = jnp.pad(c_w_in[j][:, 3 * hd:], ((0, 0), (0, LANES - FOX_HEADS)))
            b_f = jnp.pad(c_b_forget[j], (0, LANES - FOX_HEADS)).reshape(1, LANES)
            cum = forget_cumsum(h, w_f, b_f)
            cum_t = cum[:, :FOX_HEADS].T.reshape(FOX_HEADS, 1, bsz * L)
            att = flash_attention(qkv, cum_t)
            xt, hf = proj_residual(xt, att, c_w_out[j].astype(BF16), ffn_norm[i])
        last = i == depth - 1
        if last:
            g_next = final_norm
        elif (i + 1) % 2 == 0:
            g_next = ab_norm[(i + 1) // 2]
        else:
            g_next = c_norm[(i + 1) // 2]
        xt, h, xs_buf = moe_layer(xt, hf, xs_buf, router_w_group[i], router_b_group[i], router_w_expert[i],
                                  router_b_expert[i], i, moe_w_gate, moe_w_up, moe_w_down,
                                  g_next, F32 if last else BF16)
    return h.reshape(bsz, L, D)
```

```python
import functools
import math

import jax
import jax.numpy as jnp
from jax import lax
from jax.experimental import pallas as pl
from jax.experimental.pallas import tpu as pltpu

F32 = jnp.float32
BF16 = jnp.bfloat16
I32 = jnp.int32
U32 = jnp.uint32

RMS_EPS = 1e-6
LANES = 128
LOG2E = math.log2(math.e)
MIB = 1024 * 1024

CONV_K = 3
S5_GROUP = 16
S5_STATE = 64
S5_CHUNK = 16
S5_OCT = LANES // S5_GROUP
FOX_HEADS = 16
FOX_HEAD_DIM = 128
N_GROUPS = 4
EXPERTS_PER_GROUP = 8
N_EXPERTS = N_GROUPS * EXPERTS_PER_GROUP
EXPERT_LANE0 = N_GROUPS
MOE_BLOCK = 256
MOE_CODE_BITS = 16
MOE_CODE = 1 << MOE_CODE_BITS


def _cparams(sem, vmem_mib):
    return pltpu.CompilerParams(dimension_semantics=sem, vmem_limit_bytes=vmem_mib * MIB)


def _bf16_pieces(x, n):
    pieces = []
    for _ in range(n):
        p = x.astype(BF16)
        pieces.append(p)
        x = x - p.astype(F32)
    return pieces


def _rms(x, g):
    ms = jnp.mean(x * x, axis=-1, keepdims=True)
    return x * lax.rsqrt(ms + RMS_EPS) * g


def _norm_matmul_kernel(x_ref, g_ref, w_ref, o_ref, h_ref):
    @pl.when(pl.program_id(1) == 0)
    def _():
        h_ref[...] = _rms(x_ref[...], g_ref[...]).astype(BF16)

    o_ref[...] = jnp.dot(h_ref[...], w_ref[...], preferred_element_type=F32).astype(o_ref.dtype)


def norm_matmul(x, g, w, out_dtype, tm=1024, tn=1024):
    T, D = x.shape
    N = w.shape[1]
    tm, tn = min(tm, T), min(tn, N)
    return pl.pallas_call(
        _norm_matmul_kernel,
        grid=(T // tm, N // tn),
        in_specs=[pl.BlockSpec((tm, D), lambda i, j: (i, 0)),
                  pl.BlockSpec((1, D), lambda i, j: (0, 0)),
                  pl.BlockSpec((D, tn), lambda i, j: (0, j))],
        out_specs=pl.BlockSpec((tm, tn), lambda i, j: (i, j)),
        out_shape=jax.ShapeDtypeStruct((T, N), out_dtype),
        scratch_shapes=[pltpu.VMEM((tm, D), BF16)],
        compiler_params=_cparams(("parallel", "arbitrary"), 56),
        name="norm_matmul",
    )(x, g.reshape(1, D), w)


def _matmul_kernel(a_ref, w_ref, o_ref, wb_ref):
    @pl.when(pl.program_id(1) == 0)
    def _():
        wb_ref[...] = w_ref[...].astype(BF16)

    o_ref[...] = jnp.dot(a_ref[...], wb_ref[...], preferred_element_type=F32).astype(o_ref.dtype)


def matmul(a, w, out_dtype, n_cols=None, tm=1024, tn=1024):
    T, K = a.shape
    N = w.shape[1] if n_cols is None else n_cols
    tm, tn = min(tm, T), min(tn, N)
    return pl.pallas_call(
        _matmul_kernel,
        grid=(N // tn, T // tm),
        in_specs=[pl.BlockSpec((tm, K), lambda j, i: (i, 0)),
                  pl.BlockSpec((K, tn), lambda j, i: (0, j))],
        out_specs=pl.BlockSpec((tm, tn), lambda j, i: (i, j)),
        out_shape=jax.ShapeDtypeStruct((T, N), out_dtype),
        scratch_shapes=[pltpu.VMEM((K, tn), BF16)],
        compiler_params=_cparams(("parallel", "arbitrary"), 48),
        name="matmul",
    )(a, w)


def _conv_kernel(gb_ref, gc_ref, u_ref, gcp_ref, up_ref, w_ref, o_ref):
    w0, w1, w2 = w_ref[0:1, :], w_ref[1:2, :], w_ref[2:3, :]
    v = gc_ref[...] * u_ref[...]
    y = w2 * v + w1 * pltpu.roll(v, 1, 0) + w0 * pltpu.roll(v, 2, 0)
    o_ref[...] = (gb_ref[...] * y).astype(o_ref.dtype)
    vp = gcp_ref[...] * up_ref[...]
    vp = jnp.where(pl.program_id(0) > 0, vp, jnp.zeros_like(vp))
    v8 = v[0:8, :]
    row = lax.broadcasted_iota(I32, v8.shape, 0)
    v1 = jnp.where(row < 1, pltpu.roll(vp, 1, 0), pltpu.roll(v8, 1, 0))
    v2 = jnp.where(row < 2, pltpu.roll(vp, 2, 0), pltpu.roll(v8, 2, 0))
    o_ref[0:8, :] = (gb_ref[0:8, :] * (w2 * v8 + w1 * v1 + w0 * v2)).astype(o_ref.dtype)


def conv_mixer(proj, conv_w, tm=512):
    T = proj.shape[0]
    CW = conv_w.shape[1]
    tm = min(tm, T)
    r8 = tm // 8
    cur = lambda c: pl.BlockSpec((tm, CW), lambda i: (i, c))
    prev = lambda c: pl.BlockSpec((8, CW), lambda i: (jnp.maximum(i * r8 - 1, 0), c))
    return pl.pallas_call(
        _conv_kernel,
        grid=(T // tm,),
        in_specs=[cur(0), cur(1), cur(2), prev(1), prev(2),
                  pl.BlockSpec((CONV_K, CW), lambda i: (0, 0))],
        out_specs=pl.BlockSpec((tm, CW), lambda i: (i, 0)),
        out_shape=jax.ShapeDtypeStruct((T, CW), BF16),
        compiler_params=_cparams(("parallel",), 40),
        name="conv_mixer",
    )(proj, proj, proj, proj, proj, conv_w)


def _cmul(a, b):
    return a[0] * b[0] - a[1] * b[1], a[0] * b[1] + a[1] * b[0]


def s5_matrices(lam_re, lam_im, log_dt, b_re, b_im, c_re, c_im):
    C = S5_CHUNK
    dt = jnp.exp(log_dt)[:, None]
    a, b = lam_re * dt, lam_im * dt
    mag = jnp.exp(a)
    lbar = (mag * jnp.cos(b), mag * jnp.sin(b))
    den = lam_re * lam_re + lam_im * lam_im
    inv_lam = (lam_re / den, -lam_im / den)
    coef = _cmul((lbar[0] - 1.0, lbar[1]), inv_lam)
    bbar = _cmul((coef[0][..., None], coef[1][..., None]), (b_re, b_im))
    j = jnp.arange(C + 1, dtype=F32)[None, :, None]
    pmag = jnp.exp(a[:, None, :] * j)
    pw = (pmag * jnp.cos(b[:, None, :] * j), pmag * jnp.sin(b[:, None, :] * j))
    hi = lax.Precision.HIGHEST
    cp = _cmul((c_re[:, None], c_im[:, None]), (pw[0][:, :C, None, :], pw[1][:, :C, None, :]))
    kj = (jnp.einsum('gjhn,gnk->gjhk', cp[0], bbar[0], precision=hi)
          - jnp.einsum('gjhn,gnk->gjhk', cp[1], bbar[1], precision=hi))
    return pw, bbar, kj


def _lane_tile_blockdiag(m):
    Q, O, r, c = m.shape
    out = jnp.zeros((Q, O, r, O, c), m.dtype)
    for g in range(O):
        out = out.at[:, g, :, g, :].set(m[:, g])
    return out.reshape(Q, O * r, O * c)


def s5_lane_tile_operands(lam_re, lam_im, log_dt, b_re, b_im, c_re, c_im):
    C, H, N, O = S5_CHUNK, S5_GROUP, S5_STATE, S5_OCT
    pw, bbar, kj = s5_matrices(lam_re, lam_im, log_dt, b_re, b_im, c_re, c_im)
    Q = lam_re.shape[0] // O
    pw_q = jnp.stack([p.reshape(Q, O, C + 1, N).transpose(0, 2, 1, 3).reshape(Q, C + 1, O * N) for p in pw],
                     axis=1)
    bb = jnp.stack([_lane_tile_blockdiag(m.transpose(0, 2, 1).reshape(Q, O, H, N)) for m in bbar], axis=1)
    cc = jnp.stack([_lane_tile_blockdiag(m.reshape(Q, O, H, N)) for m in (c_re, c_im)], axis=1)
    kq = kj.reshape(Q, O, C, H, H).transpose(0, 2, 1, 4, 3).reshape(Q, C, O * H, H).astype(BF16)
    return pw_q, bb, cc, kq


def _s5_kernel(u_ref, pw_ref, bb_ref, cc_ref, kj_ref, d_ref, o_ref, t_ref, w_ref, v_ref, uf_ref, s_ref,
               x_ref, c_ref):
    C = S5_CHUNK
    tb = u_ref.shape[0]
    nc = tb // C
    sw = pw_ref.shape[3]

    @pl.when((pl.program_id(0) == 0) & (pl.program_id(1) == 0))
    def _():
        t_ref[...] = jnp.zeros_like(t_ref)

    @pl.when(pl.program_id(1) == 0)
    def _():
        bb = (bb_ref[0, 0], bb_ref[0, 1])
        cc = (cc_ref[0, 0], cc_ref[0, 1])
        H = kj_ref.shape[3]
        hb = H.bit_length() - 1
        spread = ((lax.broadcasted_iota(I32, (H, LANES), 1) & (H - 1))
                  == lax.broadcasted_iota(I32, (H, LANES), 0)).astype(BF16)
        same_group = ((lax.broadcasted_iota(I32, (LANES, LANES), 0) >> hb)
                      == (lax.broadcasted_iota(I32, (LANES, LANES), 1) >> hb))
        for j in range(C):
            a = _cmul(bb, (pw_ref[0, 0, j:j + 1, :], pw_ref[0, 1, j:j + 1, :]))
            s = C - 1 - j
            w_ref[s * LANES:(s + 1) * LANES, 0:sw] = a[0].astype(BF16)
            w_ref[s * LANES:(s + 1) * LANES, sw:2 * sw] = a[1].astype(BF16)
            lag = jnp.dot(kj_ref[0, j], spread, preferred_element_type=F32)
            lag = jnp.where(same_group, lag, 0.0).astype(BF16)
            for s0 in range(C - j):
                t_ref[s0 * LANES:(s0 + 1) * LANES, (s0 + j) * LANES:(s0 + j + 1) * LANES] = lag
            g = _cmul(cc, (pw_ref[0, 0, j + 1:j + 2, :], pw_ref[0, 1, j + 1:j + 2, :]))
            v_ref[j * LANES:(j + 1) * LANES, 0:sw] = g[0].astype(BF16)
            v_ref[j * LANES:(j + 1) * LANES, sw:2 * sw] = (-g[1]).astype(BF16)
        c_ref[...] = jnp.zeros_like(c_ref)

    for s in range(C):
        uf_ref[:, s * LANES:(s + 1) * LANES] = u_ref[pl.ds(s, nc, stride=C), :].astype(BF16)
    s_ref[...] = jnp.dot(uf_ref[...], w_ref[...], preferred_element_type=F32)
    lr, li = pw_ref[0, 0, C:C + 1, :], pw_ref[0, 1, C:C + 1, :]
    row = lax.broadcasted_iota(I32, (8, sw), 0)

    def tile_step(i, carry):
        xr, xi = carry
        r0 = pl.multiple_of(i * 8, 8)
        sr = s_ref[pl.ds(r0, 8), 0:sw]
        si = s_ref[pl.ds(r0, 8), sw:2 * sw]
        tr = jnp.zeros((8, sw), F32)
        ti = jnp.zeros((8, sw), F32)
        for r in range(8):
            tr = jnp.where(row == r, xr, tr)
            ti = jnp.where(row == r, xi, ti)
            xr, xi = lr * xr - li * xi + sr[r:r + 1, :], lr * xi + li * xr + si[r:r + 1, :]
        x_ref[pl.ds(r0, 8), 0:sw] = tr
        x_ref[pl.ds(r0, 8), sw:2 * sw] = ti
        return xr, xi

    xr, xi = lax.fori_loop(0, nc // 8, tile_step, (c_ref[0:1, 0:sw], c_ref[0:1, sw:2 * sw]))
    c_ref[0:1, 0:sw] = xr
    c_ref[0:1, sw:2 * sw] = xi
    y_state = lax.dot_general(x_ref[...].astype(BF16), v_ref[...], (((1,), (1,)), ((), ())),
                              preferred_element_type=F32)
    quarter = C // 4
    for qt in range(4):
        rows = (qt + 1) * quarter * LANES
        cols = slice(qt * quarter * LANES, (qt + 1) * quarter * LANES)
        y = jnp.dot(uf_ref[:, 0:rows], t_ref[0:rows, cols], preferred_element_type=F32) + y_state[:, cols]
        for s in range(quarter):
            st = qt * quarter + s
            ys = y[:, s * LANES:(s + 1) * LANES] + d_ref[...] * u_ref[pl.ds(st, nc, stride=C), :]
            o_ref[pl.ds(st, nc, stride=C), :] = jax.nn.gelu(ys)


def s5_mixer(proj, col0, lam_re, lam_im, log_dt, b_re, b_im, c_re, c_im, d, tb=8192):
    T = proj.shape[0]
    C, N, O = S5_CHUNK, S5_STATE, S5_OCT
    W = d.shape[0]
    Q = W // LANES
    tb = min(tb, T)
    pw_q, bb, cc, kq = s5_lane_tile_operands(lam_re, lam_im, log_dt, b_re, b_im, c_re, c_im)
    cb0 = col0 // LANES
    nc = tb // C
    sw = O * N
    quad = lambda a: pl.BlockSpec((1,) + a.shape[1:], lambda q, t: (q, 0, 0, 0))
    return pl.pallas_call(
        _s5_kernel,
        grid=(Q, T // tb),
        in_specs=[pl.BlockSpec((tb, LANES), lambda q, t: (t, cb0 + q)),
                  quad(pw_q), quad(bb), quad(cc), quad(kq),
                  pl.BlockSpec((1, LANES), lambda q, t: (0, q))],
        out_specs=pl.BlockSpec((tb, LANES), lambda q, t: (t, q)),
        out_shape=jax.ShapeDtypeStruct((T, W), F32),
        scratch_shapes=[pltpu.VMEM((C * LANES, C * LANES), BF16),
                        pltpu.VMEM((C * LANES, 2 * sw), BF16),
                        pltpu.VMEM((C * LANES, 2 * sw), BF16),
                        pltpu.VMEM((nc, C * LANES), BF16),
                        pltpu.VMEM((nc, 2 * sw), F32),
                        pltpu.VMEM((nc, 2 * sw), F32),
                        pltpu.VMEM((8, 2 * sw), F32)],
        compiler_params=_cparams(("arbitrary", "arbitrary"), 56),
        name="s5_scan",
    )(proj, pw_q, bb, cc, kq, d.reshape(1, W))


def _store_row_tiles(ref, val, lead=()):
    rows, width = val.shape
    n = width // LANES
    for j in range(n):
        ref[lead + (pl.ds(j, rows, stride=n), slice(None))] = val[:, j * LANES:(j + 1) * LANES]


def _load_row_tiles(ref, rows, n, lead=()):
    return jnp.concatenate([ref[lead + (pl.ds(j, rows, stride=n), slice(None))] for j in range(n)], axis=1)


def _row_tile_spec(tm, width, index_map):
    return pl.BlockSpec((tm * (width // LANES), LANES), index_map)


def _glu_out_kernel(x_ref, yc_ref, yg_ref, wglu_ref, bglu_ref, wout_ref, g_ref, xo_ref, ho_ref):
    yg = yg_ref[...]
    z = jnp.dot(yg.astype(BF16), wglu_ref[...], preferred_element_type=F32) + bglu_ref[...]
    ys = (yg * jax.nn.sigmoid(z)).astype(BF16)
    cw = yc_ref.shape[1]
    xn = (x_ref[...]
          + jnp.dot(yc_ref[...], wout_ref[0:cw, :], preferred_element_type=F32)
          + jnp.dot(ys, wout_ref[cw:, :], preferred_element_type=F32))
    xo_ref[...] = xn
    _store_row_tiles(ho_ref, _rms(xn, g_ref[...]))


def glu_out(x, y_conv, y_gelu, w_glu, b_glu, w_out, g_next, tm=512):
    T, D = x.shape
    CW, SW = y_conv.shape[1], y_gelu.shape[1]
    tm = min(tm, T)
    row = lambda w: pl.BlockSpec((tm, w), lambda i: (i, 0))
    full = lambda a: pl.BlockSpec(a.shape, lambda i: (0,) * a.ndim)
    bg, g2 = b_glu.reshape(1, SW), g_next.reshape(1, D)
    return pl.pallas_call(
        _glu_out_kernel,
        grid=(T // tm,),
        in_specs=[row(D), row(CW), row(SW), full(w_glu), full(bg), full(w_out), full(g2)],
        out_specs=[row(D), _row_tile_spec(tm, D, lambda i: (i, 0))],
        out_shape=[jax.ShapeDtypeStruct((T, D), F32), jax.ShapeDtypeStruct((T * D // LANES, LANES), F32)],
        compiler_params=_cparams(("parallel",), 56),
        name="glu_out",
    )(x, y_conv, y_gelu, w_glu, bg, w_out, g2)


def _proj_residual_kernel(x_ref, a_ref, w_ref, g_ref, xo_ref, ho_ref):
    xn = x_ref[...] + jnp.dot(a_ref[...], w_ref[...], preferred_element_type=F32)
    xo_ref[...] = xn
    _store_row_tiles(ho_ref, _rms(xn, g_ref[...]))


def proj_residual(x, a, w, g_next, tm=512):
    T, D = x.shape
    K = a.shape[1]
    tm = min(tm, T)
    row = lambda w_: pl.BlockSpec((tm, w_), lambda i: (i, 0))
    full = lambda arr: pl.BlockSpec(arr.shape, lambda i: (0,) * arr.ndim)
    g2 = g_next.reshape(1, D)
    return pl.pallas_call(
        _proj_residual_kernel,
        grid=(T // tm,),
        in_specs=[row(D), row(K), full(w), full(g2)],
        out_specs=[row(D), _row_tile_spec(tm, D, lambda i: (i, 0))],
        out_shape=[jax.ShapeDtypeStruct((T, D), F32), jax.ShapeDtypeStruct((T * D // LANES, LANES), F32)],
        compiler_params=_cparams(("parallel",), 56),
        name="proj_residual",
    )(x, a, w, g2)


def _router_kernel(h_ref, w_ref, b_ref, mi_ref, mf_ref, cnt_ref, run_ref):
    i = pl.program_id(0)

    @pl.when(i == 0)
    def _():
        run_ref[...] = jnp.zeros_like(run_ref)

    D = w_ref.shape[0]
    tm = h_ref.shape[0] * LANES // D
    h = _load_row_tiles(h_ref, tm, D // LANES)
    h_hi, h_lo = _bf16_pieces(h, 2)
    w_hi, w_lo = _bf16_pieces(w_ref[...], 2)
    logits = (jnp.dot(h_hi, w_hi, preferred_element_type=F32)
              + (jnp.dot(h_hi, w_lo, preferred_element_type=F32)
                 + jnp.dot(h_lo, w_hi, preferred_element_type=F32))) + b_ref[...]
    lane = lax.broadcasted_iota(I32, logits.shape, 1)
    neg = jnp.float32(-jnp.inf)
    gl = jnp.where(lane < N_GROUPS, logits, neg)
    gmax = jnp.max(gl, axis=1, keepdims=True)
    gsum = jnp.sum(jnp.where(lane < N_GROUPS, jnp.exp(gl - gmax), 0.0), axis=1, keepdims=True)
    gw = 1.0 / gsum
    gidx = jnp.min(jnp.where(gl == gmax, lane, LANES), axis=1, keepdims=True)
    lo = EXPERT_LANE0 + EXPERTS_PER_GROUP * gidx
    el = jnp.where((lane >= lo) & (lane < lo + EXPERTS_PER_GROUP), logits, neg)
    v1 = jnp.max(el, axis=1, keepdims=True)
    i1 = jnp.min(jnp.where(el == v1, lane, LANES), axis=1, keepdims=True)
    el2 = jnp.where(lane == i1, neg, el)
    v2 = jnp.max(el2, axis=1, keepdims=True)
    i2 = jnp.min(jnp.where(el2 == v2, lane, LANES), axis=1, keepdims=True)
    t = jnp.exp(v2 - v1)
    w1 = gw / (1.0 + t)
    w2 = gw * t / (1.0 + t)
    hit1 = lane == i1
    hit2 = lane == i2
    cnt = (hit1 | hit2).astype(BF16)
    r = lax.broadcasted_iota(I32, (tm, tm), 0)
    c = lax.broadcasted_iota(I32, (tm, tm), 1)
    before = (c < r).astype(BF16)
    cum = jnp.dot(before, cnt, preferred_element_type=F32) + run_ref[0:1, :]
    rank1 = jnp.sum(jnp.where(hit1, cum, 0.0), axis=1, keepdims=True).astype(I32)
    rank2 = jnp.sum(jnp.where(hit2, cum, 0.0), axis=1, keepdims=True).astype(I32)
    run = run_ref[0:1, :] + jnp.sum(cnt.astype(F32), axis=0, keepdims=True)
    run_ref[...] = jnp.broadcast_to(run, run_ref.shape)
    cnt_ref[...] = jnp.broadcast_to(run, cnt_ref.shape)
    code1 = (i1 - EXPERT_LANE0) * MOE_CODE + rank1
    code2 = (i2 - EXPERT_LANE0) * MOE_CODE + rank2
    mi_ref[...] = jnp.where(lane == 0, code1, jnp.where(lane == 1, code2, 0))
    mf_ref[...] = jnp.where(lane == 0, w1, jnp.where(lane == 1, w2, 0.0))


def router(h, w_r, b_r, tm=256):
    D = w_r.shape[0]
    T = h.shape[0] * LANES // D
    tm = min(tm, T)
    row = lambda w: pl.BlockSpec((tm, w), lambda i: (i, 0))
    full = lambda a: pl.BlockSpec(a.shape, lambda i: (0,) * a.ndim)
    return pl.pallas_call(
        _router_kernel,
        grid=(T // tm,),
        in_specs=[_row_tile_spec(tm, D, lambda i: (i, 0)), full(w_r), full(b_r)],
        out_specs=[row(LANES), row(LANES), pl.BlockSpec((8, LANES), lambda i: (0, 0))],
        out_shape=[jax.ShapeDtypeStruct((T, LANES), I32), jax.ShapeDtypeStruct((T, LANES), F32),
                   jax.ShapeDtypeStruct((8, LANES), F32)],
        scratch_shapes=[pltpu.VMEM((8, LANES), F32)],
        compiler_params=_cparams(("arbitrary",), 40),
        name="moe_router",
    )(h, w_r, b_r)


def moe_plan(meta_i, counts, n_blocks):
    B = MOE_BLOCK
    code1, code2 = meta_i[:, 0], meta_i[:, 1]
    cnt = counts[0, EXPERT_LANE0:EXPERT_LANE0 + N_EXPERTS].astype(I32)
    nblk = (cnt + B - 1) // B
    blk_end = jnp.cumsum(nblk)
    blk_off = blk_end - nblk
    b = jnp.arange(n_blocks, dtype=I32)
    total = blk_end[-1]
    owner = jnp.minimum(jnp.sum((blk_end[None, :] <= b[:, None]).astype(I32), axis=1), N_EXPERTS - 1)
    valid = b < total
    last = jnp.maximum(total - 1, 0)
    blk_expert = jnp.where(valid, owner, owner[last])
    blk_src = jnp.where(valid, b, last)
    blk_first = (valid & (b == blk_off[owner])).astype(I32)
    e = jnp.arange(N_EXPERTS, dtype=I32)
    used = nblk > 0
    slot = (jnp.cumsum(used.astype(I32)) - 1) % 2
    later = (e[None, :] > e[:, None]) & used[None, :]
    next_used = jnp.min(jnp.where(later, e[None, :], N_EXPERTS), axis=1)
    next_used = jnp.where(next_used < N_EXPERTS, next_used, -1)
    return (code1, code2, blk_off * B, blk_expert, blk_src, valid.astype(I32), blk_first,
            next_used[blk_expert], slot[blk_expert])


def _row_copy(src, dst, sem):
    return pltpu.make_async_copy(src, dst, sem)


def _sorted_row(code, off_ref):
    return off_ref[code >> MOE_CODE_BITS] + (code & (MOE_CODE - 1))


def _tile_rows(row, n):
    return pl.ds(pl.multiple_of(row * n, n), n)


def _dispatch_kernel(code1_ref, code2_ref, off_ref, hp_ref, xs_in_ref, xs_ref, sem, *, n):
    del xs_in_ref
    tm = hp_ref.shape[0] // n
    base = pl.program_id(0) * tm

    def copies(r):
        src = hp_ref.at[_tile_rows(r, n)]
        return (_row_copy(src, xs_ref.at[_tile_rows(_sorted_row(code1_ref[base + r], off_ref), n)], sem),
                _row_copy(src, xs_ref.at[_tile_rows(_sorted_row(code2_ref[base + r], off_ref), n)], sem))

    def start(r, carry):
        for cp in copies(r):
            cp.start()
        return carry

    lax.fori_loop(0, tm, start, 0, unroll=8)
    for _ in range(2):
        _row_copy(hp_ref, xs_ref.at[pl.ds(0, tm * n)], sem).wait()


def dispatch(hp, width, code1, code2, row_off, xs0, tm=256):
    n = width // LANES
    T = hp.shape[0] // n
    tm = min(tm, T)
    return pl.pallas_call(
        functools.partial(_dispatch_kernel, n=n),
        grid_spec=pltpu.PrefetchScalarGridSpec(
            num_scalar_prefetch=3,
            grid=(T // tm,),
            in_specs=[_row_tile_spec(tm, width, lambda i, c1, c2, off: (i, 0)),
                      pl.BlockSpec(memory_space=pl.ANY)],
            out_specs=pl.BlockSpec(memory_space=pl.ANY),
            scratch_shapes=[pltpu.SemaphoreType.DMA(())]),
        out_shape=jax.ShapeDtypeStruct(xs0.shape, hp.dtype),
        input_output_aliases={4: 0},
        compiler_params=_cparams(("arbitrary",), 32),
        name="moe_dispatch",
    )(code1, code2, row_off, hp, xs0)


def _experts_kernel(be_ref, bs_ref, bv_ref, bf_ref, bn_ref, bl_ref, xs_ref, wg_ref, wu_ref, wd_ref, ys_ref,
                    wg_f, wu_f, wd_f, wg_s, wu_s, wd_s, sem):
    del bs_ref
    b = pl.program_id(0)

    def weight_copies(e, slot):
        return (pltpu.make_async_copy(wg_ref.at[e], wg_f.at[slot], sem.at[slot]),
                pltpu.make_async_copy(wu_ref.at[e], wu_f.at[slot], sem.at[slot]),
                pltpu.make_async_copy(wd_ref.at[e], wd_f.at[slot], sem.at[slot]))

    @pl.when(bv_ref[b] == 0)
    def _():
        ys_ref[...] = jnp.zeros_like(ys_ref)

    @pl.when(bv_ref[b] > 0)
    def _():
        @pl.when(bf_ref[b] == 1)
        def _():
            slot = bl_ref[b]

            @pl.when(b == 0)
            def _():
                for cp in weight_copies(be_ref[b], slot):
                    cp.start()

            for cp in weight_copies(be_ref[b], slot):
                cp.wait()
            wg_s[...] = wg_f[slot].astype(BF16)
            wu_s[...] = wu_f[slot].astype(BF16)
            wd_s[...] = wd_f[slot].astype(BF16)

            @pl.when(bn_ref[b] >= 0)
            def _():
                for cp in weight_copies(bn_ref[b], 1 - slot):
                    cp.start()

        D = wg_s.shape[0]
        a = _load_row_tiles(xs_ref, MOE_BLOCK, D // LANES).astype(BF16)
        gate = jnp.dot(a, wg_s[...], preferred_element_type=F32)
        up = jnp.dot(a, wu_s[...], preferred_element_type=F32)
        mid = (jax.nn.silu(gate) * up).astype(BF16)
        _store_row_tiles(ys_ref, jnp.dot(mid, wd_s[...], preferred_element_type=F32))


def experts(xs, blk_expert, blk_src, blk_valid, blk_first, blk_next, blk_slot, w_gate, w_up, w_down):
    B = MOE_BLOCK
    E, D, FF = w_gate.shape
    nb = xs.shape[0] * LANES // D // B
    hbm = pl.BlockSpec(memory_space=pl.ANY)
    return pl.pallas_call(
        _experts_kernel,
        grid_spec=pltpu.PrefetchScalarGridSpec(
            num_scalar_prefetch=6,
            grid=(nb,),
            in_specs=[_row_tile_spec(B, D, lambda b, be, bs, bv, bf, bn, bl: (bs[b], 0)), hbm, hbm, hbm],
            out_specs=_row_tile_spec(B, D, lambda b, be, bs, bv, bf, bn, bl: (b, 0)),
            scratch_shapes=[pltpu.VMEM((2, D, FF), F32), pltpu.VMEM((2, D, FF), F32),
                            pltpu.VMEM((2, FF, D), F32),
                            pltpu.VMEM((D, FF), BF16), pltpu.VMEM((D, FF), BF16),
                            pltpu.VMEM((FF, D), BF16), pltpu.SemaphoreType.DMA((2,))]),
        out_shape=jax.ShapeDtypeStruct(xs.shape, F32),
        compiler_params=_cparams(("arbitrary",), 56),
        name="moe_experts",
    )(blk_expert, blk_src, blk_valid, blk_first, blk_next, blk_slot, xs, w_gate, w_up, w_down)


def _combine_kernel(code1_ref, code2_ref, off_ref, x_ref, mf_ref, g_ref, ys_ref, xo_ref, ho_ref, buf, sem):
    tm, D = x_ref.shape
    n = D // LANES
    i = pl.program_id(0)
    slot = i % 2

    def fetch(step, dst_slot):
        base = step * tm

        def start(r, carry):
            dst = _tile_rows(r, n)
            _row_copy(ys_ref.at[_tile_rows(_sorted_row(code1_ref[base + r], off_ref), n)],
                      buf.at[dst_slot, 0, dst], sem.at[dst_slot]).start()
            _row_copy(ys_ref.at[_tile_rows(_sorted_row(code2_ref[base + r], off_ref), n)],
                      buf.at[dst_slot, 1, dst], sem.at[dst_slot]).start()
            return carry

        lax.fori_loop(0, tm, start, 0, unroll=8)

    @pl.when(i == 0)
    def _():
        fetch(0, 0)

    @pl.when(i + 1 < pl.num_programs(0))
    def _():
        fetch(i + 1, 1 - slot)

    for k in range(2):
        _row_copy(ys_ref.at[pl.ds(0, tm * n)], buf.at[slot, k], sem.at[slot]).wait()
    mf = mf_ref[...]
    xn = (x_ref[...] + mf[:, 0:1] * _load_row_tiles(buf, tm, n, lead=(slot, 0))
          + mf[:, 1:2] * _load_row_tiles(buf, tm, n, lead=(slot, 1)))
    xo_ref[...] = xn
    ho_ref[...] = _rms(xn, g_ref[...]).astype(ho_ref.dtype)


def combine(x, ys, meta_f, code1, code2, row_off, g_next, h_dtype, tm=256):
    T, D = x.shape
    tm = min(tm, T)
    row = lambda w: pl.BlockSpec((tm, w), lambda i, c1, c2, off: (i, 0))
    return pl.pallas_call(
        _combine_kernel,
        grid_spec=pltpu.PrefetchScalarGridSpec(
            num_scalar_prefetch=3,
            grid=(T // tm,),
            in_specs=[row(D), row(LANES), pl.BlockSpec((1, D), lambda i, c1, c2, off: (0, 0)),
                      pl.BlockSpec(memory_space=pl.ANY)],
            out_specs=[row(D), row(D)],
            scratch_shapes=[pltpu.VMEM((2, 2, tm * D // LANES, LANES), F32),
                            pltpu.SemaphoreType.DMA((2,))]),
        out_shape=[jax.ShapeDtypeStruct((T, D), F32), jax.ShapeDtypeStruct((T, D), h_dtype)],
        compiler_params=_cparams(("arbitrary",), 40),
        name="moe_combine",
    )(code1, code2, row_off, x, meta_f, g_next.reshape(1, D), ys)


def moe_row_blocks(T):
    return (2 * T) // MOE_BLOCK + N_EXPERTS


def moe_layer(x, h, xs_buf, w_group, b_group, w_expert, b_expert, layer, w_gate, w_up, w_down, g_next,
              h_dtype):
    T, D = x.shape
    assert T <= MOE_CODE, "ranks inside one expert must fit the packed (expert, rank) code"
    n_blocks = xs_buf.shape[0] * LANES // D // MOE_BLOCK
    pad = LANES - N_GROUPS - N_EXPERTS
    w_r = jnp.concatenate([w_group, w_expert.reshape(D, N_EXPERTS), jnp.zeros((D, pad), F32)], axis=1)
    b_r = jnp.concatenate([b_group, b_expert.reshape(N_EXPERTS), jnp.zeros((pad,), F32)]).reshape(1, LANES)
    meta_i, meta_f, counts = router(h, w_r, b_r)
    (code1, code2, row_off, blk_expert, blk_src, blk_valid, blk_first, blk_next,
     blk_slot) = moe_plan(meta_i, counts, n_blocks)
    xs = dispatch(h, D, code1, code2, row_off, xs_buf)
    FF = w_gate.shape[-1]
    first_row = layer * N_EXPERTS
    ys = experts(xs, blk_expert + first_row, blk_src, blk_valid, blk_first,
                 jnp.where(blk_next >= 0, blk_next + first_row, -1), blk_slot,
                 w_gate.reshape(-1, D, FF), w_up.reshape(-1, D, FF), w_down.reshape(-1, FF, D))
    xo, ho = combine(x, ys, meta_f, code1, code2, row_off, g_next, h_dtype)
    return xo, ho, xs


def _forget_kernel(h_ref, w_ref, b_ref, o_ref, run_ref):
    @pl.when(pl.program_id(0) == 0)
    def _():
        run_ref[...] = jnp.zeros_like(run_ref)

    h = h_ref[...]
    z = sum(jnp.dot(h, w, preferred_element_type=F32) for w in _bf16_pieces(w_ref[...], 2)) + b_ref[...]
    log_f = jax.nn.log_sigmoid(z)
    tm = z.shape[0]
    r = lax.broadcasted_iota(I32, (tm, tm), 0)
    c = lax.broadcasted_iota(I32, (tm, tm), 1)
    upto = (c <= r).astype(BF16)
    cum = sum(jnp.dot(upto, p, preferred_element_type=F32) for p in _bf16_pieces(log_f, 3)) + run_ref[0:1, :]
    o_ref[...] = cum
    run_ref[...] = jnp.broadcast_to(cum[tm - 1:tm, :], run_ref.shape)


def forget_cumsum(h, w_f, b_f, tm=256):
    T, D = h.shape
    tm = min(tm, T)
    return pl.pallas_call(
        _forget_kernel,
        grid=(T // tm,),
        in_specs=[pl.BlockSpec((tm, D), lambda i: (i, 0)),
                  pl.BlockSpec((D, LANES), lambda i: (0, 0)),
                  pl.BlockSpec((1, LANES), lambda i: (0, 0))],
        out_specs=pl.BlockSpec((tm, LANES), lambda i: (i, 0)),
        out_shape=jax.ShapeDtypeStruct((T, LANES), F32),
        scratch_shapes=[pltpu.VMEM((8, LANES), F32)],
        compiler_params=_cparams(("arbitrary",), 40),
        name="forget_cumsum",
    )(h, w_f, b_f)


def _flash_kernel(q_ref, k_ref, v_ref, ck_ref, cq_ref, o_ref, m_ref, acc_ref, *, scale, sub):
    qi = pl.program_id(1)
    bq = q_ref.shape[0]
    dh = FOX_HEAD_DIM
    heads = q_ref.shape[1] // dh
    m_ref[...] = jnp.full_like(m_ref, -jnp.inf)
    acc_ref[...] = jnp.zeros_like(acc_ref)
    ones = jnp.ones((bq, dh), BF16)
    n_sub = bq // sub

    def chunk(start, diagonal):
        def scores(hh):
            cols = slice(hh * dh, (hh + 1) * dh)
            k = k_ref[pl.ds(start, bq), cols]
            return lax.dot_general(q_ref[:, cols], k, (((1,), (1,)), ((), ())), preferred_element_type=F32)

        def softmax(hh, s_all):
            c0 = cq_ref[hh, :, 0:1]
            bias = (c0 - ck_ref[hh, :, pl.ds(start, bq)]) * LOG2E
            ps = []
            for r in range(n_sub):
                rows = slice(r * sub, (r + 1) * sub)
                s = s_all[rows, :] * (scale * LOG2E) + bias
                if diagonal:
                    row = lax.broadcasted_iota(I32, s.shape, 0) + r * sub
                    col = lax.broadcasted_iota(I32, s.shape, 1)
                    s = jnp.where(col <= row, s, -jnp.inf)
                m_prev = m_ref[hh, rows, :]
                m_new = jnp.maximum(m_prev, jnp.max(s, axis=1, keepdims=True))
                m_ref[hh, rows, :] = m_new
                ps.append(jnp.exp2(s - jnp.concatenate([m_new] * (bq // LANES), axis=1)).astype(BF16))
                alpha = jnp.exp2(m_prev - m_new)
                acc_ref[hh, rows, :] = jnp.concatenate([alpha] * 2, axis=1) * acc_ref[hh, rows, :]
            return jnp.concatenate(ps, axis=0)

        def update(hh, p):
            cols = slice(hh * dh, (hh + 1) * dh)
            v1 = jnp.concatenate([v_ref[pl.ds(start, bq), cols], ones], axis=1)
            acc_ref[hh] = acc_ref[hh] + jnp.dot(p, v1, preferred_element_type=F32)

        s_next = scores(0)
        for hh in range(heads):
            s_cur = s_next
            if hh + 1 < heads:
                s_next = scores(hh + 1)
            update(hh, softmax(hh, s_cur))

    def body(j, carry):
        chunk(pl.multiple_of(j * bq, bq), False)
        return carry

    lax.fori_loop(0, qi, body, 0)
    chunk(pl.multiple_of(qi * bq, bq), True)
    for hh in range(heads):
        o_ref[:, hh * dh:(hh + 1) * dh] = (acc_ref[hh, :, 0:dh] / acc_ref[hh, :, dh:2 * dh]).astype(o_ref.dtype)


def flash_attention(qkv, cum_t, bq=512, sub=32, heads=4):
    T = qkv.shape[0]
    H, dh = FOX_HEADS, FOX_HEAD_DIM
    bq = min(bq, T)
    kern = functools.partial(_flash_kernel, scale=dh ** -0.5, sub=min(sub, bq))
    hw = heads * dh
    nhb = H // heads
    return pl.pallas_call(
        kern,
        grid=(nhb, T // bq),
        in_specs=[pl.BlockSpec((bq, hw), lambda h, i: (i, h)),
                  pl.BlockSpec((T, hw), lambda h, i: (0, nhb + h)),
                  pl.BlockSpec((T, hw), lambda h, i: (0, 2 * nhb + h)),
                  pl.BlockSpec((heads, 1, T), lambda h, i: (h, 0, 0)),
                  pl.BlockSpec((heads, 1, bq), lambda h, i: (h, 0, i))],
        out_specs=pl.BlockSpec((bq, hw), lambda h, i: (i, h)),
        out_shape=jax.ShapeDtypeStruct((T, H * dh), BF16),
        scratch_shapes=[pltpu.VMEM((heads, bq, LANES), F32), pltpu.VMEM((heads, bq, 2 * dh), F32)],
        compiler_params=_cparams(("parallel", "arbitrary"), 48),
        name="fox_attention",
    )(qkv, qkv, qkv, cum_t, cum_t)


def kernel(x, ab_norm, ab_w_in, ab_conv_w, s5_lambda_re, s5_lambda_im, s5_log_dt, s5_b_re, s5_b_im,
           s5_c_re, s5_c_im, s5_d, s5_w_glu, s5_b_glu, ab_w_out, c_norm, c_w_in, c_b_forget, c_w_out,
           ffn_norm, router_w_group, router_b_group, router_w_expert, router_b_expert,
           moe_w_gate, moe_w_up, moe_w_down, final_norm):
    bsz, L, D = x.shape
    depth = ffn_norm.shape[0]
    cw = ab_conv_w.shape[-1]
    xt = x.reshape(bsz * L, D)
    h = None
    xs_buf = jnp.zeros((moe_row_blocks(bsz * L) * MOE_BLOCK * D // LANES, LANES), F32)
    for i in range(depth):
        j = i // 2
        if i % 2 == 0:
            g_in = ab_norm[j]
            if h is not None:
                proj = matmul(h, ab_w_in[j], F32)
            else:
                proj = norm_matmul(xt, g_in, ab_w_in[j].astype(BF16), F32)
            y_conv = conv_mixer(proj, ab_conv_w[j])
            y_gelu = s5_mixer(proj, 3 * cw, s5_lambda_re[j], s5_lambda_im[j], s5_log_dt[j],
                              s5_b_re[j], s5_b_im[j], s5_c_re[j], s5_c_im[j], s5_d[j])
            xt, hf = glu_out(xt, y_conv, y_gelu, s5_w_glu[j].astype(BF16), s5_b_glu[j],
                             ab_w_out[j].astype(BF16), ffn_norm[i])
        else:
            assert h is not None, "an attention layer always follows a MoE combine that emits its norm"
            hd = FOX_HEADS * FOX_HEAD_DIM
            qkv = matmul(h, c_w_in[j], BF16, n_cols=3 * hd)
            w_f = jnp.pad(c_w_in[j][:, 3 * hd:], ((0, 0), (0, LANES - FOX_HEADS)))
            b_f = jnp.pad(c_b_forget[j], (0, LANES - FOX_HEADS)).reshape(1, LANES)
            cum = forget_cumsum(h, w_f, b_f)
            cum_t = cum[:, :FOX_HEADS].T.reshape(FOX_HEADS, 1, bsz * L)
            att = flash_attention(qkv, cum_t)
            xt, hf = proj_residual(xt, att, c_w_out[j].astype(BF16), ffn_norm[i])
        last = i == depth - 1
        if last:
            g_next = final_norm
        elif (i + 1) % 2 == 0:
            g_next = ab_norm[(i + 1) // 2]
        else:
            g_next = c_norm[(i + 1) // 2]
        xt, h, xs_buf = moe_layer(xt, hf, xs_buf, router_w_group[i], router_b_group[i], router_w_expert[i],
                                  router_b_expert[i], i, moe_w_gate, moe_w_up, moe_w_down,
                                  g_next, F32 if last else BF16)
    return h.reshape(bsz, L, D)
```

```python
import functools
import math

import jax
import jax.numpy as jnp
from jax import lax
from jax.experimental import pallas as pl
from jax.experimental.pallas import tpu as pltpu

F32 = jnp.float32
BF16 = jnp.bfloat16
I32 = jnp.int32
U32 = jnp.uint32

RMS_EPS = 1e-6
LANES = 128
LOG2E = math.log2(math.e)
MIB = 1024 * 1024

CONV_K = 3
S5_GROUP = 16
S5_STATE = 64
S5_CHUNK = 16
S5_OCT = LANES // S5_GROUP
FOX_HEADS = 16
FOX_HEAD_DIM = 128
N_GROUPS = 4
EXPERTS_PER_GROUP = 8
N_EXPERTS = N_GROUPS * EXPERTS_PER_GROUP
EXPERT_LANE0 = N_GROUPS
MOE_BLOCK = 256
MOE_CODE_BITS = 16
MOE_CODE = 1 << MOE_CODE_BITS


def _cparams(sem, vmem_mib):
    return pltpu.CompilerParams(dimension_semantics=sem, vmem_limit_bytes=vmem_mib * MIB)


def _bf16_pieces(x, n):
    pieces = []
    for _ in range(n):
        p = x.astype(BF16)
        pieces.append(p)
        x = x - p.astype(F32)
    return pieces


def _rms(x, g):
    ms = jnp.mean(x * x, axis=-1, keepdims=True)
    return x * lax.rsqrt(ms + RMS_EPS) * g


def _norm_matmul_kernel(x_ref, g_ref, w_ref, o_ref, h_ref):
    @pl.when(pl.program_id(1) == 0)
    def _():
        h_ref[...] = _rms(x_ref[...], g_ref[...]).astype(BF16)

    o_ref[...] = jnp.dot(h_ref[...], w_ref[...], preferred_element_type=F32).astype(o_ref.dtype)


def norm_matmul(x, g, w, out_dtype, tm=1024, tn=1024):
    T, D = x.shape
    N = w.shape[1]
    tm, tn = min(tm, T), min(tn, N)
    return pl.pallas_call(
        _norm_matmul_kernel,
        grid=(T // tm, N // tn),
        in_specs=[pl.BlockSpec((tm, D), lambda i, j: (i, 0)),
                  pl.BlockSpec((1, D), lambda i, j: (0, 0)),
                  pl.BlockSpec((D, tn), lambda i, j: (0, j))],
        out_specs=pl.BlockSpec((tm, tn), lambda i, j: (i, j)),
        out_shape=jax.ShapeDtypeStruct((T, N), out_dtype),
        scratch_shapes=[pltpu.VMEM((tm, D), BF16)],
        compiler_params=_cparams(("parallel", "arbitrary"), 56),
        name="norm_matmul",
    )(x, g.reshape(1, D), w)


def _matmul_kernel(a_ref, w_ref, o_ref, wb_ref):
    @pl.when(pl.program_id(1) == 0)
    def _():
        wb_ref[...] = w_ref[...].astype(BF16)

    o_ref[...] = jnp.dot(a_ref[...], wb_ref[...], preferred_element_type=F32).astype(o_ref.dtype)


def matmul(a, w, out_dtype, n_cols=None, tm=1024, tn=1024):
    T, K = a.shape
    N = w.shape[1] if n_cols is None else n_cols
    tm, tn = min(tm, T), min(tn, N)
    return pl.pallas_call(
        _matmul_kernel,
        grid=(N // tn, T // tm),
        in_specs=[pl.BlockSpec((tm, K), lambda j, i: (i, 0)),
                  pl.BlockSpec((K, tn), lambda j, i: (0, j))],
        out_specs=pl.BlockSpec((tm, tn), lambda j, i: (i, j)),
        out_shape=jax.ShapeDtypeStruct((T, N), out_dtype),
        scratch_shapes=[pltpu.VMEM((K, tn), BF16)],
        compiler_params=_cparams(("parallel", "arbitrary"), 48),
        name="matmul",
    )(a, w)


def _conv_kernel(gb_ref, gc_ref, u_ref, gcp_ref, up_ref, w_ref, o_ref):
    w0, w1, w2 = w_ref[0:1, :], w_ref[1:2, :], w_ref[2:3, :]
    v = gc_ref[...] * u_ref[...]
    y = w2 * v + w1 * pltpu.roll(v, 1, 0) + w0 * pltpu.roll(v, 2, 0)
    o_ref[...] = (gb_ref[...] * y).astype(o_ref.dtype)
    vp = gcp_ref[...] * up_ref[...]
    vp = jnp.where(pl.program_id(0) > 0, vp, jnp.zeros_like(vp))
    v8 = v[0:8, :]
    row = lax.broadcasted_iota(I32, v8.shape, 0)
    v1 = jnp.where(row < 1, pltpu.roll(vp, 1, 0), pltpu.roll(v8, 1, 0))
    v2 = jnp.where(row < 2, pltpu.roll(vp, 2, 0), pltpu.roll(v8, 2, 0))
    o_ref[0:8, :] = (gb_ref[0:8, :] * (w2 * v8 + w1 * v1 + w0 * v2)).astype(o_ref.dtype)


def conv_mixer(proj, conv_w, tm=512):
    T = proj.shape[0]
    CW = conv_w.shape[1]
    tm = min(tm, T)
    r8 = tm // 8
    cur = lambda c: pl.BlockSpec((tm, CW), lambda i: (i, c))
    prev = lambda c: pl.BlockSpec((8, CW), lambda i: (jnp.maximum(i * r8 - 1, 0), c))
    return pl.pallas_call(
        _conv_kernel,
        grid=(T // tm,),
        in_specs=[cur(0), cur(1), cur(2), prev(1), prev(2),
                  pl.BlockSpec((CONV_K, CW), lambda i: (0, 0))],
        out_specs=pl.BlockSpec((tm, CW), lambda i: (i, 0)),
        out_shape=jax.ShapeDtypeStruct((T, CW), BF16),
        compiler_params=_cparams(("parallel",), 40),
        name="conv_mixer",
    )(proj, proj, proj, proj, proj, conv_w)


def _cmul(a, b):
    return a[0] * b[0] - a[1] * b[1], a[0] * b[1] + a[1] * b[0]


def s5_matrices(lam_re, lam_im, log_dt, b_re, b_im, c_re, c_im):
    C = S5_CHUNK
    dt = jnp.exp(log_dt)[:, None]
    a, b = lam_re * dt, lam_im * dt
    mag = jnp.exp(a)
    lbar = (mag * jnp.cos(b), mag * jnp.sin(b))
    den = lam_re * lam_re + lam_im * lam_im
    inv_lam = (lam_re / den, -lam_im / den)
    coef = _cmul((lbar[0] - 1.0, lbar[1]), inv_lam)
    bbar = _cmul((coef[0][..., None], coef[1][..., None]), (b_re, b_im))
    j = jnp.arange(C + 1, dtype=F32)[None, :, None]
    pmag = jnp.exp(a[:, None, :] * j)
    pw = (pmag * jnp.cos(b[:, None, :] * j), pmag * jnp.sin(b[:, None, :] * j))
    return pw, bbar


def _lane_tile_blockdiag(m):
    Q, O, r, c = m.shape
    out = jnp.zeros((Q, O, r, O, c), m.dtype)
    for g in range(O):
        out = out.at[:, g, :, g, :].set(m[:, g])
    return out.reshape(Q, O * r, O * c)


def s5_lane_tile_operands(lam_re, lam_im, log_dt, b_re, b_im, c_re, c_im):
    C, H, N, O = S5_CHUNK, S5_GROUP, S5_STATE, S5_OCT
    pw, bbar = s5_matrices(lam_re, lam_im, log_dt, b_re, b_im, c_re, c_im)
    Q = lam_re.shape[0] // O
    pw_q = jnp.stack([p.reshape(Q, O, C + 1, N).transpose(0, 2, 1, 3).reshape(Q, C + 1, O * N) for p in pw],
                     axis=1)
    bb = jnp.stack([_lane_tile_blockdiag(m.transpose(0, 2, 1).reshape(Q, O, H, N)) for m in bbar], axis=1)
    cc = jnp.stack([_lane_tile_blockdiag(m.reshape(Q, O, H, N)) for m in (c_re, c_im)], axis=1)
    return pw_q, bb, cc


def _s5_kernel(u_ref, pw_ref, bb_ref, cc_ref, d_ref, o_ref, t_ref, w_ref, v_ref, uf_ref, s_ref, x_ref, c_ref):
    C = S5_CHUNK
    tb = u_ref.shape[0]
    nc = tb // C
    sw = pw_ref.shape[3]

    @pl.when((pl.program_id(0) == 0) & (pl.program_id(1) == 0))
    def _():
        t_ref[...] = jnp.zeros_like(t_ref)

    @pl.when(pl.program_id(1) == 0)
    def _():
        bb = (bb_ref[0, 0], bb_ref[0, 1])
        cc = (cc_ref[0, 0], cc_ref[0, 1])
        c_hi, c_lo = _bf16_pieces(jnp.concatenate([cc[0], -cc[1]], axis=1), 2)
        nt = (((1,), (1,)), ((), ()))
        for j in range(C):
            a = _cmul(bb, (pw_ref[0, 0, j:j + 1, :], pw_ref[0, 1, j:j + 1, :]))
            s = C - 1 - j
            w_ref[s * LANES:(s + 1) * LANES, 0:sw] = a[0].astype(BF16)
            w_ref[s * LANES:(s + 1) * LANES, sw:2 * sw] = a[1].astype(BF16)
            a_hi, a_lo = _bf16_pieces(jnp.concatenate(a, axis=1), 2)
            lag = (lax.dot_general(a_hi, c_hi, nt, preferred_element_type=F32)
                   + (lax.dot_general(a_hi, c_lo, nt, preferred_element_type=F32)
                      + lax.dot_general(a_lo, c_hi, nt, preferred_element_type=F32))).astype(BF16)
            for s0 in range(C - j):
                t_ref[s0 * LANES:(s0 + 1) * LANES, (s0 + j) * LANES:(s0 + j + 1) * LANES] = lag
            g = _cmul(cc, (pw_ref[0, 0, j + 1:j + 2, :], pw_ref[0, 1, j + 1:j + 2, :]))
            v_ref[j * LANES:(j + 1) * LANES, 0:sw] = g[0].astype(BF16)
            v_ref[j * LANES:(j + 1) * LANES, sw:2 * sw] = (-g[1]).astype(BF16)
        c_ref[...] = jnp.zeros_like(c_ref)

    for s in range(C):
        uf_ref[:, s * LANES:(s + 1) * LANES] = u_ref[pl.ds(s, nc, stride=C), :].astype(BF16)
    s_ref[...] = jnp.dot(uf_ref[...], w_ref[...], preferred_element_type=F32)
    lr, li = pw_ref[0, 0, C:C + 1, :], pw_ref[0, 1, C:C + 1, :]
    row = lax.broadcasted_iota(I32, (8, sw), 0)

    def tile_step(i, carry):
        xr, xi = carry
        r0 = pl.multiple_of(i * 8, 8)
        sr = s_ref[pl.ds(r0, 8), 0:sw]
        si = s_ref[pl.ds(r0, 8), sw:2 * sw]
        tr = jnp.zeros((8, sw), F32)
        ti = jnp.zeros((8, sw), F32)
        for r in range(8):
            tr = jnp.where(row == r, xr, tr)
            ti = jnp.where(row == r, xi, ti)
            xr, xi = lr * xr - li * xi + sr[r:r + 1, :], lr * xi + li * xr + si[r:r + 1, :]
        x_ref[pl.ds(r0, 8), 0:sw] = tr
        x_ref[pl.ds(r0, 8), sw:2 * sw] = ti
        return xr, xi

    xr, xi = lax.fori_loop(0, nc // 8, tile_step, (c_ref[0:1, 0:sw], c_ref[0:1, sw:2 * sw]))
    c_ref[0:1, 0:sw] = xr
    c_ref[0:1, sw:2 * sw] = xi
    y_state = lax.dot_general(x_ref[...].astype(BF16), v_ref[...], (((1,), (1,)), ((), ())),
                              preferred_element_type=F32)
    quarter = C // 4
    for qt in range(4):
        rows = (qt + 1) * quarter * LANES
        cols = slice(qt * quarter * LANES, (qt + 1) * quarter * LANES)
        y = jnp.dot(uf_ref[:, 0:rows], t_ref[0:rows, cols], preferred_element_type=F32) + y_state[:, cols]
        for s in range(quarter):
            st = qt * quarter + s
            ys = y[:, s * LANES:(s + 1) * LANES] + d_ref[...] * u_ref[pl.ds(st, nc, stride=C), :]
            o_ref[pl.ds(st, nc, stride=C), :] = jax.nn.gelu(ys)


def s5_mixer(proj, col0, lam_re, lam_im, log_dt, b_re, b_im, c_re, c_im, d, tb=8192):
    T = proj.shape[0]
    C, N, O = S5_CHUNK, S5_STATE, S5_OCT
    W = d.shape[0]
    Q = W // LANES
    tb = min(tb, T)
    pw_q, bb, cc = s5_lane_tile_operands(lam_re, lam_im, log_dt, b_re, b_im, c_re, c_im)
    cb0 = col0 // LANES
    nc = tb // C
    sw = O * N
    quad = lambda a: pl.BlockSpec((1,) + a.shape[1:], lambda q, t: (q, 0, 0, 0))
    return pl.pallas_call(
        _s5_kernel,
        grid=(Q, T // tb),
        in_specs=[pl.BlockSpec((tb, LANES), lambda q, t: (t, cb0 + q)),
                  quad(pw_q), quad(bb), quad(cc),
                  pl.BlockSpec((1, LANES), lambda q, t: (0, q))],
        out_specs=pl.BlockSpec((tb, LANES), lambda q, t: (t, q)),
        out_shape=jax.ShapeDtypeStruct((T, W), F32),
        scratch_shapes=[pltpu.VMEM((C * LANES, C * LANES), BF16),
                        pltpu.VMEM((C * LANES, 2 * sw), BF16),
                        pltpu.VMEM((C * LANES, 2 * sw), BF16),
                        pltpu.VMEM((nc, C * LANES), BF16),
                        pltpu.VMEM((nc, 2 * sw), F32),
                        pltpu.VMEM((nc, 2 * sw), F32),
                        pltpu.VMEM((8, 2 * sw), F32)],
        compiler_params=_cparams(("arbitrary", "arbitrary"), 56),
        name="s5_scan",
    )(proj, pw_q, bb, cc, d.reshape(1, W))


def _store_row_tiles(ref, val, lead=()):
    rows, width = val.shape
    n = width // LANES
    for j in range(n):
        ref[lead + (pl.ds(j, rows, stride=n), slice(None))] = val[:, j * LANES:(j + 1) * LANES]


def _load_row_tiles(ref, rows, n, lead=()):
    return jnp.concatenate([ref[lead + (pl.ds(j, rows, stride=n), slice(None))] for j in range(n)], axis=1)


def _row_tile_spec(tm, width, index_map):
    return pl.BlockSpec((tm * (width // LANES), LANES), index_map)


def _glu_out_kernel(x_ref, yc_ref, yg_ref, wglu_ref, bglu_ref, wout_ref, g_ref, xo_ref, ho_ref):
    yg = yg_ref[...]
    z = jnp.dot(yg.astype(BF16), wglu_ref[...], preferred_element_type=F32) + bglu_ref[...]
    ys = (yg * jax.nn.sigmoid(z)).astype(BF16)
    cw = yc_ref.shape[1]
    xn = (x_ref[...]
          + jnp.dot(yc_ref[...], wout_ref[0:cw, :], preferred_element_type=F32)
          + jnp.dot(ys, wout_ref[cw:, :], preferred_element_type=F32))
    xo_ref[...] = xn
    _store_row_tiles(ho_ref, _rms(xn, g_ref[...]))


def glu_out(x, y_conv, y_gelu, w_glu, b_glu, w_out, g_next, tm=512):
    T, D = x.shape
    CW, SW = y_conv.shape[1], y_gelu.shape[1]
    tm = min(tm, T)
    row = lambda w: pl.BlockSpec((tm, w), lambda i: (i, 0))
    full = lambda a: pl.BlockSpec(a.shape, lambda i: (0,) * a.ndim)
    bg, g2 = b_glu.reshape(1, SW), g_next.reshape(1, D)
    return pl.pallas_call(
        _glu_out_kernel,
        grid=(T // tm,),
        in_specs=[row(D), row(CW), row(SW), full(w_glu), full(bg), full(w_out), full(g2)],
        out_specs=[row(D), _row_tile_spec(tm, D, lambda i: (i, 0))],
        out_shape=[jax.ShapeDtypeStruct((T, D), F32), jax.ShapeDtypeStruct((T * D // LANES, LANES), F32)],
        compiler_params=_cparams(("parallel",), 56),
        name="glu_out",
    )(x, y_conv, y_gelu, w_glu, bg, w_out, g2)


def _proj_residual_kernel(x_ref, a_ref, w_ref, g_ref, xo_ref, ho_ref):
    xn = x_ref[...] + jnp.dot(a_ref[...], w_ref[...], preferred_element_type=F32)
    xo_ref[...] = xn
    _store_row_tiles(ho_ref, _rms(xn, g_ref[...]))


def proj_residual(x, a, w, g_next, tm=512):
    T, D = x.shape
    K = a.shape[1]
    tm = min(tm, T)
    row = lambda w_: pl.BlockSpec((tm, w_), lambda i: (i, 0))
    full = lambda arr: pl.BlockSpec(arr.shape, lambda i: (0,) * arr.ndim)
    g2 = g_next.reshape(1, D)
    return pl.pallas_call(
        _proj_residual_kernel,
        grid=(T // tm,),
        in_specs=[row(D), row(K), full(w), full(g2)],
        out_specs=[row(D), _row_tile_spec(tm, D, lambda i: (i, 0))],
        out_shape=[jax.ShapeDtypeStruct((T, D), F32), jax.ShapeDtypeStruct((T * D // LANES, LANES), F32)],
        compiler_params=_cparams(("parallel",), 56),
        name="proj_residual",
    )(x, a, w, g2)


def _router_kernel(h_ref, w_ref, b_ref, mi_ref, mf_ref, cnt_ref, run_ref):
    i = pl.program_id(0)

    @pl.when(i == 0)
    def _():
        run_ref[...] = jnp.zeros_like(run_ref)

    D = w_ref.shape[0]
    tm = h_ref.shape[0] * LANES // D
    h = _load_row_tiles(h_ref, tm, D // LANES)
    h_hi, h_lo = _bf16_pieces(h, 2)
    w_hi, w_lo = _bf16_pieces(w_ref[...], 2)
    logits = (jnp.dot(h_hi, w_hi, preferred_element_type=F32)
              + (jnp.dot(h_hi, w_lo, preferred_element_type=F32)
                 + jnp.dot(h_lo, w_hi, preferred_element_type=F32))) + b_ref[...]
    lane = lax.broadcasted_iota(I32, logits.shape, 1)
    neg = jnp.float32(-jnp.inf)
    gl = jnp.where(lane < N_GROUPS, logits, neg)
    gmax = jnp.max(gl, axis=1, keepdims=True)
    gsum = jnp.sum(jnp.where(lane < N_GROUPS, jnp.exp(gl - gmax), 0.0), axis=1, keepdims=True)
    gw = 1.0 / gsum
    gidx = jnp.min(jnp.where(gl == gmax, lane, LANES), axis=1, keepdims=True)
    lo = EXPERT_LANE0 + EXPERTS_PER_GROUP * gidx
    el = jnp.where((lane >= lo) & (lane < lo + EXPERTS_PER_GROUP), logits, neg)
    v1 = jnp.max(el, axis=1, keepdims=True)
    i1 = jnp.min(jnp.where(el == v1, lane, LANES), axis=1, keepdims=True)
    el2 = jnp.where(lane == i1, neg, el)
    v2 = jnp.max(el2, axis=1, keepdims=True)
    i2 = jnp.min(jnp.where(el2 == v2, lane, LANES), axis=1, keepdims=True)
    t = jnp.exp(v2 - v1)
    w1 = gw / (1.0 + t)
    w2 = gw * t / (1.0 + t)
    hit1 = lane == i1
    hit2 = lane == i2
    cnt = (hit1 | hit2).astype(BF16)
    r = lax.broadcasted_iota(I32, (tm, tm), 0)
    c = lax.broadcasted_iota(I32, (tm, tm), 1)
    before = (c < r).astype(BF16)
    cum = jnp.dot(before, cnt, preferred_element_type=F32) + run_ref[0:1, :]
    rank1 = jnp.sum(jnp.where(hit1, cum, 0.0), axis=1, keepdims=True).astype(I32)
    rank2 = jnp.sum(jnp.where(hit2, cum, 0.0), axis=1, keepdims=True).astype(I32)
    run = run_ref[0:1, :] + jnp.sum(cnt.astype(F32), axis=0, keepdims=True)
    run_ref[...] = jnp.broadcast_to(run, run_ref.shape)
    cnt_ref[...] = jnp.broadcast_to(run, cnt_ref.shape)
    code1 = (i1 - EXPERT_LANE0) * MOE_CODE + rank1
    code2 = (i2 - EXPERT_LANE0) * MOE_CODE + rank2
    mi_ref[...] = jnp.where(lane == 0, code1, jnp.where(lane == 1, code2, 0))
    mf_ref[...] = jnp.where(lane == 0, w1, jnp.where(lane == 1, w2, 0.0))


def router(h, w_r, b_r, tm=256):
    D = w_r.shape[0]
    T = h.shape[0] * LANES // D
    tm = min(tm, T)
    row = lambda w: pl.BlockSpec((tm, w), lambda i: (i, 0))
    full = lambda a: pl.BlockSpec(a.shape, lambda i: (0,) * a.ndim)
    return pl.pallas_call(
        _router_kernel,
        grid=(T // tm,),
        in_specs=[_row_tile_spec(tm, D, lambda i: (i, 0)), full(w_r), full(b_r)],
        out_specs=[row(LANES), row(LANES), pl.BlockSpec((8, LANES), lambda i: (0, 0))],
        out_shape=[jax.ShapeDtypeStruct((T, LANES), I32), jax.ShapeDtypeStruct((T, LANES), F32),
                   jax.ShapeDtypeStruct((8, LANES), F32)],
        scratch_shapes=[pltpu.VMEM((8, LANES), F32)],
        compiler_params=_cparams(("arbitrary",), 40),
        name="moe_router",
    )(h, w_r, b_r)


def moe_plan(meta_i, counts, n_blocks):
    B = MOE_BLOCK
    code1, code2 = meta_i[:, 0], meta_i[:, 1]
    cnt = counts[0, EXPERT_LANE0:EXPERT_LANE0 + N_EXPERTS].astype(I32)
    nblk = (cnt + B - 1) // B
    blk_end = jnp.cumsum(nblk)
    blk_off = blk_end - nblk
    b = jnp.arange(n_blocks, dtype=I32)
    total = blk_end[-1]
    owner = jnp.minimum(jnp.sum((blk_end[None, :] <= b[:, None]).astype(I32), axis=1), N_EXPERTS - 1)
    valid = b < total
    last = jnp.maximum(total - 1, 0)
    blk_expert = jnp.where(valid, owner, owner[last])
    blk_src = jnp.where(valid, b, last)
    blk_first = (valid & (b == blk_off[owner])).astype(I32)
    e = jnp.arange(N_EXPERTS, dtype=I32)
    used = nblk > 0
    slot = (jnp.cumsum(used.astype(I32)) - 1) % 2
    later = (e[None, :] > e[:, None]) & used[None, :]
    next_used = jnp.min(jnp.where(later, e[None, :], N_EXPERTS), axis=1)
    next_used = jnp.where(next_used < N_EXPERTS, next_used, -1)
    return (code1, code2, blk_off * B, blk_expert, blk_src, valid.astype(I32), blk_first,
            next_used[blk_expert], slot[blk_expert])


def _row_copy(src, dst, sem):
    return pltpu.make_async_copy(src, dst, sem)


def _sorted_row(code, off_ref):
    return off_ref[code >> MOE_CODE_BITS] + (code & (MOE_CODE - 1))


def _tile_rows(row, n):
    return pl.ds(pl.multiple_of(row * n, n), n)


def _dispatch_kernel(code1_ref, code2_ref, off_ref, hp_ref, xs_in_ref, xs_ref, sem, *, n, tm):
    del xs_in_ref
    i = pl.program_id(0)
    base = i * tm

    def start(r, carry):
        src = hp_ref.at[_tile_rows(base + r, n)]
        _row_copy(src, xs_ref.at[_tile_rows(_sorted_row(code1_ref[base + r], off_ref), n)], sem.at[i % 2]).start()
        _row_copy(src, xs_ref.at[_tile_rows(_sorted_row(code2_ref[base + r], off_ref), n)], sem.at[i % 2]).start()
        return carry

    def wait_step(slot):
        for _ in range(2):
            _row_copy(hp_ref.at[pl.ds(0, tm * n)], xs_ref.at[pl.ds(0, tm * n)], sem.at[slot]).wait()

    lax.fori_loop(0, tm, start, 0, unroll=8)

    @pl.when(i > 0)
    def _():
        wait_step((i - 1) % 2)

    @pl.when(i == pl.num_programs(0) - 1)
    def _():
        wait_step(i % 2)


def dispatch(hp, width, code1, code2, row_off, xs0, tm=256):
    n = width // LANES
    T = hp.shape[0] // n
    tm = min(tm, T)
    return pl.pallas_call(
        functools.partial(_dispatch_kernel, n=n, tm=tm),
        grid_spec=pltpu.PrefetchScalarGridSpec(
            num_scalar_prefetch=3,
            grid=(T // tm,),
            in_specs=[pl.BlockSpec(memory_space=pl.ANY), pl.BlockSpec(memory_space=pl.ANY)],
            out_specs=pl.BlockSpec(memory_space=pl.ANY),
            scratch_shapes=[pltpu.SemaphoreType.DMA((2,))]),
        out_shape=jax.ShapeDtypeStruct(xs0.shape, hp.dtype),
        input_output_aliases={4: 0},
        compiler_params=_cparams(("arbitrary",), 32),
        name="moe_dispatch",
    )(code1, code2, row_off, hp, xs0)


def _experts_kernel(be_ref, bs_ref, bv_ref, bf_ref, bn_ref, bl_ref, xs_ref, wg_ref, wu_ref, wd_ref, ys_ref,
                    wg_f, wu_f, wd_f, wg_s, wu_s, wd_s, sem):
    del bs_ref
    b = pl.program_id(0)

    def weight_copies(e, slot):
        return (pltpu.make_async_copy(wg_ref.at[e], wg_f.at[slot], sem.at[slot]),
                pltpu.make_async_copy(wu_ref.at[e], wu_f.at[slot], sem.at[slot]),
                pltpu.make_async_copy(wd_ref.at[e], wd_f.at[slot], sem.at[slot]))

    @pl.when(bv_ref[b] == 0)
    def _():
        ys_ref[...] = jnp.zeros_like(ys_ref)

    @pl.when(bv_ref[b] > 0)
    def _():
        @pl.when(bf_ref[b] == 1)
        def _():
            slot = bl_ref[b]

            @pl.when(b == 0)
            def _():
                for cp in weight_copies(be_ref[b], slot):
                    cp.start()

            for cp in weight_copies(be_ref[b], slot):
                cp.wait()
            wg_s[...] = wg_f[slot].astype(BF16)
            wu_s[...] = wu_f[slot].astype(BF16)
            wd_s[...] = wd_f[slot].astype(BF16)

            @pl.when(bn_ref[b] >= 0)
            def _():
                for cp in weight_copies(bn_ref[b], 1 - slot):
                    cp.start()

        D = wg_s.shape[0]
        a = _load_row_tiles(xs_ref, MOE_BLOCK, D // LANES).astype(BF16)
        gate = jnp.dot(a, wg_s[...], preferred_element_type=F32)
        up = jnp.dot(a, wu_s[...], preferred_element_type=F32)
        mid = (jax.nn.silu(gate) * up).astype(BF16)
        _store_row_tiles(ys_ref, jnp.dot(mid, wd_s[...], preferred_element_type=F32))


def experts(xs, blk_expert, blk_src, blk_valid, blk_first, blk_next, blk_slot, w_gate, w_up, w_down):
    B = MOE_BLOCK
    E, D, FF = w_gate.shape
    nb = xs.shape[0] * LANES // D // B
    hbm = pl.BlockSpec(memory_space=pl.ANY)
    return pl.pallas_call(
        _experts_kernel,
        grid_spec=pltpu.PrefetchScalarGridSpec(
            num_scalar_prefetch=6,
            grid=(nb,),
            in_specs=[_row_tile_spec(B, D, lambda b, be, bs, bv, bf, bn, bl: (bs[b], 0)), hbm, hbm, hbm],
            out_specs=_row_tile_spec(B, D, lambda b, be, bs, bv, bf, bn, bl: (b, 0)),
            scratch_shapes=[pltpu.VMEM((2, D, FF), F32), pltpu.VMEM((2, D, FF), F32),
                            pltpu.VMEM((2, FF, D), F32),
                            pltpu.VMEM((D, FF), BF16), pltpu.VMEM((D, FF), BF16),
                            pltpu.VMEM((FF, D), BF16), pltpu.SemaphoreType.DMA((2,))]),
        out_shape=jax.ShapeDtypeStruct(xs.shape, F32),
        compiler_params=_cparams(("arbitrary",), 56),
        name="moe_experts",
    )(blk_expert, blk_src, blk_valid, blk_first, blk_next, blk_slot, xs, w_gate, w_up, w_down)


def _combine_kernel(code1_ref, code2_ref, off_ref, x_ref, mf_ref, g_ref, ys_ref, xo_ref, ho_ref, buf, sem):
    tm, D = x_ref.shape
    n = D // LANES
    i = pl.program_id(0)
    slot = i % 2

    def fetch(step, dst_slot):
        base = step * tm

        def start(r, carry):
            dst = _tile_rows(r, n)
            _row_copy(ys_ref.at[_tile_rows(_sorted_row(code1_ref[base + r], off_ref), n)],
                      buf.at[dst_slot, 0, dst], sem.at[dst_slot]).start()
            _row_copy(ys_ref.at[_tile_rows(_sorted_row(code2_ref[base + r], off_ref), n)],
                      buf.at[dst_slot, 1, dst], sem.at[dst_slot]).start()
            return carry

        lax.fori_loop(0, tm, start, 0, unroll=8)

    @pl.when(i == 0)
    def _():
        fetch(0, 0)

    @pl.when(i + 1 < pl.num_programs(0))
    def _():
        fetch(i + 1, 1 - slot)

    for k in range(2):
        _row_copy(ys_ref.at[pl.ds(0, tm * n)], buf.at[slot, k], sem.at[slot]).wait()
    mf = mf_ref[...]
    xn = (x_ref[...] + mf[:, 0:1] * _load_row_tiles(buf, tm, n, lead=(slot, 0))
          + mf[:, 1:2] * _load_row_tiles(buf, tm, n, lead=(slot, 1)))
    xo_ref[...] = xn
    ho_ref[...] = _rms(xn, g_ref[...]).astype(ho_ref.dtype)


def combine(x, ys, meta_f, code1, code2, row_off, g_next, h_dtype, tm=256):
    T, D = x.shape
    tm = min(tm, T)
    row = lambda w: pl.BlockSpec((tm, w), lambda i, c1, c2, off: (i, 0))
    return pl.pallas_call(
        _combine_kernel,
        grid_spec=pltpu.PrefetchScalarGridSpec(
            num_scalar_prefetch=3,
            grid=(T // tm,),
            in_specs=[row(D), row(LANES), pl.BlockSpec((1, D), lambda i, c1, c2, off: (0, 0)),
                      pl.BlockSpec(memory_space=pl.ANY)],
            out_specs=[row(D), row(D)],
            scratch_shapes=[pltpu.VMEM((2, 2, tm * D // LANES, LANES), F32),
                            pltpu.SemaphoreType.DMA((2,))]),
        out_shape=[jax.ShapeDtypeStruct((T, D), F32), jax.ShapeDtypeStruct((T, D), h_dtype)],
        compiler_params=_cparams(("arbitrary",), 40),
        name="moe_combine",
    )(code1, code2, row_off, x, meta_f, g_next.reshape(1, D), ys)


def moe_row_blocks(T):
    return (2 * T) // MOE_BLOCK + N_EXPERTS


def moe_layer(x, h, xs_buf, w_group, b_group, w_expert, b_expert, layer, w_gate, w_up, w_down, g_next,
              h_dtype):
    T, D = x.shape
    assert T <= MOE_CODE, "ranks inside one expert must fit the packed (expert, rank) code"
    n_blocks = xs_buf.shape[0] * LANES // D // MOE_BLOCK
    pad = LANES - N_GROUPS - N_EXPERTS
    w_r = jnp.concatenate([w_group, w_expert.reshape(D, N_EXPERTS), jnp.zeros((D, pad), F32)], axis=1)
    b_r = jnp.concatenate([b_group, b_expert.reshape(N_EXPERTS), jnp.zeros((pad,), F32)]).reshape(1, LANES)
    meta_i, meta_f, counts = router(h, w_r, b_r)
    (code1, code2, row_off, blk_expert, blk_src, blk_valid, blk_first, blk_next,
     blk_slot) = moe_plan(meta_i, counts, n_blocks)
    xs = dispatch(h, D, code1, code2, row_off, xs_buf)
    FF = w_gate.shape[-1]
    first_row = layer * N_EXPERTS
    ys = experts(xs, blk_expert + first_row, blk_src, blk_valid, blk_first,
                 jnp.where(blk_next >= 0, blk_next + first_row, -1), blk_slot,
                 w_gate.reshape(-1, D, FF), w_up.reshape(-1, D, FF), w_down.reshape(-1, FF, D))
    xo, ho = combine(x, ys, meta_f, code1, code2, row_off, g_next, h_dtype)
    return xo, ho, xs


def _forget_kernel(h_ref, w_ref, b_ref, o_ref, run_ref):
    @pl.when(pl.program_id(0) == 0)
    def _():
        run_ref[...] = jnp.zeros_like(run_ref)

    h = h_ref[...]
    z = sum(jnp.dot(h, w, preferred_element_type=F32) for w in _bf16_pieces(w_ref[...], 2)) + b_ref[...]
    log_f = jax.nn.log_sigmoid(z)
    tm = z.shape[0]
    r = lax.broadcasted_iota(I32, (tm, tm), 0)
    c = lax.broadcasted_iota(I32, (tm, tm), 1)
    upto = (c <= r).astype(BF16)
    cum = sum(jnp.dot(upto, p, preferred_element_type=F32) for p in _bf16_pieces(log_f, 3)) + run_ref[0:1, :]
    o_ref[...] = cum
    run_ref[...] = jnp.broadcast_to(cum[tm - 1:tm, :], run_ref.shape)


def forget_cumsum(h, w_f, b_f, tm=256):
    T, D = h.shape
    tm = min(tm, T)
    return pl.pallas_call(
        _forget_kernel,
        grid=(T // tm,),
        in_specs=[pl.BlockSpec((tm, D), lambda i: (i, 0)),
                  pl.BlockSpec((D, LANES), lambda i: (0, 0)),
                  pl.BlockSpec((1, LANES), lambda i: (0, 0))],
        out_specs=pl.BlockSpec((tm, LANES), lambda i: (i, 0)),
        out_shape=jax.ShapeDtypeStruct((T, LANES), F32),
        scratch_shapes=[pltpu.VMEM((8, LANES), F32)],
        compiler_params=_cparams(("arbitrary",), 40),
        name="forget_cumsum",
    )(h, w_f, b_f)


def _flash_kernel(q_ref, k_ref, v_ref, ck_ref, cq_ref, o_ref, m_ref, acc_ref, *, scale, sub):
    qi = pl.program_id(1)
    bq = q_ref.shape[0]
    dh = FOX_HEAD_DIM
    heads = q_ref.shape[1] // dh
    m_ref[...] = jnp.full_like(m_ref, -jnp.inf)
    acc_ref[...] = jnp.zeros_like(acc_ref)
    ones = jnp.ones((bq, dh), BF16)
    n_sub = bq // sub

    def chunk(start, diagonal):
        def scores(hh):
            cols = slice(hh * dh, (hh + 1) * dh)
            k = k_ref[pl.ds(start, bq), cols]
            return lax.dot_general(q_ref[:, cols], k, (((1,), (1,)), ((), ())), preferred_element_type=F32)

        def softmax(hh, s_all):
            c0 = cq_ref[hh, :, 0:1]
            bias = (c0 - ck_ref[hh, :, pl.ds(start, bq)]) * LOG2E
            ps = []
            for r in range(n_sub):
                rows = slice(r * sub, (r + 1) * sub)
                s = s_all[rows, :] * (scale * LOG2E) + bias
                if diagonal:
                    row = lax.broadcasted_iota(I32, s.shape, 0) + r * sub
                    col = lax.broadcasted_iota(I32, s.shape, 1)
                    s = jnp.where(col <= row, s, -jnp.inf)
                m_prev = m_ref[hh, rows, :]
                m_new = jnp.maximum(m_prev, jnp.max(s, axis=1, keepdims=True))
                m_ref[hh, rows, :] = m_new
                ps.append(jnp.exp2(s - jnp.concatenate([m_new] * (bq // LANES), axis=1)).astype(BF16))
                alpha = jnp.exp2(m_prev - m_new)
                acc_ref[hh, rows, :] = jnp.concatenate([alpha] * 2, axis=1) * acc_ref[hh, rows, :]
            return jnp.concatenate(ps, axis=0)

        def update(hh, p):
            cols = slice(hh * dh, (hh + 1) * dh)
            v1 = jnp.concatenate([v_ref[pl.ds(start, bq), cols], ones], axis=1)
            acc_ref[hh] = acc_ref[hh] + jnp.dot(p, v1, preferred_element_type=F32)

        s_next = scores(0)
        for hh in range(heads):
            s_cur = s_next
            if hh + 1 < heads:
                s_next = scores(hh + 1)
            update(hh, softmax(hh, s_cur))

    def body(j, carry):
        chunk(pl.multiple_of(j * bq, bq), False)
        return carry

    lax.fori_loop(0, qi, body, 0)
    chunk(pl.multiple_of(qi * bq, bq), True)
    for hh in range(heads):
        o_ref[:, hh * dh:(hh + 1) * dh] = (acc_ref[hh, :, 0:dh] / acc_ref[hh, :, dh:2 * dh]).astype(o_ref.dtype)


def flash_attention(qkv, cum_t, bq=512, sub=32, heads=4):
    T = qkv.shape[0]
    H, dh = FOX_HEADS, FOX_HEAD_DIM
    bq = min(bq, T)
    kern = functools.partial(_flash_kernel, scale=dh ** -0.5, sub=min(sub, bq))
    hw = heads * dh
    nhb = H // heads
    return pl.pallas_call(
        kern,
        grid=(nhb, T // bq),
        in_specs=[pl.BlockSpec((bq, hw), lambda h, i: (i, h)),
                  pl.BlockSpec((T, hw), lambda h, i: (0, nhb + h)),
                  pl.BlockSpec((T, hw), lambda h, i: (0, 2 * nhb + h)),
                  pl.BlockSpec((heads, 1, T), lambda h, i: (h, 0, 0)),
                  pl.BlockSpec((heads, 1, bq), lambda h, i: (h, 0, i))],
        out_specs=pl.BlockSpec((bq, hw), lambda h, i: (i, h)),
        out_shape=jax.ShapeDtypeStruct((T, H * dh), BF16),
        scratch_shapes=[pltpu.VMEM((heads, bq, LANES), F32), pltpu.VMEM((heads, bq, 2 * dh), F32)],
        compiler_params=_cparams(("parallel", "arbitrary"), 48),
        name="fox_attention",
    )(qkv, qkv, qkv, cum_t, cum_t)


def kernel(x, ab_norm, ab_w_in, ab_conv_w, s5_lambda_re, s5_lambda_im, s5_log_dt, s5_b_re, s5_b_im,
           s5_c_re, s5_c_im, s5_d, s5_w_glu, s5_b_glu, ab_w_out, c_norm, c_w_in, c_b_forget, c_w_out,
           ffn_norm, router_w_group, router_b_group, router_w_expert, router_b_expert,
           moe_w_gate, moe_w_up, moe_w_down, final_norm):
    bsz, L, D = x.shape
    depth = ffn_norm.shape[0]
    cw = ab_conv_w.shape[-1]
    xt = x.reshape(bsz * L, D)
    h = None
    xs_buf = jnp.zeros((moe_row_blocks(bsz * L) * MOE_BLOCK * D // LANES, LANES), F32)
    for i in range(depth):
        j = i // 2
        if i % 2 == 0:
            g_in = ab_norm[j]
            if h is not None:
                proj = matmul(h, ab_w_in[j], F32)
            else:
                proj = norm_matmul(xt, g_in, ab_w_in[j].astype(BF16), F32)
            y_conv = conv_mixer(proj, ab_conv_w[j])
            y_gelu = s5_mixer(proj, 3 * cw, s5_lambda_re[j], s5_lambda_im[j], s5_log_dt[j],
                              s5_b_re[j], s5_b_im[j], s5_c_re[j], s5_c_im[j], s5_d[j])
            xt, hf = glu_out(xt, y_conv, y_gelu, s5_w_glu[j].astype(BF16), s5_b_glu[j],
                             ab_w_out[j].astype(BF16), ffn_norm[i])
        else:
            assert h is not None, "an attention layer always follows a MoE combine that emits its norm"
            hd = FOX_HEADS * FOX_HEAD_DIM
            qkv = matmul(h, c_w_in[j], BF16, n_cols=3 * hd)
            w_f = jnp.pad(c_w_in[j][:, 3 * hd:], ((0, 0), (0, LANES - FOX_HEADS)))
            b_f = jnp.pad(c_b_forget[j], (0, LANES - FOX_HEADS)).reshape(1, LANES)
            cum = forget_cumsum(h, w_f, b_f)
            cum_t = cum[:, :FOX_HEADS].T.reshape(FOX_HEADS, 1, bsz * L)
            att = flash_attention(qkv, cum_t)
            xt, hf = proj_residual(xt, att, c_w_out[j].astype(BF16), ffn_norm[i])
        last = i == depth - 1
        if last:
            g_next = final_norm
        elif (i + 1) % 2 == 0:
            g_next = ab_norm[(i + 1) // 2]
        else:
            g_next = c_norm[(i + 1) // 2]
        xt, h, xs_buf = moe_layer(xt, hf, xs_buf, router_w_group[i], router_b_group[i], router_w_expert[i],
                                  router_b_expert[i], i, moe_w_gate, moe_w_up, moe_w_down,
                                  g_next, F32 if last else BF16)
    return h.reshape(bsz, L, D)
```

```python
import functools
import math

import jax
import jax.numpy as jnp
from jax import lax
from jax.experimental import pallas as pl
from jax.experimental.pallas import tpu as pltpu

F32 = jnp.float32
BF16 = jnp.bfloat16
I32 = jnp.int32
U32 = jnp.uint32

RMS_EPS = 1e-6
LANES = 128
LOG2E = math.log2(math.e)
MIB = 1024 * 1024

CONV_K = 3
S5_GROUP = 16
S5_STATE = 64
S5_CHUNK = 16
S5_OCT = LANES // S5_GROUP
FOX_HEADS = 16
FOX_HEAD_DIM = 128
N_GROUPS = 4
EXPERTS_PER_GROUP = 8
N_EXPERTS = N_GROUPS * EXPERTS_PER_GROUP
EXPERT_LANE0 = N_GROUPS
MOE_BLOCK = 256
MOE_CODE_BITS = 16
MOE_CODE = 1 << MOE_CODE_BITS


def _cparams(sem, vmem_mib):
    return pltpu.CompilerParams(dimension_semantics=sem, vmem_limit_bytes=vmem_mib * MIB)


def _bf16_pieces(x, n):
    pieces = []
    for _ in range(n):
        p = x.astype(BF16)
        pieces.append(p)
        x = x - p.astype(F32)
    return pieces


def _rms(x, g):
    ms = jnp.mean(x * x, axis=-1, keepdims=True)
    return x * lax.rsqrt(ms + RMS_EPS) * g


def _norm_matmul_kernel(x_ref, g_ref, w_ref, o_ref, h_ref):
    @pl.when(pl.program_id(1) == 0)
    def _():
        h_ref[...] = _rms(x_ref[...], g_ref[...]).astype(BF16)

    o_ref[...] = jnp.dot(h_ref[...], w_ref[...], preferred_element_type=F32).astype(o_ref.dtype)


def norm_matmul(x, g, w, out_dtype, tm=1024, tn=1024):
    T, D = x.shape
    N = w.shape[1]
    tm, tn = min(tm, T), min(tn, N)
    return pl.pallas_call(
        _norm_matmul_kernel,
        grid=(T // tm, N // tn),
        in_specs=[pl.BlockSpec((tm, D), lambda i, j: (i, 0)),
                  pl.BlockSpec((1, D), lambda i, j: (0, 0)),
                  pl.BlockSpec((D, tn), lambda i, j: (0, j))],
        out_specs=pl.BlockSpec((tm, tn), lambda i, j: (i, j)),
        out_shape=jax.ShapeDtypeStruct((T, N), out_dtype),
        scratch_shapes=[pltpu.VMEM((tm, D), BF16)],
        compiler_params=_cparams(("parallel", "arbitrary"), 56),
        name="norm_matmul",
    )(x, g.reshape(1, D), w)


def _matmul_kernel(a_ref, w_ref, o_ref, wb_ref):
    @pl.when(pl.program_id(1) == 0)
    def _():
        wb_ref[...] = w_ref[...].astype(BF16)

    o_ref[...] = jnp.dot(a_ref[...], wb_ref[...], preferred_element_type=F32).astype(o_ref.dtype)


def matmul(a, w, out_dtype, n_cols=None, tm=1024, tn=1024):
    T, K = a.shape
    N = w.shape[1] if n_cols is None else n_cols
    tm, tn = min(tm, T), min(tn, N)
    return pl.pallas_call(
        _matmul_kernel,
        grid=(N // tn, T // tm),
        in_specs=[pl.BlockSpec((tm, K), lambda j, i: (i, 0)),
                  pl.BlockSpec((K, tn), lambda j, i: (0, j))],
        out_specs=pl.BlockSpec((tm, tn), lambda j, i: (i, j)),
        out_shape=jax.ShapeDtypeStruct((T, N), out_dtype),
        scratch_shapes=[pltpu.VMEM((K, tn), BF16)],
        compiler_params=_cparams(("parallel", "arbitrary"), 48),
        name="matmul",
    )(a, w)


def _conv_kernel(gb_ref, gc_ref, u_ref, gcp_ref, up_ref, w_ref, o_ref):
    w0, w1, w2 = w_ref[0:1, :], w_ref[1:2, :], w_ref[2:3, :]
    v = gc_ref[...] * u_ref[...]
    y = w2 * v + w1 * pltpu.roll(v, 1, 0) + w0 * pltpu.roll(v, 2, 0)
    o_ref[...] = (gb_ref[...] * y).astype(o_ref.dtype)
    vp = gcp_ref[...] * up_ref[...]
    vp = jnp.where(pl.program_id(0) > 0, vp, jnp.zeros_like(vp))
    v8 = v[0:8, :]
    row = lax.broadcasted_iota(I32, v8.shape, 0)
    v1 = jnp.where(row < 1, pltpu.roll(vp, 1, 0), pltpu.roll(v8, 1, 0))
    v2 = jnp.where(row < 2, pltpu.roll(vp, 2, 0), pltpu.roll(v8, 2, 0))
    o_ref[0:8, :] = (gb_ref[0:8, :] * (w2 * v8 + w1 * v1 + w0 * v2)).astype(o_ref.dtype)


def conv_mixer(proj, conv_w, tm=512):
    T = proj.shape[0]
    CW = conv_w.shape[1]
    tm = min(tm, T)
    r8 = tm // 8
    cur = lambda c: pl.BlockSpec((tm, CW), lambda i: (i, c))
    prev = lambda c: pl.BlockSpec((8, CW), lambda i: (jnp.maximum(i * r8 - 1, 0), c))
    return pl.pallas_call(
        _conv_kernel,
        grid=(T // tm,),
        in_specs=[cur(0), cur(1), cur(2), prev(1), prev(2),
                  pl.BlockSpec((CONV_K, CW), lambda i: (0, 0))],
        out_specs=pl.BlockSpec((tm, CW), lambda i: (i, 0)),
        out_shape=jax.ShapeDtypeStruct((T, CW), BF16),
        compiler_params=_cparams(("parallel",), 40),
        name="conv_mixer",
    )(proj, proj, proj, proj, proj, conv_w)


def _cmul(a, b):
    return a[0] * b[0] - a[1] * b[1], a[0] * b[1] + a[1] * b[0]


def s5_matrices(lam_re, lam_im, log_dt, b_re, b_im, c_re, c_im):
    C = S5_CHUNK
    dt = jnp.exp(log_dt)[:, None]
    a, b = lam_re * dt, lam_im * dt
    mag = jnp.exp(a)
    lbar = (mag * jnp.cos(b), mag * jnp.sin(b))
    den = lam_re * lam_re + lam_im * lam_im
    inv_lam = (lam_re / den, -lam_im / den)
    coef = _cmul((lbar[0] - 1.0, lbar[1]), inv_lam)
    bbar = _cmul((coef[0][..., None], coef[1][..., None]), (b_re, b_im))
    j = jnp.arange(C + 1, dtype=F32)[None, :, None]
    pmag = jnp.exp(a[:, None, :] * j)
    pw = (pmag * jnp.cos(b[:, None, :] * j), pmag * jnp.sin(b[:, None, :] * j))
    return pw, bbar


def _lane_tile_blockdiag(m):
    Q, O, r, c = m.shape
    out = jnp.zeros((Q, O, r, O, c), m.dtype)
    for g in range(O):
        out = out.at[:, g, :, g, :].set(m[:, g])
    return out.reshape(Q, O * r, O * c)


def s5_lane_tile_operands(lam_re, lam_im, log_dt, b_re, b_im, c_re, c_im):
    C, H, N, O = S5_CHUNK, S5_GROUP, S5_STATE, S5_OCT
    pw, bbar = s5_matrices(lam_re, lam_im, log_dt, b_re, b_im, c_re, c_im)
    Q = lam_re.shape[0] // O
    pw_q = jnp.stack([p.reshape(Q, O, C + 1, N).transpose(0, 2, 1, 3).reshape(Q, C + 1, O * N) for p in pw],
                     axis=1)
    bb = jnp.stack([_lane_tile_blockdiag(m.transpose(0, 2, 1).reshape(Q, O, H, N)) for m in bbar], axis=1)
    cc = jnp.stack([_lane_tile_blockdiag(m.reshape(Q, O, H, N)) for m in (c_re, c_im)], axis=1)
    return pw_q, bb, cc


def _s5_kernel(u_ref, pw_ref, bb_ref, cc_ref, d_ref, o_ref, t_ref, w_ref, v_ref, uf_ref, s_ref, x_ref, c_ref):
    C = S5_CHUNK
    tb = u_ref.shape[0]
    nc = tb // C
    sw = pw_ref.shape[3]

    @pl.when((pl.program_id(0) == 0) & (pl.program_id(1) == 0))
    def _():
        t_ref[...] = jnp.zeros_like(t_ref)

    @pl.when(pl.program_id(1) == 0)
    def _():
        bb = (bb_ref[0, 0], bb_ref[0, 1])
        cc = (cc_ref[0, 0], cc_ref[0, 1])
        c_hi, c_lo = _bf16_pieces(jnp.concatenate([cc[0], -cc[1]], axis=1), 2)
        nt = (((1,), (1,)), ((), ()))
        for j in range(C):
            a = _cmul(bb, (pw_ref[0, 0, j:j + 1, :], pw_ref[0, 1, j:j + 1, :]))
            s = C - 1 - j
            w_ref[s * LANES:(s + 1) * LANES, 0:sw] = a[0].astype(BF16)
            w_ref[s * LANES:(s + 1) * LANES, sw:2 * sw] = a[1].astype(BF16)
            a_hi, a_lo = _bf16_pieces(jnp.concatenate(a, axis=1), 2)
            lag = (lax.dot_general(a_hi, c_hi, nt, preferred_element_type=F32)
                   + (lax.dot_general(a_hi, c_lo, nt, preferred_element_type=F32)
                      + lax.dot_general(a_lo, c_hi, nt, preferred_element_type=F32))).astype(BF16)
            for s0 in range(C - j):
                t_ref[s0 * LANES:(s0 + 1) * LANES, (s0 + j) * LANES:(s0 + j + 1) * LANES] = lag
            g = _cmul(cc, (pw_ref[0, 0, j + 1:j + 2, :], pw_ref[0, 1, j + 1:j + 2, :]))
            v_ref[j * LANES:(j + 1) * LANES, 0:sw] = g[0].astype(BF16)
            v_ref[j * LANES:(j + 1) * LANES, sw:2 * sw] = (-g[1]).astype(BF16)
        c_ref[...] = jnp.zeros_like(c_ref)

    for s in range(C):
        uf_ref[:, s * LANES:(s + 1) * LANES] = u_ref[pl.ds(s, nc, stride=C), :].astype(BF16)
    s_ref[...] = jnp.dot(uf_ref[...], w_ref[...], preferred_element_type=F32)
    lr, li = pw_ref[0, 0, C:C + 1, :], pw_ref[0, 1, C:C + 1, :]
    row = lax.broadcasted_iota(I32, (8, sw), 0)

    def tile_step(i, carry):
        xr, xi = carry
        r0 = pl.multiple_of(i * 8, 8)
        sr = s_ref[pl.ds(r0, 8), 0:sw]
        si = s_ref[pl.ds(r0, 8), sw:2 * sw]
        tr = jnp.zeros((8, sw), F32)
        ti = jnp.zeros((8, sw), F32)
        for r in range(8):
            tr = jnp.where(row == r, xr, tr)
            ti = jnp.where(row == r, xi, ti)
            xr, xi = lr * xr - li * xi + sr[r:r + 1, :], lr * xi + li * xr + si[r:r + 1, :]
        x_ref[pl.ds(r0, 8), 0:sw] = tr
        x_ref[pl.ds(r0, 8), sw:2 * sw] = ti
        return xr, xi

    xr, xi = lax.fori_loop(0, nc // 8, tile_step, (c_ref[0:1, 0:sw], c_ref[0:1, sw:2 * sw]))
    c_ref[0:1, 0:sw] = xr
    c_ref[0:1, sw:2 * sw] = xi
    y_state = lax.dot_general(x_ref[...].astype(BF16), v_ref[...], (((1,), (1,)), ((), ())),
                              preferred_element_type=F32)
    quarter = C // 4
    for qt in range(4):
        rows = (qt + 1) * quarter * LANES
        cols = slice(qt * quarter * LANES, (qt + 1) * quarter * LANES)
        y = jnp.dot(uf_ref[:, 0:rows], t_ref[0:rows, cols], preferred_element_type=F32) + y_state[:, cols]
        for s in range(quarter):
            st = qt * quarter + s
            ys = y[:, s * LANES:(s + 1) * LANES] + d_ref[...] * u_ref[pl.ds(st, nc, stride=C), :]
            o_ref[pl.ds(st, nc, stride=C), :] = jax.nn.gelu(ys)


def s5_mixer(proj, col0, lam_re, lam_im, log_dt, b_re, b_im, c_re, c_im, d, tb=8192):
    T = proj.shape[0]
    C, N, O = S5_CHUNK, S5_STATE, S5_OCT
    W = d.shape[0]
    Q = W // LANES
    tb = min(tb, T)
    pw_q, bb, cc = s5_lane_tile_operands(lam_re, lam_im, log_dt, b_re, b_im, c_re, c_im)
    cb0 = col0 // LANES
    nc = tb // C
    sw = O * N
    quad = lambda a: pl.BlockSpec((1,) + a.shape[1:], lambda q, t: (q, 0, 0, 0))
    return pl.pallas_call(
        _s5_kernel,
        grid=(Q, T // tb),
        in_specs=[pl.BlockSpec((tb, LANES), lambda q, t: (t, cb0 + q)),
                  quad(pw_q), quad(bb), quad(cc),
                  pl.BlockSpec((1, LANES), lambda q, t: (0, q))],
        out_specs=pl.BlockSpec((tb, LANES), lambda q, t: (t, q)),
        out_shape=jax.ShapeDtypeStruct((T, W), F32),
        scratch_shapes=[pltpu.VMEM((C * LANES, C * LANES), BF16),
                        pltpu.VMEM((C * LANES, 2 * sw), BF16),
                        pltpu.VMEM((C * LANES, 2 * sw), BF16),
                        pltpu.VMEM((nc, C * LANES), BF16),
                        pltpu.VMEM((nc, 2 * sw), F32),
                        pltpu.VMEM((nc, 2 * sw), F32),
                        pltpu.VMEM((8, 2 * sw), F32)],
        compiler_params=_cparams(("arbitrary", "arbitrary"), 56),
        name="s5_scan",
    )(proj, pw_q, bb, cc, d.reshape(1, W))


def _store_row_tiles(ref, val, lead=()):
    rows, width = val.shape
    n = width // LANES
    for j in range(n):
        ref[lead + (pl.ds(j, rows, stride=n), slice(None))] = val[:, j * LANES:(j + 1) * LANES]


def _load_row_tiles(ref, rows, n, lead=()):
    return jnp.concatenate([ref[lead + (pl.ds(j, rows, stride=n), slice(None))] for j in range(n)], axis=1)


def _row_tile_spec(tm, width, index_map):
    return pl.BlockSpec((tm * (width // LANES), LANES), index_map)


def _glu_out_kernel(x_ref, yc_ref, yg_ref, wglu_ref, bglu_ref, wout_ref, g_ref, xo_ref, ho_ref):
    yg = yg_ref[...]
    z = jnp.dot(yg.astype(BF16), wglu_ref[...], preferred_element_type=F32) + bglu_ref[...]
    ys = (yg * jax.nn.sigmoid(z)).astype(BF16)
    cw = yc_ref.shape[1]
    xn = (x_ref[...]
          + jnp.dot(yc_ref[...], wout_ref[0:cw, :], preferred_element_type=F32)
          + jnp.dot(ys, wout_ref[cw:, :], preferred_element_type=F32))
    xo_ref[...] = xn
    _store_row_tiles(ho_ref, _rms(xn, g_ref[...]))


def glu_out(x, y_conv, y_gelu, w_glu, b_glu, w_out, g_next, tm=512):
    T, D = x.shape
    CW, SW = y_conv.shape[1], y_gelu.shape[1]
    tm = min(tm, T)
    row = lambda w: pl.BlockSpec((tm, w), lambda i: (i, 0))
    full = lambda a: pl.BlockSpec(a.shape, lambda i: (0,) * a.ndim)
    bg, g2 = b_glu.reshape(1, SW), g_next.reshape(1, D)
    return pl.pallas_call(
        _glu_out_kernel,
        grid=(T // tm,),
        in_specs=[row(D), row(CW), row(SW), full(w_glu), full(bg), full(w_out), full(g2)],
        out_specs=[row(D), _row_tile_spec(tm, D, lambda i: (i, 0))],
        out_shape=[jax.ShapeDtypeStruct((T, D), F32), jax.ShapeDtypeStruct((T * D // LANES, LANES), F32)],
        compiler_params=_cparams(("parallel",), 56),
        name="glu_out",
    )(x, y_conv, y_gelu, w_glu, bg, w_out, g2)


def _proj_residual_kernel(x_ref, a_ref, w_ref, g_ref, xo_ref, ho_ref):
    xn = x_ref[...] + jnp.dot(a_ref[...], w_ref[...], preferred_element_type=F32)
    xo_ref[...] = xn
    _store_row_tiles(ho_ref, _rms(xn, g_ref[...]))


def proj_residual(x, a, w, g_next, tm=512):
    T, D = x.shape
    K = a.shape[1]
    tm = min(tm, T)
    row = lambda w_: pl.BlockSpec((tm, w_), lambda i: (i, 0))
    full = lambda arr: pl.BlockSpec(arr.shape, lambda i: (0,) * arr.ndim)
    g2 = g_next.reshape(1, D)
    return pl.pallas_call(
        _proj_residual_kernel,
        grid=(T // tm,),
        in_specs=[row(D), row(K), full(w), full(g2)],
        out_specs=[row(D), _row_tile_spec(tm, D, lambda i: (i, 0))],
        out_shape=[jax.ShapeDtypeStruct((T, D), F32), jax.ShapeDtypeStruct((T * D // LANES, LANES), F32)],
        compiler_params=_cparams(("parallel",), 56),
        name="proj_residual",
    )(x, a, w, g2)


def _router_kernel(h_ref, w_ref, b_ref, mi_ref, mf_ref, cnt_ref, run_ref):
    i = pl.program_id(0)

    @pl.when(i == 0)
    def _():
        run_ref[...] = jnp.zeros_like(run_ref)

    D = w_ref.shape[0]
    tm = h_ref.shape[0] * LANES // D
    h = _load_row_tiles(h_ref, tm, D // LANES)
    h_hi, h_lo = _bf16_pieces(h, 2)
    w_hi, w_lo = _bf16_pieces(w_ref[...], 2)
    logits = (jnp.dot(h_hi, w_hi, preferred_element_type=F32)
              + (jnp.dot(h_hi, w_lo, preferred_element_type=F32)
                 + jnp.dot(h_lo, w_hi, preferred_element_type=F32))) + b_ref[...]
    lane = lax.broadcasted_iota(I32, logits.shape, 1)
    neg = jnp.float32(-jnp.inf)
    gl = jnp.where(lane < N_GROUPS, logits, neg)
    gmax = jnp.max(gl, axis=1, keepdims=True)
    gsum = jnp.sum(jnp.where(lane < N_GROUPS, jnp.exp(gl - gmax), 0.0), axis=1, keepdims=True)
    gw = 1.0 / gsum
    gidx = jnp.min(jnp.where(gl == gmax, lane, LANES), axis=1, keepdims=True)
    lo = EXPERT_LANE0 + EXPERTS_PER_GROUP * gidx
    el = jnp.where((lane >= lo) & (lane < lo + EXPERTS_PER_GROUP), logits, neg)
    v1 = jnp.max(el, axis=1, keepdims=True)
    i1 = jnp.min(jnp.where(el == v1, lane, LANES), axis=1, keepdims=True)
    el2 = jnp.where(lane == i1, neg, el)
    v2 = jnp.max(el2, axis=1, keepdims=True)
    i2 = jnp.min(jnp.where(el2 == v2, lane, LANES), axis=1, keepdims=True)
    t = jnp.exp(v2 - v1)
    w1 = gw / (1.0 + t)
    w2 = gw * t / (1.0 + t)
    hit1 = lane == i1
    hit2 = lane == i2
    cnt = (hit1 | hit2).astype(BF16)
    r = lax.broadcasted_iota(I32, (tm, tm), 0)
    c = lax.broadcasted_iota(I32, (tm, tm), 1)
    before = (c < r).astype(BF16)
    cum = jnp.dot(before, cnt, preferred_element_type=F32) + run_ref[0:1, :]
    rank1 = jnp.sum(jnp.where(hit1, cum, 0.0), axis=1, keepdims=True).astype(I32)
    rank2 = jnp.sum(jnp.where(hit2, cum, 0.0), axis=1, keepdims=True).astype(I32)
    run = run_ref[0:1, :] + jnp.sum(cnt.astype(F32), axis=0, keepdims=True)
    run_ref[...] = jnp.broadcast_to(run, run_ref.shape)
    cnt_ref[...] = jnp.broadcast_to(run, cnt_ref.shape)
    code1 = (i1 - EXPERT_LANE0) * MOE_CODE + rank1
    code2 = (i2 - EXPERT_LANE0) * MOE_CODE + rank2
    mi_ref[...] = jnp.where(lane == 0, code1, jnp.where(lane == 1, code2, 0))
    mf_ref[...] = jnp.where(lane == 0, w1, jnp.where(lane == 1, w2, 0.0))


def router(h, w_r, b_r, tm=256):
    D = w_r.shape[0]
    T = h.shape[0] * LANES // D
    tm = min(tm, T)
    row = lambda w: pl.BlockSpec((tm, w), lambda i: (i, 0))
    full = lambda a: pl.BlockSpec(a.shape, lambda i: (0,) * a.ndim)
    return pl.pallas_call(
        _router_kernel,
        grid=(T // tm,),
        in_specs=[_row_tile_spec(tm, D, lambda i: (i, 0)), full(w_r), full(b_r)],
        out_specs=[row(LANES), row(LANES), pl.BlockSpec((8, LANES), lambda i: (0, 0))],
        out_shape=[jax.ShapeDtypeStruct((T, LANES), I32), jax.ShapeDtypeStruct((T, LANES), F32),
                   jax.ShapeDtypeStruct((8, LANES), F32)],
        scratch_shapes=[pltpu.VMEM((8, LANES), F32)],
        compiler_params=_cparams(("arbitrary",), 40),
        name="moe_router",
    )(h, w_r, b_r)


def moe_plan(meta_i, counts, n_blocks):
    B = MOE_BLOCK
    code1, code2 = meta_i[:, 0], meta_i[:, 1]
    cnt = counts[0, EXPERT_LANE0:EXPERT_LANE0 + N_EXPERTS].astype(I32)
    nblk = (cnt + B - 1) // B
    blk_end = jnp.cumsum(nblk)
    blk_off = blk_end - nblk
    b = jnp.arange(n_blocks, dtype=I32)
    total = blk_end[-1]
    owner = jnp.minimum(jnp.sum((blk_end[None, :] <= b[:, None]).astype(I32), axis=1), N_EXPERTS - 1)
    valid = b < total
    last = jnp.maximum(total - 1, 0)
    blk_expert = jnp.where(valid, owner, owner[last])
    blk_src = jnp.where(valid, b, last)
    blk_first = (valid & (b == blk_off[owner])).astype(I32)
    e = jnp.arange(N_EXPERTS, dtype=I32)
    used = nblk > 0
    slot = (jnp.cumsum(used.astype(I32)) - 1) % 2
    later = (e[None, :] > e[:, None]) & used[None, :]
    next_used = jnp.min(jnp.where(later, e[None, :], N_EXPERTS), axis=1)
    next_used = jnp.where(next_used < N_EXPERTS, next_used, -1)
    return (code1, code2, blk_off * B, blk_expert, blk_src, valid.astype(I32), blk_first,
            next_used[blk_expert], slot[blk_expert])


def _row_copy(src, dst, sem):
    return pltpu.make_async_copy(src, dst, sem)


def _sorted_row(code, off_ref):
    return off_ref[code >> MOE_CODE_BITS] + (code & (MOE_CODE - 1))


def _tile_rows(row, n):
    return pl.ds(pl.multiple_of(row * n, n), n)


def _dispatch_kernel(code1_ref, code2_ref, off_ref, hp_ref, xs_in_ref, xs_ref, sem, *, n):
    del xs_in_ref
    tm = hp_ref.shape[0] // n
    base = pl.program_id(0) * tm

    def copies(r):
        src = hp_ref.at[_tile_rows(r, n)]
        return (_row_copy(src, xs_ref.at[_tile_rows(_sorted_row(code1_ref[base + r], off_ref), n)], sem),
                _row_copy(src, xs_ref.at[_tile_rows(_sorted_row(code2_ref[base + r], off_ref), n)], sem))

    def start(r, carry):
        for cp in copies(r):
            cp.start()
        return carry

    lax.fori_loop(0, tm, start, 0, unroll=8)
    for _ in range(2):
        _row_copy(hp_ref, xs_ref.at[pl.ds(0, tm * n)], sem).wait()


def dispatch(hp, width, code1, code2, row_off, xs0, tm=256):
    n = width // LANES
    T = hp.shape[0] // n
    tm = min(tm, T)
    return pl.pallas_call(
        functools.partial(_dispatch_kernel, n=n),
        grid_spec=pltpu.PrefetchScalarGridSpec(
            num_scalar_prefetch=3,
            grid=(T // tm,),
            in_specs=[_row_tile_spec(tm, width, lambda i, c1, c2, off: (i, 0)),
                      pl.BlockSpec(memory_space=pl.ANY)],
            out_specs=pl.BlockSpec(memory_space=pl.ANY),
            scratch_shapes=[pltpu.SemaphoreType.DMA(())]),
        out_shape=jax.ShapeDtypeStruct(xs0.shape, hp.dtype),
        input_output_aliases={4: 0},
        compiler_params=_cparams(("arbitrary",), 32),
        name="moe_dispatch",
    )(code1, code2, row_off, hp, xs0)


def _experts_kernel(be_ref, bs_ref, bv_ref, bf_ref, bn_ref, bl_ref, xs_ref, wg_ref, wu_ref, wd_ref, ys_ref,
                    wg_f, wu_f, wd_f, wg_s, wu_s, wd_s, sem):
    del bs_ref
    b = pl.program_id(0)

    def weight_copies(e, slot):
        return (pltpu.make_async_copy(wg_ref.at[e], wg_f.at[slot], sem.at[slot]),
                pltpu.make_async_copy(wu_ref.at[e], wu_f.at[slot], sem.at[slot]),
                pltpu.make_async_copy(wd_ref.at[e], wd_f.at[slot], sem.at[slot]))

    @pl.when(bv_ref[b] == 0)
    def _():
        ys_ref[...] = jnp.zeros_like(ys_ref)

    @pl.when(bv_ref[b] > 0)
    def _():
        @pl.when(bf_ref[b] == 1)
        def _():
            slot = bl_ref[b]

            @pl.when(b == 0)
            def _():
                for cp in weight_copies(be_ref[b], slot):
                    cp.start()

            for cp in weight_copies(be_ref[b], slot):
                cp.wait()
            wg_s[...] = wg_f[slot].astype(BF16)
            wu_s[...] = wu_f[slot].astype(BF16)
            wd_s[...] = wd_f[slot].astype(BF16)

            @pl.when(bn_ref[b] >= 0)
            def _():
                for cp in weight_copies(bn_ref[b], 1 - slot):
                    cp.start()

        D = wg_s.shape[0]
        a = _load_row_tiles(xs_ref, MOE_BLOCK, D // LANES).astype(BF16)
        gate = jnp.dot(a, wg_s[...], preferred_element_type=F32)
        up = jnp.dot(a, wu_s[...], preferred_element_type=F32)
        mid = (jax.nn.silu(gate) * up).astype(BF16)
        _store_row_tiles(ys_ref, jnp.dot(mid, wd_s[...], preferred_element_type=F32))


def experts(xs, blk_expert, blk_src, blk_valid, blk_first, blk_next, blk_slot, w_gate, w_up, w_down):
    B = MOE_BLOCK
    E, D, FF = w_gate.shape
    nb = xs.shape[0] * LANES // D // B
    hbm = pl.BlockSpec(memory_space=pl.ANY)
    return pl.pallas_call(
        _experts_kernel,
        grid_spec=pltpu.PrefetchScalarGridSpec(
            num_scalar_prefetch=6,
            grid=(nb,),
            in_specs=[_row_tile_spec(B, D, lambda b, be, bs, bv, bf, bn, bl: (bs[b], 0)), hbm, hbm, hbm],
            out_specs=_row_tile_spec(B, D, lambda b, be, bs, bv, bf, bn, bl: (b, 0)),
            scratch_shapes=[pltpu.VMEM((2, D, FF), F32), pltpu.VMEM((2, D, FF), F32),
                            pltpu.VMEM((2, FF, D), F32),
                            pltpu.VMEM((D, FF), BF16), pltpu.VMEM((D, FF), BF16),
                            pltpu.VMEM((FF, D), BF16), pltpu.SemaphoreType.DMA((2,))]),
        out_shape=jax.ShapeDtypeStruct(xs.shape, F32),
        compiler_params=_cparams(("arbitrary",), 56),
        name="moe_experts",
    )(blk_expert, blk_src, blk_valid, blk_first, blk_next, blk_slot, xs, w_gate, w_up, w_down)


def _combine_kernel(code1_ref, code2_ref, off_ref, x_ref, mf_ref, g_ref, ys_ref, xo_ref, ho_ref, buf, sem):
    tm, D = x_ref.shape
    n = D // LANES
    i = pl.program_id(0)
    slot = i % 2

    def fetch(step, dst_slot):
        base = step * tm

        def start(r, carry):
            dst = _tile_rows(r, n)
            _row_copy(ys_ref.at[_tile_rows(_sorted_row(code1_ref[base + r], off_ref), n)],
                      buf.at[dst_slot, 0, dst], sem.at[dst_slot]).start()
            _row_copy(ys_ref.at[_tile_rows(_sorted_row(code2_ref[base + r], off_ref), n)],
                      buf.at[dst_slot, 1, dst], sem.at[dst_slot]).start()
            return carry

        lax.fori_loop(0, tm, start, 0, unroll=8)

    @pl.when(i == 0)
    def _():
        fetch(0, 0)

    @pl.when(i + 1 < pl.num_programs(0))
    def _():
        fetch(i + 1, 1 - slot)

    for k in range(2):
        _row_copy(ys_ref.at[pl.ds(0, tm * n)], buf.at[slot, k], sem.at[slot]).wait()
    mf = mf_ref[...]
    xn = (x_ref[...] + mf[:, 0:1] * _load_row_tiles(buf, tm, n, lead=(slot, 0))
          + mf[:, 1:2] * _load_row_tiles(buf, tm, n, lead=(slot, 1)))
    xo_ref[...] = xn
    ho_ref[...] = _rms(xn, g_ref[...]).astype(ho_ref.dtype)


def combine(x, ys, meta_f, code1, code2, row_off, g_next, h_dtype, tm=256):
    T, D = x.shape
    tm = min(tm, T)
    row = lambda w: pl.BlockSpec((tm, w), lambda i, c1, c2, off: (i, 0))
    return pl.pallas_call(
        _combine_kernel,
        grid_spec=pltpu.PrefetchScalarGridSpec(
            num_scalar_prefetch=3,
            grid=(T // tm,),
            in_specs=[row(D), row(LANES), pl.BlockSpec((1, D), lambda i, c1, c2, off: (0, 0)),
                      pl.BlockSpec(memory_space=pl.ANY)],
            out_specs=[row(D), row(D)],
            scratch_shapes=[pltpu.VMEM((2, 2, tm * D // LANES, LANES), F32),
                            pltpu.SemaphoreType.DMA((2,))]),
        out_shape=[jax.ShapeDtypeStruct((T, D), F32), jax.ShapeDtypeStruct((T, D), h_dtype)],
        compiler_params=_cparams(("arbitrary",), 40),
        name="moe_combine",
    )(code1, code2, row_off, x, meta_f, g_next.reshape(1, D), ys)


def moe_row_blocks(T):
    return (2 * T) // MOE_BLOCK + N_EXPERTS


def moe_layer(x, h, xs_buf, w_group, b_group, w_expert, b_expert, layer, w_gate, w_up, w_down, g_next,
              h_dtype):
    T, D = x.shape
    assert T <= MOE_CODE, "ranks inside one expert must fit the packed (expert, rank) code"
    n_blocks = xs_buf.shape[0] * LANES // D // MOE_BLOCK
    pad = LANES - N_GROUPS - N_EXPERTS
    w_r = jnp.concatenate([w_group, w_expert.reshape(D, N_EXPERTS), jnp.zeros((D, pad), F32)], axis=1)
    b_r = jnp.concatenate([b_group, b_expert.reshape(N_EXPERTS), jnp.zeros((pad,), F32)]).reshape(1, LANES)
    meta_i, meta_f, counts = router(h, w_r, b_r)
    (code1, code2, row_off, blk_expert, blk_src, blk_valid, blk_first, blk_next,
     blk_slot) = moe_plan(meta_i, counts, n_blocks)
    xs = dispatch(h, D, code1, code2, row_off, xs_buf)
    FF = w_gate.shape[-1]
    first_row = layer * N_EXPERTS
    ys = experts(xs, blk_expert + first_row, blk_src, blk_valid, blk_first,
                 jnp.where(blk_next >= 0, blk_next + first_row, -1), blk_slot,
                 w_gate.reshape(-1, D, FF), w_up.reshape(-1, D, FF), w_down.reshape(-1, FF, D))
    xo, ho = combine(x, ys, meta_f, code1, code2, row_off, g_next, h_dtype)
    return xo, ho, xs


def _forget_kernel(h_ref, w_ref, b_ref, o_ref, run_ref):
    @pl.when(pl.program_id(0) == 0)
    def _():
        run_ref[...] = jnp.zeros_like(run_ref)

    h = h_ref[...]
    z = sum(jnp.dot(h, w, preferred_element_type=F32) for w in _bf16_pieces(w_ref[...], 2)) + b_ref[...]
    log_f = jax.nn.log_sigmoid(z)
    tm = z.shape[0]
    r = lax.broadcasted_iota(I32, (tm, tm), 0)
    c = lax.broadcasted_iota(I32, (tm, tm), 1)
    upto = (c <= r).astype(BF16)
    cum = sum(jnp.dot(upto, p, preferred_element_type=F32) for p in _bf16_pieces(log_f, 3)) + run_ref[0:1, :]
    o_ref[...] = cum.T
    run_ref[...] = jnp.broadcast_to(cum[tm - 1:tm, :], run_ref.shape)


def forget_cumsum(h, w_f, b_f, tm=256):
    T, D = h.shape
    tm = min(tm, T)
    return pl.pallas_call(
        _forget_kernel,
        grid=(T // tm,),
        in_specs=[pl.BlockSpec((tm, D), lambda i: (i, 0)),
                  pl.BlockSpec((D, LANES), lambda i: (0, 0)),
                  pl.BlockSpec((1, LANES), lambda i: (0, 0))],
        out_specs=pl.BlockSpec((LANES, tm), lambda i: (0, i)),
        out_shape=jax.ShapeDtypeStruct((LANES, T), F32),
        scratch_shapes=[pltpu.VMEM((8, LANES), F32)],
        compiler_params=_cparams(("arbitrary",), 40),
        name="forget_cumsum",
    )(h, w_f, b_f)


def _flash_kernel(q_ref, k_ref, v_ref, ck_ref, cq_ref, o_ref, m_ref, acc_ref, *, scale, sub):
    qi = pl.program_id(1)
    bq = q_ref.shape[0]
    dh = FOX_HEAD_DIM
    heads = q_ref.shape[1] // dh
    m_ref[...] = jnp.full_like(m_ref, -jnp.inf)
    acc_ref[...] = jnp.zeros_like(acc_ref)
    ones = jnp.ones((bq, dh), BF16)
    n_sub = bq // sub

    def chunk(start, diagonal):
        def scores(hh):
            cols = slice(hh * dh, (hh + 1) * dh)
            k = k_ref[pl.ds(start, bq), cols]
            return lax.dot_general(q_ref[:, cols], k, (((1,), (1,)), ((), ())), preferred_element_type=F32)

        def softmax(hh, s_all):
            c0 = cq_ref[hh, :, 0:1]
            bias = (c0 - ck_ref[hh, :, pl.ds(start, bq)]) * LOG2E
            ps = []
            for r in range(n_sub):
                rows = slice(r * sub, (r + 1) * sub)
                s = s_all[rows, :] * (scale * LOG2E) + bias
                if diagonal:
                    row = lax.broadcasted_iota(I32, s.shape, 0) + r * sub
                    col = lax.broadcasted_iota(I32, s.shape, 1)
                    s = jnp.where(col <= row, s, -jnp.inf)
                m_prev = m_ref[hh, rows, :]
                m_new = jnp.maximum(m_prev, jnp.max(s, axis=1, keepdims=True))
                m_ref[hh, rows, :] = m_new
                ps.append(jnp.exp2(s - jnp.concatenate([m_new] * (bq // LANES), axis=1)).astype(BF16))
                alpha = jnp.exp2(m_prev - m_new)
                acc_ref[hh, rows, :] = jnp.concatenate([alpha] * 2, axis=1) * acc_ref[hh, rows, :]
            return jnp.concatenate(ps, axis=0)

        def update(hh, p):
            cols = slice(hh * dh, (hh + 1) * dh)
            v1 = jnp.concatenate([v_ref[pl.ds(start, bq), cols], ones], axis=1)
            acc_ref[hh] = acc_ref[hh] + jnp.dot(p, v1, preferred_element_type=F32)

        s_next = scores(0)
        for hh in range(heads):
            s_cur = s_next
            if hh + 1 < heads:
                s_next = scores(hh + 1)
            update(hh, softmax(hh, s_cur))

    def body(j, carry):
        chunk(pl.multiple_of(j * bq, bq), False)
        return carry

    lax.fori_loop(0, qi, body, 0)
    chunk(pl.multiple_of(qi * bq, bq), True)
    for hh in range(heads):
        o_ref[:, hh * dh:(hh + 1) * dh] = (acc_ref[hh, :, 0:dh] / acc_ref[hh, :, dh:2 * dh]).astype(o_ref.dtype)


def flash_attention(qkv, cum_t, bq=512, sub=32, heads=4):
    T = qkv.shape[0]
    H, dh = FOX_HEADS, FOX_HEAD_DIM
    bq = min(bq, T)
    kern = functools.partial(_flash_kernel, scale=dh ** -0.5, sub=min(sub, bq))
    hw = heads * dh
    nhb = H // heads
    return pl.pallas_call(
        kern,
        grid=(nhb, T // bq),
        in_specs=[pl.BlockSpec((bq, hw), lambda h, i: (i, h)),
                  pl.BlockSpec((T, hw), lambda h, i: (0, nhb + h)),
                  pl.BlockSpec((T, hw), lambda h, i: (0, 2 * nhb + h)),
                  pl.BlockSpec((heads, 1, T), lambda h, i: (h, 0, 0)),
                  pl.BlockSpec((heads, 1, bq), lambda h, i: (h, 0, i))],
        out_specs=pl.BlockSpec((bq, hw), lambda h, i: (i, h)),
        out_shape=jax.ShapeDtypeStruct((T, H * dh), BF16),
        scratch_shapes=[pltpu.VMEM((heads, bq, LANES), F32), pltpu.VMEM((heads, bq, 2 * dh), F32)],
        compiler_params=_cparams(("parallel", "arbitrary"), 48),
        name="fox_attention",
    )(qkv, qkv, qkv, cum_t, cum_t)


def kernel(x, ab_norm, ab_w_in, ab_conv_w, s5_lambda_re, s5_lambda_im, s5_log_dt, s5_b_re, s5_b_im,
           s5_c_re, s5_c_im, s5_d, s5_w_glu, s5_b_glu, ab_w_out, c_norm, c_w_in, c_b_forget, c_w_out,
           ffn_norm, router_w_group, router_b_group, router_w_expert, router_b_expert,
           moe_w_gate, moe_w_up, moe_w_down, final_norm):
    bsz, L, D = x.shape
    depth = ffn_norm.shape[0]
    cw = ab_conv_w.shape[-1]
    xt = x.reshape(bsz * L, D)
    h = None
    xs_buf = jnp.zeros((moe_row_blocks(bsz * L) * MOE_BLOCK * D // LANES, LANES), F32)
    for i in range(depth):
        j = i // 2
        if i % 2 == 0:
            g_in = ab_norm[j]
            if h is not None:
                proj = matmul(h, ab_w_in[j], F32)
            else:
                proj = norm_matmul(xt, g_in, ab_w_in[j].astype(BF16), F32)
            y_conv = conv_mixer(proj, ab_conv_w[j])
            y_gelu = s5_mixer(proj, 3 * cw, s5_lambda_re[j], s5_lambda_im[j], s5_log_dt[j],
                              s5_b_re[j], s5_b_im[j], s5_c_re[j], s5_c_im[j], s5_d[j])
            xt, hf = glu_out(xt, y_conv, y_gelu, s5_w_glu[j].astype(BF16), s5_b_glu[j],
                             ab_w_out[j].astype(BF16), ffn_norm[i])
        else:
            assert h is not None, "an attention layer always follows a MoE combine that emits its norm"
            hd = FOX_HEADS * FOX_HEAD_DIM
            qkv = matmul(h, c_w_in[j], BF16, n_cols=3 * hd)
            w_f = jnp.pad(c_w_in[j][:, 3 * hd:], ((0, 0), (0, LANES - FOX_HEADS)))
            b_f = jnp.pad(c_b_forget[j], (0, LANES - FOX_HEADS)).reshape(1, LANES)
            cum = forget_cumsum(h, w_f, b_f)
            cum_t = cum[:FOX_HEADS].reshape(FOX_HEADS, 1, bsz * L)
            att = flash_attention(qkv, cum_t)
            xt, hf = proj_residual(xt, att, c_w_out[j].astype(BF16), ffn_norm[i])
        last = i == depth - 1
        if last:
            g_next = final_norm
        elif (i + 1) % 2 == 0:
            g_next = ab_norm[(i + 1) // 2]
        else:
            g_next = c_norm[(i + 1) // 2]
        xt, h, xs_buf = moe_layer(xt, hf, xs_buf, router_w_group[i], router_b_group[i], router_w_expert[i],
                                  router_b_expert[i], i, moe_w_gate, moe_w_up, moe_w_down,
                                  g_next, F32 if last else BF16)
    return h.reshape(bsz, L, D)
```

```python
import functools
import math

import jax
import jax.numpy as jnp
from jax import lax
from jax.experimental import pallas as pl
from jax.experimental.pallas import tpu as pltpu

F32 = jnp.float32
BF16 = jnp.bfloat16
I32 = jnp.int32
U32 = jnp.uint32

RMS_EPS = 1e-6
LANES = 128
LOG2E = math.log2(math.e)
MIB = 1024 * 1024

CONV_K = 3
S5_GROUP = 16
S5_STATE = 64
S5_CHUNK = 16
S5_OCT = LANES // S5_GROUP
FOX_HEADS = 16
FOX_HEAD_DIM = 128
N_GROUPS = 4
EXPERTS_PER_GROUP = 8
N_EXPERTS = N_GROUPS * EXPERTS_PER_GROUP
EXPERT_LANE0 = N_GROUPS
MOE_BLOCK = 256
MOE_CODE_BITS = 16
MOE_CODE = 1 << MOE_CODE_BITS


def _cparams(sem, vmem_mib):
    return pltpu.CompilerParams(dimension_semantics=sem, vmem_limit_bytes=vmem_mib * MIB)


def _bf16_pieces(x, n):
    pieces = []
    for _ in range(n):
        p = x.astype(BF16)
        pieces.append(p)
        x = x - p.astype(F32)
    return pieces


def _rms(x, g):
    ms = jnp.mean(x * x, axis=-1, keepdims=True)
    return x * lax.rsqrt(ms + RMS_EPS) * g


def _norm_matmul_kernel(x_ref, g_ref, w_ref, o_ref, h_ref):
    @pl.when(pl.program_id(1) == 0)
    def _():
        h_ref[...] = _rms(x_ref[...], g_ref[...]).astype(BF16)

    o_ref[...] = jnp.dot(h_ref[...], w_ref[...], preferred_element_type=F32).astype(o_ref.dtype)


def norm_matmul(x, g, w, out_dtype, tm=1024, tn=1024):
    T, D = x.shape
    N = w.shape[1]
    tm, tn = min(tm, T), min(tn, N)
    return pl.pallas_call(
        _norm_matmul_kernel,
        grid=(T // tm, N // tn),
        in_specs=[pl.BlockSpec((tm, D), lambda i, j: (i, 0)),
                  pl.BlockSpec((1, D), lambda i, j: (0, 0)),
                  pl.BlockSpec((D, tn), lambda i, j: (0, j))],
        out_specs=pl.BlockSpec((tm, tn), lambda i, j: (i, j)),
        out_shape=jax.ShapeDtypeStruct((T, N), out_dtype),
        scratch_shapes=[pltpu.VMEM((tm, D), BF16)],
        compiler_params=_cparams(("parallel", "arbitrary"), 56),
        name="norm_matmul",
    )(x, g.reshape(1, D), w)


def _matmul_kernel(a_ref, w_ref, o_ref, wb_ref):
    @pl.when(pl.program_id(1) == 0)
    def _():
        wb_ref[...] = w_ref[...].astype(BF16)

    o_ref[...] = jnp.dot(a_ref[...], wb_ref[...], preferred_element_type=F32).astype(o_ref.dtype)


def matmul(a, w, out_dtype, n_cols=None, tm=1024, tn=1024):
    T, K = a.shape
    N = w.shape[1] if n_cols is None else n_cols
    tm, tn = min(tm, T), min(tn, N)
    return pl.pallas_call(
        _matmul_kernel,
        grid=(N // tn, T // tm),
        in_specs=[pl.BlockSpec((tm, K), lambda j, i: (i, 0)),
                  pl.BlockSpec((K, tn), lambda j, i: (0, j))],
        out_specs=pl.BlockSpec((tm, tn), lambda j, i: (i, j)),
        out_shape=jax.ShapeDtypeStruct((T, N), out_dtype),
        scratch_shapes=[pltpu.VMEM((K, tn), BF16)],
        compiler_params=_cparams(("parallel", "arbitrary"), 48),
        name="matmul",
    )(a, w)


def _conv_kernel(gb_ref, gc_ref, u_ref, gcp_ref, up_ref, w_ref, o_ref):
    w0, w1, w2 = w_ref[0:1, :], w_ref[1:2, :], w_ref[2:3, :]
    v = gc_ref[...] * u_ref[...]
    y = w2 * v + w1 * pltpu.roll(v, 1, 0) + w0 * pltpu.roll(v, 2, 0)
    o_ref[...] = (gb_ref[...] * y).astype(o_ref.dtype)
    vp = gcp_ref[...] * up_ref[...]
    vp = jnp.where(pl.program_id(0) > 0, vp, jnp.zeros_like(vp))
    v8 = v[0:8, :]
    row = lax.broadcasted_iota(I32, v8.shape, 0)
    v1 = jnp.where(row < 1, pltpu.roll(vp, 1, 0), pltpu.roll(v8, 1, 0))
    v2 = jnp.where(row < 2, pltpu.roll(vp, 2, 0), pltpu.roll(v8, 2, 0))
    o_ref[0:8, :] = (gb_ref[0:8, :] * (w2 * v8 + w1 * v1 + w0 * v2)).astype(o_ref.dtype)


def conv_mixer(proj, conv_w, tm=512):
    T = proj.shape[0]
    CW = conv_w.shape[1]
    tm = min(tm, T)
    r8 = tm // 8
    cur = lambda c: pl.BlockSpec((tm, CW), lambda i: (i, c))
    prev = lambda c: pl.BlockSpec((8, CW), lambda i: (jnp.maximum(i * r8 - 1, 0), c))
    return pl.pallas_call(
        _conv_kernel,
        grid=(T // tm,),
        in_specs=[cur(0), cur(1), cur(2), prev(1), prev(2),
                  pl.BlockSpec((CONV_K, CW), lambda i: (0, 0))],
        out_specs=pl.BlockSpec((tm, CW), lambda i: (i, 0)),
        out_shape=jax.ShapeDtypeStruct((T, CW), BF16),
        compiler_params=_cparams(("parallel",), 40),
        name="conv_mixer",
    )(proj, proj, proj, proj, proj, conv_w)


def _cmul(a, b):
    return a[0] * b[0] - a[1] * b[1], a[0] * b[1] + a[1] * b[0]


def s5_matrices(lam_re, lam_im, log_dt, b_re, b_im, c_re, c_im):
    C = S5_CHUNK
    dt = jnp.exp(log_dt)[:, None]
    a, b = lam_re * dt, lam_im * dt
    mag = jnp.exp(a)
    lbar = (mag * jnp.cos(b), mag * jnp.sin(b))
    den = lam_re * lam_re + lam_im * lam_im
    inv_lam = (lam_re / den, -lam_im / den)
    coef = _cmul((lbar[0] - 1.0, lbar[1]), inv_lam)
    bbar = _cmul((coef[0][..., None], coef[1][..., None]), (b_re, b_im))
    j = jnp.arange(C + 1, dtype=F32)[None, :, None]
    pmag = jnp.exp(a[:, None, :] * j)
    pw = (pmag * jnp.cos(b[:, None, :] * j), pmag * jnp.sin(b[:, None, :] * j))
    return pw, bbar


def _lane_tile_blockdiag(m):
    Q, O, r, c = m.shape
    out = jnp.zeros((Q, O, r, O, c), m.dtype)
    for g in range(O):
        out = out.at[:, g, :, g, :].set(m[:, g])
    return out.reshape(Q, O * r, O * c)


def s5_lane_tile_operands(lam_re, lam_im, log_dt, b_re, b_im, c_re, c_im):
    C, H, N, O = S5_CHUNK, S5_GROUP, S5_STATE, S5_OCT
    pw, bbar = s5_matrices(lam_re, lam_im, log_dt, b_re, b_im, c_re, c_im)
    Q = lam_re.shape[0] // O
    pw_q = jnp.stack([p.reshape(Q, O, C + 1, N).transpose(0, 2, 1, 3).reshape(Q, C + 1, O * N) for p in pw],
                     axis=1)
    bb = jnp.stack([_lane_tile_blockdiag(m.transpose(0, 2, 1).reshape(Q, O, H, N)) for m in bbar], axis=1)
    cc = jnp.stack([_lane_tile_blockdiag(m.reshape(Q, O, H, N)) for m in (c_re, c_im)], axis=1)
    return pw_q, bb, cc


def _s5_kernel(u_ref, pw_ref, bb_ref, cc_ref, d_ref, o_ref, t_ref, w_ref, v_ref, uf_ref, s_ref, x_ref, c_ref):
    C = S5_CHUNK
    tb = u_ref.shape[0]
    nc = tb // C
    sw = pw_ref.shape[3]

    @pl.when((pl.program_id(0) == 0) & (pl.program_id(1) == 0))
    def _():
        t_ref[...] = jnp.zeros_like(t_ref)

    @pl.when(pl.program_id(1) == 0)
    def _():
        bb = (bb_ref[0, 0], bb_ref[0, 1])
        cc = (cc_ref[0, 0], cc_ref[0, 1])
        c_hi, c_lo = _bf16_pieces(jnp.concatenate([cc[0], -cc[1]], axis=1), 2)
        nt = (((1,), (1,)), ((), ()))
        for j in range(C):
            a = _cmul(bb, (pw_ref[0, 0, j:j + 1, :], pw_ref[0, 1, j:j + 1, :]))
            s = C - 1 - j
            w_ref[s * LANES:(s + 1) * LANES, 0:sw] = a[0].astype(BF16)
            w_ref[s * LANES:(s + 1) * LANES, sw:2 * sw] = a[1].astype(BF16)
            a_hi, a_lo = _bf16_pieces(jnp.concatenate(a, axis=1), 2)
            lag = (lax.dot_general(a_hi, c_hi, nt, preferred_element_type=F32)
                   + (lax.dot_general(a_hi, c_lo, nt, preferred_element_type=F32)
                      + lax.dot_general(a_lo, c_hi, nt, preferred_element_type=F32))).astype(BF16)
            for s0 in range(C - j):
                t_ref[s0 * LANES:(s0 + 1) * LANES, (s0 + j) * LANES:(s0 + j + 1) * LANES] = lag
            g = _cmul(cc, (pw_ref[0, 0, j + 1:j + 2, :], pw_ref[0, 1, j + 1:j + 2, :]))
            v_ref[j * LANES:(j + 1) * LANES, 0:sw] = g[0].astype(BF16)
            v_ref[j * LANES:(j + 1) * LANES, sw:2 * sw] = (-g[1]).astype(BF16)
        c_ref[...] = jnp.zeros_like(c_ref)

    for s in range(C):
        uf_ref[:, s * LANES:(s + 1) * LANES] = u_ref[pl.ds(s, nc, stride=C), :].astype(BF16)
    s_ref[...] = jnp.dot(uf_ref[...], w_ref[...], preferred_element_type=F32)
    lr, li = pw_ref[0, 0, C:C + 1, :], pw_ref[0, 1, C:C + 1, :]
    row = lax.broadcasted_iota(I32, (8, sw), 0)

    def tile_step(i, carry):
        xr, xi = carry
        r0 = pl.multiple_of(i * 8, 8)
        sr = s_ref[pl.ds(r0, 8), 0:sw]
        si = s_ref[pl.ds(r0, 8), sw:2 * sw]
        tr = jnp.zeros((8, sw), F32)
        ti = jnp.zeros((8, sw), F32)
        for r in range(8):
            tr = jnp.where(row == r, xr, tr)
            ti = jnp.where(row == r, xi, ti)
            xr, xi = lr * xr - li * xi + sr[r:r + 1, :], lr * xi + li * xr + si[r:r + 1, :]
        x_ref[pl.ds(r0, 8), 0:sw] = tr
        x_ref[pl.ds(r0, 8), sw:2 * sw] = ti
        return xr, xi

    xr, xi = lax.fori_loop(0, nc // 8, tile_step, (c_ref[0:1, 0:sw], c_ref[0:1, sw:2 * sw]))
    c_ref[0:1, 0:sw] = xr
    c_ref[0:1, sw:2 * sw] = xi
    y_state = lax.dot_general(x_ref[...].astype(BF16), v_ref[...], (((1,), (1,)), ((), ())),
                              preferred_element_type=F32)
    quarter = C // 4
    for qt in range(4):
        rows = (qt + 1) * quarter * LANES
        cols = slice(qt * quarter * LANES, (qt + 1) * quarter * LANES)
        y = jnp.dot(uf_ref[:, 0:rows], t_ref[0:rows, cols], preferred_element_type=F32) + y_state[:, cols]
        for s in range(quarter):
            st = qt * quarter + s
            ys = y[:, s * LANES:(s + 1) * LANES] + d_ref[...] * u_ref[pl.ds(st, nc, stride=C), :]
            o_ref[pl.ds(st, nc, stride=C), :] = jax.nn.gelu(ys)


def s5_mixer(proj, col0, lam_re, lam_im, log_dt, b_re, b_im, c_re, c_im, d, tb=8192):
    T = proj.shape[0]
    C, N, O = S5_CHUNK, S5_STATE, S5_OCT
    W = d.shape[0]
    Q = W // LANES
    tb = min(tb, T)
    pw_q, bb, cc = s5_lane_tile_operands(lam_re, lam_im, log_dt, b_re, b_im, c_re, c_im)
    cb0 = col0 // LANES
    nc = tb // C
    sw = O * N
    quad = lambda a: pl.BlockSpec((1,) + a.shape[1:], lambda q, t: (q, 0, 0, 0))
    return pl.pallas_call(
        _s5_kernel,
        grid=(Q, T // tb),
        in_specs=[pl.BlockSpec((tb, LANES), lambda q, t: (t, cb0 + q)),
                  quad(pw_q), quad(bb), quad(cc),
                  pl.BlockSpec((1, LANES), lambda q, t: (0, q))],
        out_specs=pl.BlockSpec((tb, LANES), lambda q, t: (t, q)),
        out_shape=jax.ShapeDtypeStruct((T, W), F32),
        scratch_shapes=[pltpu.VMEM((C * LANES, C * LANES), BF16),
                        pltpu.VMEM((C * LANES, 2 * sw), BF16),
                        pltpu.VMEM((C * LANES, 2 * sw), BF16),
                        pltpu.VMEM((nc, C * LANES), BF16),
                        pltpu.VMEM((nc, 2 * sw), F32),
                        pltpu.VMEM((nc, 2 * sw), F32),
                        pltpu.VMEM((8, 2 * sw), F32)],
        compiler_params=_cparams(("arbitrary", "arbitrary"), 56),
        name="s5_scan",
    )(proj, pw_q, bb, cc, d.reshape(1, W))


def _store_row_tiles(ref, val, lead=()):
    rows, width = val.shape
    n = width // LANES
    for j in range(n):
        ref[lead + (pl.ds(j, rows, stride=n), slice(None))] = val[:, j * LANES:(j + 1) * LANES]


def _load_row_tiles(ref, rows, n, lead=()):
    return jnp.concatenate([ref[lead + (pl.ds(j, rows, stride=n), slice(None))] for j in range(n)], axis=1)


def _row_tile_spec(tm, width, index_map):
    return pl.BlockSpec((tm * (width // LANES), LANES), index_map)


def _glu_out_kernel(x_ref, yc_ref, yg_ref, wglu_ref, bglu_ref, wout_ref, g_ref, xo_ref, ho_ref):
    yg = yg_ref[...]
    z = jnp.dot(yg.astype(BF16), wglu_ref[...], preferred_element_type=F32) + bglu_ref[...]
    ys = (yg * jax.nn.sigmoid(z)).astype(BF16)
    cw = yc_ref.shape[1]
    xn = (x_ref[...]
          + jnp.dot(yc_ref[...], wout_ref[0:cw, :], preferred_element_type=F32)
          + jnp.dot(ys, wout_ref[cw:, :], preferred_element_type=F32))
    xo_ref[...] = xn
    _store_row_tiles(ho_ref, _rms(xn, g_ref[...]))


def glu_out(x, y_conv, y_gelu, w_glu, b_glu, w_out, g_next, tm=512):
    T, D = x.shape
    CW, SW = y_conv.shape[1], y_gelu.shape[1]
    tm = min(tm, T)
    row = lambda w: pl.BlockSpec((tm, w), lambda i: (i, 0))
    full = lambda a: pl.BlockSpec(a.shape, lambda i: (0,) * a.ndim)
    bg, g2 = b_glu.reshape(1, SW), g_next.reshape(1, D)
    return pl.pallas_call(
        _glu_out_kernel,
        grid=(T // tm,),
        in_specs=[row(D), row(CW), row(SW), full(w_glu), full(bg), full(w_out), full(g2)],
        out_specs=[row(D), _row_tile_spec(tm, D, lambda i: (i, 0))],
        out_shape=[jax.ShapeDtypeStruct((T, D), F32), jax.ShapeDtypeStruct((T * D // LANES, LANES), F32)],
        compiler_params=_cparams(("parallel",), 56),
        name="glu_out",
    )(x, y_conv, y_gelu, w_glu, bg, w_out, g2)


def _proj_residual_kernel(x_ref, a_ref, w_ref, g_ref, xo_ref, ho_ref):
    xn = x_ref[...] + jnp.dot(a_ref[...], w_ref[...], preferred_element_type=F32)
    xo_ref[...] = xn
    _store_row_tiles(ho_ref, _rms(xn, g_ref[...]))


def proj_residual(x, a, w, g_next, tm=512):
    T, D = x.shape
    K = a.shape[1]
    tm = min(tm, T)
    row = lambda w_: pl.BlockSpec((tm, w_), lambda i: (i, 0))
    full = lambda arr: pl.BlockSpec(arr.shape, lambda i: (0,) * arr.ndim)
    g2 = g_next.reshape(1, D)
    return pl.pallas_call(
        _proj_residual_kernel,
        grid=(T // tm,),
        in_specs=[row(D), row(K), full(w), full(g2)],
        out_specs=[row(D), _row_tile_spec(tm, D, lambda i: (i, 0))],
        out_shape=[jax.ShapeDtypeStruct((T, D), F32), jax.ShapeDtypeStruct((T * D // LANES, LANES), F32)],
        compiler_params=_cparams(("parallel",), 56),
        name="proj_residual",
    )(x, a, w, g2)


def _router_kernel(h_ref, w_ref, b_ref, mi_ref, mf_ref, cnt_ref, run_ref):
    i = pl.program_id(0)

    @pl.when(i == 0)
    def _():
        run_ref[...] = jnp.zeros_like(run_ref)

    D = w_ref.shape[0]
    tm = h_ref.shape[0] * LANES // D
    h = _load_row_tiles(h_ref, tm, D // LANES)
    h_hi, h_lo = _bf16_pieces(h, 2)
    w_hi, w_lo = _bf16_pieces(w_ref[...], 2)
    logits = (jnp.dot(h_hi, w_hi, preferred_element_type=F32)
              + (jnp.dot(h_hi, w_lo, preferred_element_type=F32)
                 + jnp.dot(h_lo, w_hi, preferred_element_type=F32))) + b_ref[...]
    lane = lax.broadcasted_iota(I32, logits.shape, 1)
    neg = jnp.float32(-jnp.inf)
    gl = jnp.where(lane < N_GROUPS, logits, neg)
    gmax = jnp.max(gl, axis=1, keepdims=True)
    gsum = jnp.sum(jnp.where(lane < N_GROUPS, jnp.exp(gl - gmax), 0.0), axis=1, keepdims=True)
    gw = 1.0 / gsum
    gidx = jnp.min(jnp.where(gl == gmax, lane, LANES), axis=1, keepdims=True)
    lo = EXPERT_LANE0 + EXPERTS_PER_GROUP * gidx
    el = jnp.where((lane >= lo) & (lane < lo + EXPERTS_PER_GROUP), logits, neg)
    v1 = jnp.max(el, axis=1, keepdims=True)
    i1 = jnp.min(jnp.where(el == v1, lane, LANES), axis=1, keepdims=True)
    el2 = jnp.where(lane == i1, neg, el)
    v2 = jnp.max(el2, axis=1, keepdims=True)
    i2 = jnp.min(jnp.where(el2 == v2, lane, LANES), axis=1, keepdims=True)
    t = jnp.exp(v2 - v1)
    w1 = gw / (1.0 + t)
    w2 = gw * t / (1.0 + t)
    hit1 = lane == i1
    hit2 = lane == i2
    cnt = (hit1 | hit2).astype(BF16)
    r = lax.broadcasted_iota(I32, (tm, tm), 0)
    c = lax.broadcasted_iota(I32, (tm, tm), 1)
    before = (c < r).astype(BF16)
    cum = jnp.dot(before, cnt, preferred_element_type=F32) + run_ref[0:1, :]
    rank1 = jnp.sum(jnp.where(hit1, cum, 0.0), axis=1, keepdims=True).astype(I32)
    rank2 = jnp.sum(jnp.where(hit2, cum, 0.0), axis=1, keepdims=True).astype(I32)
    run = run_ref[0:1, :] + jnp.sum(cnt.astype(F32), axis=0, keepdims=True)
    run_ref[...] = jnp.broadcast_to(run, run_ref.shape)
    cnt_ref[...] = jnp.broadcast_to(run, cnt_ref.shape)
    code1 = (i1 - EXPERT_LANE0) * MOE_CODE + rank1
    code2 = (i2 - EXPERT_LANE0) * MOE_CODE + rank2
    mi_ref[...] = jnp.where(lane == 0, code1, jnp.where(lane == 1, code2, 0))
    mf_ref[...] = jnp.where(lane == 0, w1, jnp.where(lane == 1, w2, 0.0))


def router(h, w_r, b_r, tm=256):
    D = w_r.shape[0]
    T = h.shape[0] * LANES // D
    tm = min(tm, T)
    row = lambda w: pl.BlockSpec((tm, w), lambda i: (i, 0))
    full = lambda a: pl.BlockSpec(a.shape, lambda i: (0,) * a.ndim)
    return pl.pallas_call(
        _router_kernel,
        grid=(T // tm,),
        in_specs=[_row_tile_spec(tm, D, lambda i: (i, 0)), full(w_r), full(b_r)],
        out_specs=[row(LANES), row(LANES), pl.BlockSpec((8, LANES), lambda i: (0, 0))],
        out_shape=[jax.ShapeDtypeStruct((T, LANES), I32), jax.ShapeDtypeStruct((T, LANES), F32),
                   jax.ShapeDtypeStruct((8, LANES), F32)],
        scratch_shapes=[pltpu.VMEM((8, LANES), F32)],
        compiler_params=_cparams(("arbitrary",), 40),
        name="moe_router",
    )(h, w_r, b_r)


def moe_plan(meta_i, counts, n_blocks):
    B = MOE_BLOCK
    code1, code2 = meta_i[:, 0], meta_i[:, 1]
    cnt = counts[0, EXPERT_LANE0:EXPERT_LANE0 + N_EXPERTS].astype(I32)
    nblk = (cnt + B - 1) // B
    blk_end = jnp.cumsum(nblk)
    blk_off = blk_end - nblk
    b = jnp.arange(n_blocks, dtype=I32)
    total = blk_end[-1]
    owner = jnp.minimum(jnp.sum((blk_end[None, :] <= b[:, None]).astype(I32), axis=1), N_EXPERTS - 1)
    valid = b < total
    last = jnp.maximum(total - 1, 0)
    blk_expert = jnp.where(valid, owner, owner[last])
    blk_src = jnp.where(valid, b, last)
    blk_first = (valid & (b == blk_off[owner])).astype(I32)
    e = jnp.arange(N_EXPERTS, dtype=I32)
    used = nblk > 0
    slot = (jnp.cumsum(used.astype(I32)) - 1) % 2
    later = (e[None, :] > e[:, None]) & used[None, :]
    next_used = jnp.min(jnp.where(later, e[None, :], N_EXPERTS), axis=1)
    next_used = jnp.where(next_used < N_EXPERTS, next_used, -1)
    return (code1, code2, blk_off * B, blk_expert, blk_src, valid.astype(I32), blk_first,
            next_used[blk_expert], slot[blk_expert])


def _row_copy(src, dst, sem):
    return pltpu.make_async_copy(src, dst, sem)


def _sorted_row(code, off_ref):
    return off_ref[code >> MOE_CODE_BITS] + (code & (MOE_CODE - 1))


def _tile_rows(row, n):
    return pl.ds(pl.multiple_of(row * n, n), n)


def _dispatch_kernel(code1_ref, code2_ref, off_ref, hp_ref, xs_in_ref, xs_ref, sem, *, n):
    del xs_in_ref
    tm = hp_ref.shape[0] // n
    base = pl.program_id(0) * tm

    def copies(r):
        src = hp_ref.at[_tile_rows(r, n)]
        return (_row_copy(src, xs_ref.at[_tile_rows(_sorted_row(code1_ref[base + r], off_ref), n)], sem),
                _row_copy(src, xs_ref.at[_tile_rows(_sorted_row(code2_ref[base + r], off_ref), n)], sem))

    def start(r, carry):
        for cp in copies(r):
            cp.start()
        return carry

    lax.fori_loop(0, tm, start, 0, unroll=8)
    for _ in range(2):
        _row_copy(hp_ref, xs_ref.at[pl.ds(0, tm * n)], sem).wait()


def dispatch(hp, width, code1, code2, row_off, xs0, tm=256):
    n = width // LANES
    T = hp.shape[0] // n
    tm = min(tm, T)
    return pl.pallas_call(
        functools.partial(_dispatch_kernel, n=n),
        grid_spec=pltpu.PrefetchScalarGridSpec(
            num_scalar_prefetch=3,
            grid=(T // tm,),
            in_specs=[_row_tile_spec(tm, width, lambda i, c1, c2, off: (i, 0)),
                      pl.BlockSpec(memory_space=pl.ANY)],
            out_specs=pl.BlockSpec(memory_space=pl.ANY),
            scratch_shapes=[pltpu.SemaphoreType.DMA(())]),
        out_shape=jax.ShapeDtypeStruct(xs0.shape, hp.dtype),
        input_output_aliases={4: 0},
        compiler_params=_cparams(("arbitrary",), 32),
        name="moe_dispatch",
    )(code1, code2, row_off, hp, xs0)


def _experts_kernel(be_ref, bs_ref, bv_ref, bf_ref, bn_ref, bl_ref, xs_ref, wg_ref, wu_ref, wd_ref, ys_ref,
                    wg_f, wu_f, wd_f, wg_s, wu_s, wd_s, sem):
    del bs_ref
    b = pl.program_id(0)

    def weight_copies(e, slot):
        return (pltpu.make_async_copy(wg_ref.at[e], wg_f.at[slot], sem.at[slot]),
                pltpu.make_async_copy(wu_ref.at[e], wu_f.at[slot], sem.at[slot]),
                pltpu.make_async_copy(wd_ref.at[e], wd_f.at[slot], sem.at[slot]))

    @pl.when(bv_ref[b] == 0)
    def _():
        ys_ref[...] = jnp.zeros_like(ys_ref)

    @pl.when(bv_ref[b] > 0)
    def _():
        @pl.when(bf_ref[b] == 1)
        def _():
            slot = bl_ref[b]

            @pl.when(b == 0)
            def _():
                for cp in weight_copies(be_ref[b], slot):
                    cp.start()

            for cp in weight_copies(be_ref[b], slot):
                cp.wait()
            wg_s[...] = wg_f[slot].astype(BF16)
            wu_s[...] = wu_f[slot].astype(BF16)
            wd_s[...] = wd_f[slot].astype(BF16)

            @pl.when(bn_ref[b] >= 0)
            def _():
                for cp in weight_copies(bn_ref[b], 1 - slot):
                    cp.start()

        D = wg_s.shape[0]
        a = _load_row_tiles(xs_ref, MOE_BLOCK, D // LANES).astype(BF16)
        gate = jnp.dot(a, wg_s[...], preferred_element_type=F32)
        up = jnp.dot(a, wu_s[...], preferred_element_type=F32)
        mid = (jax.nn.silu(gate) * up).astype(BF16)
        _store_row_tiles(ys_ref, jnp.dot(mid, wd_s[...], preferred_element_type=F32))


def experts(xs, blk_expert, blk_src, blk_valid, blk_first, blk_next, blk_slot, w_gate, w_up, w_down):
    B = MOE_BLOCK
    E, D, FF = w_gate.shape
    nb = xs.shape[0] * LANES // D // B
    hbm = pl.BlockSpec(memory_space=pl.ANY)
    return pl.pallas_call(
        _experts_kernel,
        grid_spec=pltpu.PrefetchScalarGridSpec(
            num_scalar_prefetch=6,
            grid=(nb,),
            in_specs=[_row_tile_spec(B, D, lambda b, be, bs, bv, bf, bn, bl: (bs[b], 0)), hbm, hbm, hbm],
            out_specs=_row_tile_spec(B, D, lambda b, be, bs, bv, bf, bn, bl: (b, 0)),
            scratch_shapes=[pltpu.VMEM((2, D, FF), F32), pltpu.VMEM((2, D, FF), F32),
                            pltpu.VMEM((2, FF, D), F32),
                            pltpu.VMEM((D, FF), BF16), pltpu.VMEM((D, FF), BF16),
                            pltpu.VMEM((FF, D), BF16), pltpu.SemaphoreType.DMA((2,))]),
        out_shape=jax.ShapeDtypeStruct(xs.shape, F32),
        compiler_params=_cparams(("arbitrary",), 56),
        name="moe_experts",
    )(blk_expert, blk_src, blk_valid, blk_first, blk_next, blk_slot, xs, w_gate, w_up, w_down)


def _combine_kernel(code1_ref, code2_ref, off_ref, x_ref, mf_ref, g_ref, ys_ref, xo_ref, ho_ref, buf, sem):
    tm, D = x_ref.shape
    n = D // LANES
    i = pl.program_id(0)
    slot = i % 2

    def fetch(step, dst_slot):
        base = step * tm

        def start(r, carry):
            dst = _tile_rows(r, n)
            _row_copy(ys_ref.at[_tile_rows(_sorted_row(code1_ref[base + r], off_ref), n)],
                      buf.at[dst_slot, 0, dst], sem.at[dst_slot]).start()
            _row_copy(ys_ref.at[_tile_rows(_sorted_row(code2_ref[base + r], off_ref), n)],
                      buf.at[dst_slot, 1, dst], sem.at[dst_slot]).start()
            return carry

        lax.fori_loop(0, tm, start, 0, unroll=8)

    @pl.when(i == 0)
    def _():
        fetch(0, 0)

    @pl.when(i + 1 < pl.num_programs(0))
    def _():
        fetch(i + 1, 1 - slot)

    for k in range(2):
        _row_copy(ys_ref.at[pl.ds(0, tm * n)], buf.at[slot, k], sem.at[slot]).wait()
    mf = mf_ref[...]
    xn = (x_ref[...] + mf[:, 0:1] * _load_row_tiles(buf, tm, n, lead=(slot, 0))
          + mf[:, 1:2] * _load_row_tiles(buf, tm, n, lead=(slot, 1)))
    xo_ref[...] = xn
    ho_ref[...] = _rms(xn, g_ref[...]).astype(ho_ref.dtype)


def combine(x, ys, meta_f, code1, code2, row_off, g_next, h_dtype, tm=256):
    T, D = x.shape
    tm = min(tm, T)
    row = lambda w: pl.BlockSpec((tm, w), lambda i, c1, c2, off: (i, 0))
    return pl.pallas_call(
        _combine_kernel,
        grid_spec=pltpu.PrefetchScalarGridSpec(
            num_scalar_prefetch=3,
            grid=(T // tm,),
            in_specs=[row(D), row(LANES), pl.BlockSpec((1, D), lambda i, c1, c2, off: (0, 0)),
                      pl.BlockSpec(memory_space=pl.ANY)],
            out_specs=[row(D), row(D)],
            scratch_shapes=[pltpu.VMEM((2, 2, tm * D // LANES, LANES), F32),
                            pltpu.SemaphoreType.DMA((2,))]),
        out_shape=[jax.ShapeDtypeStruct((T, D), F32), jax.ShapeDtypeStruct((T, D), h_dtype)],
        compiler_params=_cparams(("arbitrary",), 40),
        name="moe_combine",
    )(code1, code2, row_off, x, meta_f, g_next.reshape(1, D), ys)


def moe_row_blocks(T):
    return (2 * T) // MOE_BLOCK + N_EXPERTS


def moe_layer(x, h, xs_buf, w_group, b_group, w_expert, b_expert, layer, w_gate, w_up, w_down, g_next,
              h_dtype):
    T, D = x.shape
    assert T <= MOE_CODE, "ranks inside one expert must fit the packed (expert, rank) code"
    n_blocks = xs_buf.shape[0] * LANES // D // MOE_BLOCK
    pad = LANES - N_GROUPS - N_EXPERTS
    w_r = jnp.concatenate([w_group] + [w_expert[:, g, :] for g in range(N_GROUPS)]
                          + [jnp.zeros((D, pad), F32)], axis=1)
    b_r = jnp.concatenate([b_group, b_expert.reshape(N_EXPERTS), jnp.zeros((pad,), F32)]).reshape(1, LANES)
    meta_i, meta_f, counts = router(h, w_r, b_r)
    (code1, code2, row_off, blk_expert, blk_src, blk_valid, blk_first, blk_next,
     blk_slot) = moe_plan(meta_i, counts, n_blocks)
    xs = dispatch(h, D, code1, code2, row_off, xs_buf)
    FF = w_gate.shape[-1]
    first_row = layer * N_EXPERTS
    ys = experts(xs, blk_expert + first_row, blk_src, blk_valid, blk_first,
                 jnp.where(blk_next >= 0, blk_next + first_row, -1), blk_slot,
                 w_gate.reshape(-1, D, FF), w_up.reshape(-1, D, FF), w_down.reshape(-1, FF, D))
    xo, ho = combine(x, ys, meta_f, code1, code2, row_off, g_next, h_dtype)
    return xo, ho, xs


def _forget_kernel(h_ref, w_ref, b_ref, o_ref, run_ref):
    @pl.when(pl.program_id(0) == 0)
    def _():
        run_ref[...] = jnp.zeros_like(run_ref)

    h = h_ref[...]
    z = sum(jnp.dot(h, w, preferred_element_type=F32) for w in _bf16_pieces(w_ref[...], 2)) + b_ref[...]
    log_f = jax.nn.log_sigmoid(z)
    tm = z.shape[0]
    r = lax.broadcasted_iota(I32, (tm, tm), 0)
    c = lax.broadcasted_iota(I32, (tm, tm), 1)
    upto = (c <= r).astype(BF16)
    cum = sum(jnp.dot(upto, p, preferred_element_type=F32) for p in _bf16_pieces(log_f, 3)) + run_ref[0:1, :]
    o_ref[...] = cum.T
    run_ref[...] = jnp.broadcast_to(cum[tm - 1:tm, :], run_ref.shape)


def forget_cumsum(h, w_f, b_f, tm=256):
    T, D = h.shape
    tm = min(tm, T)
    return pl.pallas_call(
        _forget_kernel,
        grid=(T // tm,),
        in_specs=[pl.BlockSpec((tm, D), lambda i: (i, 0)),
                  pl.BlockSpec((D, LANES), lambda i: (0, 0)),
                  pl.BlockSpec((1, LANES), lambda i: (0, 0))],
        out_specs=pl.BlockSpec((LANES, tm), lambda i: (0, i)),
        out_shape=jax.ShapeDtypeStruct((LANES, T), F32),
        scratch_shapes=[pltpu.VMEM((8, LANES), F32)],
        compiler_params=_cparams(("arbitrary",), 40),
        name="forget_cumsum",
    )(h, w_f, b_f)


def _flash_kernel(q_ref, k_ref, v_ref, ck_ref, cq_ref, o_ref, m_ref, acc_ref, *, scale, sub):
    qi = pl.program_id(1)
    bq = q_ref.shape[0]
    dh = FOX_HEAD_DIM
    heads = q_ref.shape[1] // dh
    m_ref[...] = jnp.full_like(m_ref, -jnp.inf)
    acc_ref[...] = jnp.zeros_like(acc_ref)
    ones = jnp.ones((bq, dh), BF16)
    n_sub = bq // sub

    def chunk(start, diagonal):
        def scores(hh):
            cols = slice(hh * dh, (hh + 1) * dh)
            k = k_ref[pl.ds(start, bq), cols]
            return lax.dot_general(q_ref[:, cols], k, (((1,), (1,)), ((), ())), preferred_element_type=F32)

        def softmax(hh, s_all):
            c0 = cq_ref[hh, :, 0:1]
            bias = (c0 - ck_ref[hh, :, pl.ds(start, bq)]) * LOG2E
            ps = []
            for r in range(n_sub):
                rows = slice(r * sub, (r + 1) * sub)
                s = s_all[rows, :] * (scale * LOG2E) + bias
                if diagonal:
                    row = lax.broadcasted_iota(I32, s.shape, 0) + r * sub
                    col = lax.broadcasted_iota(I32, s.shape, 1)
                    s = jnp.where(col <= row, s, -jnp.inf)
                m_prev = m_ref[hh, rows, :]
                m_new = jnp.maximum(m_prev, jnp.max(s, axis=1, keepdims=True))
                m_ref[hh, rows, :] = m_new
                ps.append(jnp.exp2(s - jnp.concatenate([m_new] * (bq // LANES), axis=1)).astype(BF16))
                alpha = jnp.exp2(m_prev - m_new)
                acc_ref[hh, rows, :] = jnp.concatenate([alpha] * 2, axis=1) * acc_ref[hh, rows, :]
            return jnp.concatenate(ps, axis=0)

        def update(hh, p):
            cols = slice(hh * dh, (hh + 1) * dh)
            v1 = jnp.concatenate([v_ref[pl.ds(start, bq), cols], ones], axis=1)
            acc_ref[hh] = acc_ref[hh] + jnp.dot(p, v1, preferred_element_type=F32)

        s_next = scores(0)
        for hh in range(heads):
            s_cur = s_next
            if hh + 1 < heads:
                s_next = scores(hh + 1)
            update(hh, softmax(hh, s_cur))

    def body(j, carry):
        chunk(pl.multiple_of(j * bq, bq), False)
        return carry

    lax.fori_loop(0, qi, body, 0)
    chunk(pl.multiple_of(qi * bq, bq), True)
    for hh in range(heads):
        o_ref[:, hh * dh:(hh + 1) * dh] = (acc_ref[hh, :, 0:dh] / acc_ref[hh, :, dh:2 * dh]).astype(o_ref.dtype)


def flash_attention(qkv, cum_t, bq=512, sub=32, heads=4):
    T = qkv.shape[0]
    H, dh = FOX_HEADS, FOX_HEAD_DIM
    bq = min(bq, T)
    kern = functools.partial(_flash_kernel, scale=dh ** -0.5, sub=min(sub, bq))
    hw = heads * dh
    nhb = H // heads
    return pl.pallas_call(
        kern,
        grid=(nhb, T // bq),
        in_specs=[pl.BlockSpec((bq, hw), lambda h, i: (i, h)),
                  pl.BlockSpec((T, hw), lambda h, i: (0, nhb + h)),
                  pl.BlockSpec((T, hw), lambda h, i: (0, 2 * nhb + h)),
                  pl.BlockSpec((heads, 1, T), lambda h, i: (h, 0, 0)),
                  pl.BlockSpec((heads, 1, bq), lambda h, i: (h, 0, i))],
        out_specs=pl.BlockSpec((bq, hw), lambda h, i: (i, h)),
        out_shape=jax.ShapeDtypeStruct((T, H * dh), BF16),
        scratch_shapes=[pltpu.VMEM((heads, bq, LANES), F32), pltpu.VMEM((heads, bq, 2 * dh), F32)],
        compiler_params=_cparams(("parallel", "arbitrary"), 48),
        name="fox_attention",
    )(qkv, qkv, qkv, cum_t, cum_t)


def kernel(x, ab_norm, ab_w_in, ab_conv_w, s5_lambda_re, s5_lambda_im, s5_log_dt, s5_b_re, s5_b_im,
           s5_c_re, s5_c_im, s5_d, s5_w_glu, s5_b_glu, ab_w_out, c_norm, c_w_in, c_b_forget, c_w_out,
           ffn_norm, router_w_group, router_b_group, router_w_expert, router_b_expert,
           moe_w_gate, moe_w_up, moe_w_down, final_norm):
    bsz, L, D = x.shape
    depth = ffn_norm.shape[0]
    cw = ab_conv_w.shape[-1]
    xt = x.reshape(bsz * L, D)
    h = None
    xs_buf = jnp.zeros((moe_row_blocks(bsz * L) * MOE_BLOCK * D // LANES, LANES), F32)
    for i in range(depth):
        j = i // 2
        if i % 2 == 0:
            g_in = ab_norm[j]
            if h is not None:
                proj = matmul(h, ab_w_in[j], F32)
            else:
                proj = norm_matmul(xt, g_in, ab_w_in[j].astype(BF16), F32)
            y_conv = conv_mixer(proj, ab_conv_w[j])
            y_gelu = s5_mixer(proj, 3 * cw, s5_lambda_re[j], s5_lambda_im[j], s5_log_dt[j],
                              s5_b_re[j], s5_b_im[j], s5_c_re[j], s5_c_im[j], s5_d[j])
            xt, hf = glu_out(xt, y_conv, y_gelu, s5_w_glu[j].astype(BF16), s5_b_glu[j],
                             ab_w_out[j].astype(BF16), ffn_norm[i])
        else:
            assert h is not None, "an attention layer always follows a MoE combine that emits its norm"
            hd = FOX_HEADS * FOX_HEAD_DIM
            qkv = matmul(h, c_w_in[j], BF16, n_cols=3 * hd)
            w_f = jnp.pad(c_w_in[j][:, 3 * hd:], ((0, 0), (0, LANES - FOX_HEADS)))
            b_f = jnp.pad(c_b_forget[j], (0, LANES - FOX_HEADS)).reshape(1, LANES)
            cum = forget_cumsum(h, w_f, b_f)
            cum_t = cum[:FOX_HEADS].reshape(FOX_HEADS, 1, bsz * L)
            att = flash_attention(qkv, cum_t)
            xt, hf = proj_residual(xt, att, c_w_out[j].astype(BF16), ffn_norm[i])
        last = i == depth - 1
        if last:
            g_next = final_norm
        elif (i + 1) % 2 == 0:
            g_next = ab_norm[(i + 1) // 2]
        else:
            g_next = c_norm[(i + 1) // 2]
        xt, h, xs_buf = moe_layer(xt, hf, xs_buf, router_w_group[i], router_b_group[i], router_w_expert[i],
                                  router_b_expert[i], i, moe_w_gate, moe_w_up, moe_w_down,
                                  g_next, F32 if last else BF16)
    return h.reshape(bsz, L, D)
```

```python
import functools
import math

import jax
import jax.numpy as jnp
from jax import lax
from jax.experimental import pallas as pl
from jax.experimental.pallas import tpu as pltpu

F32 = jnp.float32
BF16 = jnp.bfloat16
I32 = jnp.int32
U32 = jnp.uint32

RMS_EPS = 1e-6
LANES = 128
LOG2E = math.log2(math.e)
MIB = 1024 * 1024

CONV_K = 3
S5_GROUP = 16
S5_STATE = 64
S5_CHUNK = 16
S5_OCT = LANES // S5_GROUP
FOX_HEADS = 16
FOX_HEAD_DIM = 128
N_GROUPS = 4
EXPERTS_PER_GROUP = 8
N_EXPERTS = N_GROUPS * EXPERTS_PER_GROUP
EXPERT_LANE0 = N_GROUPS
MOE_BLOCK = 256
MOE_CODE_BITS = 16
MOE_CODE = 1 << MOE_CODE_BITS


def _cparams(sem, vmem_mib):
    return pltpu.CompilerParams(dimension_semantics=sem, vmem_limit_bytes=vmem_mib * MIB)


def _bf16_pieces(x, n):
    pieces = []
    for _ in range(n):
        p = x.astype(BF16)
        pieces.append(p)
        x = x - p.astype(F32)
    return pieces


def _rms(x, g):
    ms = jnp.mean(x * x, axis=-1, keepdims=True)
    return x * lax.rsqrt(ms + RMS_EPS) * g


def _norm_matmul_kernel(x_ref, g_ref, w_ref, o_ref, h_ref):
    @pl.when(pl.program_id(1) == 0)
    def _():
        h_ref[...] = _rms(x_ref[...], g_ref[...]).astype(BF16)

    o_ref[...] = jnp.dot(h_ref[...], w_ref[...], preferred_element_type=F32).astype(o_ref.dtype)


def norm_matmul(x, g, w, out_dtype, tm=1024, tn=1024):
    T, D = x.shape
    N = w.shape[1]
    tm, tn = min(tm, T), min(tn, N)
    return pl.pallas_call(
        _norm_matmul_kernel,
        grid=(T // tm, N // tn),
        in_specs=[pl.BlockSpec((tm, D), lambda i, j: (i, 0)),
                  pl.BlockSpec((1, D), lambda i, j: (0, 0)),
                  pl.BlockSpec((D, tn), lambda i, j: (0, j))],
        out_specs=pl.BlockSpec((tm, tn), lambda i, j: (i, j)),
        out_shape=jax.ShapeDtypeStruct((T, N), out_dtype),
        scratch_shapes=[pltpu.VMEM((tm, D), BF16)],
        compiler_params=_cparams(("parallel", "arbitrary"), 56),
        name="norm_matmul",
    )(x, g.reshape(1, D), w)


def _matmul_kernel(a_ref, w_ref, o_ref, wb_ref, *, w_transposed):
    @pl.when(pl.program_id(1) == 0)
    def _():
        wb_ref[...] = w_ref[...].astype(BF16)

    contract = (((1,), (1 if w_transposed else 0,)), ((), ()))
    o_ref[...] = lax.dot_general(a_ref[...], wb_ref[...], contract,
                                 preferred_element_type=F32).astype(o_ref.dtype)


def matmul(a, w, out_dtype, n_cols=None, w_transposed=False, tm=1024, tn=1024):
    T, K = a.shape
    n_total = w.shape[0] if w_transposed else w.shape[1]
    N = n_total if n_cols is None else n_cols
    tm, tn = min(tm, T), min(tn, N)
    if w_transposed:
        w_spec = pl.BlockSpec((tn, K), lambda j, i: (j, 0))
        w_tile = (tn, K)
    else:
        w_spec = pl.BlockSpec((K, tn), lambda j, i: (0, j))
        w_tile = (K, tn)
    return pl.pallas_call(
        functools.partial(_matmul_kernel, w_transposed=w_transposed),
        grid=(N // tn, T // tm),
        in_specs=[pl.BlockSpec((tm, K), lambda j, i: (i, 0)), w_spec],
        out_specs=pl.BlockSpec((tm, tn), lambda j, i: (i, j)),
        out_shape=jax.ShapeDtypeStruct((T, N), out_dtype),
        scratch_shapes=[pltpu.VMEM(w_tile, BF16)],
        compiler_params=_cparams(("parallel", "arbitrary"), 48),
        name="matmul",
    )(a, w)


def _conv_kernel(gb_ref, gc_ref, u_ref, gcp_ref, up_ref, w_ref, o_ref):
    w0, w1, w2 = w_ref[0:1, :], w_ref[1:2, :], w_ref[2:3, :]
    v = gc_ref[...] * u_ref[...]
    y = w2 * v + w1 * pltpu.roll(v, 1, 0) + w0 * pltpu.roll(v, 2, 0)
    o_ref[...] = (gb_ref[...] * y).astype(o_ref.dtype)
    vp = gcp_ref[...] * up_ref[...]
    vp = jnp.where(pl.program_id(0) > 0, vp, jnp.zeros_like(vp))
    v8 = v[0:8, :]
    row = lax.broadcasted_iota(I32, v8.shape, 0)
    v1 = jnp.where(row < 1, pltpu.roll(vp, 1, 0), pltpu.roll(v8, 1, 0))
    v2 = jnp.where(row < 2, pltpu.roll(vp, 2, 0), pltpu.roll(v8, 2, 0))
    o_ref[0:8, :] = (gb_ref[0:8, :] * (w2 * v8 + w1 * v1 + w0 * v2)).astype(o_ref.dtype)


def conv_mixer(proj, conv_w, tm=512):
    T = proj.shape[0]
    CW = conv_w.shape[1]
    tm = min(tm, T)
    r8 = tm // 8
    cur = lambda c: pl.BlockSpec((tm, CW), lambda i: (i, c))
    prev = lambda c: pl.BlockSpec((8, CW), lambda i: (jnp.maximum(i * r8 - 1, 0), c))
    return pl.pallas_call(
        _conv_kernel,
        grid=(T // tm,),
        in_specs=[cur(0), cur(1), cur(2), prev(1), prev(2),
                  pl.BlockSpec((CONV_K, CW), lambda i: (0, 0))],
        out_specs=pl.BlockSpec((tm, CW), lambda i: (i, 0)),
        out_shape=jax.ShapeDtypeStruct((T, CW), BF16),
        compiler_params=_cparams(("parallel",), 40),
        name="conv_mixer",
    )(proj, proj, proj, proj, proj, conv_w)


def _cmul(a, b):
    return a[0] * b[0] - a[1] * b[1], a[0] * b[1] + a[1] * b[0]


def s5_matrices(lam_re, lam_im, log_dt, b_re, b_im, c_re, c_im):
    C = S5_CHUNK
    dt = jnp.exp(log_dt)[:, None]
    a, b = lam_re * dt, lam_im * dt
    mag = jnp.exp(a)
    lbar = (mag * jnp.cos(b), mag * jnp.sin(b))
    den = lam_re * lam_re + lam_im * lam_im
    inv_lam = (lam_re / den, -lam_im / den)
    coef = _cmul((lbar[0] - 1.0, lbar[1]), inv_lam)
    bbar = _cmul((coef[0][..., None], coef[1][..., None]), (b_re, b_im))
    j = jnp.arange(C + 1, dtype=F32)[None, :, None]
    pmag = jnp.exp(a[:, None, :] * j)
    pw = (pmag * jnp.cos(b[:, None, :] * j), pmag * jnp.sin(b[:, None, :] * j))
    return pw, bbar


def _lane_tile_blockdiag(m):
    Q, O, r, c = m.shape
    out = jnp.zeros((Q, O, r, O, c), m.dtype)
    for g in range(O):
        out = out.at[:, g, :, g, :].set(m[:, g])
    return out.reshape(Q, O * r, O * c)


def s5_lane_tile_operands(lam_re, lam_im, log_dt, b_re, b_im, c_re, c_im):
    C, H, N, O = S5_CHUNK, S5_GROUP, S5_STATE, S5_OCT
    pw, bbar = s5_matrices(lam_re, lam_im, log_dt, b_re, b_im, c_re, c_im)
    Q = lam_re.shape[0] // O
    pw_q = jnp.stack([p.reshape(Q, O, C + 1, N).transpose(0, 2, 1, 3).reshape(Q, C + 1, O * N) for p in pw],
                     axis=1)
    bb = jnp.stack([_lane_tile_blockdiag(m.transpose(0, 2, 1).reshape(Q, O, H, N)) for m in bbar], axis=1)
    cc = jnp.stack([_lane_tile_blockdiag(m.reshape(Q, O, H, N)) for m in (c_re, c_im)], axis=1)
    return pw_q, bb, cc


def _s5_kernel(u_ref, pw_ref, bb_ref, cc_ref, d_ref, o_ref, t_ref, w_ref, v_ref, uf_ref, s_ref, x_ref, c_ref):
    C = S5_CHUNK
    tb = u_ref.shape[0]
    nc = tb // C
    sw = pw_ref.shape[3]

    @pl.when((pl.program_id(0) == 0) & (pl.program_id(1) == 0))
    def _():
        t_ref[...] = jnp.zeros_like(t_ref)

    @pl.when(pl.program_id(1) == 0)
    def _():
        bb = (bb_ref[0, 0], bb_ref[0, 1])
        cc = (cc_ref[0, 0], cc_ref[0, 1])
        c_hi, c_lo = _bf16_pieces(jnp.concatenate([cc[0], -cc[1]], axis=1), 2)
        nt = (((1,), (1,)), ((), ()))
        for j in range(C):
            a = _cmul(bb, (pw_ref[0, 0, j:j + 1, :], pw_ref[0, 1, j:j + 1, :]))
            s = C - 1 - j
            w_ref[s * LANES:(s + 1) * LANES, 0:sw] = a[0].astype(BF16)
            w_ref[s * LANES:(s + 1) * LANES, sw:2 * sw] = a[1].astype(BF16)
            a_hi, a_lo = _bf16_pieces(jnp.concatenate(a, axis=1), 2)
            lag = (lax.dot_general(a_hi, c_hi, nt, preferred_element_type=F32)
                   + (lax.dot_general(a_hi, c_lo, nt, preferred_element_type=F32)
                      + lax.dot_general(a_lo, c_hi, nt, preferred_element_type=F32))).astype(BF16)
            for s0 in range(C - j):
                t_ref[s0 * LANES:(s0 + 1) * LANES, (s0 + j) * LANES:(s0 + j + 1) * LANES] = lag
            g = _cmul(cc, (pw_ref[0, 0, j + 1:j + 2, :], pw_ref[0, 1, j + 1:j + 2, :]))
            v_ref[j * LANES:(j + 1) * LANES, 0:sw] = g[0].astype(BF16)
            v_ref[j * LANES:(j + 1) * LANES, sw:2 * sw] = (-g[1]).astype(BF16)
        c_ref[...] = jnp.zeros_like(c_ref)

    for s in range(C):
        uf_ref[:, s * LANES:(s + 1) * LANES] = u_ref[pl.ds(s, nc, stride=C), :].astype(BF16)
    s_ref[...] = jnp.dot(uf_ref[...], w_ref[...], preferred_element_type=F32)
    lr, li = pw_ref[0, 0, C:C + 1, :], pw_ref[0, 1, C:C + 1, :]
    row = lax.broadcasted_iota(I32, (8, sw), 0)

    def tile_step(i, carry):
        xr, xi = carry
        r0 = pl.multiple_of(i * 8, 8)
        sr = s_ref[pl.ds(r0, 8), 0:sw]
        si = s_ref[pl.ds(r0, 8), sw:2 * sw]
        tr = jnp.zeros((8, sw), F32)
        ti = jnp.zeros((8, sw), F32)
        for r in range(8):
            tr = jnp.where(row == r, xr, tr)
            ti = jnp.where(row == r, xi, ti)
            xr, xi = lr * xr - li * xi + sr[r:r + 1, :], lr * xi + li * xr + si[r:r + 1, :]
        x_ref[pl.ds(r0, 8), 0:sw] = tr
        x_ref[pl.ds(r0, 8), sw:2 * sw] = ti
        return xr, xi

    xr, xi = lax.fori_loop(0, nc // 8, tile_step, (c_ref[0:1, 0:sw], c_ref[0:1, sw:2 * sw]))
    c_ref[0:1, 0:sw] = xr
    c_ref[0:1, sw:2 * sw] = xi
    y_state = lax.dot_general(x_ref[...].astype(BF16), v_ref[...], (((1,), (1,)), ((), ())),
                              preferred_element_type=F32)
    quarter = C // 4
    for qt in range(4):
        rows = (qt + 1) * quarter * LANES
        cols = slice(qt * quarter * LANES, (qt + 1) * quarter * LANES)
        y = jnp.dot(uf_ref[:, 0:rows], t_ref[0:rows, cols], preferred_element_type=F32) + y_state[:, cols]
        for s in range(quarter):
            st = qt * quarter + s
            ys = y[:, s * LANES:(s + 1) * LANES] + d_ref[...] * u_ref[pl.ds(st, nc, stride=C), :]
            o_ref[pl.ds(st, nc, stride=C), :] = jax.nn.gelu(ys)


def s5_mixer(proj, col0, lam_re, lam_im, log_dt, b_re, b_im, c_re, c_im, d, tb=8192):
    T = proj.shape[0]
    C, N, O = S5_CHUNK, S5_STATE, S5_OCT
    W = d.shape[0]
    Q = W // LANES
    tb = min(tb, T)
    pw_q, bb, cc = s5_lane_tile_operands(lam_re, lam_im, log_dt, b_re, b_im, c_re, c_im)
    cb0 = col0 // LANES
    nc = tb // C
    sw = O * N
    quad = lambda a: pl.BlockSpec((1,) + a.shape[1:], lambda q, t: (q, 0, 0, 0))
    return pl.pallas_call(
        _s5_kernel,
        grid=(Q, T // tb),
        in_specs=[pl.BlockSpec((tb, LANES), lambda q, t: (t, cb0 + q)),
                  quad(pw_q), quad(bb), quad(cc),
                  pl.BlockSpec((1, LANES), lambda q, t: (0, q))],
        out_specs=pl.BlockSpec((tb, LANES), lambda q, t: (t, q)),
        out_shape=jax.ShapeDtypeStruct((T, W), F32),
        scratch_shapes=[pltpu.VMEM((C * LANES, C * LANES), BF16),
                        pltpu.VMEM((C * LANES, 2 * sw), BF16),
                        pltpu.VMEM((C * LANES, 2 * sw), BF16),
                        pltpu.VMEM((nc, C * LANES), BF16),
                        pltpu.VMEM((nc, 2 * sw), F32),
                        pltpu.VMEM((nc, 2 * sw), F32),
                        pltpu.VMEM((8, 2 * sw), F32)],
        compiler_params=_cparams(("arbitrary", "arbitrary"), 56),
        name="s5_scan",
    )(proj, pw_q, bb, cc, d.reshape(1, W))


def _store_row_tiles(ref, val, lead=()):
    rows, width = val.shape
    n = width // LANES
    for j in range(n):
        ref[lead + (pl.ds(j, rows, stride=n), slice(None))] = val[:, j * LANES:(j + 1) * LANES]


def _load_row_tiles(ref, rows, n, lead=()):
    return jnp.concatenate([ref[lead + (pl.ds(j, rows, stride=n), slice(None))] for j in range(n)], axis=1)


def _row_tile_spec(tm, width, index_map):
    return pl.BlockSpec((tm * (width // LANES), LANES), index_map)


def _glu_out_kernel(x_ref, yc_ref, yg_ref, wglu_ref, bglu_ref, wout_ref, g_ref, xo_ref, ho_ref):
    yg = yg_ref[...]
    z = jnp.dot(yg.astype(BF16), wglu_ref[...], preferred_element_type=F32) + bglu_ref[...]
    ys = (yg * jax.nn.sigmoid(z)).astype(BF16)
    cw = yc_ref.shape[1]
    xn = (x_ref[...]
          + jnp.dot(yc_ref[...], wout_ref[0:cw, :], preferred_element_type=F32)
          + jnp.dot(ys, wout_ref[cw:, :], preferred_element_type=F32))
    xo_ref[...] = xn
    _store_row_tiles(ho_ref, _rms(xn, g_ref[...]))


def glu_out(x, y_conv, y_gelu, w_glu, b_glu, w_out, g_next, tm=512):
    T, D = x.shape
    CW, SW = y_conv.shape[1], y_gelu.shape[1]
    tm = min(tm, T)
    row = lambda w: pl.BlockSpec((tm, w), lambda i: (i, 0))
    full = lambda a: pl.BlockSpec(a.shape, lambda i: (0,) * a.ndim)
    bg, g2 = b_glu.reshape(1, SW), g_next.reshape(1, D)
    return pl.pallas_call(
        _glu_out_kernel,
        grid=(T // tm,),
        in_specs=[row(D), row(CW), row(SW), full(w_glu), full(bg), full(w_out), full(g2)],
        out_specs=[row(D), _row_tile_spec(tm, D, lambda i: (i, 0))],
        out_shape=[jax.ShapeDtypeStruct((T, D), F32), jax.ShapeDtypeStruct((T * D // LANES, LANES), F32)],
        compiler_params=_cparams(("parallel",), 56),
        name="glu_out",
    )(x, y_conv, y_gelu, w_glu, bg, w_out, g2)


def _proj_residual_kernel(x_ref, a_ref, w_ref, g_ref, xo_ref, ho_ref):
    xn = x_ref[...] + jnp.dot(a_ref[...], w_ref[...], preferred_element_type=F32)
    xo_ref[...] = xn
    _store_row_tiles(ho_ref, _rms(xn, g_ref[...]))


def proj_residual(x, a, w, g_next, tm=512):
    T, D = x.shape
    K = a.shape[1]
    tm = min(tm, T)
    row = lambda w_: pl.BlockSpec((tm, w_), lambda i: (i, 0))
    full = lambda arr: pl.BlockSpec(arr.shape, lambda i: (0,) * arr.ndim)
    g2 = g_next.reshape(1, D)
    return pl.pallas_call(
        _proj_residual_kernel,
        grid=(T // tm,),
        in_specs=[row(D), row(K), full(w), full(g2)],
        out_specs=[row(D), _row_tile_spec(tm, D, lambda i: (i, 0))],
        out_shape=[jax.ShapeDtypeStruct((T, D), F32), jax.ShapeDtypeStruct((T * D // LANES, LANES), F32)],
        compiler_params=_cparams(("parallel",), 56),
        name="proj_residual",
    )(x, a, w, g2)


def _router_kernel(h_ref, w_ref, b_ref, mi_ref, mf_ref, cnt_ref, run_ref):
    i = pl.program_id(0)

    @pl.when(i == 0)
    def _():
        run_ref[...] = jnp.zeros_like(run_ref)

    D = w_ref.shape[0]
    tm = h_ref.shape[0] * LANES // D
    h = _load_row_tiles(h_ref, tm, D // LANES)
    h_hi, h_lo = _bf16_pieces(h, 2)
    w_hi, w_lo = _bf16_pieces(w_ref[...], 2)
    logits = (jnp.dot(h_hi, w_hi, preferred_element_type=F32)
              + (jnp.dot(h_hi, w_lo, preferred_element_type=F32)
                 + jnp.dot(h_lo, w_hi, preferred_element_type=F32))) + b_ref[...]
    lane = lax.broadcasted_iota(I32, logits.shape, 1)
    neg = jnp.float32(-jnp.inf)
    gl = jnp.where(lane < N_GROUPS, logits, neg)
    gmax = jnp.max(gl, axis=1, keepdims=True)
    gsum = jnp.sum(jnp.where(lane < N_GROUPS, jnp.exp(gl - gmax), 0.0), axis=1, keepdims=True)
    gw = 1.0 / gsum
    gidx = jnp.min(jnp.where(gl == gmax, lane, LANES), axis=1, keepdims=True)
    lo = EXPERT_LANE0 + EXPERTS_PER_GROUP * gidx
    el = jnp.where((lane >= lo) & (lane < lo + EXPERTS_PER_GROUP), logits, neg)
    v1 = jnp.max(el, axis=1, keepdims=True)
    i1 = jnp.min(jnp.where(el == v1, lane, LANES), axis=1, keepdims=True)
    el2 = jnp.where(lane == i1, neg, el)
    v2 = jnp.max(el2, axis=1, keepdims=True)
    i2 = jnp.min(jnp.where(el2 == v2, lane, LANES), axis=1, keepdims=True)
    t = jnp.exp(v2 - v1)
    w1 = gw / (1.0 + t)
    w2 = gw * t / (1.0 + t)
    hit1 = lane == i1
    hit2 = lane == i2
    cnt = (hit1 | hit2).astype(BF16)
    r = lax.broadcasted_iota(I32, (tm, tm), 0)
    c = lax.broadcasted_iota(I32, (tm, tm), 1)
    before = (c < r).astype(BF16)
    cum = jnp.dot(before, cnt, preferred_element_type=F32) + run_ref[0:1, :]
    rank1 = jnp.sum(jnp.where(hit1, cum, 0.0), axis=1, keepdims=True).astype(I32)
    rank2 = jnp.sum(jnp.where(hit2, cum, 0.0), axis=1, keepdims=True).astype(I32)
    run = run_ref[0:1, :] + jnp.sum(cnt.astype(F32), axis=0, keepdims=True)
    run_ref[...] = jnp.broadcast_to(run, run_ref.shape)
    cnt_ref[...] = jnp.broadcast_to(run, cnt_ref.shape)
    code1 = (i1 - EXPERT_LANE0) * MOE_CODE + rank1
    code2 = (i2 - EXPERT_LANE0) * MOE_CODE + rank2
    mi_ref[...] = jnp.where(lane == 0, code1, jnp.where(lane == 1, code2, 0))
    mf_ref[...] = jnp.where(lane == 0, w1, jnp.where(lane == 1, w2, 0.0))


def router(h, w_r, b_r, tm=256):
    D = w_r.shape[0]
    T = h.shape[0] * LANES // D
    tm = min(tm, T)
    row = lambda w: pl.BlockSpec((tm, w), lambda i: (i, 0))
    full = lambda a: pl.BlockSpec(a.shape, lambda i: (0,) * a.ndim)
    return pl.pallas_call(
        _router_kernel,
        grid=(T // tm,),
        in_specs=[_row_tile_spec(tm, D, lambda i: (i, 0)), full(w_r), full(b_r)],
        out_specs=[row(LANES), row(LANES), pl.BlockSpec((8, LANES), lambda i: (0, 0))],
        out_shape=[jax.ShapeDtypeStruct((T, LANES), I32), jax.ShapeDtypeStruct((T, LANES), F32),
                   jax.ShapeDtypeStruct((8, LANES), F32)],
        scratch_shapes=[pltpu.VMEM((8, LANES), F32)],
        compiler_params=_cparams(("arbitrary",), 40),
        name="moe_router",
    )(h, w_r, b_r)


def moe_plan(meta_i, counts, n_blocks):
    B = MOE_BLOCK
    code1, code2 = meta_i[:, 0], meta_i[:, 1]
    cnt = counts[0, EXPERT_LANE0:EXPERT_LANE0 + N_EXPERTS].astype(I32)
    nblk = (cnt + B - 1) // B
    blk_end = jnp.cumsum(nblk)
    blk_off = blk_end - nblk
    b = jnp.arange(n_blocks, dtype=I32)
    total = blk_end[-1]
    owner = jnp.minimum(jnp.sum((blk_end[None, :] <= b[:, None]).astype(I32), axis=1), N_EXPERTS - 1)
    valid = b < total
    last = jnp.maximum(total - 1, 0)
    blk_expert = jnp.where(valid, owner, owner[last])
    blk_src = jnp.where(valid, b, last)
    blk_first = (valid & (b == blk_off[owner])).astype(I32)
    e = jnp.arange(N_EXPERTS, dtype=I32)
    used = nblk > 0
    slot = (jnp.cumsum(used.astype(I32)) - 1) % 2
    later = (e[None, :] > e[:, None]) & used[None, :]
    next_used = jnp.min(jnp.where(later, e[None, :], N_EXPERTS), axis=1)
    next_used = jnp.where(next_used < N_EXPERTS, next_used, -1)
    return (code1, code2, blk_off * B, blk_expert, blk_src, valid.astype(I32), blk_first,
            next_used[blk_expert], slot[blk_expert])


def _row_copy(src, dst, sem):
    return pltpu.make_async_copy(src, dst, sem)


def _sorted_row(code, off_ref):
    return off_ref[code >> MOE_CODE_BITS] + (code & (MOE_CODE - 1))


def _tile_rows(row, n):
    return pl.ds(pl.multiple_of(row * n, n), n)


def _dispatch_kernel(code1_ref, code2_ref, off_ref, hp_ref, xs_in_ref, xs_ref, sem, *, n):
    del xs_in_ref
    tm = hp_ref.shape[0] // n
    base = pl.program_id(0) * tm

    def copies(r):
        src = hp_ref.at[_tile_rows(r, n)]
        return (_row_copy(src, xs_ref.at[_tile_rows(_sorted_row(code1_ref[base + r], off_ref), n)], sem),
                _row_copy(src, xs_ref.at[_tile_rows(_sorted_row(code2_ref[base + r], off_ref), n)], sem))

    def start(r, carry):
        for cp in copies(r):
            cp.start()
        return carry

    lax.fori_loop(0, tm, start, 0, unroll=8)
    for _ in range(2):
        _row_copy(hp_ref, xs_ref.at[pl.ds(0, tm * n)], sem).wait()


def dispatch(hp, width, code1, code2, row_off, xs0, tm=256):
    n = width // LANES
    T = hp.shape[0] // n
    tm = min(tm, T)
    return pl.pallas_call(
        functools.partial(_dispatch_kernel, n=n),
        grid_spec=pltpu.PrefetchScalarGridSpec(
            num_scalar_prefetch=3,
            grid=(T // tm,),
            in_specs=[_row_tile_spec(tm, width, lambda i, c1, c2, off: (i, 0)),
                      pl.BlockSpec(memory_space=pl.ANY)],
            out_specs=pl.BlockSpec(memory_space=pl.ANY),
            scratch_shapes=[pltpu.SemaphoreType.DMA(())]),
        out_shape=jax.ShapeDtypeStruct(xs0.shape, hp.dtype),
        input_output_aliases={4: 0},
        compiler_params=_cparams(("arbitrary",), 32),
        name="moe_dispatch",
    )(code1, code2, row_off, hp, xs0)


def _experts_kernel(be_ref, bs_ref, bv_ref, bf_ref, bn_ref, bl_ref, xs_ref, wg_ref, wu_ref, wd_ref, ys_ref,
                    wg_f, wu_f, wd_f, wg_s, wu_s, wd_s, sem):
    del bs_ref
    b = pl.program_id(0)

    def weight_copies(e, slot):
        return (pltpu.make_async_copy(wg_ref.at[e], wg_f.at[slot], sem.at[slot]),
                pltpu.make_async_copy(wu_ref.at[e], wu_f.at[slot], sem.at[slot]),
                pltpu.make_async_copy(wd_ref.at[e], wd_f.at[slot], sem.at[slot]))

    @pl.when(bv_ref[b] == 0)
    def _():
        ys_ref[...] = jnp.zeros_like(ys_ref)

    @pl.when(bv_ref[b] > 0)
    def _():
        @pl.when(bf_ref[b] == 1)
        def _():
            slot = bl_ref[b]

            @pl.when(b == 0)
            def _():
                for cp in weight_copies(be_ref[b], slot):
                    cp.start()

            for cp in weight_copies(be_ref[b], slot):
                cp.wait()
            wg_s[...] = wg_f[slot].astype(BF16)
            wu_s[...] = wu_f[slot].astype(BF16)
            wd_s[...] = wd_f[slot].astype(BF16)

            @pl.when(bn_ref[b] >= 0)
            def _():
                for cp in weight_copies(bn_ref[b], 1 - slot):
                    cp.start()

        D = wg_s.shape[0]
        a = _load_row_tiles(xs_ref, MOE_BLOCK, D // LANES).astype(BF16)
        gate = jnp.dot(a, wg_s[...], preferred_element_type=F32)
        up = jnp.dot(a, wu_s[...], preferred_element_type=F32)
        mid = (jax.nn.silu(gate) * up).astype(BF16)
        _store_row_tiles(ys_ref, jnp.dot(mid, wd_s[...], preferred_element_type=F32))


def experts(xs, blk_expert, blk_src, blk_valid, blk_first, blk_next, blk_slot, w_gate, w_up, w_down):
    B = MOE_BLOCK
    E, D, FF = w_gate.shape
    nb = xs.shape[0] * LANES // D // B
    hbm = pl.BlockSpec(memory_space=pl.ANY)
    return pl.pallas_call(
        _experts_kernel,
        grid_spec=pltpu.PrefetchScalarGridSpec(
            num_scalar_prefetch=6,
            grid=(nb,),
            in_specs=[_row_tile_spec(B, D, lambda b, be, bs, bv, bf, bn, bl: (bs[b], 0)), hbm, hbm, hbm],
            out_specs=_row_tile_spec(B, D, lambda b, be, bs, bv, bf, bn, bl: (b, 0)),
            scratch_shapes=[pltpu.VMEM((2, D, FF), F32), pltpu.VMEM((2, D, FF), F32),
                            pltpu.VMEM((2, FF, D), F32),
                            pltpu.VMEM((D, FF), BF16), pltpu.VMEM((D, FF), BF16),
                            pltpu.VMEM((FF, D), BF16), pltpu.SemaphoreType.DMA((2,))]),
        out_shape=jax.ShapeDtypeStruct(xs.shape, F32),
        compiler_params=_cparams(("arbitrary",), 56),
        name="moe_experts",
    )(blk_expert, blk_src, blk_valid, blk_first, blk_next, blk_slot, xs, w_gate, w_up, w_down)


def _combine_kernel(code1_ref, code2_ref, off_ref, x_ref, mf_ref, g_ref, ys_ref, xo_ref, ho_ref, buf, sem):
    tm, D = x_ref.shape
    n = D // LANES
    i = pl.program_id(0)
    slot = i % 2

    def fetch(step, dst_slot):
        base = step * tm

        def start(r, carry):
            dst = _tile_rows(r, n)
            _row_copy(ys_ref.at[_tile_rows(_sorted_row(code1_ref[base + r], off_ref), n)],
                      buf.at[dst_slot, 0, dst], sem.at[dst_slot]).start()
            _row_copy(ys_ref.at[_tile_rows(_sorted_row(code2_ref[base + r], off_ref), n)],
                      buf.at[dst_slot, 1, dst], sem.at[dst_slot]).start()
            return carry

        lax.fori_loop(0, tm, start, 0, unroll=8)

    @pl.when(i == 0)
    def _():
        fetch(0, 0)

    @pl.when(i + 1 < pl.num_programs(0))
    def _():
        fetch(i + 1, 1 - slot)

    for k in range(2):
        _row_copy(ys_ref.at[pl.ds(0, tm * n)], buf.at[slot, k], sem.at[slot]).wait()
    mf = mf_ref[...]
    xn = (x_ref[...] + mf[:, 0:1] * _load_row_tiles(buf, tm, n, lead=(slot, 0))
          + mf[:, 1:2] * _load_row_tiles(buf, tm, n, lead=(slot, 1)))
    xo_ref[...] = xn
    ho_ref[...] = _rms(xn, g_ref[...]).astype(ho_ref.dtype)


def combine(x, ys, meta_f, code1, code2, row_off, g_next, h_dtype, tm=256):
    T, D = x.shape
    tm = min(tm, T)
    row = lambda w: pl.BlockSpec((tm, w), lambda i, c1, c2, off: (i, 0))
    return pl.pallas_call(
        _combine_kernel,
        grid_spec=pltpu.PrefetchScalarGridSpec(
            num_scalar_prefetch=3,
            grid=(T // tm,),
            in_specs=[row(D), row(LANES), pl.BlockSpec((1, D), lambda i, c1, c2, off: (0, 0)),
                      pl.BlockSpec(memory_space=pl.ANY)],
            out_specs=[row(D), row(D)],
            scratch_shapes=[pltpu.VMEM((2, 2, tm * D // LANES, LANES), F32),
                            pltpu.SemaphoreType.DMA((2,))]),
        out_shape=[jax.ShapeDtypeStruct((T, D), F32), jax.ShapeDtypeStruct((T, D), h_dtype)],
        compiler_params=_cparams(("arbitrary",), 40),
        name="moe_combine",
    )(code1, code2, row_off, x, meta_f, g_next.reshape(1, D), ys)


def moe_row_blocks(T):
    return (2 * T) // MOE_BLOCK + N_EXPERTS


def moe_layer(x, h, xs_buf, w_group, b_group, w_expert, b_expert, layer, w_gate, w_up, w_down, g_next,
              h_dtype):
    T, D = x.shape
    assert T <= MOE_CODE, "ranks inside one expert must fit the packed (expert, rank) code"
    n_blocks = xs_buf.shape[0] * LANES // D // MOE_BLOCK
    pad = LANES - N_GROUPS - N_EXPERTS
    w_r = jnp.concatenate([w_group] + [w_expert[:, g, :] for g in range(N_GROUPS)]
                          + [jnp.zeros((D, pad), F32)], axis=1)
    b_r = jnp.concatenate([b_group, b_expert.reshape(N_EXPERTS), jnp.zeros((pad,), F32)]).reshape(1, LANES)
    meta_i, meta_f, counts = router(h, w_r, b_r)
    (code1, code2, row_off, blk_expert, blk_src, blk_valid, blk_first, blk_next,
     blk_slot) = moe_plan(meta_i, counts, n_blocks)
    xs = dispatch(h, D, code1, code2, row_off, xs_buf)
    FF = w_gate.shape[-1]
    first_row = layer * N_EXPERTS
    ys = experts(xs, blk_expert + first_row, blk_src, blk_valid, blk_first,
                 jnp.where(blk_next >= 0, blk_next + first_row, -1), blk_slot,
                 w_gate.reshape(-1, D, FF), w_up.reshape(-1, D, FF), w_down.reshape(-1, FF, D))
    xo, ho = combine(x, ys, meta_f, code1, code2, row_off, g_next, h_dtype)
    return xo, ho, xs


def _forget_kernel(h_ref, w_ref, b_ref, o_ref, run_ref):
    @pl.when(pl.program_id(0) == 0)
    def _():
        run_ref[...] = jnp.zeros_like(run_ref)

    h = h_ref[...]
    nt = (((1,), (1,)), ((), ()))
    z = sum(lax.dot_general(h, w, nt, preferred_element_type=F32)
            for w in _bf16_pieces(w_ref[...], 2)) + b_ref[...]
    log_f = jax.nn.log_sigmoid(z)
    tm = z.shape[0]
    r = lax.broadcasted_iota(I32, (tm, tm), 0)
    c = lax.broadcasted_iota(I32, (tm, tm), 1)
    upto = (c <= r).astype(BF16)
    cum = sum(jnp.dot(upto, p, preferred_element_type=F32) for p in _bf16_pieces(log_f, 3)) + run_ref[0:1, :]
    o_ref[...] = cum.T
    run_ref[...] = jnp.broadcast_to(cum[tm - 1:tm, :], run_ref.shape)


def forget_cumsum(h, w_f, b_f, tm=256):
    T, D = h.shape
    tm = min(tm, T)
    return pl.pallas_call(
        _forget_kernel,
        grid=(T // tm,),
        in_specs=[pl.BlockSpec((tm, D), lambda i: (i, 0)),
                  pl.BlockSpec((LANES, D), lambda i: (0, 0)),
                  pl.BlockSpec((1, LANES), lambda i: (0, 0))],
        out_specs=pl.BlockSpec((LANES, tm), lambda i: (0, i)),
        out_shape=jax.ShapeDtypeStruct((LANES, T), F32),
        scratch_shapes=[pltpu.VMEM((8, LANES), F32)],
        compiler_params=_cparams(("arbitrary",), 40),
        name="forget_cumsum",
    )(h, w_f, b_f)


def _flash_kernel(q_ref, k_ref, v_ref, ck_ref, cq_ref, o_ref, m_ref, acc_ref, *, scale, sub):
    qi = pl.program_id(1)
    bq = q_ref.shape[0]
    dh = FOX_HEAD_DIM
    heads = q_ref.shape[1] // dh
    m_ref[...] = jnp.full_like(m_ref, -jnp.inf)
    acc_ref[...] = jnp.zeros_like(acc_ref)
    ones = jnp.ones((bq, dh), BF16)
    n_sub = bq // sub

    def chunk(start, diagonal):
        def scores(hh):
            cols = slice(hh * dh, (hh + 1) * dh)
            k = k_ref[pl.ds(start, bq), cols]
            return lax.dot_general(q_ref[:, cols], k, (((1,), (1,)), ((), ())), preferred_element_type=F32)

        def softmax(hh, s_all):
            c0 = cq_ref[hh, :, 0:1]
            bias = (c0 - ck_ref[hh, :, pl.ds(start, bq)]) * LOG2E
            ps = []
            for r in range(n_sub):
                rows = slice(r * sub, (r + 1) * sub)
                s = s_all[rows, :] * (scale * LOG2E) + bias
                if diagonal:
                    row = lax.broadcasted_iota(I32, s.shape, 0) + r * sub
                    col = lax.broadcasted_iota(I32, s.shape, 1)
                    s = jnp.where(col <= row, s, -jnp.inf)
                m_prev = m_ref[hh, rows, :]
                m_new = jnp.maximum(m_prev, jnp.max(s, axis=1, keepdims=True))
                m_ref[hh, rows, :] = m_new
                ps.append(jnp.exp2(s - jnp.concatenate([m_new] * (bq // LANES), axis=1)).astype(BF16))
                alpha = jnp.exp2(m_prev - m_new)
                acc_ref[hh, rows, :] = jnp.concatenate([alpha] * 2, axis=1) * acc_ref[hh, rows, :]
            return jnp.concatenate(ps, axis=0)

        def update(hh, p):
            cols = slice(hh * dh, (hh + 1) * dh)
            v1 = jnp.concatenate([v_ref[pl.ds(start, bq), cols], ones], axis=1)
            acc_ref[hh] = acc_ref[hh] + jnp.dot(p, v1, preferred_element_type=F32)

        s_next = scores(0)
        for hh in range(heads):
            s_cur = s_next
            if hh + 1 < heads:
                s_next = scores(hh + 1)
            update(hh, softmax(hh, s_cur))

    def body(j, carry):
        chunk(pl.multiple_of(j * bq, bq), False)
        return carry

    lax.fori_loop(0, qi, body, 0)
    chunk(pl.multiple_of(qi * bq, bq), True)
    for hh in range(heads):
        o_ref[:, hh * dh:(hh + 1) * dh] = (acc_ref[hh, :, 0:dh] / acc_ref[hh, :, dh:2 * dh]).astype(o_ref.dtype)


def flash_attention(qkv, cum_t, bq=512, sub=32, heads=4):
    T = qkv.shape[0]
    H, dh = FOX_HEADS, FOX_HEAD_DIM
    bq = min(bq, T)
    kern = functools.partial(_flash_kernel, scale=dh ** -0.5, sub=min(sub, bq))
    hw = heads * dh
    nhb = H // heads
    return pl.pallas_call(
        kern,
        grid=(nhb, T // bq),
        in_specs=[pl.BlockSpec((bq, hw), lambda h, i: (i, h)),
                  pl.BlockSpec((T, hw), lambda h, i: (0, nhb + h)),
                  pl.BlockSpec((T, hw), lambda h, i: (0, 2 * nhb + h)),
                  pl.BlockSpec((heads, 1, T), lambda h, i: (h, 0, 0)),
                  pl.BlockSpec((heads, 1, bq), lambda h, i: (h, 0, i))],
        out_specs=pl.BlockSpec((bq, hw), lambda h, i: (i, h)),
        out_shape=jax.ShapeDtypeStruct((T, H * dh), BF16),
        scratch_shapes=[pltpu.VMEM((heads, bq, LANES), F32), pltpu.VMEM((heads, bq, 2 * dh), F32)],
        compiler_params=_cparams(("parallel", "arbitrary"), 48),
        name="fox_attention",
    )(qkv, qkv, qkv, cum_t, cum_t)


def kernel(x, ab_norm, ab_w_in, ab_conv_w, s5_lambda_re, s5_lambda_im, s5_log_dt, s5_b_re, s5_b_im,
           s5_c_re, s5_c_im, s5_d, s5_w_glu, s5_b_glu, ab_w_out, c_norm, c_w_in, c_b_forget, c_w_out,
           ffn_norm, router_w_group, router_b_group, router_w_expert, router_b_expert,
           moe_w_gate, moe_w_up, moe_w_down, final_norm):
    bsz, L, D = x.shape
    depth = ffn_norm.shape[0]
    cw = ab_conv_w.shape[-1]
    xt = x.reshape(bsz * L, D)
    h = None
    xs_buf = jnp.zeros((moe_row_blocks(bsz * L) * MOE_BLOCK * D // LANES, LANES), F32)
    for i in range(depth):
        j = i // 2
        if i % 2 == 0:
            g_in = ab_norm[j]
            if h is not None:
                proj = matmul(h, ab_w_in[j], F32)
            else:
                proj = norm_matmul(xt, g_in, ab_w_in[j].astype(BF16), F32)
            y_conv = conv_mixer(proj, ab_conv_w[j])
            y_gelu = s5_mixer(proj, 3 * cw, s5_lambda_re[j], s5_lambda_im[j], s5_log_dt[j],
                              s5_b_re[j], s5_b_im[j], s5_c_re[j], s5_c_im[j], s5_d[j])
            xt, hf = glu_out(xt, y_conv, y_gelu, s5_w_glu[j].astype(BF16), s5_b_glu[j],
                             ab_w_out[j].astype(BF16), ffn_norm[i])
        else:
            assert h is not None, "an attention layer always follows a MoE combine that emits its norm"
            hd = FOX_HEADS * FOX_HEAD_DIM
            w_t = c_w_in[j].T
            qkv = matmul(h, w_t, BF16, n_cols=3 * hd, w_transposed=True)
            w_f = jnp.pad(w_t[3 * hd:], ((0, LANES - FOX_HEADS), (0, 0)))
            b_f = jnp.pad(c_b_forget[j], (0, LANES - FOX_HEADS)).reshape(1, LANES)
            cum = forget_cumsum(h, w_f, b_f)
            cum_t = cum[:FOX_HEADS].reshape(FOX_HEADS, 1, bsz * L)
            att = flash_attention(qkv, cum_t)
            xt, hf = proj_residual(xt, att, c_w_out[j].astype(BF16), ffn_norm[i])
        last = i == depth - 1
        if last:
            g_next = final_norm
        elif (i + 1) % 2 == 0:
            g_next = ab_norm[(i + 1) // 2]
        else:
            g_next = c_norm[(i + 1) // 2]
        xt, h, xs_buf = moe_layer(xt, hf, xs_buf, router_w_group[i], router_b_group[i], router_w_expert[i],
                                  router_b_expert[i], i, moe_w_gate, moe_w_up, moe_w_down,
                                  g_next, F32 if last else BF16)
    return h.reshape(bsz, L, D)
```

```python
import functools
import math

import jax
import jax.numpy as jnp
from jax import lax
from jax.experimental import pallas as pl
from jax.experimental.pallas import tpu as pltpu

F32 = jnp.float32
BF16 = jnp.bfloat16
I32 = jnp.int32
U32 = jnp.uint32

RMS_EPS = 1e-6
LANES = 128
LOG2E = math.log2(math.e)
MIB = 1024 * 1024

CONV_K = 3
S5_GROUP = 16
S5_STATE = 64
S5_CHUNK = 16
S5_OCT = LANES // S5_GROUP
FOX_HEADS = 16
FOX_HEAD_DIM = 128
N_GROUPS = 4
EXPERTS_PER_GROUP = 8
N_EXPERTS = N_GROUPS * EXPERTS_PER_GROUP
EXPERT_LANE0 = N_GROUPS
MOE_BLOCK = 256
MOE_CODE_BITS = 16
MOE_CODE = 1 << MOE_CODE_BITS


def _cparams(sem, vmem_mib):
    return pltpu.CompilerParams(dimension_semantics=sem, vmem_limit_bytes=vmem_mib * MIB)


def _bf16_pieces(x, n):
    pieces = []
    for _ in range(n):
        p = x.astype(BF16)
        pieces.append(p)
        x = x - p.astype(F32)
    return pieces


def _rms(x, g):
    ms = jnp.mean(x * x, axis=-1, keepdims=True)
    return x * lax.rsqrt(ms + RMS_EPS) * g


def _norm_matmul_kernel(x_ref, g_ref, w_ref, o_ref, h_ref):
    @pl.when(pl.program_id(1) == 0)
    def _():
        h_ref[...] = _rms(x_ref[...], g_ref[...]).astype(BF16)

    o_ref[...] = jnp.dot(h_ref[...], w_ref[...], preferred_element_type=F32).astype(o_ref.dtype)


def norm_matmul(x, g, w, out_dtype, tm=1024, tn=1024):
    T, D = x.shape
    N = w.shape[1]
    tm, tn = min(tm, T), min(tn, N)
    return pl.pallas_call(
        _norm_matmul_kernel,
        grid=(T // tm, N // tn),
        in_specs=[pl.BlockSpec((tm, D), lambda i, j: (i, 0)),
                  pl.BlockSpec((1, D), lambda i, j: (0, 0)),
                  pl.BlockSpec((D, tn), lambda i, j: (0, j))],
        out_specs=pl.BlockSpec((tm, tn), lambda i, j: (i, j)),
        out_shape=jax.ShapeDtypeStruct((T, N), out_dtype),
        scratch_shapes=[pltpu.VMEM((tm, D), BF16)],
        compiler_params=_cparams(("parallel", "arbitrary"), 56),
        name="norm_matmul",
    )(x, g.reshape(1, D), w)


def _matmul_kernel(a_ref, w_ref, o_ref, wb_ref, *, w_transposed):
    @pl.when(pl.program_id(1) == 0)
    def _():
        wb_ref[...] = w_ref[...].astype(BF16)

    contract = (((1,), (1 if w_transposed else 0,)), ((), ()))
    o_ref[...] = lax.dot_general(a_ref[...], wb_ref[...], contract,
                                 preferred_element_type=F32).astype(o_ref.dtype)


def matmul(a, w, out_dtype, n_cols=None, w_transposed=False, tm=1024, tn=1024):
    T, K = a.shape
    n_total = w.shape[0] if w_transposed else w.shape[1]
    N = n_total if n_cols is None else n_cols
    tm, tn = min(tm, T), min(tn, N)
    if w_transposed:
        w_spec = pl.BlockSpec((tn, K), lambda j, i: (j, 0))
        w_tile = (tn, K)
    else:
        w_spec = pl.BlockSpec((K, tn), lambda j, i: (0, j))
        w_tile = (K, tn)
    return pl.pallas_call(
        functools.partial(_matmul_kernel, w_transposed=w_transposed),
        grid=(N // tn, T // tm),
        in_specs=[pl.BlockSpec((tm, K), lambda j, i: (i, 0)), w_spec],
        out_specs=pl.BlockSpec((tm, tn), lambda j, i: (i, j)),
        out_shape=jax.ShapeDtypeStruct((T, N), out_dtype),
        scratch_shapes=[pltpu.VMEM(w_tile, BF16)],
        compiler_params=_cparams(("parallel", "arbitrary"), 48),
        name="matmul",
    )(a, w)


def _conv_kernel(gb_ref, gc_ref, u_ref, gcp_ref, up_ref, w_ref, o_ref):
    w0, w1, w2 = w_ref[0:1, :], w_ref[1:2, :], w_ref[2:3, :]
    v = gc_ref[...] * u_ref[...]
    y = w2 * v + w1 * pltpu.roll(v, 1, 0) + w0 * pltpu.roll(v, 2, 0)
    o_ref[...] = (gb_ref[...] * y).astype(o_ref.dtype)
    vp = gcp_ref[...] * up_ref[...]
    vp = jnp.where(pl.program_id(0) > 0, vp, jnp.zeros_like(vp))
    v8 = v[0:8, :]
    row = lax.broadcasted_iota(I32, v8.shape, 0)
    v1 = jnp.where(row < 1, pltpu.roll(vp, 1, 0), pltpu.roll(v8, 1, 0))
    v2 = jnp.where(row < 2, pltpu.roll(vp, 2, 0), pltpu.roll(v8, 2, 0))
    o_ref[0:8, :] = (gb_ref[0:8, :] * (w2 * v8 + w1 * v1 + w0 * v2)).astype(o_ref.dtype)


def conv_mixer(proj, conv_w, tm=512):
    T = proj.shape[0]
    CW = conv_w.shape[1]
    tm = min(tm, T)
    r8 = tm // 8
    cur = lambda c: pl.BlockSpec((tm, CW), lambda i: (i, c))
    prev = lambda c: pl.BlockSpec((8, CW), lambda i: (jnp.maximum(i * r8 - 1, 0), c))
    return pl.pallas_call(
        _conv_kernel,
        grid=(T // tm,),
        in_specs=[cur(0), cur(1), cur(2), prev(1), prev(2),
                  pl.BlockSpec((CONV_K, CW), lambda i: (0, 0))],
        out_specs=pl.BlockSpec((tm, CW), lambda i: (i, 0)),
        out_shape=jax.ShapeDtypeStruct((T, CW), BF16),
        compiler_params=_cparams(("parallel",), 40),
        name="conv_mixer",
    )(proj, proj, proj, proj, proj, conv_w)


def _cmul(a, b):
    return a[0] * b[0] - a[1] * b[1], a[0] * b[1] + a[1] * b[0]


def s5_matrices(lam_re, lam_im, log_dt, b_re, b_im, c_re, c_im):
    C = S5_CHUNK
    dt = jnp.exp(log_dt)[:, None]
    a, b = lam_re * dt, lam_im * dt
    mag = jnp.exp(a)
    lbar = (mag * jnp.cos(b), mag * jnp.sin(b))
    den = lam_re * lam_re + lam_im * lam_im
    inv_lam = (lam_re / den, -lam_im / den)
    coef = _cmul((lbar[0] - 1.0, lbar[1]), inv_lam)
    bbar = _cmul((coef[0][..., None], coef[1][..., None]), (b_re, b_im))
    j = jnp.arange(C + 1, dtype=F32)[None, :, None]
    pmag = jnp.exp(a[:, None, :] * j)
    pw = (pmag * jnp.cos(b[:, None, :] * j), pmag * jnp.sin(b[:, None, :] * j))
    return pw, bbar


def _lane_tile_blockdiag(m):
    Q, O, r, c = m.shape
    out = jnp.zeros((Q, O, r, O, c), m.dtype)
    for g in range(O):
        out = out.at[:, g, :, g, :].set(m[:, g])
    return out.reshape(Q, O * r, O * c)


def s5_lane_tile_operands(lam_re, lam_im, log_dt, b_re, b_im, c_re, c_im):
    C, H, N, O = S5_CHUNK, S5_GROUP, S5_STATE, S5_OCT
    pw, bbar = s5_matrices(lam_re, lam_im, log_dt, b_re, b_im, c_re, c_im)
    Q = lam_re.shape[0] // O
    pw_q = jnp.stack([p.reshape(Q, O, C + 1, N).transpose(0, 2, 1, 3).reshape(Q, C + 1, O * N) for p in pw],
                     axis=1)
    bb = jnp.stack([_lane_tile_blockdiag(m.transpose(0, 2, 1).reshape(Q, O, H, N)) for m in bbar], axis=1)
    cc = jnp.stack([_lane_tile_blockdiag(m.reshape(Q, O, H, N)) for m in (c_re, c_im)], axis=1)
    return pw_q, bb, cc


def _s5_kernel(u_ref, pw_ref, bb_ref, cc_ref, d_ref, o_ref, t_ref, w_ref, v_ref, uf_ref, s_ref, x_ref, c_ref):
    C = S5_CHUNK
    tb = u_ref.shape[0]
    nc = tb // C
    sw = pw_ref.shape[3]

    @pl.when((pl.program_id(0) == 0) & (pl.program_id(1) == 0))
    def _():
        t_ref[...] = jnp.zeros_like(t_ref)

    @pl.when(pl.program_id(1) == 0)
    def _():
        bb = (bb_ref[0, 0], bb_ref[0, 1])
        cc = (cc_ref[0, 0], cc_ref[0, 1])
        c_hi, c_lo = _bf16_pieces(jnp.concatenate([cc[0], -cc[1]], axis=1), 2)
        nt = (((1,), (1,)), ((), ()))
        for j in range(C):
            a = _cmul(bb, (pw_ref[0, 0, j:j + 1, :], pw_ref[0, 1, j:j + 1, :]))
            s = C - 1 - j
            w_ref[s * LANES:(s + 1) * LANES, 0:sw] = a[0].astype(BF16)
            w_ref[s * LANES:(s + 1) * LANES, sw:2 * sw] = a[1].astype(BF16)
            a_hi, a_lo = _bf16_pieces(jnp.concatenate(a, axis=1), 2)
            lag = (lax.dot_general(a_hi, c_hi, nt, preferred_element_type=F32)
                   + (lax.dot_general(a_hi, c_lo, nt, preferred_element_type=F32)
                      + lax.dot_general(a_lo, c_hi, nt, preferred_element_type=F32))).astype(BF16)
            for s0 in range(C - j):
                t_ref[s0 * LANES:(s0 + 1) * LANES, (s0 + j) * LANES:(s0 + j + 1) * LANES] = lag
            g = _cmul(cc, (pw_ref[0, 0, j + 1:j + 2, :], pw_ref[0, 1, j + 1:j + 2, :]))
            v_ref[j * LANES:(j + 1) * LANES, 0:sw] = g[0].astype(BF16)
            v_ref[j * LANES:(j + 1) * LANES, sw:2 * sw] = (-g[1]).astype(BF16)
        c_ref[...] = jnp.zeros_like(c_ref)

    for s in range(C):
        uf_ref[:, s * LANES:(s + 1) * LANES] = u_ref[pl.ds(s, nc, stride=C), :].astype(BF16)
    s_ref[...] = jnp.dot(uf_ref[...], w_ref[...], preferred_element_type=F32)
    lr, li = pw_ref[0, 0, C:C + 1, :], pw_ref[0, 1, C:C + 1, :]
    row = lax.broadcasted_iota(I32, (8, sw), 0)

    def tile_step(i, carry):
        xr, xi = carry
        r0 = pl.multiple_of(i * 8, 8)
        sr = s_ref[pl.ds(r0, 8), 0:sw]
        si = s_ref[pl.ds(r0, 8), sw:2 * sw]
        tr = jnp.zeros((8, sw), F32)
        ti = jnp.zeros((8, sw), F32)
        for r in range(8):
            tr = jnp.where(row == r, xr, tr)
            ti = jnp.where(row == r, xi, ti)
            xr, xi = lr * xr - li * xi + sr[r:r + 1, :], lr * xi + li * xr + si[r:r + 1, :]
        x_ref[pl.ds(r0, 8), 0:sw] = tr
        x_ref[pl.ds(r0, 8), sw:2 * sw] = ti
        return xr, xi

    xr, xi = lax.fori_loop(0, nc // 8, tile_step, (c_ref[0:1, 0:sw], c_ref[0:1, sw:2 * sw]))
    c_ref[0:1, 0:sw] = xr
    c_ref[0:1, sw:2 * sw] = xi
    y_state = lax.dot_general(x_ref[...].astype(BF16), v_ref[...], (((1,), (1,)), ((), ())),
                              preferred_element_type=F32)
    quarter = C // 4
    for qt in range(4):
        rows = (qt + 1) * quarter * LANES
        cols = slice(qt * quarter * LANES, (qt + 1) * quarter * LANES)
        y = jnp.dot(uf_ref[:, 0:rows], t_ref[0:rows, cols], preferred_element_type=F32) + y_state[:, cols]
        for s in range(quarter):
            st = qt * quarter + s
            ys = y[:, s * LANES:(s + 1) * LANES] + d_ref[...] * u_ref[pl.ds(st, nc, stride=C), :]
            o_ref[pl.ds(st, nc, stride=C), :] = jax.nn.gelu(ys)


def s5_mixer(proj, col0, lam_re, lam_im, log_dt, b_re, b_im, c_re, c_im, d, tb=8192):
    T = proj.shape[0]
    C, N, O = S5_CHUNK, S5_STATE, S5_OCT
    W = d.shape[0]
    Q = W // LANES
    tb = min(tb, T)
    pw_q, bb, cc = s5_lane_tile_operands(lam_re, lam_im, log_dt, b_re, b_im, c_re, c_im)
    cb0 = col0 // LANES
    nc = tb // C
    sw = O * N
    quad = lambda a: pl.BlockSpec((1,) + a.shape[1:], lambda q, t: (q, 0, 0, 0))
    return pl.pallas_call(
        _s5_kernel,
        grid=(Q, T // tb),
        in_specs=[pl.BlockSpec((tb, LANES), lambda q, t: (t, cb0 + q)),
                  quad(pw_q), quad(bb), quad(cc),
                  pl.BlockSpec((1, LANES), lambda q, t: (0, q))],
        out_specs=pl.BlockSpec((tb, LANES), lambda q, t: (t, q)),
        out_shape=jax.ShapeDtypeStruct((T, W), F32),
        scratch_shapes=[pltpu.VMEM((C * LANES, C * LANES), BF16),
                        pltpu.VMEM((C * LANES, 2 * sw), BF16),
                        pltpu.VMEM((C * LANES, 2 * sw), BF16),
                        pltpu.VMEM((nc, C * LANES), BF16),
                        pltpu.VMEM((nc, 2 * sw), F32),
                        pltpu.VMEM((nc, 2 * sw), F32),
                        pltpu.VMEM((8, 2 * sw), F32)],
        compiler_params=_cparams(("arbitrary", "arbitrary"), 56),
        name="s5_scan",
    )(proj, pw_q, bb, cc, d.reshape(1, W))


def _store_row_tiles(ref, val, lead=()):
    rows, width = val.shape
    n = width // LANES
    for j in range(n):
        ref[lead + (pl.ds(j, rows, stride=n), slice(None))] = val[:, j * LANES:(j + 1) * LANES]


def _load_row_tiles(ref, rows, n, lead=()):
    return jnp.concatenate([ref[lead + (pl.ds(j, rows, stride=n), slice(None))] for j in range(n)], axis=1)


def _row_tile_spec(tm, width, index_map):
    return pl.BlockSpec((tm * (width // LANES), LANES), index_map)


def _glu_out_kernel(x_ref, yc_ref, yg_ref, wglu_ref, bglu_ref, wout_ref, g_ref, xo_ref, ho_ref):
    yg = yg_ref[...]
    z = jnp.dot(yg.astype(BF16), wglu_ref[...], preferred_element_type=F32) + bglu_ref[...]
    ys = (yg * jax.nn.sigmoid(z)).astype(BF16)
    cw = yc_ref.shape[1]
    xn = (x_ref[...]
          + jnp.dot(yc_ref[...], wout_ref[0:cw, :], preferred_element_type=F32)
          + jnp.dot(ys, wout_ref[cw:, :], preferred_element_type=F32))
    xo_ref[...] = xn
    _store_row_tiles(ho_ref, _rms(xn, g_ref[...]))


def glu_out(x, y_conv, y_gelu, w_glu, b_glu, w_out, g_next, tm=512):
    T, D = x.shape
    CW, SW = y_conv.shape[1], y_gelu.shape[1]
    tm = min(tm, T)
    row = lambda w: pl.BlockSpec((tm, w), lambda i: (i, 0))
    full = lambda a: pl.BlockSpec(a.shape, lambda i: (0,) * a.ndim)
    bg, g2 = b_glu.reshape(1, SW), g_next.reshape(1, D)
    return pl.pallas_call(
        _glu_out_kernel,
        grid=(T // tm,),
        in_specs=[row(D), row(CW), row(SW), full(w_glu), full(bg), full(w_out), full(g2)],
        out_specs=[row(D), _row_tile_spec(tm, D, lambda i: (i, 0))],
        out_shape=[jax.ShapeDtypeStruct((T, D), F32), jax.ShapeDtypeStruct((T * D // LANES, LANES), F32)],
        compiler_params=_cparams(("parallel",), 56),
        name="glu_out",
    )(x, y_conv, y_gelu, w_glu, bg, w_out, g2)


def _proj_residual_kernel(x_ref, a_ref, w_ref, g_ref, xo_ref, ho_ref):
    xn = x_ref[...] + jnp.dot(a_ref[...], w_ref[...], preferred_element_type=F32)
    xo_ref[...] = xn
    _store_row_tiles(ho_ref, _rms(xn, g_ref[...]))


def proj_residual(x, a, w, g_next, tm=512):
    T, D = x.shape
    K = a.shape[1]
    tm = min(tm, T)
    row = lambda w_: pl.BlockSpec((tm, w_), lambda i: (i, 0))
    full = lambda arr: pl.BlockSpec(arr.shape, lambda i: (0,) * arr.ndim)
    g2 = g_next.reshape(1, D)
    return pl.pallas_call(
        _proj_residual_kernel,
        grid=(T // tm,),
        in_specs=[row(D), row(K), full(w), full(g2)],
        out_specs=[row(D), _row_tile_spec(tm, D, lambda i: (i, 0))],
        out_shape=[jax.ShapeDtypeStruct((T, D), F32), jax.ShapeDtypeStruct((T * D // LANES, LANES), F32)],
        compiler_params=_cparams(("parallel",), 56),
        name="proj_residual",
    )(x, a, w, g2)


def _router_kernel(h_ref, w_ref, b_ref, mi_ref, mf_ref, cnt_ref, *rest):
    run_ref = rest[-1]
    for zero_ref in rest[:-1]:
        zero_ref[...] = jnp.zeros_like(zero_ref)
    i = pl.program_id(0)

    @pl.when(i == 0)
    def _():
        run_ref[...] = jnp.zeros_like(run_ref)

    D = w_ref.shape[0]
    tm = h_ref.shape[0] * LANES // D
    h = _load_row_tiles(h_ref, tm, D // LANES)
    h_hi, h_lo = _bf16_pieces(h, 2)
    w_hi, w_lo = _bf16_pieces(w_ref[...], 2)
    logits = (jnp.dot(h_hi, w_hi, preferred_element_type=F32)
              + (jnp.dot(h_hi, w_lo, preferred_element_type=F32)
                 + jnp.dot(h_lo, w_hi, preferred_element_type=F32))) + b_ref[...]
    lane = lax.broadcasted_iota(I32, logits.shape, 1)
    neg = jnp.float32(-jnp.inf)
    gl = jnp.where(lane < N_GROUPS, logits, neg)
    gmax = jnp.max(gl, axis=1, keepdims=True)
    gsum = jnp.sum(jnp.where(lane < N_GROUPS, jnp.exp(gl - gmax), 0.0), axis=1, keepdims=True)
    gw = 1.0 / gsum
    gidx = jnp.min(jnp.where(gl == gmax, lane, LANES), axis=1, keepdims=True)
    lo = EXPERT_LANE0 + EXPERTS_PER_GROUP * gidx
    el = jnp.where((lane >= lo) & (lane < lo + EXPERTS_PER_GROUP), logits, neg)
    v1 = jnp.max(el, axis=1, keepdims=True)
    i1 = jnp.min(jnp.where(el == v1, lane, LANES), axis=1, keepdims=True)
    el2 = jnp.where(lane == i1, neg, el)
    v2 = jnp.max(el2, axis=1, keepdims=True)
    i2 = jnp.min(jnp.where(el2 == v2, lane, LANES), axis=1, keepdims=True)
    t = jnp.exp(v2 - v1)
    w1 = gw / (1.0 + t)
    w2 = gw * t / (1.0 + t)
    hit1 = lane == i1
    hit2 = lane == i2
    cnt = (hit1 | hit2).astype(BF16)
    r = lax.broadcasted_iota(I32, (tm, tm), 0)
    c = lax.broadcasted_iota(I32, (tm, tm), 1)
    before = (c < r).astype(BF16)
    cum = jnp.dot(before, cnt, preferred_element_type=F32) + run_ref[0:1, :]
    rank1 = jnp.sum(jnp.where(hit1, cum, 0.0), axis=1, keepdims=True).astype(I32)
    rank2 = jnp.sum(jnp.where(hit2, cum, 0.0), axis=1, keepdims=True).astype(I32)
    run = run_ref[0:1, :] + jnp.sum(cnt.astype(F32), axis=0, keepdims=True)
    run_ref[...] = jnp.broadcast_to(run, run_ref.shape)
    cnt_ref[...] = jnp.broadcast_to(run, cnt_ref.shape)
    code1 = (i1 - EXPERT_LANE0) * MOE_CODE + rank1
    code2 = (i2 - EXPERT_LANE0) * MOE_CODE + rank2
    mi_ref[...] = jnp.where(lane == 0, code1, jnp.where(lane == 1, code2, 0))
    mf_ref[...] = jnp.where(lane == 0, w1, jnp.where(lane == 1, w2, 0.0))


def router(h, w_r, b_r, zero_rows=0, tm=256):
    D = w_r.shape[0]
    T = h.shape[0] * LANES // D
    tm = min(tm, T)
    steps = T // tm
    row = lambda w: pl.BlockSpec((tm, w), lambda i: (i, 0))
    full = lambda a: pl.BlockSpec(a.shape, lambda i: (0,) * a.ndim)
    out_specs = [row(LANES), row(LANES), pl.BlockSpec((8, LANES), lambda i: (0, 0))]
    out_shape = [jax.ShapeDtypeStruct((T, LANES), I32), jax.ShapeDtypeStruct((T, LANES), F32),
                 jax.ShapeDtypeStruct((8, LANES), F32)]
    if zero_rows:
        out_specs.append(pl.BlockSpec((zero_rows // steps, LANES), lambda i: (i, 0)))
        out_shape.append(jax.ShapeDtypeStruct((zero_rows, LANES), F32))
    return pl.pallas_call(
        _router_kernel,
        grid=(steps,),
        in_specs=[_row_tile_spec(tm, D, lambda i: (i, 0)), full(w_r), full(b_r)],
        out_specs=out_specs,
        out_shape=out_shape,
        scratch_shapes=[pltpu.VMEM((8, LANES), F32)],
        compiler_params=_cparams(("arbitrary",), 40),
        name="moe_router",
    )(h, w_r, b_r)


def moe_plan(meta_i, counts, n_blocks):
    B = MOE_BLOCK
    code1, code2 = meta_i[:, 0], meta_i[:, 1]
    cnt = counts[0, EXPERT_LANE0:EXPERT_LANE0 + N_EXPERTS].astype(I32)
    nblk = (cnt + B - 1) // B
    blk_end = jnp.cumsum(nblk)
    blk_off = blk_end - nblk
    b = jnp.arange(n_blocks, dtype=I32)
    total = blk_end[-1]
    owner = jnp.minimum(jnp.sum((blk_end[None, :] <= b[:, None]).astype(I32), axis=1), N_EXPERTS - 1)
    valid = b < total
    last = jnp.maximum(total - 1, 0)
    blk_expert = jnp.where(valid, owner, owner[last])
    blk_src = jnp.where(valid, b, last)
    blk_first = (valid & (b == blk_off[owner])).astype(I32)
    e = jnp.arange(N_EXPERTS, dtype=I32)
    used = nblk > 0
    slot = (jnp.cumsum(used.astype(I32)) - 1) % 2
    later = (e[None, :] > e[:, None]) & used[None, :]
    next_used = jnp.min(jnp.where(later, e[None, :], N_EXPERTS), axis=1)
    next_used = jnp.where(next_used < N_EXPERTS, next_used, -1)
    return (code1, code2, blk_off * B, blk_expert, blk_src, valid.astype(I32), blk_first,
            next_used[blk_expert], slot[blk_expert])


def _row_copy(src, dst, sem):
    return pltpu.make_async_copy(src, dst, sem)


def _sorted_row(code, off_ref):
    return off_ref[code >> MOE_CODE_BITS] + (code & (MOE_CODE - 1))


def _tile_rows(row, n):
    return pl.ds(pl.multiple_of(row * n, n), n)


def _dispatch_kernel(code1_ref, code2_ref, off_ref, hp_ref, xs_in_ref, xs_ref, sem, *, n):
    del xs_in_ref
    tm = hp_ref.shape[0] // n
    base = pl.program_id(0) * tm

    def copies(r):
        src = hp_ref.at[_tile_rows(r, n)]
        return (_row_copy(src, xs_ref.at[_tile_rows(_sorted_row(code1_ref[base + r], off_ref), n)], sem),
                _row_copy(src, xs_ref.at[_tile_rows(_sorted_row(code2_ref[base + r], off_ref), n)], sem))

    def start(r, carry):
        for cp in copies(r):
            cp.start()
        return carry

    lax.fori_loop(0, tm, start, 0, unroll=8)
    for _ in range(2):
        _row_copy(hp_ref, xs_ref.at[pl.ds(0, tm * n)], sem).wait()


def dispatch(hp, width, code1, code2, row_off, xs0, tm=256):
    n = width // LANES
    T = hp.shape[0] // n
    tm = min(tm, T)
    return pl.pallas_call(
        functools.partial(_dispatch_kernel, n=n),
        grid_spec=pltpu.PrefetchScalarGridSpec(
            num_scalar_prefetch=3,
            grid=(T // tm,),
            in_specs=[_row_tile_spec(tm, width, lambda i, c1, c2, off: (i, 0)),
                      pl.BlockSpec(memory_space=pl.ANY)],
            out_specs=pl.BlockSpec(memory_space=pl.ANY),
            scratch_shapes=[pltpu.SemaphoreType.DMA(())]),
        out_shape=jax.ShapeDtypeStruct(xs0.shape, hp.dtype),
        input_output_aliases={4: 0},
        compiler_params=_cparams(("arbitrary",), 32),
        name="moe_dispatch",
    )(code1, code2, row_off, hp, xs0)


def _experts_kernel(be_ref, bs_ref, bv_ref, bf_ref, bn_ref, bl_ref, xs_ref, wg_ref, wu_ref, wd_ref, ys_ref,
                    wg_f, wu_f, wd_f, wg_s, wu_s, wd_s, sem):
    del bs_ref
    b = pl.program_id(0)

    def weight_copies(e, slot):
        return (pltpu.make_async_copy(wg_ref.at[e], wg_f.at[slot], sem.at[slot]),
                pltpu.make_async_copy(wu_ref.at[e], wu_f.at[slot], sem.at[slot]),
                pltpu.make_async_copy(wd_ref.at[e], wd_f.at[slot], sem.at[slot]))

    @pl.when(bv_ref[b] == 0)
    def _():
        ys_ref[...] = jnp.zeros_like(ys_ref)

    @pl.when(bv_ref[b] > 0)
    def _():
        @pl.when(bf_ref[b] == 1)
        def _():
            slot = bl_ref[b]

            @pl.when(b == 0)
            def _():
                for cp in weight_copies(be_ref[b], slot):
                    cp.start()

            for cp in weight_copies(be_ref[b], slot):
                cp.wait()
            wg_s[...] = wg_f[slot].astype(BF16)
            wu_s[...] = wu_f[slot].astype(BF16)
            wd_s[...] = wd_f[slot].astype(BF16)

            @pl.when(bn_ref[b] >= 0)
            def _():
                for cp in weight_copies(bn_ref[b], 1 - slot):
                    cp.start()

        D = wg_s.shape[0]
        a = _load_row_tiles(xs_ref, MOE_BLOCK, D // LANES).astype(BF16)
        gate = jnp.dot(a, wg_s[...], preferred_element_type=F32)
        up = jnp.dot(a, wu_s[...], preferred_element_type=F32)
        mid = (jax.nn.silu(gate) * up).astype(BF16)
        _store_row_tiles(ys_ref, jnp.dot(mid, wd_s[...], preferred_element_type=F32))


def experts(xs, blk_expert, blk_src, blk_valid, blk_first, blk_next, blk_slot, w_gate, w_up, w_down):
    B = MOE_BLOCK
    E, D, FF = w_gate.shape
    nb = xs.shape[0] * LANES // D // B
    hbm = pl.BlockSpec(memory_space=pl.ANY)
    return pl.pallas_call(
        _experts_kernel,
        grid_spec=pltpu.PrefetchScalarGridSpec(
            num_scalar_prefetch=6,
            grid=(nb,),
            in_specs=[_row_tile_spec(B, D, lambda b, be, bs, bv, bf, bn, bl: (bs[b], 0)), hbm, hbm, hbm],
            out_specs=_row_tile_spec(B, D, lambda b, be, bs, bv, bf, bn, bl: (b, 0)),
            scratch_shapes=[pltpu.VMEM((2, D, FF), F32), pltpu.VMEM((2, D, FF), F32),
                            pltpu.VMEM((2, FF, D), F32),
                            pltpu.VMEM((D, FF), BF16), pltpu.VMEM((D, FF), BF16),
                            pltpu.VMEM((FF, D), BF16), pltpu.SemaphoreType.DMA((2,))]),
        out_shape=jax.ShapeDtypeStruct(xs.shape, F32),
        compiler_params=_cparams(("arbitrary",), 56),
        name="moe_experts",
    )(blk_expert, blk_src, blk_valid, blk_first, blk_next, blk_slot, xs, w_gate, w_up, w_down)


def _combine_kernel(code1_ref, code2_ref, off_ref, x_ref, mf_ref, g_ref, ys_ref, xo_ref, ho_ref, buf, sem):
    tm, D = x_ref.shape
    n = D // LANES
    i = pl.program_id(0)
    slot = i % 2

    def fetch(step, dst_slot):
        base = step * tm

        def start(r, carry):
            dst = _tile_rows(r, n)
            _row_copy(ys_ref.at[_tile_rows(_sorted_row(code1_ref[base + r], off_ref), n)],
                      buf.at[dst_slot, 0, dst], sem.at[dst_slot]).start()
            _row_copy(ys_ref.at[_tile_rows(_sorted_row(code2_ref[base + r], off_ref), n)],
                      buf.at[dst_slot, 1, dst], sem.at[dst_slot]).start()
            return carry

        lax.fori_loop(0, tm, start, 0, unroll=8)

    @pl.when(i == 0)
    def _():
        fetch(0, 0)

    @pl.when(i + 1 < pl.num_programs(0))
    def _():
        fetch(i + 1, 1 - slot)

    for k in range(2):
        _row_copy(ys_ref.at[pl.ds(0, tm * n)], buf.at[slot, k], sem.at[slot]).wait()
    mf = mf_ref[...]
    xn = (x_ref[...] + mf[:, 0:1] * _load_row_tiles(buf, tm, n, lead=(slot, 0))
          + mf[:, 1:2] * _load_row_tiles(buf, tm, n, lead=(slot, 1)))
    xo_ref[...] = xn
    ho_ref[...] = _rms(xn, g_ref[...]).astype(ho_ref.dtype)


def combine(x, ys, meta_f, code1, code2, row_off, g_next, h_dtype, tm=256):
    T, D = x.shape
    tm = min(tm, T)
    row = lambda w: pl.BlockSpec((tm, w), lambda i, c1, c2, off: (i, 0))
    return pl.pallas_call(
        _combine_kernel,
        grid_spec=pltpu.PrefetchScalarGridSpec(
            num_scalar_prefetch=3,
            grid=(T // tm,),
            in_specs=[row(D), row(LANES), pl.BlockSpec((1, D), lambda i, c1, c2, off: (0, 0)),
                      pl.BlockSpec(memory_space=pl.ANY)],
            out_specs=[row(D), row(D)],
            scratch_shapes=[pltpu.VMEM((2, 2, tm * D // LANES, LANES), F32),
                            pltpu.SemaphoreType.DMA((2,))]),
        out_shape=[jax.ShapeDtypeStruct((T, D), F32), jax.ShapeDtypeStruct((T, D), h_dtype)],
        compiler_params=_cparams(("arbitrary",), 40),
        name="moe_combine",
    )(code1, code2, row_off, x, meta_f, g_next.reshape(1, D), ys)


def moe_row_blocks(T):
    return (2 * T) // MOE_BLOCK + N_EXPERTS


def moe_layer(x, h, xs_buf, w_group, b_group, w_expert, b_expert, layer, w_gate, w_up, w_down, g_next,
              h_dtype):
    T, D = x.shape
    assert T <= MOE_CODE, "ranks inside one expert must fit the packed (expert, rank) code"
    n_blocks = moe_row_blocks(T)
    pad = LANES - N_GROUPS - N_EXPERTS
    w_r = jnp.concatenate([w_group] + [w_expert[:, g, :] for g in range(N_GROUPS)]
                          + [jnp.zeros((D, pad), F32)], axis=1)
    b_r = jnp.concatenate([b_group, b_expert.reshape(N_EXPERTS), jnp.zeros((pad,), F32)]).reshape(1, LANES)
    if xs_buf is None:
        meta_i, meta_f, counts, xs_buf = router(h, w_r, b_r, zero_rows=n_blocks * MOE_BLOCK * D // LANES)
    else:
        meta_i, meta_f, counts = router(h, w_r, b_r)
    (code1, code2, row_off, blk_expert, blk_src, blk_valid, blk_first, blk_next,
     blk_slot) = moe_plan(meta_i, counts, n_blocks)
    xs = dispatch(h, D, code1, code2, row_off, xs_buf)
    FF = w_gate.shape[-1]
    first_row = layer * N_EXPERTS
    ys = experts(xs, blk_expert + first_row, blk_src, blk_valid, blk_first,
                 jnp.where(blk_next >= 0, blk_next + first_row, -1), blk_slot,
                 w_gate.reshape(-1, D, FF), w_up.reshape(-1, D, FF), w_down.reshape(-1, FF, D))
    xo, ho = combine(x, ys, meta_f, code1, code2, row_off, g_next, h_dtype)
    return xo, ho, xs


def _forget_kernel(h_ref, w_ref, b_ref, o_ref, run_ref):
    @pl.when(pl.program_id(0) == 0)
    def _():
        run_ref[...] = jnp.zeros_like(run_ref)

    h = h_ref[...]
    nt = (((1,), (1,)), ((), ()))
    z = sum(lax.dot_general(h, w, nt, preferred_element_type=F32)
            for w in _bf16_pieces(w_ref[...], 2)) + b_ref[...]
    log_f = jax.nn.log_sigmoid(z)
    tm = z.shape[0]
    r = lax.broadcasted_iota(I32, (tm, tm), 0)
    c = lax.broadcasted_iota(I32, (tm, tm), 1)
    upto = (c <= r).astype(BF16)
    cum = sum(jnp.dot(upto, p, preferred_element_type=F32) for p in _bf16_pieces(log_f, 3)) + run_ref[0:1, :]
    o_ref[...] = cum.T
    run_ref[...] = jnp.broadcast_to(cum[tm - 1:tm, :], run_ref.shape)


def forget_cumsum(h, w_f, b_f, tm=256):
    T, D = h.shape
    tm = min(tm, T)
    return pl.pallas_call(
        _forget_kernel,
        grid=(T // tm,),
        in_specs=[pl.BlockSpec((tm, D), lambda i: (i, 0)),
                  pl.BlockSpec((LANES, D), lambda i: (0, 0)),
                  pl.BlockSpec((1, LANES), lambda i: (0, 0))],
        out_specs=pl.BlockSpec((LANES, tm), lambda i: (0, i)),
        out_shape=jax.ShapeDtypeStruct((LANES, T), F32),
        scratch_shapes=[pltpu.VMEM((8, LANES), F32)],
        compiler_params=_cparams(("arbitrary",), 40),
        name="forget_cumsum",
    )(h, w_f, b_f)


def _flash_kernel(q_ref, k_ref, v_ref, ck_ref, cq_ref, o_ref, m_ref, acc_ref, *, scale, sub):
    qi = pl.program_id(1)
    bq = q_ref.shape[0]
    dh = FOX_HEAD_DIM
    heads = q_ref.shape[1] // dh
    m_ref[...] = jnp.full_like(m_ref, -jnp.inf)
    acc_ref[...] = jnp.zeros_like(acc_ref)
    ones = jnp.ones((bq, dh), BF16)
    n_sub = bq // sub

    def chunk(start, diagonal):
        def scores(hh):
            cols = slice(hh * dh, (hh + 1) * dh)
            k = k_ref[pl.ds(start, bq), cols]
            return lax.dot_general(q_ref[:, cols], k, (((1,), (1,)), ((), ())), preferred_element_type=F32)

        def softmax(hh, s_all):
            c0 = cq_ref[hh, :, 0:1]
            bias = (c0 - ck_ref[hh, :, pl.ds(start, bq)]) * LOG2E
            ps = []
            for r in range(n_sub):
                rows = slice(r * sub, (r + 1) * sub)
                s = s_all[rows, :] * (scale * LOG2E) + bias
                if diagonal:
                    row = lax.broadcasted_iota(I32, s.shape, 0) + r * sub
                    col = lax.broadcasted_iota(I32, s.shape, 1)
                    s = jnp.where(col <= row, s, -jnp.inf)
                m_prev = m_ref[hh, rows, :]
                m_new = jnp.maximum(m_prev, jnp.max(s, axis=1, keepdims=True))
                m_ref[hh, rows, :] = m_new
                ps.append(jnp.exp2(s - jnp.concatenate([m_new] * (bq // LANES), axis=1)).astype(BF16))
                alpha = jnp.exp2(m_prev - m_new)
                acc_ref[hh, rows, :] = jnp.concatenate([alpha] * 2, axis=1) * acc_ref[hh, rows, :]
            return jnp.concatenate(ps, axis=0)

        def update(hh, p):
            cols = slice(hh * dh, (hh + 1) * dh)
            v1 = jnp.concatenate([v_ref[pl.ds(start, bq), cols], ones], axis=1)
            acc_ref[hh] = acc_ref[hh] + jnp.dot(p, v1, preferred_element_type=F32)

        s_next = scores(0)
        for hh in range(heads):
            s_cur = s_next
            if hh + 1 < heads:
                s_next = scores(hh + 1)
            update(hh, softmax(hh, s_cur))

    def body(j, carry):
        chunk(pl.multiple_of(j * bq, bq), False)
        return carry

    lax.fori_loop(0, qi, body, 0)
    chunk(pl.multiple_of(qi * bq, bq), True)
    for hh in range(heads):
        o_ref[:, hh * dh:(hh + 1) * dh] = (acc_ref[hh, :, 0:dh] / acc_ref[hh, :, dh:2 * dh]).astype(o_ref.dtype)


def flash_attention(qkv, cum_t, bq=512, sub=32, heads=4):
    T = qkv.shape[0]
    H, dh = FOX_HEADS, FOX_HEAD_DIM
    bq = min(bq, T)
    kern = functools.partial(_flash_kernel, scale=dh ** -0.5, sub=min(sub, bq))
    hw = heads * dh
    nhb = H // heads
    return pl.pallas_call(
        kern,
        grid=(nhb, T // bq),
        in_specs=[pl.BlockSpec((bq, hw), lambda h, i: (i, h)),
                  pl.BlockSpec((T, hw), lambda h, i: (0, nhb + h)),
                  pl.BlockSpec((T, hw), lambda h, i: (0, 2 * nhb + h)),
                  pl.BlockSpec((heads, 1, T), lambda h, i: (h, 0, 0)),
                  pl.BlockSpec((heads, 1, bq), lambda h, i: (h, 0, i))],
        out_specs=pl.BlockSpec((bq, hw), lambda h, i: (i, h)),
        out_shape=jax.ShapeDtypeStruct((T, H * dh), BF16),
        scratch_shapes=[pltpu.VMEM((heads, bq, LANES), F32), pltpu.VMEM((heads, bq, 2 * dh), F32)],
        compiler_params=_cparams(("parallel", "arbitrary"), 48),
        name="fox_attention",
    )(qkv, qkv, qkv, cum_t, cum_t)


def kernel(x, ab_norm, ab_w_in, ab_conv_w, s5_lambda_re, s5_lambda_im, s5_log_dt, s5_b_re, s5_b_im,
           s5_c_re, s5_c_im, s5_d, s5_w_glu, s5_b_glu, ab_w_out, c_norm, c_w_in, c_b_forget, c_w_out,
           ffn_norm, router_w_group, router_b_group, router_w_expert, router_b_expert,
           moe_w_gate, moe_w_up, moe_w_down, final_norm):
    bsz, L, D = x.shape
    depth = ffn_norm.shape[0]
    cw = ab_conv_w.shape[-1]
    xt = x.reshape(bsz * L, D)
    h = None
    xs_buf = None
    for i in range(depth):
        j = i // 2
        if i % 2 == 0:
            g_in = ab_norm[j]
            if h is not None:
                proj = matmul(h, ab_w_in[j], F32)
            else:
                proj = norm_matmul(xt, g_in, ab_w_in[j].astype(BF16), F32)
            y_conv = conv_mixer(proj, ab_conv_w[j])
            y_gelu = s5_mixer(proj, 3 * cw, s5_lambda_re[j], s5_lambda_im[j], s5_log_dt[j],
                              s5_b_re[j], s5_b_im[j], s5_c_re[j], s5_c_im[j], s5_d[j])
            xt, hf = glu_out(xt, y_conv, y_gelu, s5_w_glu[j].astype(BF16), s5_b_glu[j],
                             ab_w_out[j].astype(BF16), ffn_norm[i])
        else:
            assert h is not None, "an attention layer always follows a MoE combine that emits its norm"
            hd = FOX_HEADS * FOX_HEAD_DIM
            w_t = c_w_in[j].T
            qkv = matmul(h, w_t, BF16, n_cols=3 * hd, w_transposed=True)
            w_f = jnp.pad(w_t[3 * hd:], ((0, LANES - FOX_HEADS), (0, 0)))
            b_f = jnp.pad(c_b_forget[j], (0, LANES - FOX_HEADS)).reshape(1, LANES)
            cum = forget_cumsum(h, w_f, b_f)
            cum_t = cum[:FOX_HEADS].reshape(FOX_HEADS, 1, bsz * L)
            att = flash_attention(qkv, cum_t)
            xt, hf = proj_residual(xt, att, c_w_out[j].astype(BF16), ffn_norm[i])
        last = i == depth - 1
        if last:
            g_next = final_norm
        elif (i + 1) % 2 == 0:
            g_next = ab_norm[(i + 1) // 2]
        else:
            g_next = c_norm[(i + 1) // 2]
        xt, h, xs_buf = moe_layer(xt, hf, xs_buf, router_w_group[i], router_b_group[i], router_w_expert[i],
                                  router_b_expert[i], i, moe_w_gate, moe_w_up, moe_w_down,
                                  g_next, F32 if last else BF16)
    return h.reshape(bsz, L, D)
```
